```python
import numpy as np
import jax
import jax.numpy as jnp
from jax import lax

D_MODEL = 2048
BATCH = 4
SEQ = 4096
DEPTH = 2

GRID_W = 64
CTX_LEN = 256
MIX = D_MODEL
GROUP = MIX // 4
ATT_HD = 64
NA_HEADS = GROUP // ATT_HD
NA_ROWS = 8
NA_COLS = 16
SW_HEADS = GROUP // ATT_HD
SW_KV_HEADS = SW_HEADS // 4
SW_WINDOW = 128
SW_BLOCK = 128
ROPE_BASE = 10000.0
ML_HEADS = 4
ML_DV = GROUP // ML_HEADS
ML_DK = ML_DV // 2
ML_CONV = 3
HG_HEADS = 4
HG_DK = GROUP // HG_HEADS
HG_DV = GROUP // HG_HEADS
CHUNK = 64
FFN_DIM = 256 * ((8 * D_MODEL // 3 + 255) // 256)
FFN_CONV = 3
EPS = 1e-6
NEG = -1e30

IN_SPLITS = (
    ("na_q", NA_HEADS * ATT_HD), ("na_k", NA_HEADS * ATT_HD), ("na_v", NA_HEADS * ATT_HD),
    ("sw_q", SW_HEADS * ATT_HD), ("sw_k", SW_KV_HEADS * ATT_HD), ("sw_v", SW_KV_HEADS * ATT_HD),
    ("ml_q", ML_HEADS * ML_DK), ("ml_k", ML_HEADS * ML_DK), ("ml_v", ML_HEADS * ML_DV),
    ("ml_o", ML_HEADS * ML_DV), ("ml_i", 2 * ML_HEADS), ("ml_f", 2 * ML_HEADS),
    ("hg_q", HG_HEADS * HG_DK), ("hg_i", HG_HEADS * HG_DV), ("hg_f", 2 * HG_HEADS * HG_DK),
    ("hg_g", HG_HEADS * HG_DV),
)
IN_WIDTH = sum(w for _, w in IN_SPLITS)

kernel_name = "hybrid_parallel_heads_dit_block"

F32 = jnp.float32


def rms_norm(x, gain=None):
    xf = x.astype(F32)
    y = xf * lax.rsqrt(jnp.mean(xf * xf, axis=-1, keepdims=True) + EPS)
    if gain is not None:
        y = y * gain.astype(F32)
    return y.astype(x.dtype)


def modulate(x, shift, scale):
    return x * (1 + scale) + shift


def split_heads(a, n):
    return a.reshape(*a.shape[:-1], n, a.shape[-1] // n)


def split_in(p):
    names = [n for n, _ in IN_SPLITS]
    cuts = np.cumsum([w for _, w in IN_SPLITS])[:-1].tolist()
    return dict(zip(names, jnp.split(p, cuts, axis=-1)))


def orient(a, rev):
    return a[:, ::-1] if rev else a


def dwconv(x, w):
    k, ch = w.shape
    return lax.conv_general_dilated(
        x, w[:, None, :].astype(x.dtype), window_strides=(1,),
        padding=[(k // 2, k // 2)], dimension_numbers=("NWC", "WIO", "NWC"),
        feature_group_count=ch)


def axial_rope_tables(n_tok):
    t = jnp.arange(n_tok)
    half = ATT_HD // 2
    inv = ROPE_BASE ** (-jnp.arange(0, half, 2, dtype=F32) / half)
    ang_r = (t // GRID_W).astype(F32)[:, None] * inv
    ang_c = (t % GRID_W).astype(F32)[:, None] * inv
    return (jnp.cos(ang_r)[:, None], jnp.sin(ang_r)[:, None],
            jnp.cos(ang_c)[:, None], jnp.sin(ang_c)[:, None])


def rotate(x, cos, sin):
    x1, x2 = jnp.split(x, 2, axis=-1)
    cos = cos.astype(x.dtype)
    sin = sin.astype(x.dtype)
    return jnp.concatenate([x1 * cos - x2 * sin, x1 * sin + x2 * cos], axis=-1)


def apply_axial_rope(x, tabs):
    cr, sr, cc, sc = tabs
    xr, xc = jnp.split(x, 2, axis=-1)
    return jnp.concatenate([rotate(xr, cr, sr), rotate(xc, cc, sc)], axis=-1)


def dense_attn(q, k, v, sink=None):
    bsz, lq, hq, d = q.shape
    lk, hkv = k.shape[1], k.shape[2]
    g = hq // hkv
    qg = q.reshape(bsz, lq, hkv, g, d)
    s = jnp.einsum('bqhgd,bkhd->bhgqk', qg, k).astype(F32) * d ** -0.5
    if sink is not None:
        s = jnp.concatenate([s, jnp.broadcast_to(sink.astype(F32).reshape(hkv, g, 1, 1), s.shape[:-1] + (1,))], axis=-1)
    p = jax.nn.softmax(s, axis=-1)[..., :lk].astype(v.dtype)
    return jnp.einsum('bhgqk,bkhd->bqhgd', p, v).reshape(bsz, lq, hq * d)


def neighbourhood_attn(q, k, v, k_ctx, v_ctx, rpb):
    bsz, s, h, d = q.shape
    rows = s // GRID_W
    kr = min(NA_ROWS, rows)
    nj = GRID_W // NA_COLS
    band = 2 * NA_COLS
    qcol = np.arange(GRID_W).reshape(nj, NA_COLS)
    band_start = np.clip(np.arange(nj) * NA_COLS - NA_COLS // 2, 0, GRID_W - band)
    kcol = band_start[:, None] + np.arange(band)
    cstart = np.clip(qcol - NA_COLS // 2, 0, GRID_W - NA_COLS)
    col_ok = (kcol[:, None, :] >= cstart[:, :, None]) & (kcol[:, None, :] < cstart[:, :, None] + NA_COLS)
    dc_idx = np.clip(kcol[:, None, :] - qcol[:, :, None] + NA_COLS - 1, 0, 2 * NA_COLS - 2)
    k_grid = k.reshape(bsz, rows, GRID_W, h, d)
    v_grid = v.reshape(bsz, rows, GRID_W, h, d)
    q_rows = jnp.moveaxis(q.reshape(bsz, rows, GRID_W, h, d), 1, 0)
    scale = d ** -0.5
    nkey = kr * band

    def one_row(args):
        r, q_row = args
        rs = jnp.clip(r - kr // 2, 0, rows - kr)
        kb = lax.dynamic_slice_in_dim(k_grid, rs, kr, axis=1)[:, :, kcol]
        vb = lax.dynamic_slice_in_dim(v_grid, rs, kr, axis=1)[:, :, kcol]
        qb = q_row.reshape(bsz, nj, NA_COLS, h, d)
        s_nb = jnp.einsum('bjqhd,brjkhd->bjhqrk', qb, kb).astype(F32) * scale
        dr_idx = rs + jnp.arange(kr) - r + NA_ROWS - 1
        bias = jnp.transpose(rpb[:, dr_idx][:, :, dc_idx], (2, 0, 3, 1, 4)).astype(F32)
        s_nb = jnp.where(col_ok[:, None, :, None, :], s_nb + bias, NEG).reshape(bsz, nj, h, NA_COLS, nkey)
        s_cx = jnp.einsum('bjqhd,blhd->bjhql', qb, k_ctx).astype(F32) * scale
        p = jax.nn.softmax(jnp.concatenate([s_nb, s_cx], axis=-1), axis=-1).astype(v.dtype)
        p_nb = p[..., :nkey].reshape(bsz, nj, h, NA_COLS, kr, band)
        o = (jnp.einsum('bjhqrk,brjkhd->bjqhd', p_nb, vb)
             + jnp.einsum('bjhql,blhd->bjqhd', p[..., nkey:], v_ctx))
        return o.reshape(bsz, GRID_W, h, d)

    out = lax.map(one_row, (jnp.arange(rows), q_rows))
    return jnp.moveaxis(out, 0, 1).reshape(bsz, s, h * d)


def window_attn(q, k, v, k_ctx, v_ctx, sink):
    bsz, s, hq, d = q.shape
    hkv = k.shape[2]
    g = hq // hkv
    nb = s // SW_BLOCK
    pad = ((0, 0), (SW_BLOCK, SW_BLOCK), (0, 0), (0, 0))

    def band(a):
        a = jnp.pad(a, pad).reshape(bsz, nb + 2, SW_BLOCK, hkv, d)
        return jnp.concatenate([a[:, :-2], a[:, 1:-1], a[:, 2:]], axis=2)

    kb, vb = band(k), band(v)
    qb = q.reshape(bsz, nb, SW_BLOCK, hkv, g, d)
    scale = d ** -0.5
    s_band = jnp.einsum('bnqhgd,bnkhd->bnhgqk', qb, kb).astype(F32) * scale
    rel = np.arange(3 * SW_BLOCK)[None, :] - (np.arange(SW_BLOCK)[:, None] + SW_BLOCK)
    key_pos = (np.arange(nb)[:, None] - 1) * SW_BLOCK + np.arange(3 * SW_BLOCK)[None, :]
    mask = (np.abs(rel) <= SW_WINDOW)[None] & ((key_pos >= 0) & (key_pos < s))[:, None, :]
    s_band = jnp.where(mask[:, None, None], s_band, NEG)
    s_ctx = jnp.einsum('bnqhgd,blhd->bnhgql', qb, k_ctx).astype(F32) * scale
    sink_col = jnp.broadcast_to(sink.astype(F32).reshape(hkv, g, 1, 1), s_band.shape[:-1] + (1,))
    p = jax.nn.softmax(jnp.concatenate([s_band, s_ctx, sink_col], axis=-1), axis=-1).astype(v.dtype)
    nk = 3 * SW_BLOCK
    out = (jnp.einsum('bnhgqk,bnkhd->bnqhgd', p[..., :nk], vb)
           + jnp.einsum('bnhgql,blhd->bnqhgd', p[..., nk:nk + k_ctx.shape[1]], v_ctx))
    return out.reshape(bsz, s, hq * d)


def to_chunks(a):
    b, t = a.shape[:2]
    a = a.reshape(b, t // CHUNK, CHUNK, *a.shape[2:])
    return jnp.swapaxes(jnp.moveaxis(a, 1, 0), 2, 3)


def from_chunks(a):
    a = jnp.moveaxis(jnp.swapaxes(a, 2, 3), 0, 1)
    return a.reshape(a.shape[0], a.shape[1] * a.shape[2], *a.shape[3:])


def mlstm_scan(q, k, v, logi, logf, state, emit):
    tri = jnp.tril(jnp.ones((CHUNK, CHUNK), dtype=bool))

    def step(carry, xs):
        c_mat, n_vec, m_sc = carry
        qc, kc, vc, ic, fc = xs
        fcum = jnp.cumsum(fc, axis=-1)
        ftot = fcum[..., -1]
        w_log = ftot[..., None] - fcum + ic
        m_new = jnp.maximum(ftot + m_sc, jnp.max(w_log, axis=-1))
        carry_decay = jnp.exp(ftot + m_sc - m_new)
        w = jnp.exp(w_log - m_new[..., None])
        c_new = carry_decay[..., None, None] * c_mat + jnp.einsum('bhl,bhlv,bhlk->bhvk', w, vc, kc)
        n_new = carry_decay[..., None] * n_vec + jnp.einsum('bhl,bhlk->bhk', w, kc)
        new = (c_new, n_new, m_new)
        if not emit:
            return new, None
        log_d = jnp.where(tri, fcum[..., :, None] - fcum[..., None, :] + ic[..., None, :], NEG)
        log_inter = fcum + m_sc[..., None]
        m_row = jnp.maximum(log_inter, jnp.max(log_d, axis=-1))
        d_mat = jnp.exp(log_d - m_row[..., None])
        inter_w = jnp.exp(log_inter - m_row)
        s_mat = jnp.einsum('bhtk,bhsk->bhts', qc, kc) * d_mat
        num = (inter_w[..., None] * jnp.einsum('bhvk,bhtk->bhtv', c_mat, qc)
               + jnp.einsum('bhts,bhsv->bhtv', s_mat, vc))
        den = inter_w * jnp.einsum('bhk,bhtk->bht', n_vec, qc) + jnp.sum(s_mat, axis=-1)
        h = num / jnp.maximum(jnp.abs(den), jnp.exp(-m_row))[..., None]
        return new, h

    fin, hs = lax.scan(step, state, tuple(to_chunks(a) for a in (q, k, v, logi, logf)))
    return fin, (from_chunks(hs) if emit else None)


def hgrn_scan(q, k, v, logf, state, emit):
    tri = jnp.tril(jnp.ones((CHUNK, CHUNK), dtype=bool))

    def step(s_mat, xs):
        qc, kc, vc, gc = xs
        gcum = jnp.cumsum(gc, axis=2)
        gtot = gcum[:, :, -1]
        s_new = (jnp.exp(gtot)[..., None] * s_mat
                 + jnp.einsum('bhlk,bhlv->bhkv', kc * jnp.exp(gtot[:, :, None] - gcum), vc))
        if not emit:
            return s_new, None
        o_inter = jnp.einsum('bhtk,bhkv->bhtv', qc * jnp.exp(gcum), s_mat)
        log_dec = jnp.where(tri[:, :, None], gcum[:, :, :, None] - gcum[:, :, None], NEG)
        att = jnp.einsum('bhtk,bhsk,bhtsk->bhts', qc, kc, jnp.exp(log_dec))
        return s_new, o_inter + jnp.einsum('bhts,bhsv->bhtv', att, vc)

    fin, os_ = lax.scan(step, state, tuple(to_chunks(a) for a in (q, k, v, logf)))
    return fin, (from_chunks(os_) if emit else None)


def mixer_na(pl, pc, qk_gain, rpb, emit_ctx):
    def qkv(p):
        q = rms_norm(split_heads(p['na_q'], NA_HEADS), qk_gain[0])
        k = rms_norm(split_heads(p['na_k'], NA_HEADS), qk_gain[1])
        return q, k, split_heads(p['na_v'], NA_HEADS)
    q, k, v = qkv(pl)
    qc, kc, vc = qkv(pc)
    y = neighbourhood_attn(q, k, v, kc, vc, rpb)
    yc = dense_attn(qc, kc, vc) if emit_ctx else None
    return y, yc


def mixer_sw(pl, pc, qk_gain, sink, rope, emit_ctx):
    def qkv(p):
        q = rms_norm(split_heads(p['sw_q'], SW_HEADS), qk_gain[0])
        k = rms_norm(split_heads(p['sw_k'], SW_KV_HEADS), qk_gain[1])
        return q, k, split_heads(p['sw_v'], SW_KV_HEADS)
    q, k, v = qkv(pl)
    qc, kc, vc = qkv(pc)
    q, k = apply_axial_rope(q, rope), apply_axial_rope(k, rope)
    y = window_attn(q, k, v, kc, vc, sink)
    yc = dense_attn(qc, kc, vc, sink) if emit_ctx else None
    return y, yc


def mixer_mlstm(pl, pc, conv_w, gate_bias, emit_ctx):
    def prep(p):
        qk = jax.nn.silu(dwconv(jnp.concatenate([p['ml_q'], p['ml_k']], axis=-1), conv_w)).astype(F32)
        q, k = jnp.split(qk, 2, axis=-1)
        q = split_heads(q, ML_HEADS) * ML_DK ** -0.5
        k = split_heads(k, ML_HEADS)
        v = split_heads(p['ml_v'].astype(F32), ML_HEADS)
        logi = split_heads(p['ml_i'].astype(F32), 2) + gate_bias[0].astype(F32)
        logf = jax.nn.log_sigmoid(split_heads(p['ml_f'].astype(F32), 2) + gate_bias[1].astype(F32))
        return q, k, v, logi, logf

    lat, cx = prep(pl), prep(pc)
    bsz = pl['ml_q'].shape[0]
    h_lat, h_ctx = [], []
    for d in range(2):
        rev = d == 1

        def seq(t):
            q, k, v, li, lf = t
            return (orient(q, rev), orient(k, rev), orient(v, rev),
                    orient(li[:, :, d], rev), orient(lf[:, :, d], rev))

        state0 = (jnp.zeros((bsz, ML_HEADS, ML_DV, ML_DK), F32),
                  jnp.zeros((bsz, ML_HEADS, ML_DK), F32),
                  jnp.zeros((bsz, ML_HEADS), F32))
        st_ctx, hc = mlstm_scan(*seq(cx), state0, emit_ctx)
        _, hl = mlstm_scan(*seq(lat), st_ctx, True)
        h_lat.append(orient(hl, rev))
        if emit_ctx:
            h_ctx.append(orient(hc, rev))

    def readout(h, p):
        h = h.reshape(*h.shape[:2], -1)
        return (jax.nn.sigmoid(p['ml_o'].astype(F32)) * h).astype(p['ml_o'].dtype)

    y = readout(h_lat[0] + h_lat[1], pl)
    yc = readout(h_ctx[0] + h_ctx[1], pc) if emit_ctx else None
    return y, yc


def mixer_hgrn(pl, pc, lower_bound, emit_ctx):
    lb = split_heads(lower_bound.astype(F32), HG_HEADS)

    def prep(p):
        q = split_heads(jax.nn.silu(p['hg_q'].astype(F32)), HG_HEADS)
        v = split_heads(p['hg_i'].astype(F32), HG_HEADS)
        f_pre = split_heads(split_heads(p['hg_f'].astype(F32), 2), HG_HEADS)
        logf = jnp.logaddexp(jnp.log(lb), jnp.log1p(-lb) + jax.nn.log_sigmoid(f_pre))
        k = (1.0 - lb) * jax.nn.sigmoid(-f_pre)
        return q, k, v, logf

    lat, cx = prep(pl), prep(pc)
    bsz = pl['hg_q'].shape[0]
    o_lat, o_ctx = [], []
    for d in range(2):
        rev = d == 1

        def seq(t):
            q, k, v, lf = t
            return orient(q, rev), orient(k[:, :, d], rev), orient(v, rev), orient(lf[:, :, d], rev)

        s0 = jnp.zeros((bsz, HG_HEADS, HG_DK, HG_DV), F32)
        s_ctx, oc = hgrn_scan(*seq(cx), s0, emit_ctx)
        _, ol = hgrn_scan(*seq(lat), s_ctx, True)
        o_lat.append(orient(ol, rev))
        if emit_ctx:
            o_ctx.append(orient(oc, rev))

    def readout(o, p):
        y = rms_norm(o) * jax.nn.sigmoid(split_heads(p['hg_g'].astype(F32), HG_HEADS))
        return y.reshape(*y.shape[:2], -1).astype(p['hg_g'].dtype)

    y = readout(o_lat[0] + o_lat[1], pl)
    yc = readout(o_ctx[0] + o_ctx[1], pc) if emit_ctx else None
    return y, yc


def conv_ffn(xn, w_up, w_conv, w_down):
    a, u = jnp.split(xn @ w_up, 2, axis=-1)
    return (jax.nn.silu(dwconv(a, w_conv)) * u) @ w_down


def setup_inputs(seed: int = 0) -> dict:
    key = jax.random.key(seed)
    ks = jax.random.split(key, 20)
    D = D_MODEL

    def nrm(k, shape, s):
        return jax.random.normal(k, shape, F32) * s

    f_bias = jnp.stack([jnp.zeros((ML_HEADS,), F32), jnp.linspace(3.0, 6.0, ML_HEADS, dtype=F32)])[:, None, :]
    return {
        "x": nrm(ks[0], (BATCH, SEQ, D), 1.0),
        "c": nrm(ks[1], (BATCH, D), 1.0),
        "ctx": nrm(ks[2], (BATCH, CTX_LEN, D), 1.0),
        "c_ctx": nrm(ks[3], (D,), 1.0),
        "w_mod": nrm(ks[4], (DEPTH, D, 6 * D), 0.5 * D ** -0.5),
        "b_mod": nrm(ks[5], (DEPTH, 6 * D), 0.02),
        "norm_mix": 1.0 + nrm(ks[6], (DEPTH, D), 0.02),
        "norm_ffn": 1.0 + nrm(ks[7], (DEPTH, D), 0.02),
        "w_in": nrm(ks[8], (DEPTH, D, IN_WIDTH), D ** -0.5),
        "w_out": nrm(ks[9], (DEPTH, MIX, D), MIX ** -0.5),
        "na_qk_gain": 1.0 + nrm(ks[10], (DEPTH, 2, ATT_HD), 0.02),
        "na_rpb": nrm(ks[11], (DEPTH, NA_HEADS, 2 * NA_ROWS - 1, 2 * NA_COLS - 1), 0.5),
        "sw_qk_gain": 1.0 + nrm(ks[12], (DEPTH, 2, ATT_HD), 0.02),
        "sw_sink": nrm(ks[13], (DEPTH, SW_HEADS), 0.5),
        "ml_conv": nrm(ks[14], (DEPTH, ML_CONV, 2 * ML_HEADS * ML_DK), ML_CONV ** -0.5),
        "ml_gate_bias": nrm(ks[15], (DEPTH, 2, 2, ML_HEADS), 0.1) + f_bias,
        "hg_lb": nrm(ks[16], (DEPTH, 2, HG_HEADS * HG_DK), 0.5),
        "ffn_up": nrm(ks[17], (DEPTH, D, 2 * FFN_DIM), D ** -0.5),
        "ffn_conv": nrm(ks[18], (DEPTH, FFN_CONV, FFN_DIM), FFN_CONV ** -0.5),
        "ffn_down": nrm(ks[19], (DEPTH, FFN_DIM, D), FFN_DIM ** -0.5),
    }


def reference(x, c, ctx, c_ctx, w_mod, b_mod, norm_mix, norm_ffn, w_in, w_out,
              na_qk_gain, na_rpb, sw_qk_gain, sw_sink, ml_conv, ml_gate_bias,
              hg_lb, ffn_up, ffn_conv, ffn_down):
    rope = axial_rope_tables(x.shape[1])
    lb_cum = jnp.cumsum(jax.nn.softmax(hg_lb.astype(F32), axis=0), axis=0)
    lower_bounds = lb_cum - lb_cum[:1]
    h, hc = x, ctx
    for l in range(DEPTH):
        emit_ctx = l < DEPTH - 1
        mod = jax.nn.silu(c) @ w_mod[l] + b_mod[l]
        mod_c = jax.nn.silu(c_ctx) @ w_mod[l] + b_mod[l]
        sh1, sc1, g1, sh2, sc2, g2 = jnp.split(mod[:, None, :], 6, axis=-1)
        sh1c, sc1c, g1c, sh2c, sc2c, g2c = jnp.split(mod_c, 6, axis=-1)
        pl = split_in(modulate(rms_norm(h, norm_mix[l]), sh1, sc1) @ w_in[l])
        pc = split_in(modulate(rms_norm(hc, norm_mix[l]), sh1c, sc1c) @ w_in[l])
        ya, yac = mixer_na(pl, pc, na_qk_gain[l], na_rpb[l], emit_ctx)
        yb, ybc = mixer_sw(pl, pc, sw_qk_gain[l], sw_sink[l], rope, emit_ctx)
        yc, ycc = mixer_mlstm(pl, pc, ml_conv[l], ml_gate_bias[l], emit_ctx)
        yd, ydc = mixer_hgrn(pl, pc, lower_bounds[l], emit_ctx)
        h = h + g1 * (jnp.concatenate([ya, yb, yc, yd], axis=-1) @ w_out[l])
        h = h + g2 * conv_ffn(modulate(rms_norm(h, norm_ffn[l]), sh2, sc2), ffn_up[l], ffn_conv[l], ffn_down[l])
        if emit_ctx:
            hc = hc + g1c * (jnp.concatenate([yac, ybc, ycc, ydc], axis=-1) @ w_out[l])
            hc = hc + g2c * conv_ffn(modulate(rms_norm(hc, norm_ffn[l]), sh2c, sc2c), ffn_up[l], ffn_conv[l], ffn_down[l])
    return h
```

```python
import functools

import numpy as np
import jax
import jax.numpy as jnp
from jax import lax
from jax.experimental import pallas as pl
from jax.experimental.pallas import tpu as pltpu

F32 = jnp.float32
BF16 = jnp.bfloat16

GRID_W = 64
ATT_HD = 64
NA_ROWS = 8
NA_COLS = 16
SW_BLOCK = 128
ROPE_BASE = 10000.0
EPS = 1e-6
NEG = -1e30
GROUP = 512
LANES = 128
HALO = 8
VMEM_LIMIT = 56 * 1024 * 1024

C_NAQ, C_NAK, C_NAV, C_SWQ = 0, 512, 1024, 1536
C_MLQK, C_MLV, C_MLO = 2048, 2560, 3072
C_HGQ, C_HGI, C_HGF, C_HGG = 3584, 4096, 4608, 5632
C_SWK, C_SWV = 6144, 6272
C_MLI, C_MLF = 6400, 6528
NP = 6656
SW_PERM = (0, 4, 1, 5, 2, 6, 3, 7)


def _cparams(sem):
    return pltpu.CompilerParams(dimension_semantics=sem, vmem_limit_bytes=VMEM_LIMIT)


def _sigmoid(x):
    return 1.0 / (1.0 + jnp.exp(-x))


def _log_sigmoid(x):
    return jnp.minimum(x, 0.0) - jnp.log1p(jnp.exp(-jnp.abs(x)))


def _split3(x):
    hi = x.astype(BF16)
    r = x - hi.astype(F32)
    mid = r.astype(BF16)
    lo = (r - mid.astype(F32)).astype(BF16)
    return hi, mid, lo


def _dot(a, b):
    return jnp.dot(a, b, preferred_element_type=F32)


def _dot_nt(a, b):
    return lax.dot_general(a, b, (((1,), (1,)), ((), ())), preferred_element_type=F32)


def _dot_tn(a, b):
    return lax.dot_general(a, b, (((0,), (0,)), ((), ())), preferred_element_type=F32)


def _exact_left(m01, x):
    hi, mid, lo = _split3(x)
    return _dot(m01, hi) + _dot(m01, mid) + _dot(m01, lo)


def _exact_right(x, m01):
    hi, mid, lo = _split3(x)
    return _dot(hi, m01) + _dot(mid, m01) + _dot(lo, m01)


def _mod_kernel(c_ref, w_ref, b_ref, o_ref):
    c = c_ref[...]
    a = c * _sigmoid(c)
    a_hi = a.astype(BF16)
    a_lo = (a - a_hi.astype(F32)).astype(BF16)
    w = w_ref[...]
    w_hi = w.astype(BF16)
    w_lo = (w - w_hi.astype(F32)).astype(BF16)
    o_ref[...] = _dot(a_hi, w_hi) + _dot(a_hi, w_lo) + _dot(a_lo, w_hi) + b_ref[...]


def _modulation(c_all, w_mod, b_mod):
    depth, d, n = w_mod.shape
    tn = 1024 if n % 1024 == 0 else n
    return pl.pallas_call(
        _mod_kernel,
        grid=(depth, n // tn),
        in_specs=[pl.BlockSpec((8, d), lambda l, j: (0, 0)),
                  pl.BlockSpec((None, d, tn), lambda l, j: (l, 0, j)),
                  pl.BlockSpec((None, 1, tn), lambda l, j: (l, 0, j))],
        out_specs=pl.BlockSpec((None, 8, tn), lambda l, j: (l, 0, j)),
        out_shape=jax.ShapeDtypeStruct((depth, 8, n), F32),
        compiler_params=_cparams(("parallel", "parallel")),
    )(c_all, w_mod, b_mod.reshape(depth, 1, n))


def _norm_mod(x, gain, shift, scale):
    y = x * lax.rsqrt(jnp.mean(x * x, axis=-1, keepdims=True) + EPS) * gain
    return y * (1.0 + scale) + shift


def _inproj_kernel(x_ref, g_ref, sh_ref, sc_ref, w_ref, o_ref, xn_ref):
    @pl.when(pl.program_id(1) == 0)
    def _():
        xn_ref[...] = _norm_mod(x_ref[...], g_ref[...], sh_ref[...], sc_ref[...]).astype(BF16)

    o_ref[...] = _dot(xn_ref[...], w_ref[...])


def _row_tile(rows_per_mod, pref):
    tm = min(pref, rows_per_mod)
    assert rows_per_mod % tm == 0
    return tm


def _in_proj(x, gain, shift, scale, w, rows_per_mod):
    t, d = x.shape
    n = w.shape[1]
    tm = _row_tile(rows_per_mod, 512)
    tn = 512 if n % 512 == 0 else n
    per = rows_per_mod // tm
    return pl.pallas_call(
        _inproj_kernel,
        grid=(t // tm, n // tn),
        in_specs=[pl.BlockSpec((tm, d), lambda i, j: (i, 0)),
                  pl.BlockSpec((1, d), lambda i, j: (0, 0)),
                  pl.BlockSpec((None, 1, d), lambda i, j: (i // per, 0, 0)),
                  pl.BlockSpec((None, 1, d), lambda i, j: (i // per, 0, 0)),
                  pl.BlockSpec((d, tn), lambda i, j: (0, j))],
        out_specs=pl.BlockSpec((tm, tn), lambda i, j: (i, j)),
        out_shape=jax.ShapeDtypeStruct((t, n), F32),
        scratch_shapes=[pltpu.VMEM((tm, d), BF16)],
        compiler_params=_cparams(("parallel", "arbitrary")),
    )(x, gain, shift, scale, w)


def _head_rms(x, gain_row, bd):
    xx = x * x
    hi = xx.astype(BF16)
    lo = (xx - hi.astype(F32)).astype(BF16)
    ss = _dot(hi, bd) + _dot(lo, bd)
    return x * lax.rsqrt(ss * (1.0 / ATT_HD) + EPS) * gain_row


def _rope(x, cos, sin, first):
    w = x.shape[-1]
    nxt = pltpu.roll(x, w - 16, axis=1)
    prv = pltpu.roll(x, 16, axis=1)
    return x * cos + jnp.where(first, nxt, prv) * sin


def _attn_prep_kernel(p_ref, pkv_ref, gna_ref, gsw_ref, bd_ref, cos_ref, sin_ref,
                      naq_ref, nak_ref, nav_ref, swq_ref, swk_ref, swv_ref, *, rope):
    bd = bd_ref[...]
    scale = ATT_HD ** -0.5
    naq_ref[...] = (_head_rms(p_ref[:, C_NAQ:C_NAQ + GROUP], gna_ref[0:1, :], bd) * scale).astype(BF16)
    nak_ref[...] = _head_rms(p_ref[:, C_NAK:C_NAK + GROUP], gna_ref[1:2, :], bd).astype(BF16)
    nav_ref[...] = p_ref[:, C_NAV:C_NAV + GROUP].astype(BF16)
    q = _head_rms(p_ref[:, C_SWQ:C_SWQ + GROUP], gsw_ref[0:1, :], bd)
    k = _head_rms(pkv_ref[:, 0:LANES], gsw_ref[1:2, 0:LANES], bd[0:LANES, 0:LANES])
    if rope:
        cos = cos_ref[...]
        sin = sin_ref[...]
        lane = lax.broadcasted_iota(jnp.int32, (1, GROUP), 1)
        first = (lane % 32) < 16
        q = _rope(q, jnp.concatenate([cos] * 4, axis=1), jnp.concatenate([sin] * 4, axis=1), first)
        k = _rope(k, cos, sin, first[:, 0:LANES])
    swq_ref[...] = (q * scale).astype(BF16)
    swk_ref[...] = k.astype(BF16)
    swv_ref[...] = pkv_ref[:, LANES:2 * LANES].astype(BF16)


def _attn_prep(p, gna, gsw, bd, cos, sin, seq, rope):
    t = p.shape[0]
    tm = _row_tile(seq, 512)
    per = seq // tm
    out = lambda w: jax.ShapeDtypeStruct((t, w), BF16)
    ospec = lambda w: pl.BlockSpec((tm, w), lambda i: (i, 0))
    return pl.pallas_call(
        functools.partial(_attn_prep_kernel, rope=rope),
        grid=(t // tm,),
        in_specs=[pl.BlockSpec((tm, C_MLQK), lambda i: (i, 0)),
                  pl.BlockSpec((tm, 2 * LANES), lambda i: (i, C_SWK // (2 * LANES))),
                  pl.BlockSpec((2, GROUP), lambda i: (0, 0)),
                  pl.BlockSpec((2, GROUP), lambda i: (0, 0)),
                  pl.BlockSpec((GROUP, GROUP), lambda i: (0, 0)),
                  pl.BlockSpec((tm, LANES), lambda i: (i % per, 0)),
                  pl.BlockSpec((tm, LANES), lambda i: (i % per, 0))],
        out_specs=[ospec(GROUP), ospec(GROUP), ospec(GROUP), ospec(GROUP), ospec(LANES), ospec(LANES)],
        out_shape=[out(GROUP), out(GROUP), out(GROUP), out(GROUP), out(LANES), out(LANES)],
        compiler_params=_cparams(("parallel",)),
    )(p, p, gna, gsw, bd, cos, sin)


def _pair_rows(q_pair):
    lane = lax.broadcasted_iota(jnp.int32, q_pair.shape, 1)
    zero = jnp.zeros_like(q_pair)
    return jnp.concatenate([jnp.where(lane < ATT_HD, q_pair, zero),
                            jnp.where(lane >= ATT_HD, q_pair, zero)], axis=0)


def _pair_merge(o2, inv_l):
    m = o2.shape[0] // 2
    lane = lax.broadcasted_iota(jnp.int32, (m, LANES), 1)
    return jnp.where(lane < ATT_HD, o2[0:m] * inv_l[0:m], o2[m:] * inv_l[m:])


def _na_kernel(q_ref, k_ref, v_ref, kc_ref, vc_ref, b_ref, o_ref, *, rows):
    r = pl.program_id(1)
    rs = jnp.clip(r - NA_ROWS // 2, 0, rows - NA_ROWS)
    start = pl.multiple_of(rs * GRID_W, GRID_W)
    nwin = NA_ROWS * GRID_W
    for p in range(GROUP // LANES):
        cs = slice(p * LANES, (p + 1) * LANES)
        qs = _pair_rows(q_ref[:, cs])
        kw = k_ref[pl.ds(start, nwin), cs]
        vw = v_ref[pl.ds(start, nwin), cs]
        bias = jnp.concatenate([b_ref[2 * p], b_ref[2 * p + 1]], axis=0)
        s_nb = _dot_nt(qs, kw) + bias
        s_cx = _dot_nt(qs, kc_ref[:, cs])
        m = jnp.maximum(jnp.max(s_nb, axis=-1, keepdims=True), jnp.max(s_cx, axis=-1, keepdims=True))
        p_nb = jnp.exp(s_nb - m)
        p_cx = jnp.exp(s_cx - m)
        l = jnp.sum(p_nb, axis=-1, keepdims=True) + jnp.sum(p_cx, axis=-1, keepdims=True)
        o2 = _dot(p_nb.astype(BF16), vw) + _dot(p_cx.astype(BF16), vc_ref[:, cs])
        o_ref[:, cs] = _pair_merge(o2, 1.0 / l).astype(BF16)


def _na_attention(q, k, v, kc, vc, bias, bsz, seq, ctx):
    rows = seq // GRID_W
    assert rows >= NA_ROWS
    k3, v3 = k.reshape(bsz, seq, GROUP), v.reshape(bsz, seq, GROUP)
    kc3, vc3 = kc.reshape(bsz, ctx, GROUP), vc.reshape(bsz, ctx, GROUP)

    def bias_idx(b, r):
        rs = jnp.clip(r - NA_ROWS // 2, 0, rows - NA_ROWS)
        return (rs - r + NA_ROWS - 1, 0, 0, 0)

    return pl.pallas_call(
        functools.partial(_na_kernel, rows=rows),
        grid=(bsz, rows),
        in_specs=[pl.BlockSpec((GRID_W, GROUP), lambda b, r: (b * rows + r, 0)),
                  pl.BlockSpec((None, seq, GROUP), lambda b, r: (b, 0, 0)),
                  pl.BlockSpec((None, seq, GROUP), lambda b, r: (b, 0, 0)),
                  pl.BlockSpec((None, ctx, GROUP), lambda b, r: (b, 0, 0)),
                  pl.BlockSpec((None, ctx, GROUP), lambda b, r: (b, 0, 0)),
                  pl.BlockSpec((None, 8, GRID_W, NA_ROWS * GRID_W), bias_idx)],
        out_specs=pl.BlockSpec((GRID_W, GROUP), lambda b, r: (b * rows + r, 0)),
        out_shape=jax.ShapeDtypeStruct((bsz * seq, GROUP), BF16),
        compiler_params=_cparams(("parallel", "arbitrary")),
    )(q, k3, v3, kc3, vc3, bias)


def _na_bias_table(rpb):
    c = np.arange(GRID_W)
    kc = np.arange(GRID_W)
    cstart = np.clip(c - NA_COLS // 2, 0, GRID_W - NA_COLS)
    ok = (kc[None, :] >= cstart[:, None]) & (kc[None, :] < cstart[:, None] + NA_COLS)
    dc = np.clip(kc[None, :] - c[:, None] + NA_COLS - 1, 0, 2 * NA_COLS - 2)
    dr = np.arange(NA_ROWS)[:, None] + np.arange(NA_ROWS)[None, :]
    t = rpb.astype(F32)[:, dr][:, :, :, dc]
    t = jnp.where(ok[None, None, None], t, NEG)
    t = jnp.transpose(t, (1, 0, 3, 2, 4))
    return t.reshape(NA_ROWS, rpb.shape[0], GRID_W, NA_ROWS * GRID_W)


def _sw_kernel(sink_ref, q_ref, k_ref, v_ref, kc_ref, vc_ref, o_ref, *, seq):
    n = pl.program_id(1)
    nwin = 3 * SW_BLOCK
    start = pl.multiple_of(jnp.clip((n - 1) * SW_BLOCK, 0, seq - nwin), SW_BLOCK)
    kw = k_ref[pl.ds(start, nwin), :]
    vw = v_ref[pl.ds(start, nwin), :]
    kc = kc_ref[...]
    vc = vc_ref[...]
    row = lax.broadcasted_iota(jnp.int32, (2 * SW_BLOCK, nwin), 0)
    col = lax.broadcasted_iota(jnp.int32, (2 * SW_BLOCK, nwin), 1)
    rel = (start + col) - (n * SW_BLOCK + row % SW_BLOCK)
    ok = jnp.abs(rel) <= SW_BLOCK
    half = lax.broadcasted_iota(jnp.int32, (2 * SW_BLOCK, 1), 0) < SW_BLOCK
    for j in range(GROUP // LANES):
        cs = slice(j * LANES, (j + 1) * LANES)
        qs = _pair_rows(q_ref[:, cs])
        s_bd = jnp.where(ok, _dot_nt(qs, kw), NEG)
        s_cx = _dot_nt(qs, kc)
        sink = jnp.where(half, sink_ref[j], sink_ref[j + 4])
        m = jnp.maximum(jnp.maximum(jnp.max(s_bd, axis=-1, keepdims=True),
                                    jnp.max(s_cx, axis=-1, keepdims=True)), sink)
        p_bd = jnp.exp(s_bd - m)
        p_cx = jnp.exp(s_cx - m)
        l = (jnp.sum(p_bd, axis=-1, keepdims=True) + jnp.sum(p_cx, axis=-1, keepdims=True)
             + jnp.exp(sink - m))
        o2 = _dot(p_bd.astype(BF16), vw) + _dot(p_cx.astype(BF16), vc)
        o_ref[:, cs] = _pair_merge(o2, 1.0 / l).astype(BF16)


def _sw_attention(q, k, v, kc, vc, sink, bsz, seq, ctx):
    nb = seq // SW_BLOCK
    assert nb >= 3
    k3, v3 = k.reshape(bsz, seq, LANES), v.reshape(bsz, seq, LANES)
    kc3, vc3 = kc.reshape(bsz, ctx, LANES), vc.reshape(bsz, ctx, LANES)
    return pl.pallas_call(
        functools.partial(_sw_kernel, seq=seq),
        grid=(bsz, nb),
        in_specs=[pl.BlockSpec(memory_space=pltpu.SMEM),
                  pl.BlockSpec((SW_BLOCK, GROUP), lambda b, n: (b * nb + n, 0)),
                  pl.BlockSpec((None, seq, LANES), lambda b, n: (b, 0, 0)),
                  pl.BlockSpec((None, seq, LANES), lambda b, n: (b, 0, 0)),
                  pl.BlockSpec((None, ctx, LANES), lambda b, n: (b, 0, 0)),
                  pl.BlockSpec((None, ctx, LANES), lambda b, n: (b, 0, 0))],
        out_specs=pl.BlockSpec((SW_BLOCK, GROUP), lambda b, n: (b * nb + n, 0)),
        out_shape=jax.ShapeDtypeStruct((bsz * seq, GROUP), BF16),
        compiler_params=_cparams(("parallel", "arbitrary")),
    )(sink, q, k3, v3, kc3, vc3)


def _ctx_attn_kernel(sink_ref, qa_ref, ka_ref, va_ref, qb_ref, kb_ref, vb_ref, oa_ref, ob_ref):
    for p in range(GROUP // LANES):
        cs = slice(p * LANES, (p + 1) * LANES)
        qs = _pair_rows(qa_ref[:, cs])
        s = _dot_nt(qs, ka_ref[:, cs])
        m = jnp.max(s, axis=-1, keepdims=True)
        e = jnp.exp(s - m)
        l = jnp.sum(e, axis=-1, keepdims=True)
        oa_ref[:, cs] = _pair_merge(_dot(e.astype(BF16), va_ref[:, cs]), 1.0 / l).astype(BF16)
    ctx = qb_ref.shape[0]
    half = lax.broadcasted_iota(jnp.int32, (2 * ctx, 1), 0) < ctx
    for j in range(GROUP // LANES):
        cs = slice(j * LANES, (j + 1) * LANES)
        qs = _pair_rows(qb_ref[:, cs])
        s = _dot_nt(qs, kb_ref[...])
        sink = jnp.where(half, sink_ref[j], sink_ref[j + 4])
        m = jnp.maximum(jnp.max(s, axis=-1, keepdims=True), sink)
        e = jnp.exp(s - m)
        l = jnp.sum(e, axis=-1, keepdims=True) + jnp.exp(sink - m)
        ob_ref[:, cs] = _pair_merge(_dot(e.astype(BF16), vb_ref[...]), 1.0 / l).astype(BF16)


def _ctx_attention(qa, ka, va, qb, kb, vb, sink, bsz, ctx):
    big = pl.BlockSpec((ctx, GROUP), lambda b: (b, 0))
    small = pl.BlockSpec((ctx, LANES), lambda b: (b, 0))
    return pl.pallas_call(
        _ctx_attn_kernel,
        grid=(bsz,),
        in_specs=[pl.BlockSpec(memory_space=pltpu.SMEM), big, big, big, big, small, small],
        out_specs=[big, big],
        out_shape=[jax.ShapeDtypeStruct((bsz * ctx, GROUP), BF16)] * 2,
        compiler_params=_cparams(("parallel",)),
    )(sink, qa, ka, va, qb, kb, vb)


def _halo_specs(tm, width, col_block, nblk8):
    step = tm // HALO
    return [pl.BlockSpec((tm, width), lambda i: (i, col_block)),
            pl.BlockSpec((HALO, width), lambda i: (jnp.maximum(i * step - 1, 0), col_block)),
            pl.BlockSpec((HALO, width), lambda i: (jnp.minimum((i + 1) * step, nblk8 - 1), col_block))]


def _ml_prep_kernel(x_ref, xp_ref, xn_ref, w_ref, o_ref, *, per):
    i = pl.program_id(0)
    x = x_ref[...]
    tm = x.shape[0]
    row = lax.broadcasted_iota(jnp.int32, (tm, 1), 0)
    keep_p = jnp.where(i % per == 0, 0.0, 1.0)
    keep_n = jnp.where(i % per == per - 1, 0.0, 1.0)
    prv = jnp.where(row == 0, xp_ref[HALO - 1:HALO, :] * keep_p, pltpu.roll(x, 1, axis=0))
    nxt = jnp.where(row == tm - 1, xn_ref[0:1, :] * keep_n, pltpu.roll(x, tm - 1, axis=0))
    y = prv * w_ref[0:1, :] + x * w_ref[1:2, :] + nxt * w_ref[2:3, :]
    y = y * _sigmoid(y)
    lane = lax.broadcasted_iota(jnp.int32, (1, GROUP), 1)
    o_ref[...] = (y * jnp.where(lane < GROUP // 2, ATT_HD ** -0.5, 1.0)).astype(BF16)


def _ml_prep(p, conv_w, seq):
    t = p.shape[0]
    tm = _row_tile(seq, 512)
    per = seq // tm
    return pl.pallas_call(
        functools.partial(_ml_prep_kernel, per=per),
        grid=(t // tm,),
        in_specs=_halo_specs(tm, GROUP, C_MLQK // GROUP, t // HALO)
        + [pl.BlockSpec((3, GROUP), lambda i: (0, 0))],
        out_specs=pl.BlockSpec((tm, GROUP), lambda i: (i, 0)),
        out_shape=jax.ShapeDtypeStruct((t, GROUP), BF16),
        compiler_params=_cparams(("parallel",)),
    )(p, p, p, conv_w)


def _mlstm_kernel(*refs, chunk, emit):
    (qkf_ref, vf_ref, gif_ref, gff_ref, qkr_ref, vr_ref, gir_ref, gfr_ref,
     brow_ref, bcol_ref, c0_ref, n0_ref, m0_ref) = refs[:13]
    if emit:
        hf_ref, hr_ref = refs[13:15]
        rest = refs[15:]
    else:
        hf_ref = hr_ref = None
        rest = refs[13:]
    c_out, n_out, m_out, c_s, n_s, m_s = rest
    i = pl.program_id(1)

    @pl.when(i == 0)
    def _():
        c_s[...] = c0_ref[...]
        n_s[...] = n0_ref[...]
        m_s[...] = m0_ref[...]

    row = lax.broadcasted_iota(jnp.int32, (chunk, chunk), 0)
    col = lax.broadcasted_iota(jnp.int32, (chunk, chunk), 1)
    lane = lax.broadcasted_iota(jnp.int32, (chunk, LANES), 1)
    lane1 = lax.broadcasted_iota(jnp.int32, (1, LANES), 1)
    srow = lax.broadcasted_iota(jnp.int32, (LANES, 1), 0)
    dirs = ((qkf_ref, vf_ref, gif_ref, gff_ref, hf_ref), (qkr_ref, vr_ref, gir_ref, gfr_ref, hr_ref))
    for d, (qk_ref, v_ref, gi_ref, gf_ref, h_ref) in enumerate(dirs):
        causal = (col <= row) if d == 0 else (col >= row)
        causal_t = (row <= col) if d == 0 else (row >= col)
        gi = gi_ref[...]
        gf = gf_ref[...]
        ipre = gi + brow_ref[0:1, :]
        fl = _log_sigmoid(gf + brow_ref[1:2, :])
        fcum = _exact_left(jnp.where(causal, 1.0, 0.0).astype(BF16), fl)
        ftot = jnp.sum(fl, axis=0, keepdims=True)
        m_prev = m_s[d, 0:1, :]
        w_log = ftot - fcum + ipre
        m_new = jnp.maximum(ftot + m_prev, jnp.max(w_log, axis=0, keepdims=True))
        cd = jnp.exp(ftot + m_prev - m_new)
        w = jnp.exp(w_log - m_new)
        if emit:
            ipre_t = gi.T[0:8, :] + bcol_ref[:, 0:1]
            fl_t = _log_sigmoid(gf.T[0:8, :] + bcol_ref[:, 1:2])
            fcum_t = _exact_right(fl_t, jnp.where(causal_t, 1.0, 0.0).astype(BF16))
            b_rows = ipre_t - fcum_t
            log_inter = fcum + m_prev
        for pp in range(2):
            qp = qk_ref[:, pp * LANES:(pp + 1) * LANES]
            kp = qk_ref[:, GROUP // 2 + pp * LANES:GROUP // 2 + (pp + 1) * LANES]
            ct = c_s[d, pp]
            n_row = n_s[d, pp, 0:1, :]
            inc = None
            for e in range(2):
                h = 2 * pp + e
                c = 4 * d + h
                sel = (lane < ATT_HD) if e == 0 else (lane >= ATT_HD)
                vh = v_ref[:, h * LANES:(h + 1) * LANES]
                if emit:
                    qm = jnp.where(sel, qp, jnp.zeros_like(qp))
                    fc = fcum[:, c:c + 1]
                    li = log_inter[:, c:c + 1]
                    log_d = jnp.where(causal, fc + b_rows[c:c + 1, :], NEG)
                    m_row = jnp.maximum(li, jnp.max(log_d, axis=-1, keepdims=True))
                    d_mat = jnp.exp(log_d - m_row)
                    inter_w = jnp.exp(li - m_row)
                    s_mat = _dot_nt(qm, kp) * d_mat
                    num = inter_w * _dot(qm, ct.astype(BF16)) + _dot(s_mat.astype(BF16), vh.astype(BF16))
                    den = (inter_w * jnp.sum(qm.astype(F32) * n_row, axis=-1, keepdims=True)
                           + jnp.sum(s_mat, axis=-1, keepdims=True))
                    h_ref[:, h * LANES:(h + 1) * LANES] = num / jnp.maximum(jnp.abs(den), jnp.exp(-m_row))
                km = jnp.where(sel, kp, jnp.zeros_like(kp))
                upd = _dot_tn(km, (w[:, c:c + 1] * vh).astype(BF16))
                inc = upd if inc is None else inc + upd
            c_lo, c_hi = 4 * d + 2 * pp, 4 * d + 2 * pp + 1
            c_s[d, pp] = jnp.where(srow < ATT_HD, cd[:, c_lo:c_lo + 1], cd[:, c_hi:c_hi + 1]) * ct + inc
            w_sel = jnp.where(lane < ATT_HD, w[:, c_lo:c_lo + 1], w[:, c_hi:c_hi + 1])
            cd_sel = jnp.where(lane1 < ATT_HD, cd[:, c_lo:c_lo + 1], cd[:, c_hi:c_hi + 1])
            n_s[d, pp, 0:1, :] = cd_sel * n_row + jnp.sum(w_sel * kp.astype(F32), axis=0, keepdims=True)
        m_s[d, 0:1, :] = m_new

    @pl.when(i == pl.num_programs(1) - 1)
    def _():
        c_out[...] = c_s[...]
        n_out[...] = n_s[...]
        m_out[...] = m_s[...]


def _mlstm_scan(qk, p, brow, bcol, state, bsz, seq, chunk, emit):
    nc = seq // chunk
    t = bsz * seq
    fwd = lambda b, i: b * nc + i
    rev = lambda b, i: b * nc + nc - 1 - i

    def chunk_specs(idx):
        return [pl.BlockSpec((chunk, GROUP), lambda b, i: (idx(b, i), 0)),
                pl.BlockSpec((chunk, GROUP), lambda b, i: (idx(b, i), C_MLV // GROUP)),
                pl.BlockSpec((chunk, LANES), lambda b, i: (idx(b, i), C_MLI // LANES)),
                pl.BlockSpec((chunk, LANES), lambda b, i: (idx(b, i), C_MLF // LANES))]

    st_specs = [pl.BlockSpec((None, 2, 2, LANES, LANES), lambda b, i: (b, 0, 0, 0, 0)),
                pl.BlockSpec((None, 2, 2, 8, LANES), lambda b, i: (b, 0, 0, 0, 0)),
                pl.BlockSpec((None, 2, 8, LANES), lambda b, i: (b, 0, 0, 0))]
    st_shapes = [jax.ShapeDtypeStruct((bsz, 2, 2, LANES, LANES), F32),
                 jax.ShapeDtypeStruct((bsz, 2, 2, 8, LANES), F32),
                 jax.ShapeDtypeStruct((bsz, 2, 8, LANES), F32)]
    h_specs = [pl.BlockSpec((chunk, GROUP), lambda b, i: (fwd(b, i), 0)),
               pl.BlockSpec((chunk, GROUP), lambda b, i: (rev(b, i), 0))] if emit else []
    h_shapes = [jax.ShapeDtypeStruct((t, GROUP), F32)] * 2 if emit else []
    outs = pl.pallas_call(
        functools.partial(_mlstm_kernel, chunk=chunk, emit=emit),
        grid=(bsz, nc),
        in_specs=chunk_specs(fwd) + chunk_specs(rev)
        + [pl.BlockSpec((8, LANES), lambda b, i: (0, 0)), pl.BlockSpec((8, LANES), lambda b, i: (0, 0))]
        + st_specs,
        out_specs=h_specs + st_specs,
        out_shape=h_shapes + st_shapes,
        scratch_shapes=[pltpu.VMEM((2, 2, LANES, LANES), F32), pltpu.VMEM((2, 2, 8, LANES), F32),
                        pltpu.VMEM((2, 8, LANES), F32)],
        compiler_params=_cparams(("parallel", "arbitrary")),
    )(qk, p, p, p, qk, p, p, p, brow, bcol, *state)
    if emit:
        return outs[0], outs[1], tuple(outs[2:])
    return None, None, tuple(outs)


def _hgrn_constants(chunk):
    nlev = int(np.log2(chunk))
    assert 2 ** nlev == chunk
    t = np.arange(chunk)
    mats = [(t[None, :] <= t[:, None]), (t[None, :] > t[:, None])]
    masks = []
    for lev in range(nlev):
        n = chunk >> (lev + 1)
        b0 = t - t % (2 * n)
        upper = (t % (2 * n)) >= n
        m_up = (t[None, :] >= (b0 + n)[:, None]) & (t[None, :] <= t[:, None])
        m_lo = (t[None, :] > t[:, None]) & (t[None, :] <= (b0 + n - 1)[:, None])
        mats.append(np.where(upper[:, None], m_up, m_lo))
        same = (t[:, None] // (2 * n)) == (t[None, :] // (2 * n))
        masks.append(same & upper[:, None] & ~upper[None, :])
    masks.append(t[:, None] == t[None, :])
    mstack = np.concatenate(mats, axis=0).astype(np.float32)
    lmask = np.stack([np.kron(np.eye(4), m.astype(np.float32)) for m in masks])
    nr = mstack.shape[0] // chunk
    mstack_rev = mstack.reshape(nr, chunk, chunk)[:, ::-1, ::-1].reshape(-1, chunk)
    lmask_rev = lmask.reshape(nlev + 1, 4, chunk, 4, chunk)[:, :, ::-1, :, ::-1].reshape(lmask.shape)
    return (jnp.asarray(np.stack([mstack, mstack_rev]), BF16),
            jnp.asarray(np.stack([lmask, lmask_rev]), F32), nlev)


def _stack_heads(x):
    return jnp.concatenate([x[:, h * LANES:(h + 1) * LANES] for h in range(4)], axis=0)


def _hgrn_kernel(*refs, chunk, nlev, emit):
    (qf_ref, vf_ref, ff_ref, qr_ref, vr_ref, fr_ref, lb_ref, ms_ref, lm_ref, s0_ref) = refs[:10]
    if emit:
        of_ref, or_ref = refs[10:12]
        rest = refs[12:]
    else:
        of_ref = or_ref = None
        rest = refs[10:]
    s_out, s_s = rest
    i = pl.program_id(1)

    @pl.when(i == 0)
    def _():
        s_s[...] = s0_ref[...]

    trow = lax.broadcasted_iota(jnp.int32, (chunk, 1), 0)
    dirs = ((qf_ref, vf_ref, ff_ref, of_ref), (qr_ref, vr_ref, fr_ref, or_ref))
    for d, (q_ref, v_ref, f_ref, o_ref) in enumerate(dirs):
        fpre = f_ref[...]
        v = v_ref[...]
        a = lb_ref[d, 0:1, :]
        bb = lb_ref[d, 1:2, :] + _log_sigmoid(fpre)
        logf = jnp.maximum(a, bb) + jnp.log1p(jnp.exp(-jnp.abs(a - bb)))
        k = lb_ref[d, 2:3, :] * _sigmoid(-fpre)
        ex = _exact_left(ms_ref[d], logf)
        gtot = jnp.sum(logf, axis=0, keepdims=True)
        ktil = (k * jnp.exp(ex[chunk:2 * chunk])).astype(BF16)
        vb = v.astype(BF16)
        if emit:
            qpre = q_ref[...]
            q = qpre * _sigmoid(qpre)
            qe = (q * jnp.exp(ex[0:chunk])).astype(BF16)
            tau = trow if d == 0 else (chunk - 1 - trow)
            att = lm_ref[d, nlev] * _dot_nt(_stack_heads(q).astype(BF16), _stack_heads(k).astype(BF16))
            for lev in range(nlev):
                e_l = jnp.exp(ex[(2 + lev) * chunk:(3 + lev) * chunk])
                upper = ((tau >> (nlev - 1 - lev)) & 1) == 1
                q_l = _stack_heads(jnp.where(upper, q * e_l, 0.0)).astype(BF16)
                k_l = _stack_heads(jnp.where(upper, 0.0, k * e_l)).astype(BF16)
                att = att + lm_ref[d, lev] * _dot_nt(q_l, k_l)
            o_all = _dot(att.astype(BF16), _stack_heads(vb))
        for h in range(4):
            cs = slice(h * LANES, (h + 1) * LANES)
            s_t = s_s[d, h]
            if emit:
                o_ref[:, cs] = o_all[h * chunk:(h + 1) * chunk] + _dot_nt(qe[:, cs], s_t.astype(BF16))
            s_s[d, h] = s_t * jnp.exp(gtot[:, cs]) + _dot_tn(vb[:, cs], ktil[:, cs])

    @pl.when(i == pl.num_programs(1) - 1)
    def _():
        s_out[...] = s_s[...]


def _hgrn_scan(p, lbc, consts, state, bsz, seq, chunk, emit):
    mstack, lmask, nlev = consts
    nc = seq // chunk
    t = bsz * seq
    fwd = lambda b, i: b * nc + i
    rev = lambda b, i: b * nc + nc - 1 - i

    def chunk_specs(idx, d):
        return [pl.BlockSpec((chunk, GROUP), lambda b, i: (idx(b, i), C_HGQ // GROUP)),
                pl.BlockSpec((chunk, GROUP), lambda b, i: (idx(b, i), C_HGI // GROUP)),
                pl.BlockSpec((chunk, GROUP), lambda b, i: (idx(b, i), C_HGF // GROUP + d))]

    st_spec = pl.BlockSpec((None, 2, 4, LANES, LANES), lambda b, i: (b, 0, 0, 0, 0))
    st_shape = jax.ShapeDtypeStruct((bsz, 2, 4, LANES, LANES), F32)
    o_specs = [pl.BlockSpec((chunk, GROUP), lambda b, i: (fwd(b, i), 0)),
               pl.BlockSpec((chunk, GROUP), lambda b, i: (rev(b, i), 0))] if emit else []
    o_shapes = [jax.ShapeDtypeStruct((t, GROUP), F32)] * 2 if emit else []
    full = lambda a: pl.BlockSpec(a.shape, lambda b, i: (0,) * a.ndim)
    outs = pl.pallas_call(
        functools.partial(_hgrn_kernel, chunk=chunk, nlev=nlev, emit=emit),
        grid=(bsz, nc),
        in_specs=chunk_specs(fwd, 0) + chunk_specs(rev, 1) + [full(lbc), full(mstack), full(lmask), st_spec],
        out_specs=o_specs + [st_spec],
        out_shape=o_shapes + [st_shape],
        scratch_shapes=[pltpu.VMEM((2, 4, LANES, LANES), F32)],
        compiler_params=_cparams(("parallel", "arbitrary")),
    )(p, p, p, p, p, p, lbc, mstack, lmask, state)
    if emit:
        return outs[0], outs[1], outs[2]
    return None, None, outs[0]


def _outproj_kernel(x_ref, ya_ref, yb_ref, hf_ref, hr_ref, mo_ref, of_ref, or_ref, hg_ref,
                    w_ref, g_ref, o_ref):
    acc = _dot(ya_ref[...], w_ref[0]) + _dot(yb_ref[...], w_ref[1])
    yc = _sigmoid(mo_ref[...]) * (hf_ref[...] + hr_ref[...])
    acc = acc + _dot(yc.astype(BF16), w_ref[2])
    o = of_ref[...] + or_ref[...]
    gate = _sigmoid(hg_ref[...])
    parts = []
    for h in range(4):
        cs = slice(h * LANES, (h + 1) * LANES)
        oh = o[:, cs]
        parts.append(oh * lax.rsqrt(jnp.mean(oh * oh, axis=-1, keepdims=True) + EPS) * gate[:, cs])
    yd = jnp.concatenate(parts, axis=1)
    acc = acc + _dot(yd.astype(BF16), w_ref[3])
    o_ref[...] = x_ref[...] + g_ref[...] * acc


def _out_proj(x, ya, yb, hf, hr, of, orv, p, w, gate, rows_per_mod):
    t, d = x.shape
    tm = _row_tile(rows_per_mod, 512)
    per = rows_per_mod // tm
    grp = lambda cb=0: pl.BlockSpec((tm, GROUP), lambda i: (i, cb))
    return pl.pallas_call(
        _outproj_kernel,
        grid=(t // tm,),
        in_specs=[pl.BlockSpec((tm, d), lambda i: (i, 0)), grp(), grp(), grp(), grp(),
                  grp(C_MLO // GROUP), grp(), grp(), grp(C_HGG // GROUP),
                  pl.BlockSpec((4, GROUP, d), lambda i: (0, 0, 0)),
                  pl.BlockSpec((None, 1, d), lambda i: (i // per, 0, 0))],
        out_specs=pl.BlockSpec((tm, d), lambda i: (i, 0)),
        out_shape=jax.ShapeDtypeStruct((t, d), F32),
        compiler_params=_cparams(("parallel",)),
    )(x, ya, yb, hf, hr, p, of, orv, p, w, gate)


def _ffn_kernel(x_ref, xp_ref, xn_ref, g_ref, sh_ref, sc_ref, gate_ref, wa_ref, wu_ref, wc_ref, wd_ref,
                o_ref, xs_ref, acc_ref, *, per):
    i = pl.program_id(0)
    j = pl.program_id(1)
    tm = x_ref.shape[0]

    @pl.when(j == 0)
    def _():
        nm = lambda x: _norm_mod(x, g_ref[...], sh_ref[...], sc_ref[...])
        keep_p = jnp.where(i % per == 0, 0.0, 1.0)
        keep_n = jnp.where(i % per == per - 1, 0.0, 1.0)
        xs_ref[0:HALO, :] = (nm(xp_ref[...]) * keep_p).astype(BF16)
        xs_ref[HALO:HALO + tm, :] = nm(x_ref[...]).astype(BF16)
        xs_ref[HALO + tm:, :] = (nm(xn_ref[...]) * keep_n).astype(BF16)
        acc_ref[...] = jnp.zeros_like(acc_ref)

    a = _dot(xs_ref[...], wa_ref[...])
    rows = a.shape[0]
    prv = pltpu.roll(a, 1, axis=0)[HALO:HALO + tm]
    nxt = pltpu.roll(a, rows - 1, axis=0)[HALO:HALO + tm]
    conv = prv * wc_ref[0:1, :] + a[HALO:HALO + tm] * wc_ref[1:2, :] + nxt * wc_ref[2:3, :]
    u = _dot(xs_ref[HALO:HALO + tm, :], wu_ref[...])
    act = (conv * _sigmoid(conv) * u).astype(BF16)
    acc_ref[...] += _dot(act, wd_ref[...])

    @pl.when(j == pl.num_programs(1) - 1)
    def _():
        o_ref[...] = x_ref[...] + gate_ref[...] * acc_ref[...]


def _ffn(x, gain, shift, scale, gate, w_up, w_conv, w_down, rows_per_mod):
    t, d = x.shape
    f = w_down.shape[0]
    tm = _row_tile(rows_per_mod, 512)
    tf = 512 if f % 512 == 0 else f
    per = rows_per_mod // tm
    nf = f // tf
    step = tm // HALO
    nblk8 = t // HALO
    mod = pl.BlockSpec((None, 1, d), lambda i, j: (i // per, 0, 0))
    return pl.pallas_call(
        functools.partial(_ffn_kernel, per=per),
        grid=(t // tm, nf),
        in_specs=[pl.BlockSpec((tm, d), lambda i, j: (i, 0)),
                  pl.BlockSpec((HALO, d), lambda i, j: (jnp.maximum(i * step - 1, 0), 0)),
                  pl.BlockSpec((HALO, d), lambda i, j: (jnp.minimum((i + 1) * step, nblk8 - 1), 0)),
                  pl.BlockSpec((1, d), lambda i, j: (0, 0)), mod, mod, mod,
                  pl.BlockSpec((d, tf), lambda i, j: (0, j)),
                  pl.BlockSpec((d, tf), lambda i, j: (0, nf + j)),
                  pl.BlockSpec((3, tf), lambda i, j: (0, j)),
                  pl.BlockSpec((tf, d), lambda i, j: (j, 0))],
        out_specs=pl.BlockSpec((tm, d), lambda i, j: (i, 0)),
        out_shape=jax.ShapeDtypeStruct((t, d), F32),
        scratch_shapes=[pltpu.VMEM((tm + 2 * HALO, d), BF16), pltpu.VMEM((tm, d), F32)],
        compiler_params=_cparams(("parallel", "arbitrary")),
    )(x, x, x, gain, shift, scale, gate, w_up, w_up, w_conv, w_down)


def _perm_w_in(w):
    d = w.shape[0]
    swq = w[:, 1536:2048].reshape(d, 8, ATT_HD)[:, SW_PERM, :].reshape(d, GROUP)
    pad = jnp.zeros((d, LANES - 8), w.dtype)
    cols = [w[:, 0:1536], swq, w[:, 2304:3840], w[:, 3856:6416], w[:, 2048:2304],
            w[:, 3840:3848], pad, w[:, 3848:3856], pad]
    out = jnp.concatenate(cols, axis=1).astype(BF16)
    assert out.shape[1] == NP
    return out


def _perm_w_out(w):
    d = w.shape[1]
    swo = w[GROUP:2 * GROUP].reshape(8, ATT_HD, d)[SW_PERM, :, :].reshape(GROUP, d)
    return jnp.stack([w[0:GROUP], swo, w[2 * GROUP:3 * GROUP], w[3 * GROUP:4 * GROUP]]).astype(BF16)


def _rope_tables(seq):
    t = jnp.arange(seq)
    half = ATT_HD // 2
    inv = ROPE_BASE ** (-jnp.arange(0, half, 2, dtype=F32) / half)
    ang_r = (t // GRID_W).astype(F32)[:, None] * inv
    ang_c = (t % GRID_W).astype(F32)[:, None] * inv
    cr, sr, cc, sc = jnp.cos(ang_r), jnp.sin(ang_r), jnp.cos(ang_c), jnp.sin(ang_c)
    cos = jnp.concatenate([cr, cr, cc, cc] * 2, axis=-1)
    sin = jnp.concatenate([-sr, sr, -sc, sc] * 2, axis=-1)
    return cos, sin


def kernel(x, c, ctx, c_ctx, w_mod, b_mod, norm_mix, norm_ffn, w_in, w_out, na_qk_gain, na_rpb,
           sw_qk_gain, sw_sink, ml_conv, ml_gate_bias, hg_lb, ffn_up, ffn_conv, ffn_down):
    bsz, seq, d = x.shape
    nctx = ctx.shape[1]
    depth = w_mod.shape[0]
    chunk = 64
    assert bsz + 1 <= 8

    c_all = jnp.concatenate([c, c_ctx[None], jnp.zeros((8 - bsz - 1, d), F32)], axis=0)
    mod = _modulation(c_all, w_mod, b_mod).reshape(depth, 8, 6, d)
    cos, sin = _rope_tables(seq)
    bd = jnp.asarray(np.kron(np.eye(GROUP // ATT_HD), np.ones((ATT_HD, ATT_HD))), BF16)
    lb_cum = jnp.cumsum(jax.nn.softmax(hg_lb.astype(F32), axis=0), axis=0)
    lower = lb_cum - lb_cum[:1]
    consts = _hgrn_constants(chunk)

    h = x.reshape(bsz * seq, d)
    hc = ctx.reshape(bsz * nctx, d)
    for l in range(depth):
        emit_ctx = l < depth - 1
        lat = lambda k: mod[l, :bsz, k].reshape(bsz, 1, d)
        cx = lambda k: jnp.broadcast_to(mod[l, bsz, k].reshape(1, 1, d), (bsz, 1, d))
        w_in_p = _perm_w_in(w_in[l])
        gain = norm_mix[l].reshape(1, d)
        p_lat = _in_proj(h, gain, lat(0), lat(1), w_in_p, seq)
        p_ctx = _in_proj(hc, gain, cx(0), cx(1), w_in_p, nctx)

        gna = jnp.tile(na_qk_gain[l].astype(F32), (1, GROUP // ATT_HD))
        gsw = jnp.tile(sw_qk_gain[l].astype(F32), (1, GROUP // ATT_HD))
        naq, nak, nav, swq, swk, swv = _attn_prep(p_lat, gna, gsw, bd, cos, sin, seq, True)
        cnaq, cnak, cnav, cswq, cswk, cswv = _attn_prep(p_ctx, gna, gsw, bd, cos, sin, nctx, False)
        sink = sw_sink[l].astype(F32)
        ya = _na_attention(naq, nak, nav, cnak, cnav, _na_bias_table(na_rpb[l]), bsz, seq, nctx)
        yb = _sw_attention(swq, swk, swv, cswk, cswv, sink, bsz, seq, nctx)

        qk_lat = _ml_prep(p_lat, ml_conv[l].astype(F32), seq)
        qk_ctx = _ml_prep(p_ctx, ml_conv[l].astype(F32), nctx)
        gb = ml_gate_bias[l].astype(F32).reshape(2, 8)
        brow = jnp.zeros((8, LANES), F32).at[0:2, 0:8].set(gb)
        bcol = jnp.zeros((8, LANES), F32).at[:, 0:2].set(gb.T)
        ml0 = (jnp.zeros((bsz, 2, 2, LANES, LANES), F32), jnp.zeros((bsz, 2, 2, 8, LANES), F32),
               jnp.zeros((bsz, 2, 8, LANES), F32))
        hcf, hcr, ml_state = _mlstm_scan(qk_ctx, p_ctx, brow, bcol, ml0, bsz, nctx, chunk, emit_ctx)
        hf, hr, _ = _mlstm_scan(qk_lat, p_lat, brow, bcol, ml_state, bsz, seq, chunk, True)

        lb = lower[l]
        lbc = jnp.stack([jnp.maximum(jnp.log(lb), NEG), jnp.log1p(-lb), 1.0 - lb], axis=1)
        lbc = jnp.concatenate([lbc, jnp.zeros((2, 5, GROUP), F32)], axis=1)
        hg0 = jnp.zeros((bsz, 2, 4, LANES, LANES), F32)
        ocf, ocr, hg_state = _hgrn_scan(p_ctx, lbc, consts, hg0, bsz, nctx, chunk, emit_ctx)
        of, orv, _ = _hgrn_scan(p_lat, lbc, consts, hg_state, bsz, seq, chunk, True)

        w_out_p = _perm_w_out(w_out[l])
        gain2 = norm_ffn[l].reshape(1, d)
        w_up = ffn_up[l].astype(BF16)
        w_dn = ffn_down[l].astype(BF16)
        w_cv = ffn_conv[l].astype(F32)
        h = _out_proj(h, ya, yb, hf, hr, of, orv, p_lat, w_out_p, lat(2), seq)
        h = _ffn(h, gain2, lat(3), lat(4), lat(5), w_up, w_cv, w_dn, seq)
        if emit_ctx:
            yac, ybc = _ctx_attention(cnaq, cnak, cnav, cswq, cswk, cswv, sink, bsz, nctx)
            hc = _out_proj(hc, yac, ybc, hcf, hcr, ocf, ocr, p_ctx, w_out_p, cx(2), nctx)
            hc = _ffn(hc, gain2, cx(3), cx(4), cx(5), w_up, w_cv, w_dn, nctx)
    return h.reshape(bsz, seq, d)
```

```python
import functools
import math

import numpy as np
import jax
import jax.numpy as jnp
from jax import lax
from jax.experimental import pallas as pl
from jax.experimental.pallas import tpu as pltpu

F32 = jnp.float32
BF16 = jnp.bfloat16

GRID_W = 64
ATT_HD = 64
NA_ROWS = 8
NA_COLS = 16
SW_BLOCK = 128
ROPE_BASE = 10000.0
EPS = 1e-6
NEG = -1e30
GROUP = 512
LANES = 128
HALO = 8
VMEM_LIMIT = 56 * 1024 * 1024

C_NAQ, C_NAK, C_NAV, C_SWQ = 0, 512, 1024, 1536
C_MLQK, C_MLV, C_MLO = 2048, 2560, 3072
C_HGQ, C_HGI, C_HGF, C_HGG = 3584, 4096, 4608, 5632
C_SWK, C_SWV = 6144, 6272
C_MLI, C_MLF = 6400, 6528
NP = 6656
ML_CHUNK = 256
HG_CHUNK = 64
SW_PERM = (0, 4, 1, 5, 2, 6, 3, 7)


def _cparams(sem):
    return pltpu.CompilerParams(dimension_semantics=sem, vmem_limit_bytes=VMEM_LIMIT)


def _sigmoid(x):
    return 1.0 / (1.0 + jnp.exp(-x))


def _log_sigmoid(x):
    return jnp.minimum(x, 0.0) - jnp.log1p(jnp.exp(-jnp.abs(x)))


def _split3(x):
    hi = x.astype(BF16)
    r = x - hi.astype(F32)
    mid = r.astype(BF16)
    lo = (r - mid.astype(F32)).astype(BF16)
    return hi, mid, lo


def _dot(a, b):
    return jnp.dot(a, b, preferred_element_type=F32)


def _dot_nt(a, b):
    return lax.dot_general(a, b, (((1,), (1,)), ((), ())), preferred_element_type=F32)


def _dot_tn(a, b):
    return lax.dot_general(a, b, (((0,), (0,)), ((), ())), preferred_element_type=F32)


def _exact_left(m01, x):
    hi, mid, lo = _split3(x)
    return _dot(m01, hi) + _dot(m01, mid) + _dot(m01, lo)


def _exact_right(x, m01):
    hi, mid, lo = _split3(x)
    return _dot(hi, m01) + _dot(mid, m01) + _dot(lo, m01)


def _mod_kernel(c_ref, w_ref, b_ref, o_ref):
    c = c_ref[...]
    a = c * _sigmoid(c)
    a_hi = a.astype(BF16)
    a_lo = (a - a_hi.astype(F32)).astype(BF16)
    w = w_ref[...]
    w_hi = w.astype(BF16)
    w_lo = (w - w_hi.astype(F32)).astype(BF16)
    o_ref[...] = _dot(a_hi, w_hi) + _dot(a_hi, w_lo) + _dot(a_lo, w_hi) + b_ref[...]


def _modulation(c_all, w_mod, b_mod):
    depth, d, n = w_mod.shape
    tn = 1024 if n % 1024 == 0 else n
    return pl.pallas_call(
        _mod_kernel,
        grid=(depth, n // tn),
        in_specs=[pl.BlockSpec((8, d), lambda l, j: (0, 0)),
                  pl.BlockSpec((None, d, tn), lambda l, j: (l, 0, j)),
                  pl.BlockSpec((None, 1, tn), lambda l, j: (l, 0, j))],
        out_specs=pl.BlockSpec((None, 8, tn), lambda l, j: (l, 0, j)),
        out_shape=jax.ShapeDtypeStruct((depth, 8, n), F32),
        compiler_params=_cparams(("parallel", "parallel")),
    )(c_all, w_mod, b_mod.reshape(depth, 1, n))


def _norm_mod(x, gain, shift, scale):
    y = x * lax.rsqrt(jnp.mean(x * x, axis=-1, keepdims=True) + EPS) * gain
    return y * (1.0 + scale) + shift


def _inproj_kernel(x_ref, g_ref, sh_ref, sc_ref, w_ref, o_ref, xn_ref):
    @pl.when(pl.program_id(1) == 0)
    def _():
        xn_ref[...] = _norm_mod(x_ref[...], g_ref[...], sh_ref[...], sc_ref[...]).astype(BF16)

    o_ref[...] = _dot(xn_ref[...], w_ref[...])


def _row_tile(rows_per_mod, pref):
    tm = min(pref, rows_per_mod)
    assert rows_per_mod % tm == 0
    return tm


def _in_proj(x, gain, shift, scale, w, rows_per_mod):
    t, d = x.shape
    n = w.shape[1]
    tm = _row_tile(rows_per_mod, 1024)
    tn = 512 if n % 512 == 0 else n
    per = rows_per_mod // tm
    return pl.pallas_call(
        _inproj_kernel,
        grid=(t // tm, n // tn),
        in_specs=[pl.BlockSpec((tm, d), lambda i, j: (i, 0)),
                  pl.BlockSpec((1, d), lambda i, j: (0, 0)),
                  pl.BlockSpec((None, 1, d), lambda i, j: (i // per, 0, 0)),
                  pl.BlockSpec((None, 1, d), lambda i, j: (i // per, 0, 0)),
                  pl.BlockSpec((d, tn), lambda i, j: (0, j))],
        out_specs=pl.BlockSpec((tm, tn), lambda i, j: (i, j)),
        out_shape=jax.ShapeDtypeStruct((t, n), F32),
        scratch_shapes=[pltpu.VMEM((tm, d), BF16)],
        compiler_params=_cparams(("parallel", "arbitrary")),
    )(x, gain, shift, scale, w)


def _head_rms(x, gain_row, bd):
    xx = x * x
    hi = xx.astype(BF16)
    lo = (xx - hi.astype(F32)).astype(BF16)
    ss = _dot(hi, bd) + _dot(lo, bd)
    return x * lax.rsqrt(ss * (1.0 / ATT_HD) + EPS) * gain_row


def _rope(x, cos, sin, first):
    w = x.shape[-1]
    nxt = pltpu.roll(x, w - 16, axis=1)
    prv = pltpu.roll(x, 16, axis=1)
    return x * cos + jnp.where(first, nxt, prv) * sin


def _attn_prep_kernel(p_ref, pkv_ref, gna_ref, gsw_ref, bd_ref, cos_ref, sin_ref,
                      naq_ref, nak_ref, nav_ref, swq_ref, swk_ref, swv_ref, *, rope):
    bd = bd_ref[...]
    scale = ATT_HD ** -0.5
    naq_ref[...] = (_head_rms(p_ref[:, C_NAQ:C_NAQ + GROUP], gna_ref[0:1, :], bd) * scale).astype(BF16)
    nak_ref[...] = _head_rms(p_ref[:, C_NAK:C_NAK + GROUP], gna_ref[1:2, :], bd).astype(BF16)
    nav_ref[...] = p_ref[:, C_NAV:C_NAV + GROUP].astype(BF16)
    q = _head_rms(p_ref[:, C_SWQ:C_SWQ + GROUP], gsw_ref[0:1, :], bd)
    k = _head_rms(pkv_ref[:, 0:LANES], gsw_ref[1:2, 0:LANES], bd[0:LANES, 0:LANES])
    if rope:
        cos = cos_ref[...]
        sin = sin_ref[...]
        lane = lax.broadcasted_iota(jnp.int32, (1, GROUP), 1)
        first = (lane % 32) < 16
        q = _rope(q, jnp.concatenate([cos] * 4, axis=1), jnp.concatenate([sin] * 4, axis=1), first)
        k = _rope(k, cos, sin, first[:, 0:LANES])
    swq_ref[...] = (q * scale).astype(BF16)
    swk_ref[...] = k.astype(BF16)
    swv_ref[...] = pkv_ref[:, LANES:2 * LANES].astype(BF16)


def _attn_prep(p, gna, gsw, bd, cos, sin, seq, rope):
    t = p.shape[0]
    tm = _row_tile(seq, 512)
    per = seq // tm
    out = lambda w: jax.ShapeDtypeStruct((t, w), BF16)
    ospec = lambda w: pl.BlockSpec((tm, w), lambda i: (i, 0))
    return pl.pallas_call(
        functools.partial(_attn_prep_kernel, rope=rope),
        grid=(t // tm,),
        in_specs=[pl.BlockSpec((tm, C_MLQK), lambda i: (i, 0)),
                  pl.BlockSpec((tm, 2 * LANES), lambda i: (i, C_SWK // (2 * LANES))),
                  pl.BlockSpec((2, GROUP), lambda i: (0, 0)),
                  pl.BlockSpec((2, GROUP), lambda i: (0, 0)),
                  pl.BlockSpec((GROUP, GROUP), lambda i: (0, 0)),
                  pl.BlockSpec((tm, LANES), lambda i: (i % per, 0)),
                  pl.BlockSpec((tm, LANES), lambda i: (i % per, 0))],
        out_specs=[ospec(GROUP), ospec(GROUP), ospec(GROUP), ospec(GROUP), ospec(LANES), ospec(LANES)],
        out_shape=[out(GROUP), out(GROUP), out(GROUP), out(GROUP), out(LANES), out(LANES)],
        compiler_params=_cparams(("parallel",)),
    )(p, p, gna, gsw, bd, cos, sin)


def _pair_rows(q_pair):
    lane = lax.broadcasted_iota(jnp.int32, q_pair.shape, 1)
    zero = jnp.zeros_like(q_pair)
    return jnp.concatenate([jnp.where(lane < ATT_HD, q_pair, zero),
                            jnp.where(lane >= ATT_HD, q_pair, zero)], axis=0)


def _pair_merge(o2, inv_l):
    m = o2.shape[0] // 2
    lane = lax.broadcasted_iota(jnp.int32, (m, LANES), 1)
    return jnp.where(lane < ATT_HD, o2[0:m] * inv_l[0:m], o2[m:] * inv_l[m:])


NA_RPS = 4


def _na_kernel(q_ref, k_ref, v_ref, kc_ref, vc_ref, b_ref, o_ref, *, rows):
    nwin = NA_ROWS * GRID_W
    for rr in range(NA_RPS):
        r = pl.program_id(1) * NA_RPS + rr
        rs = jnp.clip(r - NA_ROWS // 2, 0, rows - NA_ROWS)
        dr0 = rs - r + NA_ROWS - 1
        start = pl.multiple_of(rs * GRID_W, GRID_W)
        qr = slice(rr * GRID_W, (rr + 1) * GRID_W)
        for p in range(GROUP // LANES):
            cs = slice(p * LANES, (p + 1) * LANES)
            qs = _pair_rows(q_ref[qr, cs])
            kw = k_ref[pl.ds(start, nwin), cs]
            vw = v_ref[pl.ds(start, nwin), cs]
            bias = jnp.concatenate([b_ref[dr0, 2 * p], b_ref[dr0, 2 * p + 1]], axis=0)
            s_nb = _dot_nt(qs, kw) + bias
            s_cx = _dot_nt(qs, kc_ref[:, cs])
            m = jnp.maximum(jnp.max(s_nb, axis=-1, keepdims=True), jnp.max(s_cx, axis=-1, keepdims=True))
            p_nb = jnp.exp(s_nb - m)
            p_cx = jnp.exp(s_cx - m)
            l = jnp.sum(p_nb, axis=-1, keepdims=True) + jnp.sum(p_cx, axis=-1, keepdims=True)
            o2 = _dot(p_nb.astype(BF16), vw) + _dot(p_cx.astype(BF16), vc_ref[:, cs])
            o_ref[qr, cs] = _pair_merge(o2, 1.0 / l).astype(BF16)


def _na_attention(q, k, v, kc, vc, bias, bsz, seq, ctx):
    rows = seq // GRID_W
    assert rows >= NA_ROWS and rows % NA_RPS == 0
    steps = rows // NA_RPS
    k3, v3 = k.reshape(bsz, seq, GROUP), v.reshape(bsz, seq, GROUP)
    kc3, vc3 = kc.reshape(bsz, ctx, GROUP), vc.reshape(bsz, ctx, GROUP)
    qspec = pl.BlockSpec((NA_RPS * GRID_W, GROUP), lambda b, r: (b * steps + r, 0))
    return pl.pallas_call(
        functools.partial(_na_kernel, rows=rows),
        grid=(bsz, steps),
        in_specs=[qspec,
                  pl.BlockSpec((None, seq, GROUP), lambda b, r: (b, 0, 0)),
                  pl.BlockSpec((None, seq, GROUP), lambda b, r: (b, 0, 0)),
                  pl.BlockSpec((None, ctx, GROUP), lambda b, r: (b, 0, 0)),
                  pl.BlockSpec((None, ctx, GROUP), lambda b, r: (b, 0, 0)),
                  pl.BlockSpec(bias.shape, lambda b, r: (0, 0, 0, 0))],
        out_specs=qspec,
        out_shape=jax.ShapeDtypeStruct((bsz * seq, GROUP), BF16),
        compiler_params=_cparams(("parallel", "arbitrary")),
    )(q, k3, v3, kc3, vc3, bias)


def _na_bias_table(rpb):
    c = np.arange(GRID_W)
    kc = np.arange(GRID_W)
    cstart = np.clip(c - NA_COLS // 2, 0, GRID_W - NA_COLS)
    ok = (kc[None, :] >= cstart[:, None]) & (kc[None, :] < cstart[:, None] + NA_COLS)
    dc = np.clip(kc[None, :] - c[:, None] + NA_COLS - 1, 0, 2 * NA_COLS - 2)
    t = jnp.where(ok[None, None], rpb.astype(F32)[:, :, dc], NEG)
    t = jnp.transpose(t, (0, 2, 1, 3))
    win = [t[:, :, dr0:dr0 + NA_ROWS, :].reshape(rpb.shape[0], GRID_W, NA_ROWS * GRID_W)
           for dr0 in range(NA_ROWS)]
    return jnp.stack(win)


SW_BPS = 2


def _sw_kernel(sink_ref, q_ref, k_ref, v_ref, kc_ref, vc_ref, o_ref, *, seq):
    nwin = 3 * SW_BLOCK
    kc = kc_ref[...]
    vc = vc_ref[...]
    row = lax.broadcasted_iota(jnp.int32, (2 * SW_BLOCK, nwin), 0)
    col = lax.broadcasted_iota(jnp.int32, (2 * SW_BLOCK, nwin), 1)
    half = lax.broadcasted_iota(jnp.int32, (2 * SW_BLOCK, 1), 0) < SW_BLOCK
    for nn in range(SW_BPS):
        n = pl.program_id(1) * SW_BPS + nn
        start = pl.multiple_of(jnp.clip((n - 1) * SW_BLOCK, 0, seq - nwin), SW_BLOCK)
        kw = k_ref[pl.ds(start, nwin), :]
        vw = v_ref[pl.ds(start, nwin), :]
        rel = (start + col) - (n * SW_BLOCK + row % SW_BLOCK)
        ok = jnp.abs(rel) <= SW_BLOCK
        qr = slice(nn * SW_BLOCK, (nn + 1) * SW_BLOCK)
        for j in range(GROUP // LANES):
            cs = slice(j * LANES, (j + 1) * LANES)
            qs = _pair_rows(q_ref[qr, cs])
            s_bd = jnp.where(ok, _dot_nt(qs, kw), NEG)
            s_cx = _dot_nt(qs, kc)
            sink = jnp.where(half, sink_ref[j], sink_ref[j + 4])
            m = jnp.maximum(jnp.maximum(jnp.max(s_bd, axis=-1, keepdims=True),
                                        jnp.max(s_cx, axis=-1, keepdims=True)), sink)
            p_bd = jnp.exp(s_bd - m)
            p_cx = jnp.exp(s_cx - m)
            l = (jnp.sum(p_bd, axis=-1, keepdims=True) + jnp.sum(p_cx, axis=-1, keepdims=True)
                 + jnp.exp(sink - m))
            o2 = _dot(p_bd.astype(BF16), vw) + _dot(p_cx.astype(BF16), vc)
            o_ref[qr, cs] = _pair_merge(o2, 1.0 / l).astype(BF16)


def _sw_attention(q, k, v, kc, vc, sink, bsz, seq, ctx):
    assert seq % (SW_BPS * SW_BLOCK) == 0 and seq >= 3 * SW_BLOCK
    nb = seq // (SW_BPS * SW_BLOCK)
    k3, v3 = k.reshape(bsz, seq, LANES), v.reshape(bsz, seq, LANES)
    kc3, vc3 = kc.reshape(bsz, ctx, LANES), vc.reshape(bsz, ctx, LANES)
    return pl.pallas_call(
        functools.partial(_sw_kernel, seq=seq),
        grid=(bsz, nb),
        in_specs=[pl.BlockSpec(memory_space=pltpu.SMEM),
                  pl.BlockSpec((SW_BPS * SW_BLOCK, GROUP), lambda b, n: (b * nb + n, 0)),
                  pl.BlockSpec((None, seq, LANES), lambda b, n: (b, 0, 0)),
                  pl.BlockSpec((None, seq, LANES), lambda b, n: (b, 0, 0)),
                  pl.BlockSpec((None, ctx, LANES), lambda b, n: (b, 0, 0)),
                  pl.BlockSpec((None, ctx, LANES), lambda b, n: (b, 0, 0))],
        out_specs=pl.BlockSpec((SW_BPS * SW_BLOCK, GROUP), lambda b, n: (b * nb + n, 0)),
        out_shape=jax.ShapeDtypeStruct((bsz * seq, GROUP), BF16),
        compiler_params=_cparams(("parallel", "arbitrary")),
    )(sink, q, k3, v3, kc3, vc3)


def _ctx_attn_kernel(sink_ref, qa_ref, ka_ref, va_ref, qb_ref, kb_ref, vb_ref, oa_ref, ob_ref):
    for p in range(GROUP // LANES):
        cs = slice(p * LANES, (p + 1) * LANES)
        qs = _pair_rows(qa_ref[:, cs])
        s = _dot_nt(qs, ka_ref[:, cs])
        m = jnp.max(s, axis=-1, keepdims=True)
        e = jnp.exp(s - m)
        l = jnp.sum(e, axis=-1, keepdims=True)
        oa_ref[:, cs] = _pair_merge(_dot(e.astype(BF16), va_ref[:, cs]), 1.0 / l).astype(BF16)
    ctx = qb_ref.shape[0]
    half = lax.broadcasted_iota(jnp.int32, (2 * ctx, 1), 0) < ctx
    for j in range(GROUP // LANES):
        cs = slice(j * LANES, (j + 1) * LANES)
        qs = _pair_rows(qb_ref[:, cs])
        s = _dot_nt(qs, kb_ref[...])
        sink = jnp.where(half, sink_ref[j], sink_ref[j + 4])
        m = jnp.maximum(jnp.max(s, axis=-1, keepdims=True), sink)
        e = jnp.exp(s - m)
        l = jnp.sum(e, axis=-1, keepdims=True) + jnp.exp(sink - m)
        ob_ref[:, cs] = _pair_merge(_dot(e.astype(BF16), vb_ref[...]), 1.0 / l).astype(BF16)


def _ctx_attention(qa, ka, va, qb, kb, vb, sink, bsz, ctx):
    big = pl.BlockSpec((ctx, GROUP), lambda b: (b, 0))
    small = pl.BlockSpec((ctx, LANES), lambda b: (b, 0))
    return pl.pallas_call(
        _ctx_attn_kernel,
        grid=(bsz,),
        in_specs=[pl.BlockSpec(memory_space=pltpu.SMEM), big, big, big, big, small, small],
        out_specs=[big, big],
        out_shape=[jax.ShapeDtypeStruct((bsz * ctx, GROUP), BF16)] * 2,
        compiler_params=_cparams(("parallel",)),
    )(sink, qa, ka, va, qb, kb, vb)


def _halo_specs(tm, width, col_block, nblk8):
    step = tm // HALO
    return [pl.BlockSpec((tm, width), lambda i: (i, col_block)),
            pl.BlockSpec((HALO, width), lambda i: (jnp.maximum(i * step - 1, 0), col_block)),
            pl.BlockSpec((HALO, width), lambda i: (jnp.minimum((i + 1) * step, nblk8 - 1), col_block))]


def _ml_prep_kernel(x_ref, xp_ref, xn_ref, w_ref, o_ref, *, per):
    i = pl.program_id(0)
    x = x_ref[...]
    tm = x.shape[0]
    row = lax.broadcasted_iota(jnp.int32, (tm, 1), 0)
    keep_p = jnp.where(i % per == 0, 0.0, 1.0)
    keep_n = jnp.where(i % per == per - 1, 0.0, 1.0)
    prv = jnp.where(row == 0, xp_ref[HALO - 1:HALO, :] * keep_p, pltpu.roll(x, 1, axis=0))
    nxt = jnp.where(row == tm - 1, xn_ref[0:1, :] * keep_n, pltpu.roll(x, tm - 1, axis=0))
    y = prv * w_ref[0:1, :] + x * w_ref[1:2, :] + nxt * w_ref[2:3, :]
    y = y * _sigmoid(y)
    lane = lax.broadcasted_iota(jnp.int32, (1, GROUP), 1)
    o_ref[...] = (y * jnp.where(lane < GROUP // 2, ATT_HD ** -0.5, 1.0)).astype(BF16)


def _ml_prep(p, conv_w, seq):
    t = p.shape[0]
    tm = _row_tile(seq, 512)
    per = seq // tm
    return pl.pallas_call(
        functools.partial(_ml_prep_kernel, per=per),
        grid=(t // tm,),
        in_specs=_halo_specs(tm, GROUP, C_MLQK // GROUP, t // HALO)
        + [pl.BlockSpec((3, GROUP), lambda i: (0, 0))],
        out_specs=pl.BlockSpec((tm, GROUP), lambda i: (i, 0)),
        out_shape=jax.ShapeDtypeStruct((t, GROUP), BF16),
        compiler_params=_cparams(("parallel",)),
    )(p, p, p, conv_w)


def _mlstm_kernel(*refs, chunk, emit):
    (qkf_ref, vf_ref, gif_ref, gff_ref, qkr_ref, vr_ref, gir_ref, gfr_ref,
     brow_ref, bcol_ref, c0_ref, n0_ref, m0_ref) = refs[:13]
    if emit:
        hf_ref, hr_ref = refs[13:15]
        rest = refs[15:]
    else:
        hf_ref = hr_ref = None
        rest = refs[13:]
    c_out, n_out, m_out, c_s, n_s, m_s = rest
    i = pl.program_id(1)

    @pl.when(i == 0)
    def _():
        c_s[...] = c0_ref[...]
        n_s[...] = n0_ref[...]
        m_s[...] = m0_ref[...]

    row = lax.broadcasted_iota(jnp.int32, (chunk, chunk), 0)
    col = lax.broadcasted_iota(jnp.int32, (chunk, chunk), 1)
    lane = lax.broadcasted_iota(jnp.int32, (chunk, LANES), 1)
    lane1 = lax.broadcasted_iota(jnp.int32, (1, LANES), 1)
    srow = lax.broadcasted_iota(jnp.int32, (LANES, 1), 0)
    dirs = ((qkf_ref, vf_ref, gif_ref, gff_ref, hf_ref), (qkr_ref, vr_ref, gir_ref, gfr_ref, hr_ref))
    chains = [(b, d) for b in range(qkf_ref.shape[0]) for d in range(2)]
    old_c = {(b, d, pp): c_s[b, d, pp] for b, d in chains for pp in range(2)}
    old_n = {(b, d, pp): n_s[b, d, pp, 0:1, :] for b, d in chains for pp in range(2)}
    old_m = {(b, d): m_s[b, d, 0:1, :] for b, d in chains}
    new_c, new_n, new_m = {}, {}, {}
    for b, d in chains:
        qk_ref, v_ref, gi_ref, gf_ref, h_ref = [r if r is None else r.at[b] for r in dirs[d]]
        causal = (col <= row) if d == 0 else (col >= row)
        causal_t = (row <= col) if d == 0 else (row >= col)
        gi = gi_ref[...]
        gf = gf_ref[...]
        ipre = gi + brow_ref[0:1, :]
        fl = _log_sigmoid(gf + brow_ref[1:2, :])
        fcum = _exact_left(jnp.where(causal, 1.0, 0.0).astype(BF16), fl)
        ftot = jnp.sum(fl, axis=0, keepdims=True)
        m_prev = old_m[b, d]
        w_log = ftot - fcum + ipre
        m_new = jnp.maximum(ftot + m_prev, jnp.max(w_log, axis=0, keepdims=True))
        cd = jnp.exp(ftot + m_prev - m_new)
        w = jnp.exp(w_log - m_new)
        if emit:
            ipre_t = gi.T[0:8, :] + bcol_ref[:, 0:1]
            fl_t = _log_sigmoid(gf.T[0:8, :] + bcol_ref[:, 1:2])
            fcum_t = _exact_right(fl_t, jnp.where(causal_t, 1.0, 0.0).astype(BF16))
            b_rows = ipre_t - fcum_t
            log_inter = fcum + m_prev
        for pp in range(2):
            qp = qk_ref[:, pp * LANES:(pp + 1) * LANES]
            kp = qk_ref[:, GROUP // 2 + pp * LANES:GROUP // 2 + (pp + 1) * LANES]
            ct = old_c[b, d, pp]
            n_row = old_n[b, d, pp]
            inc = None
            for e in range(2):
                h = 2 * pp + e
                c = 4 * d + h
                sel = (lane < ATT_HD) if e == 0 else (lane >= ATT_HD)
                vh = v_ref[:, h * LANES:(h + 1) * LANES]
                if emit:
                    qm = jnp.where(sel, qp, jnp.zeros_like(qp))
                    fc = fcum[:, c:c + 1]
                    li = log_inter[:, c:c + 1]
                    log_d = jnp.where(causal, fc + b_rows[c:c + 1, :], NEG)
                    m_row = jnp.maximum(li, jnp.max(log_d, axis=-1, keepdims=True))
                    d_mat = jnp.exp(log_d - m_row)
                    inter_w = jnp.exp(li - m_row)
                    s_mat = _dot_nt(qm, kp) * d_mat
                    num = inter_w * _dot(qm, ct.astype(BF16)) + _dot(s_mat.astype(BF16), vh.astype(BF16))
                    den = (inter_w * jnp.sum(qm.astype(F32) * n_row, axis=-1, keepdims=True)
                           + jnp.sum(s_mat, axis=-1, keepdims=True))
                    h_ref[:, h * LANES:(h + 1) * LANES] = num / jnp.maximum(jnp.abs(den), jnp.exp(-m_row))
                km = jnp.where(sel, kp, jnp.zeros_like(kp))
                upd = _dot_tn(km, (w[:, c:c + 1] * vh).astype(BF16))
                inc = upd if inc is None else inc + upd
            c_lo, c_hi = 4 * d + 2 * pp, 4 * d + 2 * pp + 1
            new_c[b, d, pp] = (jnp.where(srow < ATT_HD, cd[:, c_lo:c_lo + 1], cd[:, c_hi:c_hi + 1]) * ct
                               + inc)
            w_sel = jnp.where(lane < ATT_HD, w[:, c_lo:c_lo + 1], w[:, c_hi:c_hi + 1])
            cd_sel = jnp.where(lane1 < ATT_HD, cd[:, c_lo:c_lo + 1], cd[:, c_hi:c_hi + 1])
            new_n[b, d, pp] = cd_sel * n_row + jnp.sum(w_sel * kp.astype(F32), axis=0, keepdims=True)
        new_m[b, d] = m_new
    for (b, d, pp), val in new_c.items():
        c_s[b, d, pp] = val
        n_s[b, d, pp, 0:1, :] = new_n[b, d, pp]
    for (b, d), val in new_m.items():
        m_s[b, d, 0:1, :] = val

    @pl.when(i == pl.num_programs(1) - 1)
    def _():
        c_out[...] = c_s[...]
        n_out[...] = n_s[...]
        m_out[...] = m_s[...]


def _mlstm_scan(qk, p, brow, bcol, state, bsz, seq, chunk, emit):
    nc = seq // chunk
    bps = bsz
    fwd = lambda g, i: i
    rev = lambda g, i: nc - 1 - i
    qk3 = qk.reshape(bsz, seq, GROUP)
    p3 = p.reshape(bsz, seq, NP)

    def chunk_specs(idx):
        return [pl.BlockSpec((bps, chunk, GROUP), lambda g, i: (g, idx(g, i), 0)),
                pl.BlockSpec((bps, chunk, GROUP), lambda g, i: (g, idx(g, i), C_MLV // GROUP)),
                pl.BlockSpec((bps, chunk, LANES), lambda g, i: (g, idx(g, i), C_MLI // LANES)),
                pl.BlockSpec((bps, chunk, LANES), lambda g, i: (g, idx(g, i), C_MLF // LANES))]

    st_dims = [(2, 2, LANES, LANES), (2, 2, 8, LANES), (2, 8, LANES)]
    st_specs = [pl.BlockSpec((bps,) + s, lambda g, i, n=len(s): (g,) + (0,) * n) for s in st_dims]
    st_shapes = [jax.ShapeDtypeStruct((bsz,) + s, F32) for s in st_dims]
    h_specs = [pl.BlockSpec((bps, chunk, GROUP), lambda g, i: (g, fwd(g, i), 0)),
               pl.BlockSpec((bps, chunk, GROUP), lambda g, i: (g, rev(g, i), 0))] if emit else []
    h_shapes = [jax.ShapeDtypeStruct((bsz, seq, GROUP), F32)] * 2 if emit else []
    outs = pl.pallas_call(
        functools.partial(_mlstm_kernel, chunk=chunk, emit=emit),
        grid=(bsz // bps, nc),
        in_specs=chunk_specs(fwd) + chunk_specs(rev)
        + [pl.BlockSpec((8, LANES), lambda g, i: (0, 0)), pl.BlockSpec((8, LANES), lambda g, i: (0, 0))]
        + st_specs,
        out_specs=h_specs + st_specs,
        out_shape=h_shapes + st_shapes,
        scratch_shapes=[pltpu.VMEM((bps,) + s, F32) for s in st_dims],
        compiler_params=_cparams(("parallel", "arbitrary")),
    )(qk3, p3, p3, p3, qk3, p3, p3, p3, brow, bcol, *state)
    if emit:
        return (outs[0].reshape(bsz * seq, GROUP), outs[1].reshape(bsz * seq, GROUP), tuple(outs[2:]))
    return None, None, tuple(outs)


def _hgrn_constants(chunk):
    nlev = int(np.log2(chunk))
    assert 2 ** nlev == chunk
    t = np.arange(chunk)
    mats = [(t[None, :] <= t[:, None]), (t[None, :] > t[:, None])]
    masks = []
    for lev in range(nlev):
        n = chunk >> (lev + 1)
        b0 = t - t % (2 * n)
        upper = (t % (2 * n)) >= n
        m_up = (t[None, :] >= (b0 + n)[:, None]) & (t[None, :] <= t[:, None])
        m_lo = (t[None, :] > t[:, None]) & (t[None, :] <= (b0 + n - 1)[:, None])
        mats.append(np.where(upper[:, None], m_up, m_lo))
        same = (t[:, None] // (2 * n)) == (t[None, :] // (2 * n))
        masks.append(same & upper[:, None] & ~upper[None, :])
    masks.append(t[:, None] == t[None, :])
    mstack = np.concatenate(mats, axis=0).astype(np.float32)
    lmask = np.stack([np.kron(np.eye(2), m.astype(np.float32)) for m in masks])
    nr = mstack.shape[0] // chunk
    mstack_rev = mstack.reshape(nr, chunk, chunk)[:, ::-1, ::-1].reshape(-1, chunk)
    lmask_rev = lmask.reshape(nlev + 1, 2, chunk, 2, chunk)[:, :, ::-1, :, ::-1].reshape(lmask.shape)
    return (jnp.asarray(np.stack([mstack, mstack_rev]), BF16),
            jnp.asarray(np.stack([lmask, lmask_rev]), F32), nlev)


def _stack_pair(x, pp):
    return jnp.concatenate([x[:, (2 * pp) * LANES:(2 * pp + 1) * LANES],
                            x[:, (2 * pp + 1) * LANES:(2 * pp + 2) * LANES]], axis=0)


def _hgrn_kernel(*refs, chunk, nlev, emit):
    (qf_ref, vf_ref, ff_ref, qr_ref, vr_ref, fr_ref, lb_ref, ms_ref, lm_ref, s0_ref) = refs[:10]
    if emit:
        of_ref, or_ref = refs[10:12]
        rest = refs[12:]
    else:
        of_ref = or_ref = None
        rest = refs[10:]
    s_out, s_s = rest
    i = pl.program_id(1)

    @pl.when(i == 0)
    def _():
        s_s[...] = s0_ref[...]

    dirs = ((qf_ref, vf_ref, ff_ref, of_ref), (qr_ref, vr_ref, fr_ref, or_ref))
    chains = [(b, d) for b in range(vf_ref.shape[0]) for d in range(2)]
    old_s = {(b, d, h): s_s[b, d, h] for b, d in chains for h in range(4)}
    new_s = {}
    for b, d in chains:
        q_ref, v_ref, f_ref, o_ref = [r if r is None else r.at[b] for r in dirs[d]]
        fpre = f_ref[...]
        vb = v_ref[...].astype(BF16)
        a = lb_ref[d, 0:1, :]
        bb = lb_ref[d, 1:2, :] + _log_sigmoid(fpre)
        logf = jnp.maximum(a, bb) + jnp.log1p(jnp.exp(-jnp.abs(a - bb)))
        k = lb_ref[d, 2:3, :] * _sigmoid(-fpre)
        hi = logf.astype(BF16)
        lo = (logf - hi.astype(F32)).astype(BF16)
        ms = ms_ref[d]
        ex = _dot(ms, hi) + _dot(ms, lo)
        gtot = jnp.sum(logf, axis=0, keepdims=True)
        ktil = (k * jnp.exp(ex[chunk:2 * chunk])).astype(BF16)
        if emit:
            qpre = q_ref[...]
            q = qpre * _sigmoid(qpre)
            qe = (q * jnp.exp(ex[0:chunk])).astype(BF16)
            qb = q.astype(BF16)
            kb = k.astype(BF16)
            att = [lm_ref[d, nlev] * _dot_nt(_stack_pair(qb, pp), _stack_pair(kb, pp)) for pp in range(2)]
            for lev in range(nlev):
                e_l = jnp.exp(ex[(2 + lev) * chunk:(3 + lev) * chunk])
                q_l = (q * e_l).astype(BF16)
                k_l = (k * e_l).astype(BF16)
                for pp in range(2):
                    att[pp] = att[pp] + lm_ref[d, lev] * _dot_nt(_stack_pair(q_l, pp), _stack_pair(k_l, pp))
            o_pair = [_dot(att[pp].astype(BF16), _stack_pair(vb, pp)) for pp in range(2)]
        for h in range(4):
            cs = slice(h * LANES, (h + 1) * LANES)
            s_t = old_s[b, d, h]
            if emit:
                pp, e = divmod(h, 2)
                o_ref[:, cs] = o_pair[pp][e * chunk:(e + 1) * chunk] + _dot_nt(qe[:, cs], s_t.astype(BF16))
            new_s[b, d, h] = s_t * jnp.exp(gtot[:, cs]) + _dot_tn(vb[:, cs], ktil[:, cs])
    for key, val in new_s.items():
        s_s[key] = val

    @pl.when(i == pl.num_programs(1) - 1)
    def _():
        s_out[...] = s_s[...]


def _hgrn_scan(p, lbc, consts, state, bsz, seq, chunk, emit):
    mstack, lmask, nlev = consts
    nc = seq // chunk
    bps = bsz
    fwd = lambda g, i: i
    rev = lambda g, i: nc - 1 - i
    p3 = p.reshape(bsz, seq, NP)

    def chunk_specs(idx, d):
        return [pl.BlockSpec((bps, chunk, GROUP), lambda g, i: (g, idx(g, i), C_HGQ // GROUP)),
                pl.BlockSpec((bps, chunk, GROUP), lambda g, i: (g, idx(g, i), C_HGI // GROUP)),
                pl.BlockSpec((bps, chunk, GROUP), lambda g, i: (g, idx(g, i), C_HGF // GROUP + d))]

    st_spec = pl.BlockSpec((bps, 2, 4, LANES, LANES), lambda g, i: (g, 0, 0, 0, 0))
    st_shape = jax.ShapeDtypeStruct((bsz, 2, 4, LANES, LANES), F32)
    o_specs = [pl.BlockSpec((bps, chunk, GROUP), lambda g, i: (g, fwd(g, i), 0)),
               pl.BlockSpec((bps, chunk, GROUP), lambda g, i: (g, rev(g, i), 0))] if emit else []
    o_shapes = [jax.ShapeDtypeStruct((bsz, seq, GROUP), F32)] * 2 if emit else []
    full = lambda a: pl.BlockSpec(a.shape, lambda g, i: (0,) * a.ndim)
    outs = pl.pallas_call(
        functools.partial(_hgrn_kernel, chunk=chunk, nlev=nlev, emit=emit),
        grid=(bsz // bps, nc),
        in_specs=chunk_specs(fwd, 0) + chunk_specs(rev, 1) + [full(lbc), full(mstack), full(lmask), st_spec],
        out_specs=o_specs + [st_spec],
        out_shape=o_shapes + [st_shape],
        scratch_shapes=[pltpu.VMEM((bps, 2, 4, LANES, LANES), F32)],
        compiler_params=_cparams(("parallel", "arbitrary")),
    )(p3, p3, p3, p3, p3, p3, lbc, mstack, lmask, state)
    if emit:
        return outs[0].reshape(bsz * seq, GROUP), outs[1].reshape(bsz * seq, GROUP), outs[2]
    return None, None, outs[0]


def _outproj_kernel(x_ref, ya_ref, yb_ref, hf_ref, hr_ref, mo_ref, of_ref, or_ref, hg_ref,
                    w_ref, g_ref, o_ref):
    acc = _dot(ya_ref[...], w_ref[0]) + _dot(yb_ref[...], w_ref[1])
    yc = _sigmoid(mo_ref[...]) * (hf_ref[...] + hr_ref[...])
    acc = acc + _dot(yc.astype(BF16), w_ref[2])
    o = of_ref[...] + or_ref[...]
    gate = _sigmoid(hg_ref[...])
    parts = []
    for h in range(4):
        cs = slice(h * LANES, (h + 1) * LANES)
        oh = o[:, cs]
        parts.append(oh * lax.rsqrt(jnp.mean(oh * oh, axis=-1, keepdims=True) + EPS) * gate[:, cs])
    yd = jnp.concatenate(parts, axis=1)
    acc = acc + _dot(yd.astype(BF16), w_ref[3])
    o_ref[...] = x_ref[...] + g_ref[...] * acc


def _out_proj(x, ya, yb, hf, hr, of, orv, p, w, gate, rows_per_mod):
    t, d = x.shape
    tm = _row_tile(rows_per_mod, 512)
    per = rows_per_mod // tm
    grp = lambda cb=0: pl.BlockSpec((tm, GROUP), lambda i: (i, cb))
    return pl.pallas_call(
        _outproj_kernel,
        grid=(t // tm,),
        in_specs=[pl.BlockSpec((tm, d), lambda i: (i, 0)), grp(), grp(), grp(), grp(),
                  grp(C_MLO // GROUP), grp(), grp(), grp(C_HGG // GROUP),
                  pl.BlockSpec((4, GROUP, d), lambda i: (0, 0, 0)),
                  pl.BlockSpec((None, 1, d), lambda i: (i // per, 0, 0))],
        out_specs=pl.BlockSpec((tm, d), lambda i: (i, 0)),
        out_shape=jax.ShapeDtypeStruct((t, d), F32),
        compiler_params=_cparams(("parallel",)),
    )(x, ya, yb, hf, hr, p, of, orv, p, w, gate)


def _ffn_kernel(x_ref, xp_ref, xn_ref, g_ref, sh_ref, sc_ref, gate_ref, wa_ref, wu_ref, wc_ref, wd_ref,
                o_ref, xs_ref, acc_ref, *, per):
    i = pl.program_id(0)
    j = pl.program_id(1)
    tm = x_ref.shape[0]

    @pl.when(j == 0)
    def _():
        nm = lambda x: _norm_mod(x, g_ref[...], sh_ref[...], sc_ref[...])
        keep_p = jnp.where(i % per == 0, 0.0, 1.0)
        keep_n = jnp.where(i % per == per - 1, 0.0, 1.0)
        xs_ref[0:HALO, :] = (nm(xp_ref[...]) * keep_p).astype(BF16)
        xs_ref[HALO:HALO + tm, :] = nm(x_ref[...]).astype(BF16)
        xs_ref[HALO + tm:, :] = (nm(xn_ref[...]) * keep_n).astype(BF16)
        acc_ref[...] = jnp.zeros_like(acc_ref)

    a = _dot(xs_ref[...], wa_ref[...])
    rows = a.shape[0]
    prv = pltpu.roll(a, 1, axis=0)[HALO:HALO + tm]
    nxt = pltpu.roll(a, rows - 1, axis=0)[HALO:HALO + tm]
    conv = prv * wc_ref[0:1, :] + a[HALO:HALO + tm] * wc_ref[1:2, :] + nxt * wc_ref[2:3, :]
    u = _dot(xs_ref[HALO:HALO + tm, :], wu_ref[...])
    act = (conv * _sigmoid(conv) * u).astype(BF16)
    acc_ref[...] += _dot(act, wd_ref[...])

    @pl.when(j == pl.num_programs(1) - 1)
    def _():
        o_ref[...] = x_ref[...] + gate_ref[...] * acc_ref[...]


def _ffn(x, gain, shift, scale, gate, w_up, w_conv, w_down, rows_per_mod):
    t, d = x.shape
    f = w_down.shape[0]
    tm = _row_tile(rows_per_mod, 512)
    tf = 512 if f % 512 == 0 else f
    per = rows_per_mod // tm
    nf = f // tf
    step = tm // HALO
    nblk8 = t // HALO
    mod = pl.BlockSpec((None, 1, d), lambda i, j: (i // per, 0, 0))
    return pl.pallas_call(
        functools.partial(_ffn_kernel, per=per),
        grid=(t // tm, nf),
        in_specs=[pl.BlockSpec((tm, d), lambda i, j: (i, 0)),
                  pl.BlockSpec((HALO, d), lambda i, j: (jnp.maximum(i * step - 1, 0), 0)),
                  pl.BlockSpec((HALO, d), lambda i, j: (jnp.minimum((i + 1) * step, nblk8 - 1), 0)),
                  pl.BlockSpec((1, d), lambda i, j: (0, 0)), mod, mod, mod,
                  pl.BlockSpec((d, tf), lambda i, j: (0, j)),
                  pl.BlockSpec((d, tf), lambda i, j: (0, nf + j)),
                  pl.BlockSpec((3, tf), lambda i, j: (0, j)),
                  pl.BlockSpec((tf, d), lambda i, j: (j, 0))],
        out_specs=pl.BlockSpec((tm, d), lambda i, j: (i, 0)),
        out_shape=jax.ShapeDtypeStruct((t, d), F32),
        scratch_shapes=[pltpu.VMEM((tm + 2 * HALO, d), BF16), pltpu.VMEM((tm, d), F32)],
        compiler_params=_cparams(("parallel", "arbitrary")),
    )(x, x, x, gain, shift, scale, gate, w_up, w_up, w_conv, w_down)


def _perm_w_in(w):
    d = w.shape[0]
    swq = w[:, 1536:2048].reshape(d, 2, 4, ATT_HD).transpose(0, 2, 1, 3).reshape(d, GROUP)
    pad = jnp.zeros((d, LANES - 8), w.dtype)
    cols = [w[:, 0:1536], swq, w[:, 2304:3840], w[:, 3856:6416], w[:, 2048:2304],
            w[:, 3840:3848], pad, w[:, 3848:3856], pad]
    out = jnp.concatenate(cols, axis=1).astype(BF16)
    assert out.shape[1] == NP
    return out


def _perm_w_out(w):
    d = w.shape[1]
    swo = w[GROUP:2 * GROUP].reshape(2, 4, ATT_HD, d).transpose(1, 0, 2, 3).reshape(GROUP, d)
    return jnp.stack([w[0:GROUP], swo, w[2 * GROUP:3 * GROUP], w[3 * GROUP:4 * GROUP]]).astype(BF16)


def _rope_tables(seq):
    t = jnp.arange(seq)
    half = ATT_HD // 2
    inv = ROPE_BASE ** (-jnp.arange(0, half, 2, dtype=F32) / half)
    ang_r = (t // GRID_W).astype(F32)[:, None] * inv
    ang_c = (t % GRID_W).astype(F32)[:, None] * inv
    cr, sr, cc, sc = jnp.cos(ang_r), jnp.sin(ang_r), jnp.cos(ang_c), jnp.sin(ang_c)
    cos = jnp.concatenate([cr, cr, cc, cc] * 2, axis=-1)
    sin = jnp.concatenate([-sr, sr, -sc, sc] * 2, axis=-1)
    return cos, sin


def kernel(x, c, ctx, c_ctx, w_mod, b_mod, norm_mix, norm_ffn, w_in, w_out, na_qk_gain, na_rpb,
           sw_qk_gain, sw_sink, ml_conv, ml_gate_bias, hg_lb, ffn_up, ffn_conv, ffn_down):
    bsz, seq, d = x.shape
    nctx = ctx.shape[1]
    depth = w_mod.shape[0]
    ml_chunk = math.gcd(ML_CHUNK, math.gcd(seq, nctx))
    chunk = math.gcd(HG_CHUNK, math.gcd(seq, nctx))
    assert bsz + 1 <= 8

    c_all = jnp.concatenate([c, c_ctx[None], jnp.zeros((8 - bsz - 1, d), F32)], axis=0)
    mod = _modulation(c_all, w_mod, b_mod).reshape(depth, 8, 6, d)
    cos, sin = _rope_tables(seq)
    bd = jnp.asarray(np.kron(np.eye(GROUP // ATT_HD), np.ones((ATT_HD, ATT_HD))), BF16)
    lb_cum = jnp.cumsum(jax.nn.softmax(hg_lb.astype(F32), axis=0), axis=0)
    lower = lb_cum - lb_cum[:1]
    consts = _hgrn_constants(chunk)

    h = x.reshape(bsz * seq, d)
    hc = ctx.reshape(bsz * nctx, d)
    for l in range(depth):
        emit_ctx = l < depth - 1
        lat = lambda k: mod[l, :bsz, k].reshape(bsz, 1, d)
        cx = lambda k: jnp.broadcast_to(mod[l, bsz, k].reshape(1, 1, d), (bsz, 1, d))
        w_in_p = _perm_w_in(w_in[l])
        gain = norm_mix[l].reshape(1, d)
        p_lat = _in_proj(h, gain, lat(0), lat(1), w_in_p, seq)
        p_ctx = _in_proj(hc, gain, cx(0), cx(1), w_in_p, nctx)

        gna = jnp.tile(na_qk_gain[l].astype(F32), (1, GROUP // ATT_HD))
        gsw = jnp.tile(sw_qk_gain[l].astype(F32), (1, GROUP // ATT_HD))
        naq, nak, nav, swq, swk, swv = _attn_prep(p_lat, gna, gsw, bd, cos, sin, seq, True)
        cnaq, cnak, cnav, cswq, cswk, cswv = _attn_prep(p_ctx, gna, gsw, bd, cos, sin, nctx, False)
        sink = sw_sink[l].astype(F32)
        ya = _na_attention(naq, nak, nav, cnak, cnav, _na_bias_table(na_rpb[l]), bsz, seq, nctx)
        yb = _sw_attention(swq, swk, swv, cswk, cswv, sink, bsz, seq, nctx)

        qk_lat = _ml_prep(p_lat, ml_conv[l].astype(F32), seq)
        qk_ctx = _ml_prep(p_ctx, ml_conv[l].astype(F32), nctx)
        gb = ml_gate_bias[l].astype(F32).reshape(2, 8)
        brow = jnp.zeros((8, LANES), F32).at[0:2, 0:8].set(gb)
        bcol = jnp.zeros((8, LANES), F32).at[:, 0:2].set(gb.T)
        ml0 = (jnp.zeros((bsz, 2, 2, LANES, LANES), F32), jnp.zeros((bsz, 2, 2, 8, LANES), F32),
               jnp.zeros((bsz, 2, 8, LANES), F32))
        hcf, hcr, ml_state = _mlstm_scan(qk_ctx, p_ctx, brow, bcol, ml0, bsz, nctx, ml_chunk, emit_ctx)
        hf, hr, _ = _mlstm_scan(qk_lat, p_lat, brow, bcol, ml_state, bsz, seq, ml_chunk, True)

        lb = lower[l]
        lbc = jnp.stack([jnp.maximum(jnp.log(lb), NEG), jnp.log1p(-lb), 1.0 - lb], axis=1)
        lbc = jnp.concatenate([lbc, jnp.zeros((2, 5, GROUP), F32)], axis=1)
        hg0 = jnp.zeros((bsz, 2, 4, LANES, LANES), F32)
        ocf, ocr, hg_state = _hgrn_scan(p_ctx, lbc, consts, hg0, bsz, nctx, chunk, emit_ctx)
        of, orv, _ = _hgrn_scan(p_lat, lbc, consts, hg_state, bsz, seq, chunk, True)

        w_out_p = _perm_w_out(w_out[l])
        gain2 = norm_ffn[l].reshape(1, d)
        w_up = ffn_up[l].astype(BF16)
        w_dn = ffn_down[l].astype(BF16)
        w_cv = ffn_conv[l].astype(F32)
        h = _out_proj(h, ya, yb, hf, hr, of, orv, p_lat, w_out_p, lat(2), seq)
        h = _ffn(h, gain2, lat(3), lat(4), lat(5), w_up, w_cv, w_dn, seq)
        if emit_ctx:
            yac, ybc = _ctx_attention(cnaq, cnak, cnav, cswq, cswk, cswv, sink, bsz, nctx)
            hc = _out_proj(hc, yac, ybc, hcf, hcr, ocf, ocr, p_ctx, w_out_p, cx(2), nctx)
            hc = _ffn(hc, gain2, cx(3), cx(4), cx(5), w_up, w_cv, w_dn, nctx)
    return h.reshape(bsz, seq, d)
```

```python
import functools
import math

import numpy as np
import jax
import jax.numpy as jnp
from jax import lax
from jax.experimental import pallas as pl
from jax.experimental.pallas import tpu as pltpu

F32 = jnp.float32
BF16 = jnp.bfloat16

GRID_W = 64
ATT_HD = 64
NA_ROWS = 8
NA_COLS = 16
SW_BLOCK = 128
ROPE_BASE = 10000.0
EPS = 1e-6
NEG = -1e30
GROUP = 512
LANES = 128
HALO = 8
VMEM_LIMIT = 56 * 1024 * 1024

C_NAQ, C_NAK, C_NAV, C_SWQ = 0, 512, 1024, 1536
C_MLQK, C_MLV, C_MLO = 2048, 2560, 3072
C_HGQ, C_HGI, C_HGF, C_HGG = 3584, 4096, 4608, 5632
C_SWK, C_SWV = 6144, 6272
C_MLI, C_MLF = 6400, 6528
NP = 6656
ML_CHUNK = 256
HG_CHUNK = 64
SW_PERM = (0, 4, 1, 5, 2, 6, 3, 7)


def _cparams(sem):
    return pltpu.CompilerParams(dimension_semantics=sem, vmem_limit_bytes=VMEM_LIMIT)


def _sigmoid(x):
    return 1.0 / (1.0 + jnp.exp(-x))


def _log_sigmoid(x):
    return jnp.minimum(x, 0.0) - jnp.log1p(jnp.exp(-jnp.abs(x)))


def _split3(x):
    hi = x.astype(BF16)
    r = x - hi.astype(F32)
    mid = r.astype(BF16)
    lo = (r - mid.astype(F32)).astype(BF16)
    return hi, mid, lo


def _dot(a, b):
    return jnp.dot(a, b, preferred_element_type=F32)


def _dot_nt(a, b):
    return lax.dot_general(a, b, (((1,), (1,)), ((), ())), preferred_element_type=F32)


def _dot_tn(a, b):
    return lax.dot_general(a, b, (((0,), (0,)), ((), ())), preferred_element_type=F32)


def _exact_left(m01, x):
    hi, mid, lo = _split3(x)
    return _dot(m01, hi) + _dot(m01, mid) + _dot(m01, lo)


def _exact_right(x, m01):
    hi, mid, lo = _split3(x)
    return _dot(hi, m01) + _dot(mid, m01) + _dot(lo, m01)


def _mod_kernel(c_ref, w_ref, b_ref, o_ref):
    c = c_ref[...]
    a = c * _sigmoid(c)
    a_hi = a.astype(BF16)
    a_lo = (a - a_hi.astype(F32)).astype(BF16)
    w = w_ref[...]
    w_hi = w.astype(BF16)
    w_lo = (w - w_hi.astype(F32)).astype(BF16)
    o_ref[...] = _dot(a_hi, w_hi) + _dot(a_hi, w_lo) + _dot(a_lo, w_hi) + b_ref[...]


def _modulation(c_all, w_mod, b_mod):
    depth, d, n = w_mod.shape
    tn = 1024 if n % 1024 == 0 else n
    return pl.pallas_call(
        _mod_kernel,
        grid=(depth, n // tn),
        in_specs=[pl.BlockSpec((8, d), lambda l, j: (0, 0)),
                  pl.BlockSpec((None, d, tn), lambda l, j: (l, 0, j)),
                  pl.BlockSpec((None, 1, tn), lambda l, j: (l, 0, j))],
        out_specs=pl.BlockSpec((None, 8, tn), lambda l, j: (l, 0, j)),
        out_shape=jax.ShapeDtypeStruct((depth, 8, n), F32),
        compiler_params=_cparams(("parallel", "parallel")),
    )(c_all, w_mod, b_mod.reshape(depth, 1, n))


def _norm_mod(x, gain, shift, scale):
    y = x * lax.rsqrt(jnp.mean(x * x, axis=-1, keepdims=True) + EPS) * gain
    return y * (1.0 + scale) + shift


def _inproj_kernel(x_ref, g_ref, sh_ref, sc_ref, w_ref, o_ref, xn_ref):
    @pl.when(pl.program_id(1) == 0)
    def _():
        xn_ref[...] = _norm_mod(x_ref[...], g_ref[...], sh_ref[...], sc_ref[...]).astype(BF16)

    o_ref[...] = _dot(xn_ref[...], w_ref[...])


def _row_tile(rows_per_mod, pref):
    tm = min(pref, rows_per_mod)
    assert rows_per_mod % tm == 0
    return tm


def _in_proj(x, gain, shift, scale, w_all, layer, rows_per_mod):
    t, d = x.shape
    n = w_all.shape[2]
    tm = _row_tile(rows_per_mod, 1024)
    tn = 512 if n % 512 == 0 else n
    per = rows_per_mod // tm
    return pl.pallas_call(
        _inproj_kernel,
        grid=(t // tm, n // tn),
        in_specs=[pl.BlockSpec((tm, d), lambda i, j: (i, 0)),
                  pl.BlockSpec((1, d), lambda i, j: (0, 0)),
                  pl.BlockSpec((None, 1, d), lambda i, j: (i // per, 0, 0)),
                  pl.BlockSpec((None, 1, d), lambda i, j: (i // per, 0, 0)),
                  pl.BlockSpec((None, d, tn), lambda i, j: (layer, 0, j))],
        out_specs=pl.BlockSpec((tm, tn), lambda i, j: (i, j)),
        out_shape=jax.ShapeDtypeStruct((t, n), F32),
        scratch_shapes=[pltpu.VMEM((tm, d), BF16)],
        compiler_params=_cparams(("parallel", "arbitrary")),
    )(x, gain, shift, scale, w_all)


def _head_rms(x, gain_row, bd):
    xx = x * x
    hi = xx.astype(BF16)
    lo = (xx - hi.astype(F32)).astype(BF16)
    ss = _dot(hi, bd) + _dot(lo, bd)
    return x * lax.rsqrt(ss * (1.0 / ATT_HD) + EPS) * gain_row


def _rope(x, cos, sin, first):
    w = x.shape[-1]
    nxt = pltpu.roll(x, w - 16, axis=1)
    prv = pltpu.roll(x, 16, axis=1)
    return x * cos + jnp.where(first, nxt, prv) * sin


def _attn_prep_kernel(p_ref, pkv_ref, gna_ref, gsw_ref, bd_ref, cos_ref, sin_ref,
                      naq_ref, nak_ref, nav_ref, swq_ref, swk_ref, swv_ref, *, rope):
    bd = bd_ref[...]
    scale = ATT_HD ** -0.5
    naq_ref[...] = (_head_rms(p_ref[:, C_NAQ:C_NAQ + GROUP], gna_ref[0:1, :], bd) * scale).astype(BF16)
    nak_ref[...] = _head_rms(p_ref[:, C_NAK:C_NAK + GROUP], gna_ref[1:2, :], bd).astype(BF16)
    nav_ref[...] = p_ref[:, C_NAV:C_NAV + GROUP].astype(BF16)
    q = _head_rms(p_ref[:, C_SWQ:C_SWQ + GROUP], gsw_ref[0:1, :], bd)
    k = _head_rms(pkv_ref[:, 0:LANES], gsw_ref[1:2, 0:LANES], bd[0:LANES, 0:LANES])
    if rope:
        cos = cos_ref[...]
        sin = sin_ref[...]
        lane = lax.broadcasted_iota(jnp.int32, (1, GROUP), 1)
        first = (lane % 32) < 16
        q = _rope(q, jnp.concatenate([cos] * 4, axis=1), jnp.concatenate([sin] * 4, axis=1), first)
        k = _rope(k, cos, sin, first[:, 0:LANES])
    swq_ref[...] = (q * scale).astype(BF16)
    swk_ref[...] = k.astype(BF16)
    swv_ref[...] = pkv_ref[:, LANES:2 * LANES].astype(BF16)


def _attn_prep(p, gna, gsw, bd, cos, sin, seq, rope):
    t = p.shape[0]
    tm = _row_tile(seq, 512)
    per = seq // tm
    out = lambda w: jax.ShapeDtypeStruct((t, w), BF16)
    ospec = lambda w: pl.BlockSpec((tm, w), lambda i: (i, 0))
    return pl.pallas_call(
        functools.partial(_attn_prep_kernel, rope=rope),
        grid=(t // tm,),
        in_specs=[pl.BlockSpec((tm, C_MLQK), lambda i: (i, 0)),
                  pl.BlockSpec((tm, 2 * LANES), lambda i: (i, C_SWK // (2 * LANES))),
                  pl.BlockSpec((2, GROUP), lambda i: (0, 0)),
                  pl.BlockSpec((2, GROUP), lambda i: (0, 0)),
                  pl.BlockSpec((GROUP, GROUP), lambda i: (0, 0)),
                  pl.BlockSpec((tm, LANES), lambda i: (i % per, 0)),
                  pl.BlockSpec((tm, LANES), lambda i: (i % per, 0))],
        out_specs=[ospec(GROUP), ospec(GROUP), ospec(GROUP), ospec(GROUP), ospec(LANES), ospec(LANES)],
        out_shape=[out(GROUP), out(GROUP), out(GROUP), out(GROUP), out(LANES), out(LANES)],
        compiler_params=_cparams(("parallel",)),
    )(p, p, gna, gsw, bd, cos, sin)


def _pair_rows(q_pair):
    lane = lax.broadcasted_iota(jnp.int32, q_pair.shape, 1)
    zero = jnp.zeros_like(q_pair)
    return jnp.concatenate([jnp.where(lane < ATT_HD, q_pair, zero),
                            jnp.where(lane >= ATT_HD, q_pair, zero)], axis=0)


def _pair_merge(o2, inv_l):
    m = o2.shape[0] // 2
    lane = lax.broadcasted_iota(jnp.int32, (m, LANES), 1)
    return jnp.where(lane < ATT_HD, o2[0:m] * inv_l[0:m], o2[m:] * inv_l[m:])


NA_RPS = 4


def _softmax_pv(parts, sink=None):
    m = functools.reduce(jnp.maximum, [jnp.max(s, axis=-1, keepdims=True) for s, _ in parts])
    if sink is not None:
        m = jnp.maximum(m, sink)
    acc = None
    for s, v in parts:
        e = jnp.exp((s - m).astype(BF16))
        o = _dot(e, jnp.concatenate([v, jnp.ones_like(v)], axis=1))
        acc = o if acc is None else acc + o
    l = acc[:, LANES:LANES + 1]
    if sink is not None:
        l = l + jnp.exp(sink - m)
    return acc[:, 0:LANES], l


NA_WIN = NA_ROWS + NA_RPS


def _na_window(step, rows):
    return jnp.clip(step * NA_RPS - NA_ROWS // 2, 0, rows - NA_WIN)


def _na_kernel(q_ref, k_ref, v_ref, kc_ref, vc_ref, b_ref, o_ref, *, rows):
    start = pl.multiple_of(_na_window(pl.program_id(1), rows) * GRID_W, GRID_W)
    nkey = NA_WIN * GRID_W
    for p in range(GROUP // LANES):
        cs = slice(p * LANES, (p + 1) * LANES)
        qs = jnp.concatenate([_pair_rows(q_ref[rr * GRID_W:(rr + 1) * GRID_W, cs]) for rr in range(NA_RPS)],
                             axis=0)
        kw = k_ref[pl.ds(start, nkey), cs]
        vw = v_ref[pl.ds(start, nkey), cs]
        s_nb = _dot_nt(qs, kw) + b_ref[p]
        s_cx = _dot_nt(qs, kc_ref[:, cs])
        o2, l = _softmax_pv([(s_nb, vw), (s_cx, vc_ref[:, cs])])
        inv_l = 1.0 / l
        for rr in range(NA_RPS):
            sl = slice(rr * 2 * GRID_W, (rr + 1) * 2 * GRID_W)
            o_ref[rr * GRID_W:(rr + 1) * GRID_W, cs] = _pair_merge(o2[sl], inv_l[sl]).astype(BF16)


def _na_attention(q, k, v, kc, vc, bias, bsz, seq, ctx):
    rows = seq // GRID_W
    assert rows >= NA_WIN and rows % NA_RPS == 0
    steps = rows // NA_RPS
    k3, v3 = k.reshape(bsz, seq, GROUP), v.reshape(bsz, seq, GROUP)
    kc3, vc3 = kc.reshape(bsz, ctx, GROUP), vc.reshape(bsz, ctx, GROUP)
    qspec = pl.BlockSpec((NA_RPS * GRID_W, GROUP), lambda b, r: (b * steps + r, 0))
    variant = lambda b, r: ((r > 0).astype(jnp.int32) + (r == steps - 1).astype(jnp.int32), 0, 0, 0)
    return pl.pallas_call(
        functools.partial(_na_kernel, rows=rows),
        grid=(bsz, steps),
        in_specs=[qspec,
                  pl.BlockSpec((None, seq, GROUP), lambda b, r: (b, 0, 0)),
                  pl.BlockSpec((None, seq, GROUP), lambda b, r: (b, 0, 0)),
                  pl.BlockSpec((None, ctx, GROUP), lambda b, r: (b, 0, 0)),
                  pl.BlockSpec((None, ctx, GROUP), lambda b, r: (b, 0, 0)),
                  pl.BlockSpec((None,) + bias.shape[1:], variant)],
        out_specs=qspec,
        out_shape=jax.ShapeDtypeStruct((bsz * seq, GROUP), BF16),
        compiler_params=_cparams(("parallel", "arbitrary")),
    )(q, k3, v3, kc3, vc3, bias)


def _na_bias_table(rpb, rows):
    nh = rpb.shape[0]
    steps = rows // NA_RPS
    c = np.arange(GRID_W)
    kc = np.arange(GRID_W)
    cstart = np.clip(c - NA_COLS // 2, 0, GRID_W - NA_COLS)
    ok = (kc[None, :] >= cstart[:, None]) & (kc[None, :] < cstart[:, None] + NA_COLS)
    dc = np.clip(kc[None, :] - c[:, None] + NA_COLS - 1, 0, 2 * NA_COLS - 2)
    t = jnp.where(ok[None, None], rpb.astype(F32)[:, :, dc], NEG)

    def step_layout(s):
        ws = int(np.clip(s * NA_RPS - NA_ROWS // 2, 0, rows - NA_WIN))
        r = s * NA_RPS + np.arange(NA_RPS)[:, None]
        rs = np.clip(r - NA_ROWS // 2, 0, rows - NA_ROWS)
        krow = ws + np.arange(NA_WIN)[None, :]
        return (krow - r + NA_ROWS - 1), (krow >= rs) & (krow < rs + NA_ROWS)

    layouts = [step_layout(s) for s in range(steps)]
    variants = [layouts[0], layouts[1], layouts[-1]]
    for s, (idx, valid) in enumerate(layouts):
        want = variants[(s > 0) + (s == steps - 1)]
        assert (valid == want[1]).all() and (idx[valid] == want[0][want[1]]).all()
    idx = np.stack([np.clip(v[0], 0, 2 * NA_ROWS - 2) for v in variants])
    valid = np.stack([v[1] for v in variants])
    tab = jnp.where(valid[None, :, :, :, None, None], t[:, idx], NEG)
    tab = tab.reshape(nh // 2, 2, 3, NA_RPS, NA_WIN, GRID_W, GRID_W)
    tab = jnp.transpose(tab, (2, 0, 3, 1, 5, 4, 6))
    return tab.reshape(3, nh // 2, NA_RPS * 2 * GRID_W, NA_WIN * GRID_W)


SW_BPS = 2


def _sw_kernel(sink_ref, q_ref, k_ref, v_ref, kc_ref, vc_ref, o_ref, *, seq):
    nq = SW_BPS * SW_BLOCK
    nwin = (SW_BPS + 2) * SW_BLOCK
    q0 = pl.program_id(1) * nq
    start = pl.multiple_of(jnp.clip(q0 - SW_BLOCK, 0, seq - nwin), SW_BLOCK)
    kw = k_ref[pl.ds(start, nwin), :]
    vw = v_ref[pl.ds(start, nwin), :]
    kc = kc_ref[...]
    vc = vc_ref[...]
    row = lax.broadcasted_iota(jnp.int32, (2 * nq, nwin), 0)
    col = lax.broadcasted_iota(jnp.int32, (2 * nq, nwin), 1)
    ok = jnp.abs((start + col) - (q0 + row % nq)) <= SW_BLOCK
    half = lax.broadcasted_iota(jnp.int32, (2 * nq, 1), 0) < nq
    for j in range(GROUP // LANES):
        cs = slice(j * LANES, (j + 1) * LANES)
        qs = _pair_rows(q_ref[:, cs])
        s_bd = jnp.where(ok, _dot_nt(qs, kw), NEG)
        s_cx = _dot_nt(qs, kc)
        sink = jnp.where(half, sink_ref[j], sink_ref[j + 4])
        o2, l = _softmax_pv([(s_bd, vw), (s_cx, vc)], sink)
        o_ref[:, cs] = _pair_merge(o2, 1.0 / l).astype(BF16)


def _sw_attention(q, k, v, kc, vc, sink, bsz, seq, ctx):
    assert seq % (SW_BPS * SW_BLOCK) == 0 and seq >= (SW_BPS + 2) * SW_BLOCK
    nb = seq // (SW_BPS * SW_BLOCK)
    k3, v3 = k.reshape(bsz, seq, LANES), v.reshape(bsz, seq, LANES)
    kc3, vc3 = kc.reshape(bsz, ctx, LANES), vc.reshape(bsz, ctx, LANES)
    return pl.pallas_call(
        functools.partial(_sw_kernel, seq=seq),
        grid=(bsz, nb),
        in_specs=[pl.BlockSpec(memory_space=pltpu.SMEM),
                  pl.BlockSpec((SW_BPS * SW_BLOCK, GROUP), lambda b, n: (b * nb + n, 0)),
                  pl.BlockSpec((None, seq, LANES), lambda b, n: (b, 0, 0)),
                  pl.BlockSpec((None, seq, LANES), lambda b, n: (b, 0, 0)),
                  pl.BlockSpec((None, ctx, LANES), lambda b, n: (b, 0, 0)),
                  pl.BlockSpec((None, ctx, LANES), lambda b, n: (b, 0, 0))],
        out_specs=pl.BlockSpec((SW_BPS * SW_BLOCK, GROUP), lambda b, n: (b * nb + n, 0)),
        out_shape=jax.ShapeDtypeStruct((bsz * seq, GROUP), BF16),
        compiler_params=_cparams(("parallel", "arbitrary")),
    )(sink, q, k3, v3, kc3, vc3)


def _ctx_attn_kernel(sink_ref, qa_ref, ka_ref, va_ref, qb_ref, kb_ref, vb_ref, oa_ref, ob_ref):
    for p in range(GROUP // LANES):
        cs = slice(p * LANES, (p + 1) * LANES)
        qs = _pair_rows(qa_ref[:, cs])
        o2, l = _softmax_pv([(_dot_nt(qs, ka_ref[:, cs]), va_ref[:, cs])])
        oa_ref[:, cs] = _pair_merge(o2, 1.0 / l).astype(BF16)
    ctx = qb_ref.shape[0]
    half = lax.broadcasted_iota(jnp.int32, (2 * ctx, 1), 0) < ctx
    for j in range(GROUP // LANES):
        cs = slice(j * LANES, (j + 1) * LANES)
        qs = _pair_rows(qb_ref[:, cs])
        sink = jnp.where(half, sink_ref[j], sink_ref[j + 4])
        o2, l = _softmax_pv([(_dot_nt(qs, kb_ref[...]), vb_ref[...])], sink)
        ob_ref[:, cs] = _pair_merge(o2, 1.0 / l).astype(BF16)


def _ctx_attention(qa, ka, va, qb, kb, vb, sink, bsz, ctx):
    big = pl.BlockSpec((ctx, GROUP), lambda b: (b, 0))
    small = pl.BlockSpec((ctx, LANES), lambda b: (b, 0))
    return pl.pallas_call(
        _ctx_attn_kernel,
        grid=(bsz,),
        in_specs=[pl.BlockSpec(memory_space=pltpu.SMEM), big, big, big, big, small, small],
        out_specs=[big, big],
        out_shape=[jax.ShapeDtypeStruct((bsz * ctx, GROUP), BF16)] * 2,
        compiler_params=_cparams(("parallel",)),
    )(sink, qa, ka, va, qb, kb, vb)


def _halo_specs(tm, width, col_block, nblk8):
    step = tm // HALO
    return [pl.BlockSpec((tm, width), lambda i: (i, col_block)),
            pl.BlockSpec((HALO, width), lambda i: (jnp.maximum(i * step - 1, 0), col_block)),
            pl.BlockSpec((HALO, width), lambda i: (jnp.minimum((i + 1) * step, nblk8 - 1), col_block))]


def _ml_prep_kernel(x_ref, xp_ref, xn_ref, w_ref, o_ref, *, per):
    i = pl.program_id(0)
    x = x_ref[...]
    tm = x.shape[0]
    row = lax.broadcasted_iota(jnp.int32, (tm, 1), 0)
    keep_p = jnp.where(i % per == 0, 0.0, 1.0)
    keep_n = jnp.where(i % per == per - 1, 0.0, 1.0)
    prv = jnp.where(row == 0, xp_ref[HALO - 1:HALO, :] * keep_p, pltpu.roll(x, 1, axis=0))
    nxt = jnp.where(row == tm - 1, xn_ref[0:1, :] * keep_n, pltpu.roll(x, tm - 1, axis=0))
    y = prv * w_ref[0:1, :] + x * w_ref[1:2, :] + nxt * w_ref[2:3, :]
    y = y * _sigmoid(y)
    lane = lax.broadcasted_iota(jnp.int32, (1, GROUP), 1)
    o_ref[...] = (y * jnp.where(lane < GROUP // 2, ATT_HD ** -0.5, 1.0)).astype(BF16)


def _ml_prep(p, conv_w, seq):
    t = p.shape[0]
    tm = _row_tile(seq, 512)
    per = seq // tm
    return pl.pallas_call(
        functools.partial(_ml_prep_kernel, per=per),
        grid=(t // tm,),
        in_specs=_halo_specs(tm, GROUP, C_MLQK // GROUP, t // HALO)
        + [pl.BlockSpec((3, GROUP), lambda i: (0, 0))],
        out_specs=pl.BlockSpec((tm, GROUP), lambda i: (i, 0)),
        out_shape=jax.ShapeDtypeStruct((t, GROUP), BF16),
        compiler_params=_cparams(("parallel",)),
    )(p, p, p, conv_w)


def _mlstm_kernel(*refs, chunk, emit):
    (qkf_ref, vf_ref, gif_ref, gff_ref, qkr_ref, vr_ref, gir_ref, gfr_ref,
     brow_ref, bcol_ref, c0_ref, n0_ref, m0_ref) = refs[:13]
    if emit:
        hf_ref, hr_ref = refs[13:15]
        rest = refs[15:]
    else:
        hf_ref = hr_ref = None
        rest = refs[13:]
    c_out, n_out, m_out, c_s, n_s, m_s = rest
    i = pl.program_id(1)

    @pl.when(i == 0)
    def _():
        c_s[...] = c0_ref[...]
        n_s[...] = n0_ref[...]
        m_s[...] = m0_ref[...]

    row = lax.broadcasted_iota(jnp.int32, (chunk, chunk), 0)
    col = lax.broadcasted_iota(jnp.int32, (chunk, chunk), 1)
    lane = lax.broadcasted_iota(jnp.int32, (chunk, LANES), 1)
    lane1 = lax.broadcasted_iota(jnp.int32, (1, LANES), 1)
    srow = lax.broadcasted_iota(jnp.int32, (LANES, 1), 0)
    dirs = ((qkf_ref, vf_ref, gif_ref, gff_ref, hf_ref), (qkr_ref, vr_ref, gir_ref, gfr_ref, hr_ref))
    chains = [(b, d) for b in range(qkf_ref.shape[0]) for d in range(2)]
    old_c = {(b, d, pp): c_s[b, d, pp] for b, d in chains for pp in range(2)}
    old_n = {(b, d, pp): n_s[b, d, pp, 0:1, :] for b, d in chains for pp in range(2)}
    old_m = {(b, d): m_s[b, d, 0:1, :] for b, d in chains}
    new_c, new_n, new_m = {}, {}, {}
    for b, d in chains:
        qk_ref, v_ref, gi_ref, gf_ref, h_ref = [r if r is None else r.at[b] for r in dirs[d]]
        causal = (col <= row) if d == 0 else (col >= row)
        causal_t = (row <= col) if d == 0 else (row >= col)
        gi = gi_ref[...]
        gf = gf_ref[...]
        ipre = gi + brow_ref[0:1, :]
        fl = _log_sigmoid(gf + brow_ref[1:2, :])
        fcum = _exact_left(jnp.where(causal, 1.0, 0.0).astype(BF16), fl)
        ftot = jnp.sum(fl, axis=0, keepdims=True)
        m_prev = old_m[b, d]
        w_log = ftot - fcum + ipre
        m_new = jnp.maximum(ftot + m_prev, jnp.max(w_log, axis=0, keepdims=True))
        cd = jnp.exp(ftot + m_prev - m_new)
        w = jnp.exp(w_log - m_new)
        if emit:
            ipre_t = gi.T[0:8, :] + bcol_ref[:, 0:1]
            fl_t = _log_sigmoid(gf.T[0:8, :] + bcol_ref[:, 1:2])
            fcum_t = _exact_right(fl_t, jnp.where(causal_t, 1.0, 0.0).astype(BF16))
            b_rows = ipre_t - fcum_t
            log_inter = fcum + m_prev
        for pp in range(2):
            qp = qk_ref[:, pp * LANES:(pp + 1) * LANES]
            kp = qk_ref[:, GROUP // 2 + pp * LANES:GROUP // 2 + (pp + 1) * LANES]
            ct = old_c[b, d, pp]
            n_row = old_n[b, d, pp]
            inc = None
            for e in range(2):
                h = 2 * pp + e
                c = 4 * d + h
                sel = (lane < ATT_HD) if e == 0 else (lane >= ATT_HD)
                vh = v_ref[:, h * LANES:(h + 1) * LANES]
                if emit:
                    qm = jnp.where(sel, qp, jnp.zeros_like(qp))
                    fc = fcum[:, c:c + 1]
                    li = log_inter[:, c:c + 1]
                    log_d = jnp.where(causal, fc + b_rows[c:c + 1, :], NEG)
                    m_row = jnp.maximum(li, jnp.max(log_d, axis=-1, keepdims=True))
                    d_mat = jnp.exp(log_d - m_row)
                    inter_w = jnp.exp(li - m_row)
                    s_mat = _dot_nt(qm, kp) * d_mat
                    num = inter_w * _dot(qm, ct.astype(BF16)) + _dot(s_mat.astype(BF16), vh.astype(BF16))
                    den = (inter_w * jnp.sum(qm.astype(F32) * n_row, axis=-1, keepdims=True)
                           + jnp.sum(s_mat, axis=-1, keepdims=True))
                    h_ref[:, h * LANES:(h + 1) * LANES] = num / jnp.maximum(jnp.abs(den), jnp.exp(-m_row))
                km = jnp.where(sel, kp, jnp.zeros_like(kp))
                upd = _dot_tn(km, (w[:, c:c + 1] * vh).astype(BF16))
                inc = upd if inc is None else inc + upd
            c_lo, c_hi = 4 * d + 2 * pp, 4 * d + 2 * pp + 1
            new_c[b, d, pp] = (jnp.where(srow < ATT_HD, cd[:, c_lo:c_lo + 1], cd[:, c_hi:c_hi + 1]) * ct
                               + inc)
            w_sel = jnp.where(lane < ATT_HD, w[:, c_lo:c_lo + 1], w[:, c_hi:c_hi + 1])
            cd_sel = jnp.where(lane1 < ATT_HD, cd[:, c_lo:c_lo + 1], cd[:, c_hi:c_hi + 1])
            new_n[b, d, pp] = cd_sel * n_row + jnp.sum(w_sel * kp.astype(F32), axis=0, keepdims=True)
        new_m[b, d] = m_new
    for (b, d, pp), val in new_c.items():
        c_s[b, d, pp] = val
        n_s[b, d, pp, 0:1, :] = new_n[b, d, pp]
    for (b, d), val in new_m.items():
        m_s[b, d, 0:1, :] = val

    @pl.when(i == pl.num_programs(1) - 1)
    def _():
        c_out[...] = c_s[...]
        n_out[...] = n_s[...]
        m_out[...] = m_s[...]


def _mlstm_scan(qk, p, brow, bcol, state, bsz, seq, chunk, emit):
    nc = seq // chunk
    bps = bsz
    fwd = lambda g, i: i
    rev = lambda g, i: nc - 1 - i
    qk3 = qk.reshape(bsz, seq, GROUP)
    p3 = p.reshape(bsz, seq, NP)

    def chunk_specs(idx):
        return [pl.BlockSpec((bps, chunk, GROUP), lambda g, i: (g, idx(g, i), 0)),
                pl.BlockSpec((bps, chunk, GROUP), lambda g, i: (g, idx(g, i), C_MLV // GROUP)),
                pl.BlockSpec((bps, chunk, LANES), lambda g, i: (g, idx(g, i), C_MLI // LANES)),
                pl.BlockSpec((bps, chunk, LANES), lambda g, i: (g, idx(g, i), C_MLF // LANES))]

    st_dims = [(2, 2, LANES, LANES), (2, 2, 8, LANES), (2, 8, LANES)]
    st_specs = [pl.BlockSpec((bps,) + s, lambda g, i, n=len(s): (g,) + (0,) * n) for s in st_dims]
    st_shapes = [jax.ShapeDtypeStruct((bsz,) + s, F32) for s in st_dims]
    h_specs = [pl.BlockSpec((bps, chunk, GROUP), lambda g, i: (g, fwd(g, i), 0)),
               pl.BlockSpec((bps, chunk, GROUP), lambda g, i: (g, rev(g, i), 0))] if emit else []
    h_shapes = [jax.ShapeDtypeStruct((bsz, seq, GROUP), F32)] * 2 if emit else []
    outs = pl.pallas_call(
        functools.partial(_mlstm_kernel, chunk=chunk, emit=emit),
        grid=(bsz // bps, nc),
        in_specs=chunk_specs(fwd) + chunk_specs(rev)
        + [pl.BlockSpec((8, LANES), lambda g, i: (0, 0)), pl.BlockSpec((8, LANES), lambda g, i: (0, 0))]
        + st_specs,
        out_specs=h_specs + st_specs,
        out_shape=h_shapes + st_shapes,
        scratch_shapes=[pltpu.VMEM((bps,) + s, F32) for s in st_dims],
        compiler_params=_cparams(("parallel", "arbitrary")),
    )(qk3, p3, p3, p3, qk3, p3, p3, p3, brow, bcol, *state)
    if emit:
        return (outs[0].reshape(bsz * seq, GROUP), outs[1].reshape(bsz * seq, GROUP), tuple(outs[2:]))
    return None, None, tuple(outs)


def _hgrn_constants(chunk):
    nlev = int(np.log2(chunk))
    assert 2 ** nlev == chunk
    t = np.arange(chunk)
    mats = [(t[None, :] <= t[:, None]), (t[None, :] > t[:, None])]
    masks = []
    for lev in range(nlev):
        n = chunk >> (lev + 1)
        b0 = t - t % (2 * n)
        upper = (t % (2 * n)) >= n
        m_up = (t[None, :] >= (b0 + n)[:, None]) & (t[None, :] <= t[:, None])
        m_lo = (t[None, :] > t[:, None]) & (t[None, :] <= (b0 + n - 1)[:, None])
        mats.append(np.where(upper[:, None], m_up, m_lo))
        same = (t[:, None] // (2 * n)) == (t[None, :] // (2 * n))
        masks.append(same & upper[:, None] & ~upper[None, :])
    masks.append(t[:, None] == t[None, :])
    mstack = np.concatenate(mats, axis=0).astype(np.float32)
    lmask = np.stack([np.kron(np.eye(2), m.astype(np.float32)) for m in masks])
    nr = mstack.shape[0] // chunk
    mstack_rev = mstack.reshape(nr, chunk, chunk)[:, ::-1, ::-1].reshape(-1, chunk)
    lmask_rev = lmask.reshape(nlev + 1, 2, chunk, 2, chunk)[:, :, ::-1, :, ::-1].reshape(lmask.shape)
    return (jnp.asarray(np.stack([mstack, mstack_rev]), BF16),
            jnp.asarray(np.stack([lmask, lmask_rev]), F32), nlev)


def _stack_pair(x, pp):
    return jnp.concatenate([x[:, (2 * pp) * LANES:(2 * pp + 1) * LANES],
                            x[:, (2 * pp + 1) * LANES:(2 * pp + 2) * LANES]], axis=0)


def _hgrn_kernel(*refs, chunk, nlev, emit):
    (qf_ref, vf_ref, ff_ref, qr_ref, vr_ref, fr_ref, lb_ref, ms_ref, lm_ref, s0_ref) = refs[:10]
    if emit:
        of_ref, or_ref = refs[10:12]
        rest = refs[12:]
    else:
        of_ref = or_ref = None
        rest = refs[10:]
    s_out, s_s = rest
    i = pl.program_id(1)

    @pl.when(i == 0)
    def _():
        s_s[...] = s0_ref[...]

    dirs = ((qf_ref, vf_ref, ff_ref, of_ref), (qr_ref, vr_ref, fr_ref, or_ref))
    chains = [(b, d) for b in range(vf_ref.shape[0]) for d in range(2)]
    old_s = {(b, d, h): s_s[b, d, h] for b, d in chains for h in range(4)}
    new_s = {}

    for b, d in chains:
        q_ref, v_ref, f_ref, o_ref = [r if r is None else r.at[b] for r in dirs[d]]
        fpre = f_ref[...]
        vb = v_ref[...].astype(BF16)
        a = lb_ref[d, 0:1, :]
        bb = lb_ref[d, 1:2, :] + _log_sigmoid(fpre)
        logf = jnp.maximum(a, bb) + jnp.log1p(jnp.exp(-jnp.abs(a - bb)))
        k = lb_ref[d, 2:3, :] * _sigmoid(-fpre)
        hi = logf.astype(BF16)
        lo = (logf - hi.astype(F32)).astype(BF16)
        gtot = jnp.sum(logf, axis=0, keepdims=True)
        if emit:
            qpre = q_ref[...]
            q = qpre * _sigmoid(qpre)
            qb = q.astype(BF16)
            kb = k.astype(BF16)
        ms = ms_ref[d]
        ex = _dot(ms, hi) + _dot(ms, lo)
        ktil = (k * jnp.exp(ex[chunk:2 * chunk])).astype(BF16)
        if emit:
            qe = (q * jnp.exp(ex[0:chunk])).astype(BF16)
            lev_ops = []
            for lev in range(nlev):
                e_l = jnp.exp(ex[(2 + lev) * chunk:(3 + lev) * chunk])
                lev_ops.append(((q * e_l).astype(BF16), (k * e_l).astype(BF16)))
        if emit:
            att = [lm_ref[d, nlev] * _dot_nt(_stack_pair(qb, pp), _stack_pair(kb, pp)) for pp in range(2)]
            for lev, (q_l, k_l) in enumerate(lev_ops):
                for pp in range(2):
                    att[pp] = att[pp] + lm_ref[d, lev] * _dot_nt(_stack_pair(q_l, pp), _stack_pair(k_l, pp))
            o_pair = [_dot(att[pp].astype(BF16), _stack_pair(vb, pp)) for pp in range(2)]
        for h in range(4):
            cs = slice(h * LANES, (h + 1) * LANES)
            s_t = old_s[b, d, h]
            if emit:
                pp, e = divmod(h, 2)
                o_ref[:, cs] = o_pair[pp][e * chunk:(e + 1) * chunk] + _dot_nt(qe[:, cs], s_t.astype(BF16))
            new_s[b, d, h] = s_t * jnp.exp(gtot[:, cs]) + _dot_tn(vb[:, cs], ktil[:, cs])

    for key, val in new_s.items():
        s_s[key] = val

    @pl.when(i == pl.num_programs(1) - 1)
    def _():
        s_out[...] = s_s[...]


def _hgrn_scan(p, lbc, consts, state, bsz, seq, chunk, emit):
    mstack, lmask, nlev = consts
    nc = seq // chunk
    bps = bsz
    fwd = lambda g, i: i
    rev = lambda g, i: nc - 1 - i
    p3 = p.reshape(bsz, seq, NP)

    def chunk_specs(idx, d):
        return [pl.BlockSpec((bps, chunk, GROUP), lambda g, i: (g, idx(g, i), C_HGQ // GROUP)),
                pl.BlockSpec((bps, chunk, GROUP), lambda g, i: (g, idx(g, i), C_HGI // GROUP)),
                pl.BlockSpec((bps, chunk, GROUP), lambda g, i: (g, idx(g, i), C_HGF // GROUP + d))]

    st_spec = pl.BlockSpec((bps, 2, 4, LANES, LANES), lambda g, i: (g, 0, 0, 0, 0))
    st_shape = jax.ShapeDtypeStruct((bsz, 2, 4, LANES, LANES), F32)
    o_specs = [pl.BlockSpec((bps, chunk, GROUP), lambda g, i: (g, fwd(g, i), 0)),
               pl.BlockSpec((bps, chunk, GROUP), lambda g, i: (g, rev(g, i), 0))] if emit else []
    o_shapes = [jax.ShapeDtypeStruct((bsz, seq, GROUP), F32)] * 2 if emit else []
    full = lambda a: pl.BlockSpec(a.shape, lambda g, i: (0,) * a.ndim)
    outs = pl.pallas_call(
        functools.partial(_hgrn_kernel, chunk=chunk, nlev=nlev, emit=emit),
        grid=(bsz // bps, nc),
        in_specs=chunk_specs(fwd, 0) + chunk_specs(rev, 1) + [full(lbc), full(mstack), full(lmask), st_spec],
        out_specs=o_specs + [st_spec],
        out_shape=o_shapes + [st_shape],
        scratch_shapes=[pltpu.VMEM((bps, 2, 4, LANES, LANES), F32)],
        compiler_params=_cparams(("parallel", "arbitrary")),
    )(p3, p3, p3, p3, p3, p3, lbc, mstack, lmask, state)
    if emit:
        return outs[0].reshape(bsz * seq, GROUP), outs[1].reshape(bsz * seq, GROUP), outs[2]
    return None, None, outs[0]


def _outproj_kernel(x_ref, ya_ref, yb_ref, hf_ref, hr_ref, mo_ref, of_ref, or_ref, hg_ref,
                    w_ref, g_ref, o_ref):
    acc = _dot(ya_ref[...], w_ref[0]) + _dot(yb_ref[...], w_ref[1])
    yc = _sigmoid(mo_ref[...]) * (hf_ref[...] + hr_ref[...])
    acc = acc + _dot(yc.astype(BF16), w_ref[2])
    o = of_ref[...] + or_ref[...]
    gate = _sigmoid(hg_ref[...])
    parts = []
    for h in range(4):
        cs = slice(h * LANES, (h + 1) * LANES)
        oh = o[:, cs]
        parts.append(oh * lax.rsqrt(jnp.mean(oh * oh, axis=-1, keepdims=True) + EPS) * gate[:, cs])
    yd = jnp.concatenate(parts, axis=1)
    acc = acc + _dot(yd.astype(BF16), w_ref[3])
    o_ref[...] = x_ref[...] + g_ref[...] * acc


def _out_proj(x, ya, yb, hf, hr, of, orv, p, w_all, layer, gate, rows_per_mod):
    t, d = x.shape
    tm = _row_tile(rows_per_mod, 512)
    per = rows_per_mod // tm
    grp = lambda cb=0: pl.BlockSpec((tm, GROUP), lambda i: (i, cb))
    return pl.pallas_call(
        _outproj_kernel,
        grid=(t // tm,),
        in_specs=[pl.BlockSpec((tm, d), lambda i: (i, 0)), grp(), grp(), grp(), grp(),
                  grp(C_MLO // GROUP), grp(), grp(), grp(C_HGG // GROUP),
                  pl.BlockSpec((None, 4, GROUP, d), lambda i: (layer, 0, 0, 0)),
                  pl.BlockSpec((None, 1, d), lambda i: (i // per, 0, 0))],
        out_specs=pl.BlockSpec((tm, d), lambda i: (i, 0)),
        out_shape=jax.ShapeDtypeStruct((t, d), F32),
        compiler_params=_cparams(("parallel",)),
    )(x, ya, yb, hf, hr, p, of, orv, p, w_all, gate)


def _ffn_kernel(x_ref, xp_ref, xn_ref, g_ref, sh_ref, sc_ref, gate_ref, wa_ref, wu_ref, wc_ref, wd_ref,
                o_ref, xs_ref, acc_ref, *, per):
    i = pl.program_id(0)
    j = pl.program_id(1)
    tm = x_ref.shape[0]

    @pl.when(j == 0)
    def _():
        nm = lambda x: _norm_mod(x, g_ref[...], sh_ref[...], sc_ref[...])
        keep_p = jnp.where(i % per == 0, 0.0, 1.0)
        keep_n = jnp.where(i % per == per - 1, 0.0, 1.0)
        xs_ref[0:HALO, :] = (nm(xp_ref[...]) * keep_p).astype(BF16)
        xs_ref[HALO:HALO + tm, :] = nm(x_ref[...]).astype(BF16)
        xs_ref[HALO + tm:, :] = (nm(xn_ref[...]) * keep_n).astype(BF16)
        acc_ref[...] = jnp.zeros_like(acc_ref)

    a = _dot(xs_ref[...], wa_ref[...])
    rows = a.shape[0]
    prv = pltpu.roll(a, 1, axis=0)[HALO:HALO + tm]
    nxt = pltpu.roll(a, rows - 1, axis=0)[HALO:HALO + tm]
    conv = prv * wc_ref[0:1, :] + a[HALO:HALO + tm] * wc_ref[1:2, :] + nxt * wc_ref[2:3, :]
    u = _dot(xs_ref[HALO:HALO + tm, :], wu_ref[...])
    act = (conv * _sigmoid(conv) * u).astype(BF16)
    acc_ref[...] += _dot(act, wd_ref[...])

    @pl.when(j == pl.num_programs(1) - 1)
    def _():
        o_ref[...] = x_ref[...] + gate_ref[...] * acc_ref[...]


def _ffn(x, gain, shift, scale, gate, w_up, w_conv, w_down, layer, rows_per_mod):
    t, d = x.shape
    f = w_down.shape[1]
    tm = _row_tile(rows_per_mod, 512)
    tf = 512 if f % 512 == 0 else f
    per = rows_per_mod // tm
    nf = f // tf
    step = tm // HALO
    nblk8 = t // HALO
    mod = pl.BlockSpec((None, 1, d), lambda i, j: (i // per, 0, 0))
    return pl.pallas_call(
        functools.partial(_ffn_kernel, per=per),
        grid=(t // tm, nf),
        in_specs=[pl.BlockSpec((tm, d), lambda i, j: (i, 0)),
                  pl.BlockSpec((HALO, d), lambda i, j: (jnp.maximum(i * step - 1, 0), 0)),
                  pl.BlockSpec((HALO, d), lambda i, j: (jnp.minimum((i + 1) * step, nblk8 - 1), 0)),
                  pl.BlockSpec((1, d), lambda i, j: (0, 0)), mod, mod, mod,
                  pl.BlockSpec((None, d, tf), lambda i, j: (layer, 0, j)),
                  pl.BlockSpec((None, d, tf), lambda i, j: (layer, 0, nf + j)),
                  pl.BlockSpec((None, 3, tf), lambda i, j: (layer, 0, j)),
                  pl.BlockSpec((None, tf, d), lambda i, j: (layer, j, 0))],
        out_specs=pl.BlockSpec((tm, d), lambda i, j: (i, 0)),
        out_shape=jax.ShapeDtypeStruct((t, d), F32),
        scratch_shapes=[pltpu.VMEM((tm + 2 * HALO, d), BF16), pltpu.VMEM((tm, d), F32)],
        compiler_params=_cparams(("parallel", "arbitrary")),
    )(x, x, x, gain, shift, scale, gate, w_up, w_up, w_conv, w_down)


def _perm_w_in(w):
    n, d = w.shape[:2]
    swq = w[..., 1536:2048].reshape(n, d, 2, 4, ATT_HD).transpose(0, 1, 3, 2, 4).reshape(n, d, GROUP)
    pad = jnp.zeros((n, d, LANES - 8), w.dtype)
    cols = [w[..., 0:1536], swq, w[..., 2304:3840], w[..., 3856:6416], w[..., 2048:2304],
            w[..., 3840:3848], pad, w[..., 3848:3856], pad]
    out = jnp.concatenate(cols, axis=-1).astype(BF16)
    assert out.shape[-1] == NP
    return out


def _perm_w_out(w):
    n, _, d = w.shape
    w4 = w.reshape(n, 4, GROUP, d)
    swo = w4[:, 1].reshape(n, 2, 4, ATT_HD, d).transpose(0, 2, 1, 3, 4).reshape(n, GROUP, d)
    return jnp.stack([w4[:, 0], swo, w4[:, 2], w4[:, 3]], axis=1).astype(BF16)


def _rope_tables(seq):
    t = jnp.arange(seq)
    half = ATT_HD // 2
    inv = ROPE_BASE ** (-jnp.arange(0, half, 2, dtype=F32) / half)
    ang_r = (t // GRID_W).astype(F32)[:, None] * inv
    ang_c = (t % GRID_W).astype(F32)[:, None] * inv
    cr, sr, cc, sc = jnp.cos(ang_r), jnp.sin(ang_r), jnp.cos(ang_c), jnp.sin(ang_c)
    cos = jnp.concatenate([cr, cr, cc, cc] * 2, axis=-1)
    sin = jnp.concatenate([-sr, sr, -sc, sc] * 2, axis=-1)
    return cos, sin


def kernel(x, c, ctx, c_ctx, w_mod, b_mod, norm_mix, norm_ffn, w_in, w_out, na_qk_gain, na_rpb,
           sw_qk_gain, sw_sink, ml_conv, ml_gate_bias, hg_lb, ffn_up, ffn_conv, ffn_down):
    bsz, seq, d = x.shape
    nctx = ctx.shape[1]
    depth = w_mod.shape[0]
    ml_chunk = math.gcd(ML_CHUNK, math.gcd(seq, nctx))
    chunk = math.gcd(HG_CHUNK, math.gcd(seq, nctx))
    assert bsz + 1 <= 8

    c_all = jnp.concatenate([c, c_ctx[None], jnp.zeros((8 - bsz - 1, d), F32)], axis=0)
    mod = _modulation(c_all, w_mod, b_mod).reshape(depth, 8, 6, d)
    cos, sin = _rope_tables(seq)
    bd = jnp.asarray(np.kron(np.eye(GROUP // ATT_HD), np.ones((ATT_HD, ATT_HD))), BF16)
    lb_cum = jnp.cumsum(jax.nn.softmax(hg_lb.astype(F32), axis=0), axis=0)
    lower = lb_cum - lb_cum[:1]
    consts = _hgrn_constants(chunk)
    w_in_p = _perm_w_in(w_in)
    w_out_p = _perm_w_out(w_out)
    w_up = ffn_up.astype(BF16)
    w_dn = ffn_down.astype(BF16)
    w_cv = ffn_conv.astype(F32)

    h = x.reshape(bsz * seq, d)
    hc = ctx.reshape(bsz * nctx, d)
    for l in range(depth):
        emit_ctx = l < depth - 1
        lat = lambda k: mod[l, :bsz, k].reshape(bsz, 1, d)
        cx = lambda k: jnp.broadcast_to(mod[l, bsz, k].reshape(1, 1, d), (bsz, 1, d))
        gain = norm_mix[l].reshape(1, d)
        p_lat = _in_proj(h, gain, lat(0), lat(1), w_in_p, l, seq)
        p_ctx = _in_proj(hc, gain, cx(0), cx(1), w_in_p, l, nctx)

        gna = jnp.tile(na_qk_gain[l].astype(F32), (1, GROUP // ATT_HD))
        gsw = jnp.tile(sw_qk_gain[l].astype(F32), (1, GROUP // ATT_HD))
        naq, nak, nav, swq, swk, swv = _attn_prep(p_lat, gna, gsw, bd, cos, sin, seq, True)
        cnaq, cnak, cnav, cswq, cswk, cswv = _attn_prep(p_ctx, gna, gsw, bd, cos, sin, nctx, False)
        sink = sw_sink[l].astype(F32)
        ya = _na_attention(naq, nak, nav, cnak, cnav, _na_bias_table(na_rpb[l], seq // GRID_W), bsz, seq, nctx)
        yb = _sw_attention(swq, swk, swv, cswk, cswv, sink, bsz, seq, nctx)

        qk_lat = _ml_prep(p_lat, ml_conv[l].astype(F32), seq)
        qk_ctx = _ml_prep(p_ctx, ml_conv[l].astype(F32), nctx)
        gb = ml_gate_bias[l].astype(F32).reshape(2, 8)
        brow = jnp.zeros((8, LANES), F32).at[0:2, 0:8].set(gb)
        bcol = jnp.zeros((8, LANES), F32).at[:, 0:2].set(gb.T)
        ml0 = (jnp.zeros((bsz, 2, 2, LANES, LANES), F32), jnp.zeros((bsz, 2, 2, 8, LANES), F32),
               jnp.zeros((bsz, 2, 8, LANES), F32))
        hcf, hcr, ml_state = _mlstm_scan(qk_ctx, p_ctx, brow, bcol, ml0, bsz, nctx, ml_chunk, emit_ctx)
        hf, hr, _ = _mlstm_scan(qk_lat, p_lat, brow, bcol, ml_state, bsz, seq, ml_chunk, True)

        lb = lower[l]
        lbc = jnp.stack([jnp.maximum(jnp.log(lb), NEG), jnp.log1p(-lb), 1.0 - lb], axis=1)
        lbc = jnp.concatenate([lbc, jnp.zeros((2, 5, GROUP), F32)], axis=1)
        hg0 = jnp.zeros((bsz, 2, 4, LANES, LANES), F32)
        ocf, ocr, hg_state = _hgrn_scan(p_ctx, lbc, consts, hg0, bsz, nctx, chunk, emit_ctx)
        of, orv, _ = _hgrn_scan(p_lat, lbc, consts, hg_state, bsz, seq, chunk, True)

        gain2 = norm_ffn[l].reshape(1, d)
        h = _out_proj(h, ya, yb, hf, hr, of, orv, p_lat, w_out_p, l, lat(2), seq)
        h = _ffn(h, gain2, lat(3), lat(4), lat(5), w_up, w_cv, w_dn, l, seq)
        if emit_ctx:
            yac, ybc = _ctx_attention(cnaq, cnak, cnav, cswq, cswk, cswv, sink, bsz, nctx)
            hc = _out_proj(hc, yac, ybc, hcf, hcr, ocf, ocr, p_ctx, w_out_p, l, cx(2), nctx)
            hc = _ffn(hc, gain2, cx(3), cx(4), cx(5), w_up, w_cv, w_dn, l, nctx)
    return h.reshape(bsz, seq, d)
```

```python
import functools
import math

import numpy as np
import jax
import jax.numpy as jnp
from jax import lax
from jax.experimental import pallas as pl
from jax.experimental.pallas import tpu as pltpu

F32 = jnp.float32
BF16 = jnp.bfloat16

GRID_W = 64
ATT_HD = 64
NA_ROWS = 8
NA_COLS = 16
SW_BLOCK = 128
ROPE_BASE = 10000.0
EPS = 1e-6
NEG = -1e30
GROUP = 512
LANES = 128
HALO = 8
VMEM_LIMIT = 56 * 1024 * 1024

C_NAQ, C_NAK, C_NAV, C_SWQ = 0, 512, 1024, 1536
C_MLQK, C_MLV, C_MLO = 2048, 2560, 3072
C_HGQ, C_HGI, C_HGF, C_HGG = 3584, 4096, 4608, 5632
C_SWK, C_SWV = 6144, 6272
C_MLI, C_MLF = 6400, 6528
NP = 6656
ML_CHUNK = 256
ML_STEP_ROWS = 1024
HG_CHUNK = 64
SW_PERM = (0, 4, 1, 5, 2, 6, 3, 7)


def _cparams(sem):
    return pltpu.CompilerParams(dimension_semantics=sem, vmem_limit_bytes=VMEM_LIMIT)


def _sigmoid(x):
    return 1.0 / (1.0 + jnp.exp(-x))


def _log1p_unit(e):
    return jnp.log(1.0 + e)


def _log_sigmoid(x):
    return jnp.minimum(x, 0.0) - _log1p_unit(jnp.exp(-jnp.abs(x)))


def _split3(x):
    hi = x.astype(BF16)
    r = x - hi.astype(F32)
    mid = r.astype(BF16)
    lo = (r - mid.astype(F32)).astype(BF16)
    return hi, mid, lo


def _dot(a, b):
    return jnp.dot(a, b, preferred_element_type=F32)


def _dot_nt(a, b):
    return lax.dot_general(a, b, (((1,), (1,)), ((), ())), preferred_element_type=F32)


def _dot_tn(a, b):
    return lax.dot_general(a, b, (((0,), (0,)), ((), ())), preferred_element_type=F32)


def _exact_left(m01, x):
    hi, mid, lo = _split3(x)
    return _dot(m01, hi) + _dot(m01, mid) + _dot(m01, lo)


def _exact_right(x, m01):
    hi, mid, lo = _split3(x)
    return _dot(hi, m01) + _dot(mid, m01) + _dot(lo, m01)


def _mod_kernel(c_ref, w_ref, b_ref, o_ref):
    c = c_ref[...]
    a = c * _sigmoid(c)
    a_hi = a.astype(BF16)
    a_lo = (a - a_hi.astype(F32)).astype(BF16)
    w = w_ref[...]
    w_hi = w.astype(BF16)
    w_lo = (w - w_hi.astype(F32)).astype(BF16)
    o_ref[...] = _dot(a_hi, w_hi) + _dot(a_hi, w_lo) + _dot(a_lo, w_hi) + b_ref[...]


def _modulation(c_all, w_mod, b_mod):
    depth, d, n = w_mod.shape
    tn = 1024 if n % 1024 == 0 else n
    return pl.pallas_call(
        _mod_kernel,
        grid=(depth, n // tn),
        in_specs=[pl.BlockSpec((8, d), lambda l, j: (0, 0)),
                  pl.BlockSpec((None, d, tn), lambda l, j: (l, 0, j)),
                  pl.BlockSpec((None, 1, tn), lambda l, j: (l, 0, j))],
        out_specs=pl.BlockSpec((None, 8, tn), lambda l, j: (l, 0, j)),
        out_shape=jax.ShapeDtypeStruct((depth, 8, n), F32),
        compiler_params=_cparams(("parallel", "parallel")),
    )(c_all, w_mod, b_mod.reshape(depth, 1, n))


def _norm_mod(x, gain, shift, scale):
    y = x * lax.rsqrt(jnp.mean(x * x, axis=-1, keepdims=True) + EPS) * gain
    return y * (1.0 + scale) + shift


def _inproj_kernel(x_ref, g_ref, sh_ref, sc_ref, w_ref, o_ref, xn_ref):
    @pl.when(pl.program_id(1) == 0)
    def _():
        xn_ref[...] = _norm_mod(x_ref[...], g_ref[...], sh_ref[...], sc_ref[...]).astype(BF16)

    o_ref[...] = _dot(xn_ref[...], w_ref[...])


def _row_tile(rows_per_mod, pref):
    tm = min(pref, rows_per_mod)
    assert rows_per_mod % tm == 0
    return tm


def _in_proj(x, gain, shift, scale, w_all, layer, rows_per_mod):
    t, d = x.shape
    n = w_all.shape[2]
    tm = _row_tile(rows_per_mod, 1024)
    tn = 512 if n % 512 == 0 else n
    per = rows_per_mod // tm
    return pl.pallas_call(
        _inproj_kernel,
        grid=(t // tm, n // tn),
        in_specs=[pl.BlockSpec((tm, d), lambda i, j: (i, 0)),
                  pl.BlockSpec((1, d), lambda i, j: (0, 0)),
                  pl.BlockSpec((None, 1, d), lambda i, j: (i // per, 0, 0)),
                  pl.BlockSpec((None, 1, d), lambda i, j: (i // per, 0, 0)),
                  pl.BlockSpec((None, d, tn), lambda i, j: (layer, 0, j))],
        out_specs=pl.BlockSpec((tm, tn), lambda i, j: (i, j)),
        out_shape=jax.ShapeDtypeStruct((t, n), F32),
        scratch_shapes=[pltpu.VMEM((tm, d), BF16)],
        compiler_params=_cparams(("parallel", "arbitrary")),
    )(x, gain, shift, scale, w_all)


def _head_rms(x, gain_row, bd):
    xx = x * x
    hi = xx.astype(BF16)
    lo = (xx - hi.astype(F32)).astype(BF16)
    ss = _dot(hi, bd) + _dot(lo, bd)
    return x * lax.rsqrt(ss * (1.0 / ATT_HD) + EPS) * gain_row


def _rope(x, cos, sin, first):
    w = x.shape[-1]
    nxt = pltpu.roll(x, w - 16, axis=1)
    prv = pltpu.roll(x, 16, axis=1)
    return x * cos + jnp.where(first, nxt, prv) * sin


def _attn_prep_kernel(p_ref, pkv_ref, gna_ref, gsw_ref, bd_ref, cos_ref, sin_ref,
                      naq_ref, nak_ref, nav_ref, swq_ref, swk_ref, swv_ref, *, rope):
    bd = bd_ref[...]
    scale = ATT_HD ** -0.5
    naq_ref[...] = (_head_rms(p_ref[:, C_NAQ:C_NAQ + GROUP], gna_ref[0:1, :], bd) * scale).astype(BF16)
    nak_ref[...] = _head_rms(p_ref[:, C_NAK:C_NAK + GROUP], gna_ref[1:2, :], bd).astype(BF16)
    nav_ref[...] = p_ref[:, C_NAV:C_NAV + GROUP].astype(BF16)
    q = _head_rms(p_ref[:, C_SWQ:C_SWQ + GROUP], gsw_ref[0:1, :], bd)
    k = _head_rms(pkv_ref[:, 0:LANES], gsw_ref[1:2, 0:LANES], bd[0:LANES, 0:LANES])
    if rope:
        cos = cos_ref[...]
        sin = sin_ref[...]
        lane = lax.broadcasted_iota(jnp.int32, (1, GROUP), 1)
        first = (lane % 32) < 16
        q = _rope(q, jnp.concatenate([cos] * 4, axis=1), jnp.concatenate([sin] * 4, axis=1), first)
        k = _rope(k, cos, sin, first[:, 0:LANES])
    swq_ref[...] = (q * scale).astype(BF16)
    swk_ref[...] = k.astype(BF16)
    swv_ref[...] = pkv_ref[:, LANES:2 * LANES].astype(BF16)


def _attn_prep(p, gna, gsw, bd, cos, sin, seq, rope):
    t = p.shape[0]
    tm = _row_tile(seq, 512)
    per = seq // tm
    out = lambda w: jax.ShapeDtypeStruct((t, w), BF16)
    ospec = lambda w: pl.BlockSpec((tm, w), lambda i: (i, 0))
    return pl.pallas_call(
        functools.partial(_attn_prep_kernel, rope=rope),
        grid=(t // tm,),
        in_specs=[pl.BlockSpec((tm, C_MLQK), lambda i: (i, 0)),
                  pl.BlockSpec((tm, 2 * LANES), lambda i: (i, C_SWK // (2 * LANES))),
                  pl.BlockSpec((2, GROUP), lambda i: (0, 0)),
                  pl.BlockSpec((2, GROUP), lambda i: (0, 0)),
                  pl.BlockSpec((GROUP, GROUP), lambda i: (0, 0)),
                  pl.BlockSpec((tm, LANES), lambda i: (i % per, 0)),
                  pl.BlockSpec((tm, LANES), lambda i: (i % per, 0))],
        out_specs=[ospec(GROUP), ospec(GROUP), ospec(GROUP), ospec(GROUP), ospec(LANES), ospec(LANES)],
        out_shape=[out(GROUP), out(GROUP), out(GROUP), out(GROUP), out(LANES), out(LANES)],
        compiler_params=_cparams(("parallel",)),
    )(p, p, gna, gsw, bd, cos, sin)


def _pair_rows(q_pair):
    lane = lax.broadcasted_iota(jnp.int32, q_pair.shape, 1)
    zero = jnp.zeros_like(q_pair)
    return jnp.concatenate([jnp.where(lane < ATT_HD, q_pair, zero),
                            jnp.where(lane >= ATT_HD, q_pair, zero)], axis=0)


def _pair_merge(o2, inv_l):
    m = o2.shape[0] // 2
    lane = lax.broadcasted_iota(jnp.int32, (m, LANES), 1)
    return jnp.where(lane < ATT_HD, o2[0:m] * inv_l[0:m], o2[m:] * inv_l[m:])


NA_RPS = 4


def _softmax_pv(parts, sink=None):
    m = functools.reduce(jnp.maximum, [jnp.max(s, axis=-1, keepdims=True) for s, _ in parts])
    if sink is not None:
        m = jnp.maximum(m, sink)
    acc = None
    for s, v in parts:
        e = jnp.exp((s - m).astype(BF16))
        o = _dot(e, jnp.concatenate([v, jnp.ones_like(v)], axis=1))
        acc = o if acc is None else acc + o
    l = acc[:, LANES:LANES + 1]
    if sink is not None:
        l = l + jnp.exp(sink - m)
    return acc[:, 0:LANES], l


NA_WIN = NA_ROWS + NA_RPS


def _na_window(step, rows):
    return jnp.clip(step * NA_RPS - NA_ROWS // 2, 0, rows - NA_WIN)


def _na_kernel(q_ref, k_ref, v_ref, kc_ref, vc_ref, b_ref, o_ref, *, rows):
    start = pl.multiple_of(_na_window(pl.program_id(1), rows) * GRID_W, GRID_W)
    nkey = NA_WIN * GRID_W
    for p in range(GROUP // LANES):
        cs = slice(p * LANES, (p + 1) * LANES)
        qs = jnp.concatenate([_pair_rows(q_ref[rr * GRID_W:(rr + 1) * GRID_W, cs]) for rr in range(NA_RPS)],
                             axis=0)
        kw = k_ref[pl.ds(start, nkey), cs]
        vw = v_ref[pl.ds(start, nkey), cs]
        s_nb = _dot_nt(qs, kw) + b_ref[p]
        s_cx = _dot_nt(qs, kc_ref[:, cs])
        o2, l = _softmax_pv([(s_nb, vw), (s_cx, vc_ref[:, cs])])
        inv_l = 1.0 / l
        for rr in range(NA_RPS):
            sl = slice(rr * 2 * GRID_W, (rr + 1) * 2 * GRID_W)
            o_ref[rr * GRID_W:(rr + 1) * GRID_W, cs] = _pair_merge(o2[sl], inv_l[sl]).astype(BF16)


def _na_attention(q, k, v, kc, vc, bias, bsz, seq, ctx):
    rows = seq // GRID_W
    assert rows >= NA_WIN and rows % NA_RPS == 0
    steps = rows // NA_RPS
    k3, v3 = k.reshape(bsz, seq, GROUP), v.reshape(bsz, seq, GROUP)
    kc3, vc3 = kc.reshape(bsz, ctx, GROUP), vc.reshape(bsz, ctx, GROUP)
    qspec = pl.BlockSpec((NA_RPS * GRID_W, GROUP), lambda b, r: (b * steps + r, 0))
    variant = lambda b, r: ((r > 0).astype(jnp.int32) + (r == steps - 1).astype(jnp.int32), 0, 0, 0)
    return pl.pallas_call(
        functools.partial(_na_kernel, rows=rows),
        grid=(bsz, steps),
        in_specs=[qspec,
                  pl.BlockSpec((None, seq, GROUP), lambda b, r: (b, 0, 0)),
                  pl.BlockSpec((None, seq, GROUP), lambda b, r: (b, 0, 0)),
                  pl.BlockSpec((None, ctx, GROUP), lambda b, r: (b, 0, 0)),
                  pl.BlockSpec((None, ctx, GROUP), lambda b, r: (b, 0, 0)),
                  pl.BlockSpec((None,) + bias.shape[1:], variant)],
        out_specs=qspec,
        out_shape=jax.ShapeDtypeStruct((bsz * seq, GROUP), BF16),
        compiler_params=_cparams(("parallel", "arbitrary")),
    )(q, k3, v3, kc3, vc3, bias)


def _na_bias_table(rpb, rows):
    nh = rpb.shape[0]
    steps = rows // NA_RPS
    c = np.arange(GRID_W)
    kc = np.arange(GRID_W)
    cstart = np.clip(c - NA_COLS // 2, 0, GRID_W - NA_COLS)
    ok = (kc[None, :] >= cstart[:, None]) & (kc[None, :] < cstart[:, None] + NA_COLS)
    dc = np.clip(kc[None, :] - c[:, None] + NA_COLS - 1, 0, 2 * NA_COLS - 2)
    t = jnp.where(ok[None, None], rpb.astype(F32)[:, :, dc], NEG)

    def step_layout(s):
        ws = int(np.clip(s * NA_RPS - NA_ROWS // 2, 0, rows - NA_WIN))
        r = s * NA_RPS + np.arange(NA_RPS)[:, None]
        rs = np.clip(r - NA_ROWS // 2, 0, rows - NA_ROWS)
        krow = ws + np.arange(NA_WIN)[None, :]
        return (krow - r + NA_ROWS - 1), (krow >= rs) & (krow < rs + NA_ROWS)

    layouts = [step_layout(s) for s in range(steps)]
    variants = [layouts[0], layouts[1], layouts[-1]]
    for s, (idx, valid) in enumerate(layouts):
        want = variants[(s > 0) + (s == steps - 1)]
        assert (valid == want[1]).all() and (idx[valid] == want[0][want[1]]).all()

    ndr = 2 * NA_ROWS - 1
    width = (ndr + 3) * GRID_W
    t2 = jnp.transpose(t, (0, 2, 1, 3)).reshape(nh, GRID_W, ndr * GRID_W)
    t2 = jnp.pad(t2, ((0, 0), (0, 0), (GRID_W, width - (ndr + 1) * GRID_W)), constant_values=NEG)
    t2s = jnp.pad(t2[:, :, GRID_W:], ((0, 0), (0, 0), (0, GRID_W)), constant_values=NEG)

    def build(ta_ref, tb_ref, o_ref):
        lane = lax.broadcasted_iota(jnp.int32, (GRID_W, LANES), 1)
        neg = jnp.full((GRID_W, LANES), NEG, F32)
        for v, (idx, valid) in enumerate(variants):
            for rr in range(NA_RPS):
                for e in range(2):
                    row0 = (rr * 2 + e) * GRID_W
                    for jj in range(NA_WIN // 2):
                        j = 2 * jj
                        ok0, ok1 = bool(valid[rr, j]), bool(valid[rr, j + 1])
                        if not (ok0 or ok1):
                            piece = neg
                        else:
                            blk = int(idx[rr, j] if ok0 else idx[rr, j + 1] - 1) + 1
                            src = ta_ref if blk % 2 == 0 else tb_ref
                            off = (blk - blk % 2) * GRID_W
                            piece = src[e, :, off:off + LANES]
                            if not ok0:
                                piece = jnp.where(lane >= GRID_W, piece, NEG)
                            if not ok1:
                                piece = jnp.where(lane < GRID_W, piece, NEG)
                        o_ref[v, row0:row0 + GRID_W, j * GRID_W:(j + 2) * GRID_W] = piece

    npair = nh // 2
    tspec = pl.BlockSpec((2, GRID_W, width), lambda p: (p, 0, 0))
    return pl.pallas_call(
        build,
        grid=(npair,),
        in_specs=[tspec, tspec],
        out_specs=pl.BlockSpec((3, None, NA_RPS * 2 * GRID_W, NA_WIN * GRID_W), lambda p: (0, p, 0, 0)),
        out_shape=jax.ShapeDtypeStruct((3, npair, NA_RPS * 2 * GRID_W, NA_WIN * GRID_W), F32),
        compiler_params=_cparams(("parallel",)),
    )(t2, t2s)


SW_BPS = 2


def _sw_kernel(sink_ref, q_ref, k_ref, v_ref, kc_ref, vc_ref, o_ref, *, seq):
    nq = SW_BPS * SW_BLOCK
    nwin = (SW_BPS + 2) * SW_BLOCK
    q0 = pl.program_id(1) * nq
    start = pl.multiple_of(jnp.clip(q0 - SW_BLOCK, 0, seq - nwin), SW_BLOCK)
    kw = k_ref[pl.ds(start, nwin), :]
    vw = v_ref[pl.ds(start, nwin), :]
    kc = kc_ref[...]
    vc = vc_ref[...]
    row = lax.broadcasted_iota(jnp.int32, (2 * nq, nwin), 0)
    col = lax.broadcasted_iota(jnp.int32, (2 * nq, nwin), 1)
    ok = jnp.abs((start + col) - (q0 + row % nq)) <= SW_BLOCK
    half = lax.broadcasted_iota(jnp.int32, (2 * nq, 1), 0) < nq
    for j in range(GROUP // LANES):
        cs = slice(j * LANES, (j + 1) * LANES)
        qs = _pair_rows(q_ref[:, cs])
        s_bd = jnp.where(ok, _dot_nt(qs, kw), NEG)
        s_cx = _dot_nt(qs, kc)
        sink = jnp.where(half, sink_ref[j], sink_ref[j + 4])
        o2, l = _softmax_pv([(s_bd, vw), (s_cx, vc)], sink)
        o_ref[:, cs] = _pair_merge(o2, 1.0 / l).astype(BF16)


def _sw_attention(q, k, v, kc, vc, sink, bsz, seq, ctx):
    assert seq % (SW_BPS * SW_BLOCK) == 0 and seq >= (SW_BPS + 2) * SW_BLOCK
    nb = seq // (SW_BPS * SW_BLOCK)
    k3, v3 = k.reshape(bsz, seq, LANES), v.reshape(bsz, seq, LANES)
    kc3, vc3 = kc.reshape(bsz, ctx, LANES), vc.reshape(bsz, ctx, LANES)
    return pl.pallas_call(
        functools.partial(_sw_kernel, seq=seq),
        grid=(bsz, nb),
        in_specs=[pl.BlockSpec(memory_space=pltpu.SMEM),
                  pl.BlockSpec((SW_BPS * SW_BLOCK, GROUP), lambda b, n: (b * nb + n, 0)),
                  pl.BlockSpec((None, seq, LANES), lambda b, n: (b, 0, 0)),
                  pl.BlockSpec((None, seq, LANES), lambda b, n: (b, 0, 0)),
                  pl.BlockSpec((None, ctx, LANES), lambda b, n: (b, 0, 0)),
                  pl.BlockSpec((None, ctx, LANES), lambda b, n: (b, 0, 0))],
        out_specs=pl.BlockSpec((SW_BPS * SW_BLOCK, GROUP), lambda b, n: (b * nb + n, 0)),
        out_shape=jax.ShapeDtypeStruct((bsz * seq, GROUP), BF16),
        compiler_params=_cparams(("parallel", "arbitrary")),
    )(sink, q, k3, v3, kc3, vc3)


def _ctx_attn_kernel(sink_ref, qa_ref, ka_ref, va_ref, qb_ref, kb_ref, vb_ref, oa_ref, ob_ref):
    for p in range(GROUP // LANES):
        cs = slice(p * LANES, (p + 1) * LANES)
        qs = _pair_rows(qa_ref[:, cs])
        o2, l = _softmax_pv([(_dot_nt(qs, ka_ref[:, cs]), va_ref[:, cs])])
        oa_ref[:, cs] = _pair_merge(o2, 1.0 / l).astype(BF16)
    ctx = qb_ref.shape[0]
    half = lax.broadcasted_iota(jnp.int32, (2 * ctx, 1), 0) < ctx
    for j in range(GROUP // LANES):
        cs = slice(j * LANES, (j + 1) * LANES)
        qs = _pair_rows(qb_ref[:, cs])
        sink = jnp.where(half, sink_ref[j], sink_ref[j + 4])
        o2, l = _softmax_pv([(_dot_nt(qs, kb_ref[...]), vb_ref[...])], sink)
        ob_ref[:, cs] = _pair_merge(o2, 1.0 / l).astype(BF16)


def _ctx_attention(qa, ka, va, qb, kb, vb, sink, bsz, ctx):
    big = pl.BlockSpec((ctx, GROUP), lambda b: (b, 0))
    small = pl.BlockSpec((ctx, LANES), lambda b: (b, 0))
    return pl.pallas_call(
        _ctx_attn_kernel,
        grid=(bsz,),
        in_specs=[pl.BlockSpec(memory_space=pltpu.SMEM), big, big, big, big, small, small],
        out_specs=[big, big],
        out_shape=[jax.ShapeDtypeStruct((bsz * ctx, GROUP), BF16)] * 2,
        compiler_params=_cparams(("parallel",)),
    )(sink, qa, ka, va, qb, kb, vb)


def _halo_specs(tm, width, col_block, nblk8):
    step = tm // HALO
    return [pl.BlockSpec((tm, width), lambda i: (i, col_block)),
            pl.BlockSpec((HALO, width), lambda i: (jnp.maximum(i * step - 1, 0), col_block)),
            pl.BlockSpec((HALO, width), lambda i: (jnp.minimum((i + 1) * step, nblk8 - 1), col_block))]


def _ml_prep_kernel(x_ref, xp_ref, xn_ref, w_ref, o_ref, *, per):
    i = pl.program_id(0)
    x = x_ref[...]
    tm = x.shape[0]
    row = lax.broadcasted_iota(jnp.int32, (tm, 1), 0)
    keep_p = jnp.where(i % per == 0, 0.0, 1.0)
    keep_n = jnp.where(i % per == per - 1, 0.0, 1.0)
    prv = jnp.where(row == 0, xp_ref[HALO - 1:HALO, :] * keep_p, pltpu.roll(x, 1, axis=0))
    nxt = jnp.where(row == tm - 1, xn_ref[0:1, :] * keep_n, pltpu.roll(x, tm - 1, axis=0))
    y = prv * w_ref[0:1, :] + x * w_ref[1:2, :] + nxt * w_ref[2:3, :]
    y = y * _sigmoid(y)
    lane = lax.broadcasted_iota(jnp.int32, (1, GROUP), 1)
    o_ref[...] = (y * jnp.where(lane < GROUP // 2, ATT_HD ** -0.5, 1.0)).astype(BF16)


def _ml_prep(p, conv_w, seq):
    t = p.shape[0]
    tm = _row_tile(seq, 512)
    per = seq // tm
    return pl.pallas_call(
        functools.partial(_ml_prep_kernel, per=per),
        grid=(t // tm,),
        in_specs=_halo_specs(tm, GROUP, C_MLQK // GROUP, t // HALO)
        + [pl.BlockSpec((3, GROUP), lambda i: (0, 0))],
        out_specs=pl.BlockSpec((tm, GROUP), lambda i: (i, 0)),
        out_shape=jax.ShapeDtypeStruct((t, GROUP), BF16),
        compiler_params=_cparams(("parallel",)),
    )(p, p, p, conv_w)


def _mlstm_kernel(*refs, chunk, emit):
    (qkf_ref, vf_ref, gif_ref, gff_ref, qkr_ref, vr_ref, gir_ref, gfr_ref,
     brow_ref, bcol_ref, c0_ref, n0_ref, m0_ref) = refs[:13]
    if emit:
        hf_ref, hr_ref = refs[13:15]
        rest = refs[15:]
    else:
        hf_ref = hr_ref = None
        rest = refs[13:]
    c_out, n_out, m_out, c_s, n_s, m_s = rest
    i = pl.program_id(1)

    @pl.when(i == 0)
    def _():
        c_s[...] = c0_ref[...]
        n_s[...] = n0_ref[...]
        m_s[...] = m0_ref[...]

    row = lax.broadcasted_iota(jnp.int32, (chunk, chunk), 0)
    col = lax.broadcasted_iota(jnp.int32, (chunk, chunk), 1)
    lane = lax.broadcasted_iota(jnp.int32, (chunk, LANES), 1)
    lane1 = lax.broadcasted_iota(jnp.int32, (1, LANES), 1)
    srow = lax.broadcasted_iota(jnp.int32, (LANES, 1), 0)
    dirs = ((qkf_ref, vf_ref, gif_ref, gff_ref, hf_ref), (qkr_ref, vr_ref, gir_ref, gfr_ref, hr_ref))
    chains = [(b, d) for b in range(qkf_ref.shape[0]) for d in range(2)]
    old_c ={(b, d, pp): c_s[b, d, pp] for b, d in chains for pp in range(2)}
    old_n = {(b, d, pp): n_s[b, d, pp, 0:1, :] for b, d in chains for pp in range(2)}
    old_m = {(b, d): m_s[b, d, 0:1, :] for b, d in chains}
    new_c, new_n, new_m = {}, {}, {}
    for b, d in chains:
        qk_ref, v_ref, gi_ref, gf_ref, h_ref = [r if r is None else r.at[b] for r in dirs[d]]
        causal = (col <= row) if d == 0 else (col >= row)
        causal_t = (row <= col) if d == 0 else (row >= col)
        gi = gi_ref[...]
        gf = gf_ref[...]
        ipre = gi + brow_ref[0:1, :]
        fl = _log_sigmoid(gf + brow_ref[1:2, :])
        fcum = _exact_left(jnp.where(causal, 1.0, 0.0).astype(BF16), fl)
        ftot = jnp.sum(fl, axis=0, keepdims=True)
        m_prev = old_m[b, d]
        w_log = ftot - fcum + ipre
        m_new = jnp.maximum(ftot + m_prev, jnp.max(w_log, axis=0, keepdims=True))
        cd = jnp.exp(ftot + m_prev - m_new)
        w = jnp.exp(w_log - m_new)
        if emit:
            ipre_t = gi.T[0:8, :] + bcol_ref[:, 0:1]
            fl_t = _log_sigmoid(gf.T[0:8, :] + bcol_ref[:, 1:2])
            fcum_t = _exact_right(fl_t, jnp.where(causal_t, 1.0, 0.0).astype(BF16))
            b_rows = ipre_t - fcum_t
            log_inter = fcum + m_prev
        for pp in range(2):
            qp = qk_ref[:, pp * LANES:(pp + 1) * LANES]
            kp = qk_ref[:, GROUP // 2 + pp * LANES:GROUP // 2 + (pp + 1) * LANES]
            ct = old_c[b, d, pp]
            n_row = old_n[b, d, pp]
            inc = None
            for e in range(2):
                h = 2 * pp + e
                c = 4 * d + h
                sel = (lane < ATT_HD) if e == 0 else (lane >= ATT_HD)
                vh = v_ref[:, h * LANES:(h + 1) * LANES]
                if emit:
                    qm = jnp.where(sel, qp, jnp.zeros_like(qp))
                    fc = fcum[:, c:c + 1]
                    li = log_inter[:, c:c + 1]
                    log_d = jnp.where(causal, fc + b_rows[c:c + 1, :], NEG)
                    m_row = jnp.maximum(li, jnp.max(log_d, axis=-1, keepdims=True))
                    d_mat = jnp.exp(log_d - m_row)
                    inter_w = jnp.exp(li - m_row)
                    s_mat = _dot_nt(qm, kp) * d_mat
                    num = inter_w * _dot(qm, ct.astype(BF16)) + _dot(s_mat.astype(BF16), vh.astype(BF16))
                    den = (inter_w * jnp.sum(qm.astype(F32) * n_row, axis=-1, keepdims=True)
                           + jnp.sum(s_mat, axis=-1, keepdims=True))
                    h_ref[:, h * LANES:(h + 1) * LANES] = num / jnp.maximum(jnp.abs(den), jnp.exp(-m_row))
                km = jnp.where(sel, kp, jnp.zeros_like(kp))
                upd = _dot_tn(km, (w[:, c:c + 1] * vh).astype(BF16))
                inc = upd if inc is None else inc + upd
            c_lo, c_hi = 4 * d + 2 * pp, 4 * d + 2 * pp + 1
            new_c[b, d, pp] = (jnp.where(srow < ATT_HD, cd[:, c_lo:c_lo + 1], cd[:, c_hi:c_hi + 1]) * ct
                               + inc)
            w_sel = jnp.where(lane < ATT_HD, w[:, c_lo:c_lo + 1], w[:, c_hi:c_hi + 1])
            cd_sel = jnp.where(lane1 < ATT_HD, cd[:, c_lo:c_lo + 1], cd[:, c_hi:c_hi + 1])
            new_n[b, d, pp] = cd_sel * n_row + jnp.sum(w_sel * kp.astype(F32), axis=0, keepdims=True)
        new_m[b, d] = m_new
    for (b, d, pp), val in new_c.items():
        c_s[b, d, pp] = val
        n_s[b, d, pp, 0:1, :] = new_n[b, d, pp]
    for (b, d), val in new_m.items():
        m_s[b, d, 0:1, :] = val

    @pl.when(i == pl.num_programs(1) - 1)
    def _():
        c_out[...] = c_s[...]
        n_out[...] = n_s[...]
        m_out[...] = m_s[...]


def _mlstm_scan(qk, p, brow, bcol, state, bsz, seq, chunk, emit):
    nc = seq // chunk
    bps = bsz if bsz * chunk <= ML_STEP_ROWS else max(1, ML_STEP_ROWS // chunk)
    assert bsz % bps == 0
    fwd = lambda g, i: i
    rev = lambda g, i: nc - 1 - i
    qk3 = qk.reshape(bsz, seq, GROUP)
    p3 = p.reshape(bsz, seq, NP)

    def chunk_specs(idx):
        return [pl.BlockSpec((bps, chunk, GROUP), lambda g, i: (g, idx(g, i), 0)),
                pl.BlockSpec((bps, chunk, GROUP), lambda g, i: (g, idx(g, i), C_MLV // GROUP)),
                pl.BlockSpec((bps, chunk, LANES), lambda g, i: (g, idx(g, i), C_MLI // LANES)),
                pl.BlockSpec((bps, chunk, LANES), lambda g, i: (g, idx(g, i), C_MLF // LANES))]

    st_dims = [(2, 2, LANES, LANES), (2, 2, 8, LANES), (2, 8, LANES)]
    st_specs = [pl.BlockSpec((bps,) + s, lambda g, i, n=len(s): (g,) + (0,) * n) for s in st_dims]
    st_shapes = [jax.ShapeDtypeStruct((bsz,) + s, F32) for s in st_dims]
    h_specs = [pl.BlockSpec((bps, chunk, GROUP), lambda g, i: (g, fwd(g, i), 0)),
               pl.BlockSpec((bps, chunk, GROUP), lambda g, i: (g, rev(g, i), 0))] if emit else []
    h_shapes = [jax.ShapeDtypeStruct((bsz, seq, GROUP), F32)] * 2 if emit else []
    outs = pl.pallas_call(
        functools.partial(_mlstm_kernel, chunk=chunk, emit=emit),
        grid=(bsz // bps, nc),
        in_specs=chunk_specs(fwd) + chunk_specs(rev)
        + [pl.BlockSpec((8, LANES), lambda g, i: (0, 0)), pl.BlockSpec((8, LANES), lambda g, i: (0, 0))]
        + st_specs,
        out_specs=h_specs + st_specs,
        out_shape=h_shapes + st_shapes,
        scratch_shapes=[pltpu.VMEM((bps,) + s, F32) for s in st_dims],
        compiler_params=_cparams(("parallel", "arbitrary")),
    )(qk3, p3, p3, p3, qk3, p3, p3, p3, brow, bcol, *state)
    if emit:
        return (outs[0].reshape(bsz * seq, GROUP), outs[1].reshape(bsz * seq, GROUP), tuple(outs[2:]))
    return None, None, tuple(outs)


def _hgrn_constants(chunk):
    nlev = int(np.log2(chunk))
    assert 2 ** nlev == chunk
    t = np.arange(chunk)
    mats = [(t[None, :] <= t[:, None]), (t[None, :] > t[:, None])]
    masks = []
    for lev in range(nlev):
        n = chunk >> (lev + 1)
        b0 = t - t % (2 * n)
        upper = (t % (2 * n)) >= n
        m_up = (t[None, :] >= (b0 + n)[:, None]) & (t[None, :] <= t[:, None])
        m_lo = (t[None, :] > t[:, None]) & (t[None, :] <= (b0 + n - 1)[:, None])
        mats.append(np.where(upper[:, None], m_up, m_lo))
        same = (t[:, None] // (2 * n)) == (t[None, :] // (2 * n))
        masks.append(same & upper[:, None] & ~upper[None, :])
    masks.append(t[:, None] == t[None, :])
    mstack = np.concatenate(mats, axis=0).astype(np.float32)
    lmask = np.stack([np.kron(np.eye(2), m.astype(np.float32)) for m in masks])
    nr = mstack.shape[0] // chunk
    mstack_rev = mstack.reshape(nr, chunk, chunk)[:, ::-1, ::-1].reshape(-1, chunk)
    lmask_rev = lmask.reshape(nlev + 1, 2, chunk, 2, chunk)[:, :, ::-1, :, ::-1].reshape(lmask.shape)
    return (jnp.asarray(np.stack([mstack, mstack_rev]), BF16),
            jnp.asarray(np.stack([lmask, lmask_rev]), F32), nlev)


def _stack_pair(x, pp):
    return jnp.concatenate([x[:, (2 * pp) * LANES:(2 * pp + 1) * LANES],
                            x[:, (2 * pp + 1) * LANES:(2 * pp + 2) * LANES]], axis=0)


def _hgrn_kernel(*refs, chunk, nlev, emit):
    (qf_ref, vf_ref, ff_ref, qr_ref, vr_ref, fr_ref, lb_ref, ms_ref, lm_ref, s0_ref) = refs[:10]
    if emit:
        of_ref, or_ref = refs[10:12]
        rest = refs[12:]
    else:
        of_ref = or_ref = None
        rest = refs[10:]
    s_out, s_s = rest
    i = pl.program_id(1)

    @pl.when(i == 0)
    def _():
        s_s[...] = s0_ref[...]

    dirs = ((qf_ref, vf_ref, ff_ref, of_ref), (qr_ref, vr_ref, fr_ref, or_ref))
    chains = [(b, d) for b in range(vf_ref.shape[0]) for d in range(2)]
    old_s = {(b, d, h): s_s[b, d, h] for b, d in chains for h in range(4)}
    new_s = {}

    for b, d in chains:
        q_ref, v_ref, f_ref, o_ref = [r if r is None else r.at[b] for r in dirs[d]]
        fpre = f_ref[...]
        vb = v_ref[...].astype(BF16)
        a = lb_ref[d, 0:1, :]
        bb = lb_ref[d, 1:2, :] + _log_sigmoid(fpre)
        logf = jnp.maximum(a, bb) + _log1p_unit(jnp.exp(-jnp.abs(a - bb)))
        k = lb_ref[d, 2:3, :] * _sigmoid(-fpre)
        hi = logf.astype(BF16)
        lo = (logf - hi.astype(F32)).astype(BF16)
        gtot = jnp.sum(logf, axis=0, keepdims=True)
        if emit:
            qpre = q_ref[...]
            q = qpre * _sigmoid(qpre)
            qb = q.astype(BF16)
            kb = k.astype(BF16)
        ms = ms_ref[d]
        ex = _dot(ms, hi) + _dot(ms, lo)
        ktil = (k * jnp.exp(ex[chunk:2 * chunk])).astype(BF16)
        if emit:
            qe = (q * jnp.exp(ex[0:chunk])).astype(BF16)
            lev_ops = []
            for lev in range(nlev):
                e_l = jnp.exp(ex[(2 + lev) * chunk:(3 + lev) * chunk])
                lev_ops.append(((q * e_l).astype(BF16), (k * e_l).astype(BF16)))
        if emit:
            att = [lm_ref[d, nlev] * _dot_nt(_stack_pair(qb, pp), _stack_pair(kb, pp)) for pp in range(2)]
            for lev, (q_l, k_l) in enumerate(lev_ops):
                for pp in range(2):
                    att[pp] = att[pp] + lm_ref[d, lev] * _dot_nt(_stack_pair(q_l, pp), _stack_pair(k_l, pp))
            o_pair = [_dot(att[pp].astype(BF16), _stack_pair(vb, pp)) for pp in range(2)]
        for h in range(4):
            cs = slice(h * LANES, (h + 1) * LANES)
            s_t = old_s[b, d, h]
            if emit:
                pp, e = divmod(h, 2)
                o_ref[:, cs] = o_pair[pp][e * chunk:(e + 1) * chunk] + _dot_nt(qe[:, cs], s_t.astype(BF16))
            new_s[b, d, h] = s_t * jnp.exp(gtot[:, cs]) + _dot_tn(vb[:, cs], ktil[:, cs])

    for key, val in new_s.items():
        s_s[key] = val

    @pl.when(i == pl.num_programs(1) - 1)
    def _():
        s_out[...] = s_s[...]


def _hgrn_scan(p, lbc, consts, state, bsz, seq, chunk, emit):
    mstack, lmask, nlev = consts
    nc = seq // chunk
    bps = bsz
    fwd = lambda g, i: i
    rev = lambda g, i: nc - 1 - i
    p3 = p.reshape(bsz, seq, NP)

    def chunk_specs(idx, d):
        return [pl.BlockSpec((bps, chunk, GROUP), lambda g, i: (g, idx(g, i), C_HGQ // GROUP)),
                pl.BlockSpec((bps, chunk, GROUP), lambda g, i: (g, idx(g, i), C_HGI // GROUP)),
                pl.BlockSpec((bps, chunk, GROUP), lambda g, i: (g, idx(g, i), C_HGF // GROUP + d))]

    st_spec = pl.BlockSpec((bps, 2, 4, LANES, LANES), lambda g, i: (g, 0, 0, 0, 0))
    st_shape = jax.ShapeDtypeStruct((bsz, 2, 4, LANES, LANES), F32)
    o_specs = [pl.BlockSpec((bps, chunk, GROUP), lambda g, i: (g, fwd(g, i), 0)),
               pl.BlockSpec((bps, chunk, GROUP), lambda g, i: (g, rev(g, i), 0))] if emit else []
    o_shapes = [jax.ShapeDtypeStruct((bsz, seq, GROUP), F32)] * 2 if emit else []
    full = lambda a: pl.BlockSpec(a.shape, lambda g, i: (0,) * a.ndim)
    outs = pl.pallas_call(
        functools.partial(_hgrn_kernel, chunk=chunk, nlev=nlev, emit=emit),
        grid=(bsz // bps, nc),
        in_specs=chunk_specs(fwd, 0) + chunk_specs(rev, 1) + [full(lbc), full(mstack), full(lmask), st_spec],
        out_specs=o_specs + [st_spec],
        out_shape=o_shapes + [st_shape],
        scratch_shapes=[pltpu.VMEM((bps, 2, 4, LANES, LANES), F32)],
        compiler_params=_cparams(("parallel", "arbitrary")),
    )(p3, p3, p3, p3, p3, p3, lbc, mstack, lmask, state)
    if emit:
        return outs[0].reshape(bsz * seq, GROUP), outs[1].reshape(bsz * seq, GROUP), outs[2]
    return None, None, outs[0]


def _outproj_kernel(x_ref, ya_ref, yb_ref, hf_ref, hr_ref, mo_ref, of_ref, or_ref, hg_ref,
                    w_ref, g_ref, o_ref):
    acc = _dot(ya_ref[...], w_ref[0]) + _dot(yb_ref[...], w_ref[1])
    yc = _sigmoid(mo_ref[...]) * (hf_ref[...] + hr_ref[...])
    acc = acc + _dot(yc.astype(BF16), w_ref[2])
    o = of_ref[...] + or_ref[...]
    gate = _sigmoid(hg_ref[...])
    parts = []
    for h in range(4):
        cs = slice(h * LANES, (h + 1) * LANES)
        oh = o[:, cs]
        parts.append(oh * lax.rsqrt(jnp.mean(oh * oh, axis=-1, keepdims=True) + EPS) * gate[:, cs])
    yd = jnp.concatenate(parts, axis=1)
    acc = acc + _dot(yd.astype(BF16), w_ref[3])
    o_ref[...] = x_ref[...] + g_ref[...] * acc


def _out_proj(x, ya, yb, hf, hr, of, orv, p, w_all, layer, gate, rows_per_mod):
    t, d = x.shape
    tm = _row_tile(rows_per_mod, 512)
    per = rows_per_mod // tm
    grp = lambda cb=0: pl.BlockSpec((tm, GROUP), lambda i: (i, cb))
    return pl.pallas_call(
        _outproj_kernel,
        grid=(t // tm,),
        in_specs=[pl.BlockSpec((tm, d), lambda i: (i, 0)), grp(), grp(), grp(), grp(),
                  grp(C_MLO // GROUP), grp(), grp(), grp(C_HGG // GROUP),
                  pl.BlockSpec((None, 4, GROUP, d), lambda i: (layer, 0, 0, 0)),
                  pl.BlockSpec((None, 1, d), lambda i: (i // per, 0, 0))],
        out_specs=pl.BlockSpec((tm, d), lambda i: (i, 0)),
        out_shape=jax.ShapeDtypeStruct((t, d), F32),
        compiler_params=_cparams(("parallel",)),
    )(x, ya, yb, hf, hr, p, of, orv, p, w_all, gate)


def _ffn_kernel(x_ref, xp_ref, xn_ref, g_ref, sh_ref, sc_ref, gate_ref, wa_ref, wu_ref, wc_ref, wd_ref,
                o_ref, xs_ref, acc_ref, *, per):
    i = pl.program_id(0)
    j = pl.program_id(1)
    tm = x_ref.shape[0]

    @pl.when(j == 0)
    def _():
        nm = lambda x: _norm_mod(x, g_ref[...], sh_ref[...], sc_ref[...])
        keep_p = jnp.where(i % per == 0, 0.0, 1.0)
        keep_n = jnp.where(i % per == per - 1, 0.0, 1.0)
        xs_ref[0:HALO, :] = (nm(xp_ref[...]) * keep_p).astype(BF16)
        xs_ref[HALO:HALO + tm, :] = nm(x_ref[...]).astype(BF16)
        xs_ref[HALO + tm:, :] = (nm(xn_ref[...]) * keep_n).astype(BF16)
        acc_ref[...] = jnp.zeros_like(acc_ref)

    a = _dot(xs_ref[...], wa_ref[...])
    rows = a.shape[0]
    prv = pltpu.roll(a, 1, axis=0)[HALO:HALO + tm]
    nxt = pltpu.roll(a, rows - 1, axis=0)[HALO:HALO + tm]
    conv = prv * wc_ref[0:1, :] + a[HALO:HALO + tm] * wc_ref[1:2, :] + nxt * wc_ref[2:3, :]
    u = _dot(xs_ref[HALO:HALO + tm, :], wu_ref[...])
    act = (conv * _sigmoid(conv) * u).astype(BF16)
    acc_ref[...] += _dot(act, wd_ref[...])

    @pl.when(j == pl.num_programs(1) - 1)
    def _():
        o_ref[...] = x_ref[...] + gate_ref[...] * acc_ref[...]


def _ffn(x, gain, shift, scale, gate, w_up, w_conv, w_down, layer, rows_per_mod):
    t, d = x.shape
    f = w_down.shape[1]
    tm = _row_tile(rows_per_mod, 512)
    tf = 512 if f % 512 == 0 else f
    per = rows_per_mod // tm
    nf = f // tf
    step = tm // HALO
    nblk8 = t // HALO
    mod = pl.BlockSpec((None, 1, d), lambda i, j: (i // per, 0, 0))
    return pl.pallas_call(
        functools.partial(_ffn_kernel, per=per),
        grid=(t // tm, nf),
        in_specs=[pl.BlockSpec((tm, d), lambda i, j: (i, 0)),
                  pl.BlockSpec((HALO, d), lambda i, j: (jnp.maximum(i * step - 1, 0), 0)),
                  pl.BlockSpec((HALO, d), lambda i, j: (jnp.minimum((i + 1) * step, nblk8 - 1), 0)),
                  pl.BlockSpec((1, d), lambda i, j: (0, 0)), mod, mod, mod,
                  pl.BlockSpec((None, d, tf), lambda i, j: (layer, 0, j)),
                  pl.BlockSpec((None, d, tf), lambda i, j: (layer, 0, nf + j)),
                  pl.BlockSpec((None, 3, tf), lambda i, j: (layer, 0, j)),
                  pl.BlockSpec((None, tf, d), lambda i, j: (layer, j, 0))],
        out_specs=pl.BlockSpec((tm, d), lambda i, j: (i, 0)),
        out_shape=jax.ShapeDtypeStruct((t, d), F32),
        scratch_shapes=[pltpu.VMEM((tm + 2 * HALO, d), BF16), pltpu.VMEM((tm, d), F32)],
        compiler_params=_cparams(("parallel", "arbitrary")),
    )(x, x, x, gain, shift, scale, gate, w_up, w_up, w_conv, w_down)


def _perm_w_in(w):
    n, d = w.shape[:2]
    swq = w[..., 1536:2048].reshape(n, d, 2, 4, ATT_HD).transpose(0, 1, 3, 2, 4).reshape(n, d, GROUP)
    pad = jnp.zeros((n, d, LANES - 8), w.dtype)
    cols = [w[..., 0:1536], swq, w[..., 2304:3840], w[..., 3856:6416], w[..., 2048:2304],
            w[..., 3840:3848], pad, w[..., 3848:3856], pad]
    out = jnp.concatenate(cols, axis=-1).astype(BF16)
    assert out.shape[-1] == NP
    return out


def _perm_w_out(w):
    n, _, d = w.shape
    w4 = w.reshape(n, 4, GROUP, d)
    swo = w4[:, 1].reshape(n, 2, 4, ATT_HD, d).transpose(0, 2, 1, 3, 4).reshape(n, GROUP, d)
    return jnp.stack([w4[:, 0], swo, w4[:, 2], w4[:, 3]], axis=1).astype(BF16)


def _rope_tables(seq):
    t = jnp.arange(seq)
    half = ATT_HD // 2
    inv = ROPE_BASE ** (-jnp.arange(0, half, 2, dtype=F32) / half)
    ang_r = (t // GRID_W).astype(F32)[:, None] * inv
    ang_c = (t % GRID_W).astype(F32)[:, None] * inv
    cr, sr, cc, sc = jnp.cos(ang_r), jnp.sin(ang_r), jnp.cos(ang_c), jnp.sin(ang_c)
    cos = jnp.concatenate([cr, cr, cc, cc] * 2, axis=-1)
    sin = jnp.concatenate([-sr, sr, -sc, sc] * 2, axis=-1)
    return cos, sin


def kernel(x, c, ctx, c_ctx, w_mod, b_mod, norm_mix, norm_ffn, w_in, w_out, na_qk_gain, na_rpb,
           sw_qk_gain, sw_sink, ml_conv, ml_gate_bias, hg_lb, ffn_up, ffn_conv, ffn_down):
    bsz, seq, d = x.shape
    nctx = ctx.shape[1]
    depth = w_mod.shape[0]
    ml_chunk_ctx, ml_chunk = math.gcd(ML_CHUNK, nctx), math.gcd(ML_CHUNK, seq)
    chunk = math.gcd(HG_CHUNK, math.gcd(seq, nctx))
    assert bsz + 1 <= 8

    c_all = jnp.concatenate([c, c_ctx[None], jnp.zeros((8 - bsz - 1, d), F32)], axis=0)
    mod = _modulation(c_all, w_mod, b_mod).reshape(depth, 8, 6, d)
    cos, sin = _rope_tables(seq)
    bd = jnp.asarray(np.kron(np.eye(GROUP // ATT_HD), np.ones((ATT_HD, ATT_HD))), BF16)
    lb_cum = jnp.cumsum(jax.nn.softmax(hg_lb.astype(F32), axis=0), axis=0)
    lower = lb_cum - lb_cum[:1]
    consts = _hgrn_constants(chunk)
    w_in_p = _perm_w_in(w_in)
    w_out_p = _perm_w_out(w_out)
    w_up = ffn_up.astype(BF16)
    w_dn = ffn_down.astype(BF16)
    w_cv = ffn_conv.astype(F32)

    h = x.reshape(bsz * seq, d)
    hc = ctx.reshape(bsz * nctx, d)
    for l in range(depth):
        emit_ctx = l < depth - 1
        lat = lambda k: mod[l, :bsz, k].reshape(bsz, 1, d)
        cx = lambda k: jnp.broadcast_to(mod[l, bsz, k].reshape(1, 1, d), (bsz, 1, d))
        gain = norm_mix[l].reshape(1, d)
        p_lat = _in_proj(h, gain, lat(0), lat(1), w_in_p, l, seq)
        p_ctx = _in_proj(hc, gain, cx(0), cx(1), w_in_p, l, nctx)

        gna = jnp.tile(na_qk_gain[l].astype(F32), (1, GROUP // ATT_HD))
        gsw = jnp.tile(sw_qk_gain[l].astype(F32), (1, GROUP // ATT_HD))
        naq, nak, nav, swq, swk, swv = _attn_prep(p_lat, gna, gsw, bd, cos, sin, seq, True)
        cnaq, cnak, cnav, cswq, cswk, cswv = _attn_prep(p_ctx, gna, gsw, bd, cos, sin, nctx, False)
        sink = sw_sink[l].astype(F32)
        ya = _na_attention(naq, nak, nav, cnak, cnav, _na_bias_table(na_rpb[l], seq // GRID_W), bsz, seq, nctx)
        yb = _sw_attention(swq, swk, swv, cswk, cswv, sink, bsz, seq, nctx)

        qk_lat = _ml_prep(p_lat, ml_conv[l].astype(F32), seq)
        qk_ctx = _ml_prep(p_ctx, ml_conv[l].astype(F32), nctx)
        gb = ml_gate_bias[l].astype(F32).reshape(2, 8)
        brow = jnp.zeros((8, LANES), F32).at[0:2, 0:8].set(gb)
        bcol = jnp.zeros((8, LANES), F32).at[:, 0:2].set(gb.T)
        ml0 = (jnp.zeros((bsz, 2, 2, LANES, LANES), F32), jnp.zeros((bsz, 2, 2, 8, LANES), F32),
               jnp.zeros((bsz, 2, 8, LANES), F32))
        hcf, hcr, ml_state = _mlstm_scan(qk_ctx, p_ctx, brow, bcol, ml0, bsz, nctx, ml_chunk_ctx, emit_ctx)
        hf, hr, _ = _mlstm_scan(qk_lat, p_lat, brow, bcol, ml_state, bsz, seq, ml_chunk, True)

        lb = lower[l]
        lbc = jnp.stack([jnp.maximum(jnp.log(lb), NEG), jnp.log1p(-lb), 1.0 - lb], axis=1)
        lbc = jnp.concatenate([lbc, jnp.zeros((2, 5, GROUP), F32)], axis=1)
        hg0 = jnp.zeros((bsz, 2, 4, LANES, LANES), F32)
        ocf, ocr, hg_state = _hgrn_scan(p_ctx, lbc, consts, hg0, bsz, nctx, chunk, emit_ctx)
        of, orv, _ = _hgrn_scan(p_lat, lbc, consts, hg_state, bsz, seq, chunk, True)

        gain2 = norm_ffn[l].reshape(1, d)
        h = _out_proj(h, ya, yb, hf, hr, of, orv, p_lat, w_out_p, l, lat(2), seq)
        h = _ffn(h, gain2, lat(3), lat(4), lat(5), w_up, w_cv, w_dn, l, seq)
        if emit_ctx:
            yac, ybc = _ctx_attention(cnaq, cnak, cnav, cswq, cswk, cswv, sink, bsz, nctx)
            hc = _out_proj(hc, yac, ybc, hcf, hcr, ocf, ocr, p_ctx, w_out_p, l, cx(2), nctx)
            hc = _ffn(hc, gain2, cx(3), cx(4), cx(5), w_up, w_cv, w_dn, l, nctx)
    return h.reshape(bsz, seq, d)
```

```python
import functools
import math

import numpy as np
import jax
import jax.numpy as jnp
from jax import lax
from jax.experimental import pallas as pl
from jax.experimental.pallas import tpu as pltpu

F32 = jnp.float32
BF16 = jnp.bfloat16

GRID_W = 64
ATT_HD = 64
NA_ROWS = 8
NA_COLS = 16
SW_BLOCK = 128
ROPE_BASE = 10000.0
EPS = 1e-6
NEG = -1e30
GROUP = 512
LANES = 128
HALO = 8
VMEM_LIMIT = 56 * 1024 * 1024

C_NAQ, C_NAK, C_NAV, C_SWQ = 0, 512, 1024, 1536
C_MLQK, C_MLV, C_MLO = 2048, 2560, 3072
C_HGQ, C_HGI, C_HGF, C_HGG = 3584, 4096, 4608, 5632
C_SWK, C_SWV = 6144, 6272
C_MLI, C_MLF = 6400, 6528
NP = 6656
ML_CHUNK = 256
ML_STEP_ROWS = 1024
HG_CHUNK = 64
SW_PERM = (0, 4, 1, 5, 2, 6, 3, 7)


def _cparams(sem):
    return pltpu.CompilerParams(dimension_semantics=sem, vmem_limit_bytes=VMEM_LIMIT)


def _sigmoid(x):
    return 1.0 / (1.0 + jnp.exp(-x))


def _log1p_unit(e):
    return jnp.log(1.0 + e)


def _log_sigmoid(x):
    return jnp.minimum(x, 0.0) - _log1p_unit(jnp.exp(-jnp.abs(x)))


def _split3(x):
    hi = x.astype(BF16)
    r = x - hi.astype(F32)
    mid = r.astype(BF16)
    lo = (r - mid.astype(F32)).astype(BF16)
    return hi, mid, lo


def _dot(a, b):
    return jnp.dot(a, b, preferred_element_type=F32)


def _dot_nt(a, b):
    return lax.dot_general(a, b, (((1,), (1,)), ((), ())), preferred_element_type=F32)


def _dot_tn(a, b):
    return lax.dot_general(a, b, (((0,), (0,)), ((), ())), preferred_element_type=F32)


def _exact_left(m01, x):
    hi, mid, lo = _split3(x)
    return _dot(m01, hi) + _dot(m01, mid) + _dot(m01, lo)


def _exact_right(x, m01):
    hi, mid, lo = _split3(x)
    return _dot(hi, m01) + _dot(mid, m01) + _dot(lo, m01)


def _mod_kernel(c_ref, w_ref, b_ref, o_ref):
    c = c_ref[...]
    a = c * _sigmoid(c)
    a_hi = a.astype(BF16)
    a_lo = (a - a_hi.astype(F32)).astype(BF16)
    w = w_ref[...]
    w_hi = w.astype(BF16)
    w_lo = (w - w_hi.astype(F32)).astype(BF16)
    o_ref[...] = _dot(a_hi, w_hi) + _dot(a_hi, w_lo) + _dot(a_lo, w_hi) + b_ref[...]


def _modulation(c_all, w_mod, b_mod):
    depth, d, n = w_mod.shape
    tn = 1024 if n % 1024 == 0 else n
    return pl.pallas_call(
        _mod_kernel,
        grid=(depth, n // tn),
        in_specs=[pl.BlockSpec((8, d), lambda l, j: (0, 0)),
                  pl.BlockSpec((None, d, tn), lambda l, j: (l, 0, j)),
                  pl.BlockSpec((None, 1, tn), lambda l, j: (l, 0, j))],
        out_specs=pl.BlockSpec((None, 8, tn), lambda l, j: (l, 0, j)),
        out_shape=jax.ShapeDtypeStruct((depth, 8, n), F32),
        compiler_params=_cparams(("parallel", "parallel")),
    )(c_all, w_mod, b_mod.reshape(depth, 1, n))


def _norm_mod(x, gain, shift, scale):
    y = x * lax.rsqrt(jnp.mean(x * x, axis=-1, keepdims=True) + EPS) * gain
    return y * (1.0 + scale) + shift


NORM_ROWS = 16


def _norm_mod_rows(src_ref, dst_ref, nrows, gain, shift, scale):
    amp = gain * (1.0 + scale)

    def body(c, carry):
        rows = pl.ds(pl.multiple_of(c * NORM_ROWS, NORM_ROWS), NORM_ROWS)
        x = src_ref[rows, :]
        y = x * lax.rsqrt(jnp.mean(x * x, axis=-1, keepdims=True) + EPS) * amp + shift
        dst_ref[rows, :] = y.astype(BF16)
        return carry

    assert nrows % NORM_ROWS == 0
    lax.fori_loop(0, nrows // NORM_ROWS, body, 0, unroll=4)


def _inproj_kernel(x_ref, g_ref, sh_ref, sc_ref, w_ref, o_ref, xn_ref):
    @pl.when(pl.program_id(1) == 0)
    def _():
        _norm_mod_rows(x_ref, xn_ref, x_ref.shape[0], g_ref[...], sh_ref[...], sc_ref[...])

    o_ref[...] = _dot(xn_ref[...], w_ref[...])


def _row_tile(rows_per_mod, pref):
    tm = min(pref, rows_per_mod)
    assert rows_per_mod % tm == 0
    return tm


def _in_proj(x, gain, shift, scale, w_all, layer, rows_per_mod):
    t, d = x.shape
    n = w_all.shape[2]
    tm = _row_tile(rows_per_mod, 1024)
    tn = 512 if n % 512 == 0 else n
    per = rows_per_mod // tm
    return pl.pallas_call(
        _inproj_kernel,
        grid=(t // tm, n // tn),
        in_specs=[pl.BlockSpec((tm, d), lambda i, j: (i, 0)),
                  pl.BlockSpec((1, d), lambda i, j: (0, 0)),
                  pl.BlockSpec((None, 1, d), lambda i, j: (i // per, 0, 0)),
                  pl.BlockSpec((None, 1, d), lambda i, j: (i // per, 0, 0)),
                  pl.BlockSpec((None, d, tn), lambda i, j: (layer, 0, j))],
        out_specs=pl.BlockSpec((tm, tn), lambda i, j: (i, j)),
        out_shape=jax.ShapeDtypeStruct((t, n), F32),
        scratch_shapes=[pltpu.VMEM((tm, d), BF16)],
        compiler_params=_cparams(("parallel", "arbitrary")),
    )(x, gain, shift, scale, w_all)


def _head_rms(x, gain_row, bd):
    xx = x * x
    hi = xx.astype(BF16)
    lo = (xx - hi.astype(F32)).astype(BF16)
    ss = _dot(hi, bd) + _dot(lo, bd)
    return x * lax.rsqrt(ss * (1.0 / ATT_HD) + EPS) * gain_row


def _rope(x, cos, sin, first):
    w = x.shape[-1]
    nxt = pltpu.roll(x, w - 16, axis=1)
    prv = pltpu.roll(x, 16, axis=1)
    return x * cos + jnp.where(first, nxt, prv) * sin


def _attn_prep_kernel(p_ref, pkv_ref, gna_ref, gsw_ref, bd_ref, cos_ref, sin_ref,
                      naq_ref, nak_ref, nav_ref, swq_ref, swk_ref, swv_ref, *, rope):
    bd = bd_ref[...]
    scale = ATT_HD ** -0.5
    naq_ref[...] = (_head_rms(p_ref[:, C_NAQ:C_NAQ + GROUP], gna_ref[0:1, :], bd) * scale).astype(BF16)
    nak_ref[...] = _head_rms(p_ref[:, C_NAK:C_NAK + GROUP], gna_ref[1:2, :], bd).astype(BF16)
    nav_ref[...] = p_ref[:, C_NAV:C_NAV + GROUP].astype(BF16)
    q = _head_rms(p_ref[:, C_SWQ:C_SWQ + GROUP], gsw_ref[0:1, :], bd)
    k = _head_rms(pkv_ref[:, 0:LANES], gsw_ref[1:2, 0:LANES], bd[0:LANES, 0:LANES])
    if rope:
        cos = cos_ref[...]
        sin = sin_ref[...]
        lane = lax.broadcasted_iota(jnp.int32, (1, GROUP), 1)
        first = (lane % 32) < 16
        q = _rope(q, jnp.concatenate([cos] * 4, axis=1), jnp.concatenate([sin] * 4, axis=1), first)
        k = _rope(k, cos, sin, first[:, 0:LANES])
    swq_ref[...] = (q * scale).astype(BF16)
    swk_ref[...] = k.astype(BF16)
    swv_ref[...] = pkv_ref[:, LANES:2 * LANES].astype(BF16)


def _attn_prep(p, gna, gsw, bd, cos, sin, seq, rope):
    t = p.shape[0]
    tm = _row_tile(seq, 512)
    per = seq // tm
    out = lambda w: jax.ShapeDtypeStruct((t, w), BF16)
    ospec = lambda w: pl.BlockSpec((tm, w), lambda i: (i, 0))
    return pl.pallas_call(
        functools.partial(_attn_prep_kernel, rope=rope),
        grid=(t // tm,),
        in_specs=[pl.BlockSpec((tm, C_MLQK), lambda i: (i, 0)),
                  pl.BlockSpec((tm, 2 * LANES), lambda i: (i, C_SWK // (2 * LANES))),
                  pl.BlockSpec((2, GROUP), lambda i: (0, 0)),
                  pl.BlockSpec((2, GROUP), lambda i: (0, 0)),
                  pl.BlockSpec((GROUP, GROUP), lambda i: (0, 0)),
                  pl.BlockSpec((tm, LANES), lambda i: (i % per, 0)),
                  pl.BlockSpec((tm, LANES), lambda i: (i % per, 0))],
        out_specs=[ospec(GROUP), ospec(GROUP), ospec(GROUP), ospec(GROUP), ospec(LANES), ospec(LANES)],
        out_shape=[out(GROUP), out(GROUP), out(GROUP), out(GROUP), out(LANES), out(LANES)],
        compiler_params=_cparams(("parallel",)),
    )(p, p, gna, gsw, bd, cos, sin)


def _pair_rows(q_pair):
    lane = lax.broadcasted_iota(jnp.int32, q_pair.shape, 1)
    zero = jnp.zeros_like(q_pair)
    return jnp.concatenate([jnp.where(lane < ATT_HD, q_pair, zero),
                            jnp.where(lane >= ATT_HD, q_pair, zero)], axis=0)


def _pair_merge(o2, inv_l):
    m = o2.shape[0] // 2
    lane = lax.broadcasted_iota(jnp.int32, (m, LANES), 1)
    return jnp.where(lane < ATT_HD, o2[0:m] * inv_l[0:m], o2[m:] * inv_l[m:])


NA_RPS = 4


def _softmax_pv(parts, sink=None):
    m = functools.reduce(jnp.maximum, [jnp.max(s, axis=-1, keepdims=True) for s, _ in parts])
    if sink is not None:
        m = jnp.maximum(m, sink)
    acc = None
    for s, v in parts:
        e = jnp.exp((s - m).astype(BF16))
        o = _dot(e, jnp.concatenate([v, jnp.ones_like(v)], axis=1))
        acc = o if acc is None else acc + o
    l = acc[:, LANES:LANES + 1]
    if sink is not None:
        l = l + jnp.exp(sink - m)
    return acc[:, 0:LANES], l


NA_WIN = NA_ROWS + NA_RPS


def _na_window(step, rows):
    return jnp.clip(step * NA_RPS - NA_ROWS // 2, 0, rows - NA_WIN)


def _na_kernel(q_ref, k_ref, v_ref, kc_ref, vc_ref, b_ref, o_ref, *, rows):
    start = pl.multiple_of(_na_window(pl.program_id(1), rows) * GRID_W, GRID_W)
    nkey = NA_WIN * GRID_W
    for p in range(GROUP // LANES):
        cs = slice(p * LANES, (p + 1) * LANES)
        qs = jnp.concatenate([_pair_rows(q_ref[rr * GRID_W:(rr + 1) * GRID_W, cs]) for rr in range(NA_RPS)],
                             axis=0)
        kw = k_ref[pl.ds(start, nkey), cs]
        vw = v_ref[pl.ds(start, nkey), cs]
        s_nb = _dot_nt(qs, kw) + b_ref[p]
        s_cx = _dot_nt(qs, kc_ref[:, cs])
        o2, l = _softmax_pv([(s_nb, vw), (s_cx, vc_ref[:, cs])])
        inv_l = 1.0 / l
        for rr in range(NA_RPS):
            sl = slice(rr * 2 * GRID_W, (rr + 1) * 2 * GRID_W)
            o_ref[rr * GRID_W:(rr + 1) * GRID_W, cs] = _pair_merge(o2[sl], inv_l[sl]).astype(BF16)


def _na_attention(q, k, v, kc, vc, bias, bsz, seq, ctx):
    rows = seq // GRID_W
    assert rows >= NA_WIN and rows % NA_RPS == 0
    steps = rows // NA_RPS
    k3, v3 = k.reshape(bsz, seq, GROUP), v.reshape(bsz, seq, GROUP)
    kc3, vc3 = kc.reshape(bsz, ctx, GROUP), vc.reshape(bsz, ctx, GROUP)
    qspec = pl.BlockSpec((NA_RPS * GRID_W, GROUP), lambda b, r: (b * steps + r, 0))
    variant = lambda b, r: ((r > 0).astype(jnp.int32) + (r == steps - 1).astype(jnp.int32), 0, 0, 0)
    return pl.pallas_call(
        functools.partial(_na_kernel, rows=rows),
        grid=(bsz, steps),
        in_specs=[qspec,
                  pl.BlockSpec((None, seq, GROUP), lambda b, r: (b, 0, 0)),
                  pl.BlockSpec((None, seq, GROUP), lambda b, r: (b, 0, 0)),
                  pl.BlockSpec((None, ctx, GROUP), lambda b, r: (b, 0, 0)),
                  pl.BlockSpec((None, ctx, GROUP), lambda b, r: (b, 0, 0)),
                  pl.BlockSpec((None,) + bias.shape[1:], variant)],
        out_specs=qspec,
        out_shape=jax.ShapeDtypeStruct((bsz * seq, GROUP), BF16),
        compiler_params=_cparams(("parallel", "arbitrary")),
    )(q, k3, v3, kc3, vc3, bias)


def _na_bias_table(rpb, rows):
    nh = rpb.shape[0]
    steps = rows // NA_RPS
    c = np.arange(GRID_W)
    kc = np.arange(GRID_W)
    cstart = np.clip(c - NA_COLS // 2, 0, GRID_W - NA_COLS)
    ok = (kc[None, :] >= cstart[:, None]) & (kc[None, :] < cstart[:, None] + NA_COLS)
    dc = np.clip(kc[None, :] - c[:, None] + NA_COLS - 1, 0, 2 * NA_COLS - 2)
    t = jnp.where(ok[None, None], rpb.astype(F32)[:, :, dc], NEG)

    def step_layout(s):
        ws = int(np.clip(s * NA_RPS - NA_ROWS // 2, 0, rows - NA_WIN))
        r = s * NA_RPS + np.arange(NA_RPS)[:, None]
        rs = np.clip(r - NA_ROWS // 2, 0, rows - NA_ROWS)
        krow = ws + np.arange(NA_WIN)[None, :]
        return (krow - r + NA_ROWS - 1), (krow >= rs) & (krow < rs + NA_ROWS)

    layouts = [step_layout(s) for s in range(steps)]
    variants = [layouts[0], layouts[1], layouts[-1]]
    for s, (idx, valid) in enumerate(layouts):
        want = variants[(s > 0) + (s == steps - 1)]
        assert (valid == want[1]).all() and (idx[valid] == want[0][want[1]]).all()

    ndr = 2 * NA_ROWS - 1
    width = (ndr + 3) * GRID_W
    t2 = jnp.transpose(t, (0, 2, 1, 3)).reshape(nh, GRID_W, ndr * GRID_W)
    t2 = jnp.pad(t2, ((0, 0), (0, 0), (GRID_W, width - (ndr + 1) * GRID_W)), constant_values=NEG)
    t2s = jnp.pad(t2[:, :, GRID_W:], ((0, 0), (0, 0), (0, GRID_W)), constant_values=NEG)

    def build(ta_ref, tb_ref, o_ref):
        lane = lax.broadcasted_iota(jnp.int32, (GRID_W, LANES), 1)
        neg = jnp.full((GRID_W, LANES), NEG, F32)
        for v, (idx, valid) in enumerate(variants):
            for rr in range(NA_RPS):
                for e in range(2):
                    row0 = (rr * 2 + e) * GRID_W
                    for jj in range(NA_WIN // 2):
                        j = 2 * jj
                        ok0, ok1 = bool(valid[rr, j]), bool(valid[rr, j + 1])
                        if not (ok0 or ok1):
                            piece = neg
                        else:
                            blk = int(idx[rr, j] if ok0 else idx[rr, j + 1] - 1) + 1
                            src = ta_ref if blk % 2 == 0 else tb_ref
                            off = (blk - blk % 2) * GRID_W
                            piece = src[e, :, off:off + LANES]
                            if not ok0:
                                piece = jnp.where(lane >= GRID_W, piece, NEG)
                            if not ok1:
                                piece = jnp.where(lane < GRID_W, piece, NEG)
                        o_ref[v, row0:row0 + GRID_W, j * GRID_W:(j + 2) * GRID_W] = piece

    npair = nh // 2
    tspec = pl.BlockSpec((2, GRID_W, width), lambda p: (p, 0, 0))
    return pl.pallas_call(
        build,
        grid=(npair,),
        in_specs=[tspec, tspec],
        out_specs=pl.BlockSpec((3, None, NA_RPS * 2 * GRID_W, NA_WIN * GRID_W), lambda p: (0, p, 0, 0)),
        out_shape=jax.ShapeDtypeStruct((3, npair, NA_RPS * 2 * GRID_W, NA_WIN * GRID_W), F32),
        compiler_params=_cparams(("parallel",)),
    )(t2, t2s)


SW_BPS = 2


def _sw_kernel(sink_ref, q_ref, k_ref, v_ref, kc_ref, vc_ref, o_ref, *, seq):
    nq = SW_BPS * SW_BLOCK
    nwin = (SW_BPS + 2) * SW_BLOCK
    q0 = pl.program_id(1) * nq
    start = pl.multiple_of(jnp.clip(q0 - SW_BLOCK, 0, seq - nwin), SW_BLOCK)
    kw = k_ref[pl.ds(start, nwin), :]
    vw = v_ref[pl.ds(start, nwin), :]
    kc = kc_ref[...]
    vc = vc_ref[...]
    row = lax.broadcasted_iota(jnp.int32, (2 * nq, nwin), 0)
    col = lax.broadcasted_iota(jnp.int32, (2 * nq, nwin), 1)
    ok = jnp.abs((start + col) - (q0 + row % nq)) <= SW_BLOCK
    half = lax.broadcasted_iota(jnp.int32, (2 * nq, 1), 0) < nq
    for j in range(GROUP // LANES):
        cs = slice(j * LANES, (j + 1) * LANES)
        qs = _pair_rows(q_ref[:, cs])
        s_bd = jnp.where(ok, _dot_nt(qs, kw), NEG)
        s_cx = _dot_nt(qs, kc)
        sink = jnp.where(half, sink_ref[j], sink_ref[j + 4])
        o2, l = _softmax_pv([(s_bd, vw), (s_cx, vc)], sink)
        o_ref[:, cs] = _pair_merge(o2, 1.0 / l).astype(BF16)


def _sw_attention(q, k, v, kc, vc, sink, bsz, seq, ctx):
    assert seq % (SW_BPS * SW_BLOCK) == 0 and seq >= (SW_BPS + 2) * SW_BLOCK
    nb = seq // (SW_BPS * SW_BLOCK)
    k3, v3 = k.reshape(bsz, seq, LANES), v.reshape(bsz, seq, LANES)
    kc3, vc3 = kc.reshape(bsz, ctx, LANES), vc.reshape(bsz, ctx, LANES)
    return pl.pallas_call(
        functools.partial(_sw_kernel, seq=seq),
        grid=(bsz, nb),
        in_specs=[pl.BlockSpec(memory_space=pltpu.SMEM),
                  pl.BlockSpec((SW_BPS * SW_BLOCK, GROUP), lambda b, n: (b * nb + n, 0)),
                  pl.BlockSpec((None, seq, LANES), lambda b, n: (b, 0, 0)),
                  pl.BlockSpec((None, seq, LANES), lambda b, n: (b, 0, 0)),
                  pl.BlockSpec((None, ctx, LANES), lambda b, n: (b, 0, 0)),
                  pl.BlockSpec((None, ctx, LANES), lambda b, n: (b, 0, 0))],
        out_specs=pl.BlockSpec((SW_BPS * SW_BLOCK, GROUP), lambda b, n: (b * nb + n, 0)),
        out_shape=jax.ShapeDtypeStruct((bsz * seq, GROUP), BF16),
        compiler_params=_cparams(("parallel", "arbitrary")),
    )(sink, q, k3, v3, kc3, vc3)


def _ctx_attn_kernel(sink_ref, qa_ref, ka_ref, va_ref, qb_ref, kb_ref, vb_ref, oa_ref, ob_ref):
    for p in range(GROUP // LANES):
        cs = slice(p * LANES, (p + 1) * LANES)
        qs = _pair_rows(qa_ref[:, cs])
        o2, l = _softmax_pv([(_dot_nt(qs, ka_ref[:, cs]), va_ref[:, cs])])
        oa_ref[:, cs] = _pair_merge(o2, 1.0 / l).astype(BF16)
    ctx = qb_ref.shape[0]
    half = lax.broadcasted_iota(jnp.int32, (2 * ctx, 1), 0) < ctx
    for j in range(GROUP // LANES):
        cs = slice(j * LANES, (j + 1) * LANES)
        qs = _pair_rows(qb_ref[:, cs])
        sink = jnp.where(half, sink_ref[j], sink_ref[j + 4])
        o2, l = _softmax_pv([(_dot_nt(qs, kb_ref[...]), vb_ref[...])], sink)
        ob_ref[:, cs] = _pair_merge(o2, 1.0 / l).astype(BF16)


def _ctx_attention(qa, ka, va, qb, kb, vb, sink, bsz, ctx):
    big = pl.BlockSpec((ctx, GROUP), lambda b: (b, 0))
    small = pl.BlockSpec((ctx, LANES), lambda b: (b, 0))
    return pl.pallas_call(
        _ctx_attn_kernel,
        grid=(bsz,),
        in_specs=[pl.BlockSpec(memory_space=pltpu.SMEM), big, big, big, big, small, small],
        out_specs=[big, big],
        out_shape=[jax.ShapeDtypeStruct((bsz * ctx, GROUP), BF16)] * 2,
        compiler_params=_cparams(("parallel",)),
    )(sink, qa, ka, va, qb, kb, vb)


def _halo_specs(tm, width, col_block, nblk8):
    step = tm // HALO
    return [pl.BlockSpec((tm, width), lambda i: (i, col_block)),
            pl.BlockSpec((HALO, width), lambda i: (jnp.maximum(i * step - 1, 0), col_block)),
            pl.BlockSpec((HALO, width), lambda i: (jnp.minimum((i + 1) * step, nblk8 - 1), col_block))]


def _ml_prep_kernel(x_ref, xp_ref, xn_ref, w_ref, o_ref, *, per):
    i = pl.program_id(0)
    x = x_ref[...]
    tm = x.shape[0]
    row = lax.broadcasted_iota(jnp.int32, (tm, 1), 0)
    keep_p = jnp.where(i % per == 0, 0.0, 1.0)
    keep_n = jnp.where(i % per == per - 1, 0.0, 1.0)
    prv = jnp.where(row == 0, xp_ref[HALO - 1:HALO, :] * keep_p, pltpu.roll(x, 1, axis=0))
    nxt = jnp.where(row == tm - 1, xn_ref[0:1, :] * keep_n, pltpu.roll(x, tm - 1, axis=0))
    y = prv * w_ref[0:1, :] + x * w_ref[1:2, :] + nxt * w_ref[2:3, :]
    y = y * _sigmoid(y)
    lane = lax.broadcasted_iota(jnp.int32, (1, GROUP), 1)
    o_ref[...] = (y * jnp.where(lane < GROUP // 2, ATT_HD ** -0.5, 1.0)).astype(BF16)


def _ml_prep(p, conv_w, seq):
    t = p.shape[0]
    tm = _row_tile(seq, 512)
    per = seq // tm
    return pl.pallas_call(
        functools.partial(_ml_prep_kernel, per=per),
        grid=(t // tm,),
        in_specs=_halo_specs(tm, GROUP, C_MLQK // GROUP, t // HALO)
        + [pl.BlockSpec((3, GROUP), lambda i: (0, 0))],
        out_specs=pl.BlockSpec((tm, GROUP), lambda i: (i, 0)),
        out_shape=jax.ShapeDtypeStruct((t, GROUP), BF16),
        compiler_params=_cparams(("parallel",)),
    )(p, p, p, conv_w)


def _mlstm_kernel(*refs, chunk, emit):
    (qkf_ref, vf_ref, gif_ref, gff_ref, qkr_ref, vr_ref, gir_ref, gfr_ref,
     brow_ref, bcol_ref, c0_ref, n0_ref, m0_ref) = refs[:13]
    if emit:
        hf_ref, hr_ref = refs[13:15]
        rest = refs[15:]
    else:
        hf_ref = hr_ref = None
        rest = refs[13:]
    c_out, n_out, m_out, c_s, n_s, m_s = rest
    i = pl.program_id(1)

    @pl.when(i == 0)
    def _():
        c_s[...] = c0_ref[...]
        n_s[...] = n0_ref[...]
        m_s[...] = m0_ref[...]

    row = lax.broadcasted_iota(jnp.int32, (chunk, chunk), 0)
    col = lax.broadcasted_iota(jnp.int32, (chunk, chunk), 1)
    lane = lax.broadcasted_iota(jnp.int32, (chunk, LANES), 1)
    lane1 = lax.broadcasted_iota(jnp.int32, (1, LANES), 1)
    srow = lax.broadcasted_iota(jnp.int32, (LANES, 1), 0)
    dirs = ((qkf_ref, vf_ref, gif_ref, gff_ref, hf_ref), (qkr_ref, vr_ref, gir_ref, gfr_ref, hr_ref))
    chains = [(b, d) for b in range(qkf_ref.shape[0]) for d in range(2)]
    old_c ={(b, d, pp): c_s[b, d, pp] for b, d in chains for pp in range(2)}
    old_n = {(b, d, pp): n_s[b, d, pp, 0:1, :] for b, d in chains for pp in range(2)}
    old_m = {(b, d): m_s[b, d, 0:1, :] for b, d in chains}
    new_c, new_n, new_m = {}, {}, {}
    for b, d in chains:
        qk_ref, v_ref, gi_ref, gf_ref, h_ref = [r if r is None else r.at[b] for r in dirs[d]]
        causal = (col <= row) if d == 0 else (col >= row)
        causal_t = (row <= col) if d == 0 else (row >= col)
        gi = gi_ref[...]
        gf = gf_ref[...]
        ipre = gi + brow_ref[0:1, :]
        fl = _log_sigmoid(gf + brow_ref[1:2, :])
        fcum = _exact_left(jnp.where(causal, 1.0, 0.0).astype(BF16), fl)
        ftot = jnp.sum(fl, axis=0, keepdims=True)
        m_prev = old_m[b, d]
        w_log = ftot - fcum + ipre
        m_new = jnp.maximum(ftot + m_prev, jnp.max(w_log, axis=0, keepdims=True))
        cd = jnp.exp(ftot + m_prev - m_new)
        w = jnp.exp(w_log - m_new)
        if emit:
            ipre_t = gi.T[0:8, :] + bcol_ref[:, 0:1]
            fl_t = _log_sigmoid(gf.T[0:8, :] + bcol_ref[:, 1:2])
            fcum_t = _exact_right(fl_t, jnp.where(causal_t, 1.0, 0.0).astype(BF16))
            b_rows = ipre_t - fcum_t
            log_inter = fcum + m_prev
        for pp in range(2):
            qp = qk_ref[:, pp * LANES:(pp + 1) * LANES]
            kp = qk_ref[:, GROUP // 2 + pp * LANES:GROUP // 2 + (pp + 1) * LANES]
            ct = old_c[b, d, pp]
            n_row = old_n[b, d, pp]
            inc = None
            for e in range(2):
                h = 2 * pp + e
                c = 4 * d + h
                sel = (lane < ATT_HD) if e == 0 else (lane >= ATT_HD)
                vh = v_ref[:, h * LANES:(h + 1) * LANES]
                if emit:
                    qm = jnp.where(sel, qp, jnp.zeros_like(qp))
                    fc = fcum[:, c:c + 1]
                    li = log_inter[:, c:c + 1]
                    log_d = jnp.where(causal, fc + b_rows[c:c + 1, :], NEG)
                    m_row = jnp.maximum(li, jnp.max(log_d, axis=-1, keepdims=True))
                    d_mat = jnp.exp(log_d - m_row)
                    inter_w = jnp.exp(li - m_row)
                    s_mat = _dot_nt(qm, kp) * d_mat
                    num = inter_w * _dot(qm, ct.astype(BF16)) + _dot(s_mat.astype(BF16), vh.astype(BF16))
                    den = (inter_w * jnp.sum(qm.astype(F32) * n_row, axis=-1, keepdims=True)
                           + jnp.sum(s_mat, axis=-1, keepdims=True))
                    h_ref[:, h * LANES:(h + 1) * LANES] = num / jnp.maximum(jnp.abs(den), jnp.exp(-m_row))
                km = jnp.where(sel, kp, jnp.zeros_like(kp))
                upd = _dot_tn(km, (w[:, c:c + 1] * vh).astype(BF16))
                inc = upd if inc is None else inc + upd
            c_lo, c_hi = 4 * d + 2 * pp, 4 * d + 2 * pp + 1
            new_c[b, d, pp] = (jnp.where(srow < ATT_HD, cd[:, c_lo:c_lo + 1], cd[:, c_hi:c_hi + 1]) * ct
                               + inc)
            w_sel = jnp.where(lane < ATT_HD, w[:, c_lo:c_lo + 1], w[:, c_hi:c_hi + 1])
            cd_sel = jnp.where(lane1 < ATT_HD, cd[:, c_lo:c_lo + 1], cd[:, c_hi:c_hi + 1])
            new_n[b, d, pp] = cd_sel * n_row + jnp.sum(w_sel * kp.astype(F32), axis=0, keepdims=True)
        new_m[b, d] = m_new
    for (b, d, pp), val in new_c.items():
        c_s[b, d, pp] = val
        n_s[b, d, pp, 0:1, :] = new_n[b, d, pp]
    for (b, d), val in new_m.items():
        m_s[b, d, 0:1, :] = val

    @pl.when(i == pl.num_programs(1) - 1)
    def _():
        c_out[...] = c_s[...]
        n_out[...] = n_s[...]
        m_out[...] = m_s[...]


def _mlstm_scan(qk, p, brow, bcol, state, bsz, seq, chunk, emit):
    nc = seq // chunk
    bps = bsz if bsz * chunk <= ML_STEP_ROWS else max(1, ML_STEP_ROWS // chunk)
    assert bsz % bps == 0
    fwd = lambda g, i: i
    rev = lambda g, i: nc - 1 - i
    qk3 = qk.reshape(bsz, seq, GROUP)
    p3 = p.reshape(bsz, seq, NP)

    def chunk_specs(idx):
        return [pl.BlockSpec((bps, chunk, GROUP), lambda g, i: (g, idx(g, i), 0)),
                pl.BlockSpec((bps, chunk, GROUP), lambda g, i: (g, idx(g, i), C_MLV // GROUP)),
                pl.BlockSpec((bps, chunk, LANES), lambda g, i: (g, idx(g, i), C_MLI // LANES)),
                pl.BlockSpec((bps, chunk, LANES), lambda g, i: (g, idx(g, i), C_MLF // LANES))]

    st_dims = [(2, 2, LANES, LANES), (2, 2, 8, LANES), (2, 8, LANES)]
    st_specs = [pl.BlockSpec((bps,) + s, lambda g, i, n=len(s): (g,) + (0,) * n) for s in st_dims]
    st_shapes = [jax.ShapeDtypeStruct((bsz,) + s, F32) for s in st_dims]
    h_specs = [pl.BlockSpec((bps, chunk, GROUP), lambda g, i: (g, fwd(g, i), 0)),
               pl.BlockSpec((bps, chunk, GROUP), lambda g, i: (g, rev(g, i), 0))] if emit else []
    h_shapes = [jax.ShapeDtypeStruct((bsz, seq, GROUP), F32)] * 2 if emit else []
    outs = pl.pallas_call(
        functools.partial(_mlstm_kernel, chunk=chunk, emit=emit),
        grid=(bsz // bps, nc),
        in_specs=chunk_specs(fwd) + chunk_specs(rev)
        + [pl.BlockSpec((8, LANES), lambda g, i: (0, 0)), pl.BlockSpec((8, LANES), lambda g, i: (0, 0))]
        + st_specs,
        out_specs=h_specs + st_specs,
        out_shape=h_shapes + st_shapes,
        scratch_shapes=[pltpu.VMEM((bps,) + s, F32) for s in st_dims],
        compiler_params=_cparams(("parallel", "arbitrary")),
    )(qk3, p3, p3, p3, qk3, p3, p3, p3, brow, bcol, *state)
    if emit:
        return (outs[0].reshape(bsz * seq, GROUP), outs[1].reshape(bsz * seq, GROUP), tuple(outs[2:]))
    return None, None, tuple(outs)


def _hgrn_constants(chunk):
    nlev = int(np.log2(chunk))
    assert 2 ** nlev == chunk
    t = np.arange(chunk)
    mats = [(t[None, :] <= t[:, None]), (t[None, :] > t[:, None])]
    masks = []
    for lev in range(nlev):
        n = chunk >> (lev + 1)
        b0 = t - t % (2 * n)
        upper = (t % (2 * n)) >= n
        m_up = (t[None, :] >= (b0 + n)[:, None]) & (t[None, :] <= t[:, None])
        m_lo = (t[None, :] > t[:, None]) & (t[None, :] <= (b0 + n - 1)[:, None])
        mats.append(np.where(upper[:, None], m_up, m_lo))
        same = (t[:, None] // (2 * n)) == (t[None, :] // (2 * n))
        masks.append(same & upper[:, None] & ~upper[None, :])
    masks.append(t[:, None] == t[None, :])
    mstack = np.concatenate(mats, axis=0).astype(np.float32)
    lmask = np.stack([np.kron(np.eye(2), m.astype(np.float32)) for m in masks])
    nr = mstack.shape[0] // chunk
    mstack_rev = mstack.reshape(nr, chunk, chunk)[:, ::-1, ::-1].reshape(-1, chunk)
    lmask_rev = lmask.reshape(nlev + 1, 2, chunk, 2, chunk)[:, :, ::-1, :, ::-1].reshape(lmask.shape)
    return (jnp.asarray(np.stack([mstack, mstack_rev]), BF16),
            jnp.asarray(np.stack([lmask, lmask_rev]), F32), nlev)


def _stack_pair(x, pp):
    return jnp.concatenate([x[:, (2 * pp) * LANES:(2 * pp + 1) * LANES],
                            x[:, (2 * pp + 1) * LANES:(2 * pp + 2) * LANES]], axis=0)


def _hgrn_kernel(*refs, chunk, nlev, emit, zero_lb):
    (qf_ref, vf_ref, ff_ref, qr_ref, vr_ref, fr_ref, lb_ref, ms_ref, lm_ref, s0_ref) = refs[:10]
    if emit:
        of_ref, or_ref = refs[10:12]
        rest = refs[12:]
    else:
        of_ref = or_ref = None
        rest = refs[10:]
    s_out, s_s = rest
    i = pl.program_id(1)

    @pl.when(i == 0)
    def _():
        s_s[...] = s0_ref[...]

    dirs = ((qf_ref, vf_ref, ff_ref, of_ref), (qr_ref, vr_ref, fr_ref, or_ref))
    chains = [(b, d) for b in range(vf_ref.shape[0]) for d in range(2)]
    old_s = {(b, d, h): s_s[b, d, h] for b, d in chains for h in range(4)}
    new_s = {}

    for b, d in chains:
        q_ref, v_ref, f_ref, o_ref = [r if r is None else r.at[b] for r in dirs[d]]
        fpre = f_ref[...]
        vb = v_ref[...].astype(BF16)
        e_f = jnp.exp(-jnp.abs(fpre))
        ls = jnp.minimum(fpre, 0.0) - _log1p_unit(e_f)
        sig_neg = jnp.where(fpre >= 0.0, e_f, 1.0) / (1.0 + e_f)
        if zero_lb:
            logf, k = ls, sig_neg
        else:
            a = lb_ref[d, 0:1, :]
            bb = lb_ref[d, 1:2, :] + ls
            logf = jnp.maximum(a, bb) + _log1p_unit(jnp.exp(-jnp.abs(a - bb)))
            k = lb_ref[d, 2:3, :] * sig_neg
        hi = logf.astype(BF16)
        lo = (logf - hi.astype(F32)).astype(BF16)
        gtot = jnp.sum(logf, axis=0, keepdims=True)
        if emit:
            qpre = q_ref[...]
            q = qpre * _sigmoid(qpre)
            qb = q.astype(BF16)
            kb = k.astype(BF16)
        ms = ms_ref[d]
        ex = _dot(ms, hi) + _dot(ms, lo)
        ktil = (k * jnp.exp(ex[chunk:2 * chunk])).astype(BF16)
        if emit:
            qe = (q * jnp.exp(ex[0:chunk])).astype(BF16)
            lev_ops = []
            for lev in range(nlev):
                e_l = jnp.exp(ex[(2 + lev) * chunk:(3 + lev) * chunk])
                lev_ops.append(((q * e_l).astype(BF16), (k * e_l).astype(BF16)))
        if emit:
            att = [lm_ref[d, nlev] * _dot_nt(_stack_pair(qb, pp), _stack_pair(kb, pp)) for pp in range(2)]
            for lev, (q_l, k_l) in enumerate(lev_ops):
                for pp in range(2):
                    att[pp] = att[pp] + lm_ref[d, lev] * _dot_nt(_stack_pair(q_l, pp), _stack_pair(k_l, pp))
            o_pair = [_dot(att[pp].astype(BF16), _stack_pair(vb, pp)) for pp in range(2)]
        for h in range(4):
            cs = slice(h * LANES, (h + 1) * LANES)
            s_t = old_s[b, d, h]
            if emit:
                pp, e = divmod(h, 2)
                o_ref[:, cs] = o_pair[pp][e * chunk:(e + 1) * chunk] + _dot_nt(qe[:, cs], s_t.astype(BF16))
            new_s[b, d, h] = s_t * jnp.exp(gtot[:, cs]) + _dot_tn(vb[:, cs], ktil[:, cs])

    for key, val in new_s.items():
        s_s[key] = val

    @pl.when(i == pl.num_programs(1) - 1)
    def _():
        s_out[...] = s_s[...]


def _hgrn_scan(p, lbc, consts, state, bsz, seq, chunk, emit, zero_lb):
    mstack, lmask, nlev = consts
    nc = seq // chunk
    bps = bsz
    fwd = lambda g, i: i
    rev = lambda g, i: nc - 1 - i
    p3 = p.reshape(bsz, seq, NP)

    def chunk_specs(idx, d):
        return [pl.BlockSpec((bps, chunk, GROUP), lambda g, i: (g, idx(g, i), C_HGQ // GROUP)),
                pl.BlockSpec((bps, chunk, GROUP), lambda g, i: (g, idx(g, i), C_HGI // GROUP)),
                pl.BlockSpec((bps, chunk, GROUP), lambda g, i: (g, idx(g, i), C_HGF // GROUP + d))]

    st_spec = pl.BlockSpec((bps, 2, 4, LANES, LANES), lambda g, i: (g, 0, 0, 0, 0))
    st_shape = jax.ShapeDtypeStruct((bsz, 2, 4, LANES, LANES), F32)
    o_specs = [pl.BlockSpec((bps, chunk, GROUP), lambda g, i: (g, fwd(g, i), 0)),
               pl.BlockSpec((bps, chunk, GROUP), lambda g, i: (g, rev(g, i), 0))] if emit else []
    o_shapes = [jax.ShapeDtypeStruct((bsz, seq, GROUP), F32)] * 2 if emit else []
    full = lambda a: pl.BlockSpec(a.shape, lambda g, i: (0,) * a.ndim)
    outs = pl.pallas_call(
        functools.partial(_hgrn_kernel, chunk=chunk, nlev=nlev, emit=emit, zero_lb=zero_lb),
        grid=(bsz // bps, nc),
        in_specs=chunk_specs(fwd, 0) + chunk_specs(rev, 1) + [full(lbc), full(mstack), full(lmask), st_spec],
        out_specs=o_specs + [st_spec],
        out_shape=o_shapes + [st_shape],
        scratch_shapes=[pltpu.VMEM((bps, 2, 4, LANES, LANES), F32)],
        compiler_params=_cparams(("parallel", "arbitrary")),
    )(p3, p3, p3, p3, p3, p3, lbc, mstack, lmask, state)
    if emit:
        return outs[0].reshape(bsz * seq, GROUP), outs[1].reshape(bsz * seq, GROUP), outs[2]
    return None, None, outs[0]


def _outproj_kernel(x_ref, ya_ref, yb_ref, hf_ref, hr_ref, mo_ref, of_ref, or_ref, hg_ref,
                    w_ref, g_ref, o_ref):
    acc = _dot(ya_ref[...], w_ref[0]) + _dot(yb_ref[...], w_ref[1])
    yc = _sigmoid(mo_ref[...]) * (hf_ref[...] + hr_ref[...])
    acc = acc + _dot(yc.astype(BF16), w_ref[2])
    o = of_ref[...] + or_ref[...]
    gate = _sigmoid(hg_ref[...])
    parts = []
    for h in range(4):
        cs = slice(h * LANES, (h + 1) * LANES)
        oh = o[:, cs]
        parts.append(oh * lax.rsqrt(jnp.mean(oh * oh, axis=-1, keepdims=True) + EPS) * gate[:, cs])
    yd = jnp.concatenate(parts, axis=1)
    acc = acc + _dot(yd.astype(BF16), w_ref[3])
    o_ref[...] = x_ref[...] + g_ref[...] * acc


def _out_proj(x, ya, yb, hf, hr, of, orv, p, w_all, layer, gate, rows_per_mod):
    t, d = x.shape
    tm = _row_tile(rows_per_mod, 512)
    per = rows_per_mod // tm
    grp = lambda cb=0: pl.BlockSpec((tm, GROUP), lambda i: (i, cb))
    return pl.pallas_call(
        _outproj_kernel,
        grid=(t // tm,),
        in_specs=[pl.BlockSpec((tm, d), lambda i: (i, 0)), grp(), grp(), grp(), grp(),
                  grp(C_MLO // GROUP), grp(), grp(), grp(C_HGG // GROUP),
                  pl.BlockSpec((None, 4, GROUP, d), lambda i: (layer, 0, 0, 0)),
                  pl.BlockSpec((None, 1, d), lambda i: (i // per, 0, 0))],
        out_specs=pl.BlockSpec((tm, d), lambda i: (i, 0)),
        out_shape=jax.ShapeDtypeStruct((t, d), F32),
        compiler_params=_cparams(("parallel",)),
    )(x, ya, yb, hf, hr, p, of, orv, p, w_all, gate)


def _ffn_kernel(x_ref, xp_ref, xn_ref, g_ref, sh_ref, sc_ref, gate_ref, wa_ref, wu_ref, wc_ref, wd_ref,
                o_ref, xs_ref, acc_ref, *, per):
    i = pl.program_id(0)
    j = pl.program_id(1)
    tm = x_ref.shape[0]

    @pl.when(j == 0)
    def _():
        nm = lambda x: _norm_mod(x, g_ref[...], sh_ref[...], sc_ref[...])
        keep_p = jnp.where(i % per == 0, 0.0, 1.0)
        keep_n = jnp.where(i % per == per - 1, 0.0, 1.0)
        _norm_mod_rows(x_ref, xs_ref, tm, g_ref[...], sh_ref[...], sc_ref[...])
        xs_ref[tm:tm + HALO, :] = (nm(xn_ref[...]) * keep_n).astype(BF16)
        xs_ref[tm + HALO:, :] = (nm(xp_ref[...]) * keep_p).astype(BF16)
        acc_ref[...] = jnp.zeros_like(acc_ref)

    a = _dot(xs_ref[...], wa_ref[...])
    rows = a.shape[0]
    prv = pltpu.roll(a, 1, axis=0)[0:tm]
    nxt = pltpu.roll(a, rows - 1, axis=0)[0:tm]
    conv = prv * wc_ref[0:1, :] + a[0:tm] * wc_ref[1:2, :] + nxt * wc_ref[2:3, :]
    u = _dot(xs_ref[0:tm, :], wu_ref[...])
    act = (conv * _sigmoid(conv) * u).astype(BF16)
    acc_ref[...] += _dot(act, wd_ref[...])

    @pl.when(j == pl.num_programs(1) - 1)
    def _():
        o_ref[...] = x_ref[...] + gate_ref[...] * acc_ref[...]


def _ffn(x, gain, shift, scale, gate, w_up, w_conv, w_down, layer, rows_per_mod):
    t, d = x.shape
    f = w_down.shape[1]
    tm = _row_tile(rows_per_mod, 512)
    tf = 512 if f % 512 == 0 else f
    per = rows_per_mod // tm
    nf = f // tf
    step = tm // HALO
    nblk8 = t // HALO
    mod = pl.BlockSpec((None, 1, d), lambda i, j: (i // per, 0, 0))
    return pl.pallas_call(
        functools.partial(_ffn_kernel, per=per),
        grid=(t // tm, nf),
        in_specs=[pl.BlockSpec((tm, d), lambda i, j: (i, 0)),
                  pl.BlockSpec((HALO, d), lambda i, j: (jnp.maximum(i * step - 1, 0), 0)),
                  pl.BlockSpec((HALO, d), lambda i, j: (jnp.minimum((i + 1) * step, nblk8 - 1), 0)),
                  pl.BlockSpec((1, d), lambda i, j: (0, 0)), mod, mod, mod,
                  pl.BlockSpec((None, d, tf), lambda i, j: (layer, 0, j)),
                  pl.BlockSpec((None, d, tf), lambda i, j: (layer, 0, nf + j)),
                  pl.BlockSpec((None, 3, tf), lambda i, j: (layer, 0, j)),
                  pl.BlockSpec((None, tf, d), lambda i, j: (layer, j, 0))],
        out_specs=pl.BlockSpec((tm, d), lambda i, j: (i, 0)),
        out_shape=jax.ShapeDtypeStruct((t, d), F32),
        scratch_shapes=[pltpu.VMEM((tm + 2 * HALO, d), BF16), pltpu.VMEM((tm, d), F32)],
        compiler_params=_cparams(("parallel", "arbitrary")),
    )(x, x, x, gain, shift, scale, gate, w_up, w_up, w_conv, w_down)


def _perm_w_in(w):
    n, d = w.shape[:2]
    swq = w[..., 1536:2048].reshape(n, d, 2, 4, ATT_HD).transpose(0, 1, 3, 2, 4).reshape(n, d, GROUP)
    pad = jnp.zeros((n, d, LANES - 8), w.dtype)
    cols = [w[..., 0:1536], swq, w[..., 2304:3840], w[..., 3856:6416], w[..., 2048:2304],
            w[..., 3840:3848], pad, w[..., 3848:3856], pad]
    out = jnp.concatenate(cols, axis=-1).astype(BF16)
    assert out.shape[-1] == NP
    return out


def _perm_w_out(w):
    n, _, d = w.shape
    w4 = w.reshape(n, 4, GROUP, d)
    swo = w4[:, 1].reshape(n, 2, 4, ATT_HD, d).transpose(0, 2, 1, 3, 4).reshape(n, GROUP, d)
    return jnp.stack([w4[:, 0], swo, w4[:, 2], w4[:, 3]], axis=1).astype(BF16)


def _rope_tables(seq):
    t = jnp.arange(seq)
    half = ATT_HD // 2
    inv = ROPE_BASE ** (-jnp.arange(0, half, 2, dtype=F32) / half)
    ang_r = (t // GRID_W).astype(F32)[:, None] * inv
    ang_c = (t % GRID_W).astype(F32)[:, None] * inv
    cr, sr, cc, sc = jnp.cos(ang_r), jnp.sin(ang_r), jnp.cos(ang_c), jnp.sin(ang_c)
    cos = jnp.concatenate([cr, cr, cc, cc] * 2, axis=-1)
    sin = jnp.concatenate([-sr, sr, -sc, sc] * 2, axis=-1)
    return cos, sin


def kernel(x, c, ctx, c_ctx, w_mod, b_mod, norm_mix, norm_ffn, w_in, w_out, na_qk_gain, na_rpb,
           sw_qk_gain, sw_sink, ml_conv, ml_gate_bias, hg_lb, ffn_up, ffn_conv, ffn_down):
    bsz, seq, d = x.shape
    nctx = ctx.shape[1]
    depth = w_mod.shape[0]
    ml_chunk_ctx, ml_chunk = math.gcd(ML_CHUNK, nctx), math.gcd(ML_CHUNK, seq)
    chunk = math.gcd(HG_CHUNK, math.gcd(seq, nctx))
    assert bsz + 1 <= 8

    c_all = jnp.concatenate([c, c_ctx[None], jnp.zeros((8 - bsz - 1, d), F32)], axis=0)
    mod = _modulation(c_all, w_mod, b_mod).reshape(depth, 8, 6, d)
    cos, sin = _rope_tables(seq)
    bd = jnp.asarray(np.kron(np.eye(GROUP // ATT_HD), np.ones((ATT_HD, ATT_HD))), BF16)
    lb_cum = jnp.cumsum(jax.nn.softmax(hg_lb.astype(F32), axis=0), axis=0)
    lower = lb_cum - lb_cum[:1]
    consts = _hgrn_constants(chunk)
    w_in_p = _perm_w_in(w_in)
    w_out_p = _perm_w_out(w_out)
    w_up = ffn_up.astype(BF16)
    w_dn = ffn_down.astype(BF16)
    w_cv = ffn_conv.astype(F32)

    h = x.reshape(bsz * seq, d)
    hc = ctx.reshape(bsz * nctx, d)
    for l in range(depth):
        emit_ctx = l < depth - 1
        lat = lambda k: mod[l, :bsz, k].reshape(bsz, 1, d)
        cx = lambda k: jnp.broadcast_to(mod[l, bsz, k].reshape(1, 1, d), (bsz, 1, d))
        gain = norm_mix[l].reshape(1, d)
        p_lat = _in_proj(h, gain, lat(0), lat(1), w_in_p, l, seq)
        p_ctx = _in_proj(hc, gain, cx(0), cx(1), w_in_p, l, nctx)

        gna = jnp.tile(na_qk_gain[l].astype(F32), (1, GROUP // ATT_HD))
        gsw = jnp.tile(sw_qk_gain[l].astype(F32), (1, GROUP // ATT_HD))
        naq, nak, nav, swq, swk, swv = _attn_prep(p_lat, gna, gsw, bd, cos, sin, seq, True)
        cnaq, cnak, cnav, cswq, cswk, cswv = _attn_prep(p_ctx, gna, gsw, bd, cos, sin, nctx, False)
        sink = sw_sink[l].astype(F32)
        ya = _na_attention(naq, nak, nav, cnak, cnav, _na_bias_table(na_rpb[l], seq // GRID_W), bsz, seq, nctx)
        yb = _sw_attention(swq, swk, swv, cswk, cswv, sink, bsz, seq, nctx)

        qk_lat = _ml_prep(p_lat, ml_conv[l].astype(F32), seq)
        qk_ctx = _ml_prep(p_ctx, ml_conv[l].astype(F32), nctx)
        gb = ml_gate_bias[l].astype(F32).reshape(2, 8)
        brow = jnp.zeros((8, LANES), F32).at[0:2, 0:8].set(gb)
        bcol = jnp.zeros((8, LANES), F32).at[:, 0:2].set(gb.T)
        ml0 = (jnp.zeros((bsz, 2, 2, LANES, LANES), F32), jnp.zeros((bsz, 2, 2, 8, LANES), F32),
               jnp.zeros((bsz, 2, 8, LANES), F32))
        hcf, hcr, ml_state = _mlstm_scan(qk_ctx, p_ctx, brow, bcol, ml0, bsz, nctx, ml_chunk_ctx, emit_ctx)
        hf, hr, _ = _mlstm_scan(qk_lat, p_lat, brow, bcol, ml_state, bsz, seq, ml_chunk, True)

        lb = lower[l]
        lbc = jnp.stack([jnp.maximum(jnp.log(lb), NEG), jnp.log1p(-lb), 1.0 - lb], axis=1)
        lbc = jnp.concatenate([lbc, jnp.zeros((2, 5, GROUP), F32)], axis=1)
        hg0 = jnp.zeros((bsz, 2, 4, LANES, LANES), F32)
        ocf, ocr, hg_state = _hgrn_scan(p_ctx, lbc, consts, hg0, bsz, nctx, chunk, emit_ctx, l == 0)
        of, orv, _ = _hgrn_scan(p_lat, lbc, consts, hg_state, bsz, seq, chunk, True, l == 0)

        gain2 = norm_ffn[l].reshape(1, d)
        h = _out_proj(h, ya, yb, hf, hr, of, orv, p_lat, w_out_p, l, lat(2), seq)
        h = _ffn(h, gain2, lat(3), lat(4), lat(5), w_up, w_cv, w_dn, l, seq)
        if emit_ctx:
            yac, ybc = _ctx_attention(cnaq, cnak, cnav, cswq, cswk, cswv, sink, bsz, nctx)
            hc = _out_proj(hc, yac, ybc, hcf, hcr, ocf, ocr, p_ctx, w_out_p, l, cx(2), nctx)
            hc = _ffn(hc, gain2, cx(3), cx(4), cx(5), w_up, w_cv, w_dn, l, nctx)
    return h.reshape(bsz, seq, d)
```

```python
import functools
import math

import numpy as np
import jax
import jax.numpy as jnp
from jax import lax
from jax.experimental import pallas as pl
from jax.experimental.pallas import tpu as pltpu

F32 = jnp.float32
BF16 = jnp.bfloat16

GRID_W = 64
ATT_HD = 64
NA_ROWS = 8
NA_COLS = 16
SW_BLOCK = 128
ROPE_BASE = 10000.0
EPS = 1e-6
NEG = -1e30
GROUP = 512
LANES = 128
HALO = 8
VMEM_LIMIT = 56 * 1024 * 1024

C_NAQ, C_NAK, C_NAV, C_SWQ = 0, 512, 1024, 1536
C_MLQK, C_MLV, C_MLO = 2048, 2560, 3072
C_HGQ, C_HGI, C_HGF, C_HGG = 3584, 4096, 4608, 5632
C_SWK, C_SWV = 6144, 6272
C_MLI, C_MLF = 6400, 6528
NP = 6656
ML_CHUNK = 256
ML_STEP_ROWS = 1024
HG_CHUNK = 64
SW_PERM = (0, 4, 1, 5, 2, 6, 3, 7)


def _cparams(sem):
    return pltpu.CompilerParams(dimension_semantics=sem, vmem_limit_bytes=VMEM_LIMIT)


def _sigmoid(x):
    return 1.0 / (1.0 + jnp.exp(-x))


def _log1p_unit(e):
    return jnp.log(1.0 + e)


def _log_sigmoid(x):
    return jnp.minimum(x, 0.0) - _log1p_unit(jnp.exp(-jnp.abs(x)))


def _split3(x):
    hi = x.astype(BF16)
    r = x - hi.astype(F32)
    mid = r.astype(BF16)
    lo = (r - mid.astype(F32)).astype(BF16)
    return hi, mid, lo


def _dot(a, b):
    return jnp.dot(a, b, preferred_element_type=F32)


def _dot_nt(a, b):
    return lax.dot_general(a, b, (((1,), (1,)), ((), ())), preferred_element_type=F32)


def _dot_tn(a, b):
    return lax.dot_general(a, b, (((0,), (0,)), ((), ())), preferred_element_type=F32)


def _exact_left(m01, x):
    hi, mid, lo = _split3(x)
    return _dot(m01, hi) + _dot(m01, mid) + _dot(m01, lo)


def _mod_kernel(c_ref, w_ref, b_ref, o_ref):
    c = c_ref[...]
    a = c * _sigmoid(c)
    a_hi = a.astype(BF16)
    a_lo = (a - a_hi.astype(F32)).astype(BF16)
    w = w_ref[...]
    w_hi = w.astype(BF16)
    w_lo = (w - w_hi.astype(F32)).astype(BF16)
    o_ref[...] = _dot(a_hi, w_hi) + _dot(a_hi, w_lo) + _dot(a_lo, w_hi) + b_ref[...]


def _modulation(c_all, w_mod, b_mod):
    depth, d, n = w_mod.shape
    tn = 1024 if n % 1024 == 0 else n
    return pl.pallas_call(
        _mod_kernel,
        grid=(depth, n // tn),
        in_specs=[pl.BlockSpec((8, d), lambda l, j: (0, 0)),
                  pl.BlockSpec((None, d, tn), lambda l, j: (l, 0, j)),
                  pl.BlockSpec((None, 1, tn), lambda l, j: (l, 0, j))],
        out_specs=pl.BlockSpec((None, 8, tn), lambda l, j: (l, 0, j)),
        out_shape=jax.ShapeDtypeStruct((depth, 8, n), F32),
        compiler_params=_cparams(("parallel", "parallel")),
    )(c_all, w_mod, b_mod.reshape(depth, 1, n))


def _norm_mod(x, gain, shift, scale):
    y = x * lax.rsqrt(jnp.mean(x * x, axis=-1, keepdims=True) + EPS) * gain
    return y * (1.0 + scale) + shift


NORM_ROWS = 16


def _norm_mod_rows(src_ref, dst_ref, nrows, gain, shift, scale):
    amp = gain * (1.0 + scale)

    def body(c, carry):
        rows = pl.ds(pl.multiple_of(c * NORM_ROWS, NORM_ROWS), NORM_ROWS)
        x = src_ref[rows, :]
        y = x * lax.rsqrt(jnp.mean(x * x, axis=-1, keepdims=True) + EPS) * amp + shift
        dst_ref[rows, :] = y.astype(BF16)
        return carry

    assert nrows % NORM_ROWS == 0
    lax.fori_loop(0, nrows // NORM_ROWS, body, 0, unroll=4)


def _inproj_kernel(x_ref, g_ref, sh_ref, sc_ref, w_ref, o_ref, xn_ref):
    @pl.when(pl.program_id(1) == 0)
    def _():
        _norm_mod_rows(x_ref, xn_ref, x_ref.shape[0], g_ref[...], sh_ref[...], sc_ref[...])

    o_ref[...] = _dot(xn_ref[...], w_ref[...])


def _row_tile(rows_per_mod, pref):
    tm = min(pref, rows_per_mod)
    assert rows_per_mod % tm == 0
    return tm


def _in_proj(x, gain, shift, scale, w_all, layer, rows_per_mod):
    t, d = x.shape
    n = w_all.shape[2]
    tm = _row_tile(rows_per_mod, 1024)
    tn = 512 if n % 512 == 0 else n
    per = rows_per_mod // tm
    return pl.pallas_call(
        _inproj_kernel,
        grid=(t // tm, n // tn),
        in_specs=[pl.BlockSpec((tm, d), lambda i, j: (i, 0)),
                  pl.BlockSpec((1, d), lambda i, j: (0, 0)),
                  pl.BlockSpec((None, 1, d), lambda i, j: (i // per, 0, 0)),
                  pl.BlockSpec((None, 1, d), lambda i, j: (i // per, 0, 0)),
                  pl.BlockSpec((None, d, tn), lambda i, j: (layer, 0, j))],
        out_specs=pl.BlockSpec((tm, tn), lambda i, j: (i, j)),
        out_shape=jax.ShapeDtypeStruct((t, n), F32),
        scratch_shapes=[pltpu.VMEM((tm, d), BF16)],
        compiler_params=_cparams(("parallel", "arbitrary")),
    )(x, gain, shift, scale, w_all)


def _head_rms(x, gain_row, bd):
    xx = x * x
    hi = xx.astype(BF16)
    lo = (xx - hi.astype(F32)).astype(BF16)
    ss = _dot(hi, bd) + _dot(lo, bd)
    return x * lax.rsqrt(ss * (1.0 / ATT_HD) + EPS) * gain_row


def _rope(x, cos, sin, first):
    w = x.shape[-1]
    nxt = pltpu.roll(x, w - 16, axis=1)
    prv = pltpu.roll(x, 16, axis=1)
    return x * cos + jnp.where(first, nxt, prv) * sin


def _attn_prep_kernel(p_ref, pkv_ref, gna_ref, gsw_ref, bd_ref, cos_ref, sin_ref,
                      naq_ref, nak_ref, nav_ref, swq_ref, swk_ref, swv_ref, *, rope):
    bd = bd_ref[...]
    scale = ATT_HD ** -0.5
    naq_ref[...] = (_head_rms(p_ref[:, C_NAQ:C_NAQ + GROUP], gna_ref[0:1, :], bd) * scale).astype(BF16)
    nak_ref[...] = _head_rms(p_ref[:, C_NAK:C_NAK + GROUP], gna_ref[1:2, :], bd).astype(BF16)
    nav_ref[...] = p_ref[:, C_NAV:C_NAV + GROUP].astype(BF16)
    q = _head_rms(p_ref[:, C_SWQ:C_SWQ + GROUP], gsw_ref[0:1, :], bd)
    k = _head_rms(pkv_ref[:, 0:LANES], gsw_ref[1:2, 0:LANES], bd[0:LANES, 0:LANES])
    if rope:
        cos = cos_ref[...]
        sin = sin_ref[...]
        lane = lax.broadcasted_iota(jnp.int32, (1, GROUP), 1)
        first = (lane % 32) < 16
        q = _rope(q, jnp.concatenate([cos] * 4, axis=1), jnp.concatenate([sin] * 4, axis=1), first)
        k = _rope(k, cos, sin, first[:, 0:LANES])
    swq_ref[...] = (q * scale).astype(BF16)
    swk_ref[...] = k.astype(BF16)
    swv_ref[...] = pkv_ref[:, LANES:2 * LANES].astype(BF16)


def _attn_prep(p, gna, gsw, bd, cos, sin, seq, rope):
    t = p.shape[0]
    tm = _row_tile(seq, 512)
    per = seq // tm
    out = lambda w: jax.ShapeDtypeStruct((t, w), BF16)
    ospec = lambda w: pl.BlockSpec((tm, w), lambda i: (i, 0))
    return pl.pallas_call(
        functools.partial(_attn_prep_kernel, rope=rope),
        grid=(t // tm,),
        in_specs=[pl.BlockSpec((tm, C_MLQK), lambda i: (i, 0)),
                  pl.BlockSpec((tm, 2 * LANES), lambda i: (i, C_SWK // (2 * LANES))),
                  pl.BlockSpec((2, GROUP), lambda i: (0, 0)),
                  pl.BlockSpec((2, GROUP), lambda i: (0, 0)),
                  pl.BlockSpec((GROUP, GROUP), lambda i: (0, 0)),
                  pl.BlockSpec((tm, LANES), lambda i: (i % per, 0)),
                  pl.BlockSpec((tm, LANES), lambda i: (i % per, 0))],
        out_specs=[ospec(GROUP), ospec(GROUP), ospec(GROUP), ospec(GROUP), ospec(LANES), ospec(LANES)],
        out_shape=[out(GROUP), out(GROUP), out(GROUP), out(GROUP), out(LANES), out(LANES)],
        compiler_params=_cparams(("parallel",)),
    )(p, p, gna, gsw, bd, cos, sin)


def _pair_rows(q_pair):
    lane = lax.broadcasted_iota(jnp.int32, q_pair.shape, 1)
    zero = jnp.zeros_like(q_pair)
    return jnp.concatenate([jnp.where(lane < ATT_HD, q_pair, zero),
                            jnp.where(lane >= ATT_HD, q_pair, zero)], axis=0)


def _pair_merge(o2, inv_l):
    m = o2.shape[0] // 2
    lane = lax.broadcasted_iota(jnp.int32, (m, LANES), 1)
    return jnp.where(lane < ATT_HD, o2[0:m] * inv_l[0:m], o2[m:] * inv_l[m:])


NA_RPS = 4


def _softmax_pv(parts, sink=None):
    m = functools.reduce(jnp.maximum, [jnp.max(s, axis=-1, keepdims=True) for s, _ in parts])
    if sink is not None:
        m = jnp.maximum(m, sink)
    acc = None
    for s, v in parts:
        e = jnp.exp((s - m).astype(BF16))
        o = _dot(e, jnp.concatenate([v, jnp.ones_like(v)], axis=1))
        acc = o if acc is None else acc + o
    l = acc[:, LANES:LANES + 1]
    if sink is not None:
        l = l + jnp.exp(sink - m)
    return acc[:, 0:LANES], l


NA_WIN = NA_ROWS + NA_RPS


def _na_window(step, rows):
    return jnp.clip(step * NA_RPS - NA_ROWS // 2, 0, rows - NA_WIN)


def _na_kernel(q_ref, k_ref, v_ref, kc_ref, vc_ref, b_ref, o_ref, *, rows):
    start = pl.multiple_of(_na_window(pl.program_id(1), rows) * GRID_W, GRID_W)
    nkey = NA_WIN * GRID_W
    for p in range(GROUP // LANES):
        cs = slice(p * LANES, (p + 1) * LANES)
        qs = jnp.concatenate([_pair_rows(q_ref[rr * GRID_W:(rr + 1) * GRID_W, cs]) for rr in range(NA_RPS)],
                             axis=0)
        kw = k_ref[pl.ds(start, nkey), cs]
        vw = v_ref[pl.ds(start, nkey), cs]
        s_nb = _dot_nt(qs, kw) + b_ref[p]
        s_cx = _dot_nt(qs, kc_ref[:, cs])
        o2, l = _softmax_pv([(s_nb, vw), (s_cx, vc_ref[:, cs])])
        inv_l = 1.0 / l
        for rr in range(NA_RPS):
            sl = slice(rr * 2 * GRID_W, (rr + 1) * 2 * GRID_W)
            o_ref[rr * GRID_W:(rr + 1) * GRID_W, cs] = _pair_merge(o2[sl], inv_l[sl]).astype(BF16)


def _na_attention(q, k, v, kc, vc, bias, bsz, seq, ctx):
    rows = seq // GRID_W
    assert rows >= NA_WIN and rows % NA_RPS == 0
    steps = rows // NA_RPS
    k3, v3 = k.reshape(bsz, seq, GROUP), v.reshape(bsz, seq, GROUP)
    kc3, vc3 = kc.reshape(bsz, ctx, GROUP), vc.reshape(bsz, ctx, GROUP)
    qspec = pl.BlockSpec((NA_RPS * GRID_W, GROUP), lambda b, r: (b * steps + r, 0))
    variant = lambda b, r: ((r > 0).astype(jnp.int32) + (r == steps - 1).astype(jnp.int32), 0, 0, 0)
    return pl.pallas_call(
        functools.partial(_na_kernel, rows=rows),
        grid=(bsz, steps),
        in_specs=[qspec,
                  pl.BlockSpec((None, seq, GROUP), lambda b, r: (b, 0, 0)),
                  pl.BlockSpec((None, seq, GROUP), lambda b, r: (b, 0, 0)),
                  pl.BlockSpec((None, ctx, GROUP), lambda b, r: (b, 0, 0)),
                  pl.BlockSpec((None, ctx, GROUP), lambda b, r: (b, 0, 0)),
                  pl.BlockSpec((None,) + bias.shape[1:], variant)],
        out_specs=qspec,
        out_shape=jax.ShapeDtypeStruct((bsz * seq, GROUP), BF16),
        compiler_params=_cparams(("parallel", "arbitrary")),
    )(q, k3, v3, kc3, vc3, bias)


def _na_bias_table(rpb, rows):
    nh = rpb.shape[0]
    steps = rows // NA_RPS
    c = np.arange(GRID_W)
    kc = np.arange(GRID_W)
    cstart = np.clip(c - NA_COLS // 2, 0, GRID_W - NA_COLS)
    ok = (kc[None, :] >= cstart[:, None]) & (kc[None, :] < cstart[:, None] + NA_COLS)
    dc = np.clip(kc[None, :] - c[:, None] + NA_COLS - 1, 0, 2 * NA_COLS - 2)
    onehot = (dc.reshape(1, -1) == np.arange(2 * NA_COLS - 1)[:, None]).astype(np.float32)
    t = jnp.dot(rpb.astype(F32).reshape(-1, 2 * NA_COLS - 1), onehot, precision=lax.Precision.HIGHEST)
    t = jnp.where(ok[None, None], t.reshape(nh, -1, GRID_W, GRID_W), NEG)

    def step_layout(s):
        ws = int(np.clip(s * NA_RPS - NA_ROWS // 2, 0, rows - NA_WIN))
        r = s * NA_RPS + np.arange(NA_RPS)[:, None]
        rs = np.clip(r - NA_ROWS // 2, 0, rows - NA_ROWS)
        krow = ws + np.arange(NA_WIN)[None, :]
        return (krow - r + NA_ROWS - 1), (krow >= rs) & (krow < rs + NA_ROWS)

    layouts = [step_layout(s) for s in range(steps)]
    variants = [layouts[0], layouts[1], layouts[-1]]
    for s, (idx, valid) in enumerate(layouts):
        want = variants[(s > 0) + (s == steps - 1)]
        assert (valid == want[1]).all() and (idx[valid] == want[0][want[1]]).all()

    ndr = 2 * NA_ROWS - 1
    width = (ndr + 3) * GRID_W
    t2 = jnp.transpose(t, (0, 2, 1, 3)).reshape(nh, GRID_W, ndr * GRID_W)
    t2 = jnp.pad(t2, ((0, 0), (0, 0), (GRID_W, width - (ndr + 1) * GRID_W)), constant_values=NEG)
    t2s = jnp.pad(t2[:, :, GRID_W:], ((0, 0), (0, 0), (0, GRID_W)), constant_values=NEG)

    def build(ta_ref, tb_ref, o_ref):
        lane = lax.broadcasted_iota(jnp.int32, (GRID_W, LANES), 1)
        neg = jnp.full((GRID_W, LANES), NEG, F32)
        for v, (idx, valid) in enumerate(variants):
            for rr in range(NA_RPS):
                for e in range(2):
                    row0 = (rr * 2 + e) * GRID_W
                    for jj in range(NA_WIN // 2):
                        j = 2 * jj
                        ok0, ok1 = bool(valid[rr, j]), bool(valid[rr, j + 1])
                        if not (ok0 or ok1):
                            piece = neg
                        else:
                            blk = int(idx[rr, j] if ok0 else idx[rr, j + 1] - 1) + 1
                            src = ta_ref if blk % 2 == 0 else tb_ref
                            off = (blk - blk % 2) * GRID_W
                            piece = src[e, :, off:off + LANES]
                            if not ok0:
                                piece = jnp.where(lane >= GRID_W, piece, NEG)
                            if not ok1:
                                piece = jnp.where(lane < GRID_W, piece, NEG)
                        o_ref[v, row0:row0 + GRID_W, j * GRID_W:(j + 2) * GRID_W] = piece

    npair = nh // 2
    tspec = pl.BlockSpec((2, GRID_W, width), lambda p: (p, 0, 0))
    return pl.pallas_call(
        build,
        grid=(npair,),
        in_specs=[tspec, tspec],
        out_specs=pl.BlockSpec((3, None, NA_RPS * 2 * GRID_W, NA_WIN * GRID_W), lambda p: (0, p, 0, 0)),
        out_shape=jax.ShapeDtypeStruct((3, npair, NA_RPS * 2 * GRID_W, NA_WIN * GRID_W), F32),
        compiler_params=_cparams(("parallel",)),
    )(t2, t2s)


SW_BPS = 2


def _sw_kernel(sink_ref, q_ref, k_ref, v_ref, kc_ref, vc_ref, o_ref, *, seq):
    nq = SW_BPS * SW_BLOCK
    nwin = (SW_BPS + 2) * SW_BLOCK
    q0 = pl.program_id(1) * nq
    start = pl.multiple_of(jnp.clip(q0 - SW_BLOCK, 0, seq - nwin), SW_BLOCK)
    kw = k_ref[pl.ds(start, nwin), :]
    vw = v_ref[pl.ds(start, nwin), :]
    kc = kc_ref[...]
    vc = vc_ref[...]
    row = lax.broadcasted_iota(jnp.int32, (2 * nq, nwin), 0)
    col = lax.broadcasted_iota(jnp.int32, (2 * nq, nwin), 1)
    ok = jnp.abs((start + col) - (q0 + row % nq)) <= SW_BLOCK
    half = lax.broadcasted_iota(jnp.int32, (2 * nq, 1), 0) < nq
    for j in range(GROUP // LANES):
        cs = slice(j * LANES, (j + 1) * LANES)
        qs = _pair_rows(q_ref[:, cs])
        s_bd = jnp.where(ok, _dot_nt(qs, kw), NEG)
        s_cx = _dot_nt(qs, kc)
        sink = jnp.where(half, sink_ref[j], sink_ref[j + 4])
        o2, l = _softmax_pv([(s_bd, vw), (s_cx, vc)], sink)
        o_ref[:, cs] = _pair_merge(o2, 1.0 / l).astype(BF16)


def _sw_attention(q, k, v, kc, vc, sink, bsz, seq, ctx):
    assert seq % (SW_BPS * SW_BLOCK) == 0 and seq >= (SW_BPS + 2) * SW_BLOCK
    nb = seq // (SW_BPS * SW_BLOCK)
    k3, v3 = k.reshape(bsz, seq, LANES), v.reshape(bsz, seq, LANES)
    kc3, vc3 = kc.reshape(bsz, ctx, LANES), vc.reshape(bsz, ctx, LANES)
    return pl.pallas_call(
        functools.partial(_sw_kernel, seq=seq),
        grid=(bsz, nb),
        in_specs=[pl.BlockSpec(memory_space=pltpu.SMEM),
                  pl.BlockSpec((SW_BPS * SW_BLOCK, GROUP), lambda b, n: (b * nb + n, 0)),
                  pl.BlockSpec((None, seq, LANES), lambda b, n: (b, 0, 0)),
                  pl.BlockSpec((None, seq, LANES), lambda b, n: (b, 0, 0)),
                  pl.BlockSpec((None, ctx, LANES), lambda b, n: (b, 0, 0)),
                  pl.BlockSpec((None, ctx, LANES), lambda b, n: (b, 0, 0))],
        out_specs=pl.BlockSpec((SW_BPS * SW_BLOCK, GROUP), lambda b, n: (b * nb + n, 0)),
        out_shape=jax.ShapeDtypeStruct((bsz * seq, GROUP), BF16),
        compiler_params=_cparams(("parallel", "arbitrary")),
    )(sink, q, k3, v3, kc3, vc3)


def _ctx_attn_kernel(sink_ref, qa_ref, ka_ref, va_ref, qb_ref, kb_ref, vb_ref, oa_ref, ob_ref):
    for p in range(GROUP // LANES):
        cs = slice(p * LANES, (p + 1) * LANES)
        qs = _pair_rows(qa_ref[:, cs])
        o2, l = _softmax_pv([(_dot_nt(qs, ka_ref[:, cs]), va_ref[:, cs])])
        oa_ref[:, cs] = _pair_merge(o2, 1.0 / l).astype(BF16)
    ctx = qb_ref.shape[0]
    half = lax.broadcasted_iota(jnp.int32, (2 * ctx, 1), 0) < ctx
    for j in range(GROUP // LANES):
        cs = slice(j * LANES, (j + 1) * LANES)
        qs = _pair_rows(qb_ref[:, cs])
        sink = jnp.where(half, sink_ref[j], sink_ref[j + 4])
        o2, l = _softmax_pv([(_dot_nt(qs, kb_ref[...]), vb_ref[...])], sink)
        ob_ref[:, cs] = _pair_merge(o2, 1.0 / l).astype(BF16)


def _ctx_attention(qa, ka, va, qb, kb, vb, sink, bsz, ctx):
    big = pl.BlockSpec((ctx, GROUP), lambda b: (b, 0))
    small = pl.BlockSpec((ctx, LANES), lambda b: (b, 0))
    return pl.pallas_call(
        _ctx_attn_kernel,
        grid=(bsz,),
        in_specs=[pl.BlockSpec(memory_space=pltpu.SMEM), big, big, big, big, small, small],
        out_specs=[big, big],
        out_shape=[jax.ShapeDtypeStruct((bsz * ctx, GROUP), BF16)] * 2,
        compiler_params=_cparams(("parallel",)),
    )(sink, qa, ka, va, qb, kb, vb)


def _halo_specs(tm, width, col_block, nblk8):
    step = tm // HALO
    return [pl.BlockSpec((tm, width), lambda i: (i, col_block)),
            pl.BlockSpec((HALO, width), lambda i: (jnp.maximum(i * step - 1, 0), col_block)),
            pl.BlockSpec((HALO, width), lambda i: (jnp.minimum((i + 1) * step, nblk8 - 1), col_block))]


def _ml_prep_kernel(x_ref, xp_ref, xn_ref, w_ref, o_ref, *, per):
    i = pl.program_id(0)
    x = x_ref[...]
    tm = x.shape[0]
    row = lax.broadcasted_iota(jnp.int32, (tm, 1), 0)
    keep_p = jnp.where(i % per == 0, 0.0, 1.0)
    keep_n = jnp.where(i % per == per - 1, 0.0, 1.0)
    prv = jnp.where(row == 0, xp_ref[HALO - 1:HALO, :] * keep_p, pltpu.roll(x, 1, axis=0))
    nxt = jnp.where(row == tm - 1, xn_ref[0:1, :] * keep_n, pltpu.roll(x, tm - 1, axis=0))
    y = prv * w_ref[0:1, :] + x * w_ref[1:2, :] + nxt * w_ref[2:3, :]
    y = y * _sigmoid(y)
    lane = lax.broadcasted_iota(jnp.int32, (1, GROUP), 1)
    o_ref[...] = (y * jnp.where(lane < GROUP // 2, ATT_HD ** -0.5, 1.0)).astype(BF16)


def _ml_prep(p, conv_w, seq):
    t = p.shape[0]
    tm = _row_tile(seq, 512)
    per = seq // tm
    return pl.pallas_call(
        functools.partial(_ml_prep_kernel, per=per),
        grid=(t // tm,),
        in_specs=_halo_specs(tm, GROUP, C_MLQK // GROUP, t // HALO)
        + [pl.BlockSpec((3, GROUP), lambda i: (0, 0))],
        out_specs=pl.BlockSpec((tm, GROUP), lambda i: (i, 0)),
        out_shape=jax.ShapeDtypeStruct((t, GROUP), BF16),
        compiler_params=_cparams(("parallel",)),
    )(p, p, p, conv_w)


def _mlstm_kernel(*refs, chunk, emit):
    (qkf_ref, vf_ref, gif_ref, gff_ref, qkr_ref, vr_ref, gir_ref, gfr_ref,
     brow_ref, c0_ref, m0_ref) = refs[:11]
    if emit:
        hf_ref, hr_ref = refs[11:13]
        rest = refs[13:]
    else:
        hf_ref = hr_ref = None
        rest = refs[11:]
    c_out, m_out, c_s, m_s = rest
    i = pl.program_id(1)

    @pl.when(i == 0)
    def _():
        c_s[...] = c0_ref[...]
        m_s[...] = m0_ref[...]

    row = lax.broadcasted_iota(jnp.int32, (chunk, chunk), 0)
    col = lax.broadcasted_iota(jnp.int32, (chunk, chunk), 1)
    trow = lax.broadcasted_iota(jnp.int32, (chunk, 1), 0)
    lane = lax.broadcasted_iota(jnp.int32, (chunk, LANES), 1)
    srow = lax.broadcasted_iota(jnp.int32, (LANES, 1), 0)
    dirs = ((qkf_ref, vf_ref, gif_ref, gff_ref, hf_ref), (qkr_ref, vr_ref, gir_ref, gfr_ref, hr_ref))
    chains = [(b, d) for b in range(qkf_ref.shape[0]) for d in range(2)]
    old_c = {(b, d, pp): c_s[b, d, pp] for b, d in chains for pp in range(2)}
    old_m = {(b, d): m_s[b, d, 0:1, :] for b, d in chains}
    new_c, new_m = {}, {}
    for b, d in chains:
        qk_ref, v_ref, gi_ref, gf_ref, h_ref = [r if r is None else r.at[b] for r in dirs[d]]
        causal = (col <= row) if d == 0 else (col >= row)
        ipre = gi_ref[...] + brow_ref[0:1, :]
        fl = _log_sigmoid(gf_ref[...] + brow_ref[1:2, :])
        fcum = _exact_left(jnp.where(causal, 1.0, 0.0).astype(BF16), fl)
        ftot = jnp.sum(fl, axis=0, keepdims=True)
        m_prev = old_m[b, d]
        w_log = ftot - fcum + ipre
        m_new = jnp.maximum(ftot + m_prev, jnp.max(w_log, axis=0, keepdims=True))
        cd = jnp.exp(ftot + m_prev - m_new)
        w = jnp.exp(w_log - m_new)
        if emit:
            b_col = ipre - fcum
            run = b_col
            sh = 1
            while sh < chunk:
                if d == 0:
                    run = jnp.maximum(run, jnp.where(trow >= sh, pltpu.roll(run, sh, axis=0), NEG))
                else:
                    run = jnp.maximum(run, jnp.where(trow < chunk - sh, pltpu.roll(run, chunk - sh, axis=0), NEG))
                sh *= 2
            log_inter = fcum + m_prev
            m_row = jnp.maximum(log_inter, fcum + run)
            inter_w = jnp.exp(log_inter - m_row)
            fcm = fcum - m_row
            floor = jnp.exp(-m_row)
            b_rows = b_col.T[0:8, :]
        for pp in range(2):
            qp = qk_ref[:, pp * LANES:(pp + 1) * LANES]
            kp = qk_ref[:, GROUP // 2 + pp * LANES:GROUP // 2 + (pp + 1) * LANES]
            cn = old_c[b, d, pp]
            inc = None
            for e in range(2):
                h = 2 * pp + e
                c = 4 * d + h
                sel = (lane < ATT_HD) if e == 0 else (lane >= ATT_HD)
                vh = v_ref[:, h * LANES:(h + 1) * LANES]
                if emit:
                    qm = jnp.where(sel, qp, jnp.zeros_like(qp))
                    d_mat = jnp.exp(jnp.where(causal, fcm[:, c:c + 1] + b_rows[c:c + 1, :], NEG))
                    s_mat = (_dot_nt(qm, kp) * d_mat).astype(BF16)
                    vb = vh.astype(BF16)
                    acc = (inter_w[:, c:c + 1] * _dot(qm, cn.astype(BF16))
                           + _dot(s_mat, jnp.concatenate([vb, jnp.ones_like(vb)], axis=1)))
                    den = jnp.maximum(jnp.abs(acc[:, LANES:]), floor[:, c:c + 1])
                    h_ref[:, h * LANES:(h + 1) * LANES] = acc[:, 0:LANES] / den
                km = jnp.where(sel, kp, jnp.zeros_like(kp))
                wc = jnp.broadcast_to(w[:, c:c + 1], (chunk, LANES))
                upd = _dot_tn(km, jnp.concatenate([wc * vh, wc], axis=1).astype(BF16))
                inc = upd if inc is None else inc + upd
            c_lo, c_hi = 4 * d + 2 * pp, 4 * d + 2 * pp + 1
            new_c[b, d, pp] = (jnp.where(srow < ATT_HD, cd[:, c_lo:c_lo + 1], cd[:, c_hi:c_hi + 1]) * cn
                               + inc)
        new_m[b, d] = m_new
    for key, val in new_c.items():
        c_s[key] = val
    for (b, d), val in new_m.items():
        m_s[b, d, 0:1, :] = val

    @pl.when(i == pl.num_programs(1) - 1)
    def _():
        c_out[...] = c_s[...]
        m_out[...] = m_s[...]


def _mlstm_scan(qk, p, brow, state, bsz, seq, chunk, emit):
    nc = seq // chunk
    bps = bsz if bsz * chunk <= ML_STEP_ROWS else max(1, ML_STEP_ROWS // chunk)
    assert bsz % bps == 0
    fwd = lambda g, i: i
    rev = lambda g, i: nc - 1 - i
    qk3 = qk.reshape(bsz, seq, GROUP)
    p3 = p.reshape(bsz, seq, NP)

    def chunk_specs(idx):
        return [pl.BlockSpec((bps, chunk, GROUP), lambda g, i: (g, idx(g, i), 0)),
                pl.BlockSpec((bps, chunk, GROUP), lambda g, i: (g, idx(g, i), C_MLV // GROUP)),
                pl.BlockSpec((bps, chunk, LANES), lambda g, i: (g, idx(g, i), C_MLI // LANES)),
                pl.BlockSpec((bps, chunk, LANES), lambda g, i: (g, idx(g, i), C_MLF // LANES))]

    st_dims = [(2, 2, LANES, 2 * LANES), (2, 8, LANES)]
    st_specs = [pl.BlockSpec((bps,) + s, lambda g, i, n=len(s): (g,) + (0,) * n) for s in st_dims]
    st_shapes = [jax.ShapeDtypeStruct((bsz,) + s, F32) for s in st_dims]
    h_specs = [pl.BlockSpec((bps, chunk, GROUP), lambda g, i: (g, fwd(g, i), 0)),
               pl.BlockSpec((bps, chunk, GROUP), lambda g, i: (g, rev(g, i), 0))] if emit else []
    h_shapes = [jax.ShapeDtypeStruct((bsz, seq, GROUP), F32)] * 2 if emit else []
    outs = pl.pallas_call(
        functools.partial(_mlstm_kernel, chunk=chunk, emit=emit),
        grid=(bsz // bps, nc),
        in_specs=chunk_specs(fwd) + chunk_specs(rev)
        + [pl.BlockSpec((8, LANES), lambda g, i: (0, 0))] + st_specs,
        out_specs=h_specs + st_specs,
        out_shape=h_shapes + st_shapes,
        scratch_shapes=[pltpu.VMEM((bps,) + s, F32) for s in st_dims],
        compiler_params=_cparams(("parallel", "arbitrary")),
    )(qk3, p3, p3, p3, qk3, p3, p3, p3, brow, *state)
    if emit:
        return (outs[0].reshape(bsz * seq, GROUP), outs[1].reshape(bsz * seq, GROUP), tuple(outs[2:]))
    return None, None, tuple(outs)


def _hgrn_constants(chunk):
    nlev = int(np.log2(chunk))
    assert 2 ** nlev == chunk
    t = np.arange(chunk)
    mats = [(t[None, :] <= t[:, None]), (t[None, :] > t[:, None])]
    masks = []
    for lev in range(nlev):
        n = chunk >> (lev + 1)
        b0 = t - t % (2 * n)
        upper = (t % (2 * n)) >= n
        m_up = (t[None, :] >= (b0 + n)[:, None]) & (t[None, :] <= t[:, None])
        m_lo = (t[None, :] > t[:, None]) & (t[None, :] <= (b0 + n - 1)[:, None])
        mats.append(np.where(upper[:, None], m_up, m_lo))
        same = (t[:, None] // (2 * n)) == (t[None, :] // (2 * n))
        masks.append(same & upper[:, None] & ~upper[None, :])
    masks.append(t[:, None] == t[None, :])
    mstack = np.concatenate(mats, axis=0).astype(np.float32)
    lmask = np.stack([np.kron(np.eye(2), m.astype(np.float32)) for m in masks])
    nr = mstack.shape[0] // chunk
    mstack_rev = mstack.reshape(nr, chunk, chunk)[:, ::-1, ::-1].reshape(-1, chunk)
    lmask_rev = lmask.reshape(nlev + 1, 2, chunk, 2, chunk)[:, :, ::-1, :, ::-1].reshape(lmask.shape)
    return (jnp.asarray(np.stack([mstack, mstack_rev]), BF16),
            jnp.asarray(np.stack([lmask, lmask_rev]), F32), nlev)


def _stack_pair(x, pp):
    return jnp.concatenate([x[:, (2 * pp) * LANES:(2 * pp + 1) * LANES],
                            x[:, (2 * pp + 1) * LANES:(2 * pp + 2) * LANES]], axis=0)


def _hgrn_kernel(*refs, chunk, nlev, emit, zero_lb):
    (qf_ref, vf_ref, ff_ref, qr_ref, vr_ref, fr_ref, lb_ref, ms_ref, lm_ref, s0_ref) = refs[:10]
    if emit:
        of_ref, or_ref = refs[10:12]
        rest = refs[12:]
    else:
        of_ref = or_ref = None
        rest = refs[10:]
    s_out, s_s = rest
    i = pl.program_id(1)

    @pl.when(i == 0)
    def _():
        s_s[...] = s0_ref[...]

    dirs = ((qf_ref, vf_ref, ff_ref, of_ref), (qr_ref, vr_ref, fr_ref, or_ref))
    chains = [(b, d) for b in range(vf_ref.shape[0]) for d in range(2)]
    old_s = {(b, d, h): s_s[b, d, h] for b, d in chains for h in range(4)}
    new_s = {}

    for b, d in chains:
        q_ref, v_ref, f_ref, o_ref = [r if r is None else r.at[b] for r in dirs[d]]
        fpre = f_ref[...]
        vb = v_ref[...].astype(BF16)
        e_f = jnp.exp(-jnp.abs(fpre))
        ls = jnp.minimum(fpre, 0.0) - _log1p_unit(e_f)
        sig_neg = jnp.where(fpre >= 0.0, e_f, 1.0) / (1.0 + e_f)
        if zero_lb:
            logf, k = ls, sig_neg
        else:
            a = lb_ref[d, 0:1, :]
            bb = lb_ref[d, 1:2, :] + ls
            logf = jnp.maximum(a, bb) + _log1p_unit(jnp.exp(-jnp.abs(a - bb)))
            k = lb_ref[d, 2:3, :] * sig_neg
        hi = logf.astype(BF16)
        lo = (logf - hi.astype(F32)).astype(BF16)
        gtot = jnp.sum(logf, axis=0, keepdims=True)
        if emit:
            qpre = q_ref[...]
            q = qpre * _sigmoid(qpre)
            qb = q.astype(BF16)
            kb = k.astype(BF16)
        ms = ms_ref[d]
        ex = _dot(ms, hi) + _dot(ms, lo)
        ktil = (k * jnp.exp(ex[chunk:2 * chunk])).astype(BF16)
        if emit:
            qe = (q * jnp.exp(ex[0:chunk])).astype(BF16)
            lev_ops = []
            for lev in range(nlev):
                e_l = jnp.exp(ex[(2 + lev) * chunk:(3 + lev) * chunk])
                lev_ops.append(((q * e_l).astype(BF16), (k * e_l).astype(BF16)))
        if emit:
            att = [lm_ref[d, nlev] * _dot_nt(_stack_pair(qb, pp), _stack_pair(kb, pp)) for pp in range(2)]
            for lev, (q_l, k_l) in enumerate(lev_ops):
                for pp in range(2):
                    att[pp] = att[pp] + lm_ref[d, lev] * _dot_nt(_stack_pair(q_l, pp), _stack_pair(k_l, pp))
            o_pair = [_dot(att[pp].astype(BF16), _stack_pair(vb, pp)) for pp in range(2)]
        for h in range(4):
            cs = slice(h * LANES, (h + 1) * LANES)
            s_t = old_s[b, d, h]
            if emit:
                pp, e = divmod(h, 2)
                o_ref[:, cs] = o_pair[pp][e * chunk:(e + 1) * chunk] + _dot_nt(qe[:, cs], s_t.astype(BF16))
            new_s[b, d, h] = s_t * jnp.exp(gtot[:, cs]) + _dot_tn(vb[:, cs], ktil[:, cs])

    for key, val in new_s.items():
        s_s[key] = val

    @pl.when(i == pl.num_programs(1) - 1)
    def _():
        s_out[...] = s_s[...]


def _hgrn_scan(p, lbc, consts, state, bsz, seq, chunk, emit, zero_lb):
    mstack, lmask, nlev = consts
    nc = seq // chunk
    bps = bsz
    fwd = lambda g, i: i
    rev = lambda g, i: nc - 1 - i
    p3 = p.reshape(bsz, seq, NP)

    def chunk_specs(idx, d):
        return [pl.BlockSpec((bps, chunk, GROUP), lambda g, i: (g, idx(g, i), C_HGQ // GROUP)),
                pl.BlockSpec((bps, chunk, GROUP), lambda g, i: (g, idx(g, i), C_HGI // GROUP)),
                pl.BlockSpec((bps, chunk, GROUP), lambda g, i: (g, idx(g, i), C_HGF // GROUP + d))]

    st_spec = pl.BlockSpec((bps, 2, 4, LANES, LANES), lambda g, i: (g, 0, 0, 0, 0))
    st_shape = jax.ShapeDtypeStruct((bsz, 2, 4, LANES, LANES), F32)
    o_specs = [pl.BlockSpec((bps, chunk, GROUP), lambda g, i: (g, fwd(g, i), 0)),
               pl.BlockSpec((bps, chunk, GROUP), lambda g, i: (g, rev(g, i), 0))] if emit else []
    o_shapes = [jax.ShapeDtypeStruct((bsz, seq, GROUP), F32)] * 2 if emit else []
    full = lambda a: pl.BlockSpec(a.shape, lambda g, i: (0,) * a.ndim)
    outs = pl.pallas_call(
        functools.partial(_hgrn_kernel, chunk=chunk, nlev=nlev, emit=emit, zero_lb=zero_lb),
        grid=(bsz // bps, nc),
        in_specs=chunk_specs(fwd, 0) + chunk_specs(rev, 1) + [full(lbc), full(mstack), full(lmask), st_spec],
        out_specs=o_specs + [st_spec],
        out_shape=o_shapes + [st_shape],
        scratch_shapes=[pltpu.VMEM((bps, 2, 4, LANES, LANES), F32)],
        compiler_params=_cparams(("parallel", "arbitrary")),
    )(p3, p3, p3, p3, p3, p3, lbc, mstack, lmask, state)
    if emit:
        return outs[0].reshape(bsz * seq, GROUP), outs[1].reshape(bsz * seq, GROUP), outs[2]
    return None, None, outs[0]


def _outproj_kernel(x_ref, ya_ref, yb_ref, hf_ref, hr_ref, mo_ref, of_ref, or_ref, hg_ref,
                    w_ref, g_ref, o_ref):
    acc = _dot(ya_ref[...], w_ref[0]) + _dot(yb_ref[...], w_ref[1])
    yc = _sigmoid(mo_ref[...]) * (hf_ref[...] + hr_ref[...])
    acc = acc + _dot(yc.astype(BF16), w_ref[2])
    o = of_ref[...] + or_ref[...]
    gate = _sigmoid(hg_ref[...])
    parts = []
    for h in range(4):
        cs = slice(h * LANES, (h + 1) * LANES)
        oh = o[:, cs]
        parts.append(oh * lax.rsqrt(jnp.mean(oh * oh, axis=-1, keepdims=True) + EPS) * gate[:, cs])
    yd = jnp.concatenate(parts, axis=1)
    acc = acc + _dot(yd.astype(BF16), w_ref[3])
    o_ref[...] = x_ref[...] + g_ref[...] * acc


def _out_proj(x, ya, yb, hf, hr, of, orv, p, w_all, layer, gate, rows_per_mod):
    t, d = x.shape
    tm = _row_tile(rows_per_mod, 512)
    per = rows_per_mod // tm
    grp = lambda cb=0: pl.BlockSpec((tm, GROUP), lambda i: (i, cb))
    return pl.pallas_call(
        _outproj_kernel,
        grid=(t // tm,),
        in_specs=[pl.BlockSpec((tm, d), lambda i: (i, 0)), grp(), grp(), grp(), grp(),
                  grp(C_MLO // GROUP), grp(), grp(), grp(C_HGG // GROUP),
                  pl.BlockSpec((None, 4, GROUP, d), lambda i: (layer, 0, 0, 0)),
                  pl.BlockSpec((None, 1, d), lambda i: (i // per, 0, 0))],
        out_specs=pl.BlockSpec((tm, d), lambda i: (i, 0)),
        out_shape=jax.ShapeDtypeStruct((t, d), F32),
        compiler_params=_cparams(("parallel",)),
    )(x, ya, yb, hf, hr, p, of, orv, p, w_all, gate)


def _ffn_kernel(x_ref, xp_ref, xn_ref, g_ref, sh_ref, sc_ref, gate_ref, wa_ref, wu_ref, wc_ref, wd_ref,
                o_ref, xs_ref, acc_ref, *, per, inner):
    i = pl.program_id(0)
    j = pl.program_id(1)
    tm = x_ref.shape[0]

    @pl.when(j == 0)
    def _():
        nm = lambda x: _norm_mod(x, g_ref[...], sh_ref[...], sc_ref[...])
        keep_p = jnp.where(i % per == 0, 0.0, 1.0)
        keep_n = jnp.where(i % per == per - 1, 0.0, 1.0)
        _norm_mod_rows(x_ref, xs_ref, tm, g_ref[...], sh_ref[...], sc_ref[...])
        xs_ref[tm:tm + HALO, :] = (nm(xn_ref[...]) * keep_n).astype(BF16)
        xs_ref[tm + HALO:, :] = (nm(xp_ref[...]) * keep_p).astype(BF16)
        acc_ref[...] = jnp.zeros_like(acc_ref)

    a = _dot(xs_ref[...], wa_ref[...])
    rows = a.shape[0]
    prv = pltpu.roll(a, 1, axis=0)[0:tm]
    nxt = pltpu.roll(a, rows - 1, axis=0)[0:tm]
    if inner:
        pos = lax.broadcasted_iota(jnp.int32, (tm, 1), 0) % inner
        prv = jnp.where(pos == 0, 0.0, prv)
        nxt = jnp.where(pos == inner - 1, 0.0, nxt)
    conv = prv * wc_ref[0:1, :] + a[0:tm] * wc_ref[1:2, :] + nxt * wc_ref[2:3, :]
    u = _dot(xs_ref[0:tm, :], wu_ref[...])
    act = (conv * _sigmoid(conv) * u).astype(BF16)
    acc_ref[...] += _dot(act, wd_ref[...])

    @pl.when(j == pl.num_programs(1) - 1)
    def _():
        o_ref[...] = x_ref[...] + gate_ref[...] * acc_ref[...]


def _ffn(x, gain, shift, scale, gate, w_up, w_conv, w_down, layer, seq):
    t, d = x.shape
    f = w_down.shape[1]
    tm = 512 if t % 512 == 0 else t
    assert seq % tm == 0 or (tm % seq == 0 and shift.shape[0] == 1)
    tf = 512 if f % 512 == 0 else f
    per = max(seq // tm, 1)
    inner = seq if seq < tm else 0
    nf = f // tf
    step = tm // HALO
    nblk8 = t // HALO
    shared = shift.shape[0] == 1
    mod = pl.BlockSpec((None, 1, d), lambda i, j: (0 if shared else i // per, 0, 0))
    return pl.pallas_call(
        functools.partial(_ffn_kernel, per=per, inner=inner),
        grid=(t // tm, nf),
        in_specs=[pl.BlockSpec((tm, d), lambda i, j: (i, 0)),
                  pl.BlockSpec((HALO, d), lambda i, j: (jnp.maximum(i * step - 1, 0), 0)),
                  pl.BlockSpec((HALO, d), lambda i, j: (jnp.minimum((i + 1) * step, nblk8 - 1), 0)),
                  pl.BlockSpec((1, d), lambda i, j: (0, 0)), mod, mod, mod,
                  pl.BlockSpec((None, d, tf), lambda i, j: (layer, 0, j)),
                  pl.BlockSpec((None, d, tf), lambda i, j: (layer, 0, nf + j)),
                  pl.BlockSpec((None, 3, tf), lambda i, j: (layer, 0, j)),
                  pl.BlockSpec((None, tf, d), lambda i, j: (layer, j, 0))],
        out_specs=pl.BlockSpec((tm, d), lambda i, j: (i, 0)),
        out_shape=jax.ShapeDtypeStruct((t, d), F32),
        scratch_shapes=[pltpu.VMEM((tm + 2 * HALO, d), BF16), pltpu.VMEM((tm, d), F32)],
        compiler_params=_cparams(("parallel", "arbitrary")),
    )(x, x, x, gain, shift, scale, gate, w_up, w_up, w_conv, w_down)


def _perm_w_in(w):
    n, d = w.shape[:2]
    swq = w[..., 1536:2048].reshape(n, d, 2, 4, ATT_HD).transpose(0, 1, 3, 2, 4).reshape(n, d, GROUP)
    pad = jnp.zeros((n, d, LANES - 8), w.dtype)
    cols = [w[..., 0:1536], swq, w[..., 2304:3840], w[..., 3856:6416], w[..., 2048:2304],
            w[..., 3840:3848], pad, w[..., 3848:3856], pad]
    out = jnp.concatenate(cols, axis=-1).astype(BF16)
    assert out.shape[-1] == NP
    return out


def _perm_w_out(w):
    n, _, d = w.shape
    w4 = w.reshape(n, 4, GROUP, d)
    swo = w4[:, 1].reshape(n, 2, 4, ATT_HD, d).transpose(0, 2, 1, 3, 4).reshape(n, GROUP, d)
    return jnp.stack([w4[:, 0], swo, w4[:, 2], w4[:, 3]], axis=1).astype(BF16)


def _rope_tables(seq):
    t = jnp.arange(seq)
    half = ATT_HD // 2
    inv = ROPE_BASE ** (-jnp.arange(0, half, 2, dtype=F32) / half)
    ang_r = (t // GRID_W).astype(F32)[:, None] * inv
    ang_c = (t % GRID_W).astype(F32)[:, None] * inv
    cr, sr, cc, sc = jnp.cos(ang_r), jnp.sin(ang_r), jnp.cos(ang_c), jnp.sin(ang_c)
    cos = jnp.concatenate([cr, cr, cc, cc] * 2, axis=-1)
    sin = jnp.concatenate([-sr, sr, -sc, sc] * 2, axis=-1)
    return cos, sin


def kernel(x, c, ctx, c_ctx, w_mod, b_mod, norm_mix, norm_ffn, w_in, w_out, na_qk_gain, na_rpb,
           sw_qk_gain, sw_sink, ml_conv, ml_gate_bias, hg_lb, ffn_up, ffn_conv, ffn_down):
    bsz, seq, d = x.shape
    nctx = ctx.shape[1]
    depth = w_mod.shape[0]
    ml_chunk_ctx, ml_chunk = math.gcd(ML_CHUNK, nctx), math.gcd(ML_CHUNK, seq)
    chunk = math.gcd(HG_CHUNK, math.gcd(seq, nctx))
    assert bsz + 1 <= 8

    c_all = jnp.concatenate([c, c_ctx[None], jnp.zeros((8 - bsz - 1, d), F32)], axis=0)
    mod = _modulation(c_all, w_mod, b_mod).reshape(depth, 8, 6, d)
    cos, sin = _rope_tables(seq)
    bd = jnp.asarray(np.kron(np.eye(GROUP // ATT_HD), np.ones((ATT_HD, ATT_HD))), BF16)
    lb_cum = jnp.cumsum(jax.nn.softmax(hg_lb.astype(F32), axis=0), axis=0)
    lower = lb_cum - lb_cum[:1]
    consts = _hgrn_constants(chunk)
    w_in_p = _perm_w_in(w_in)
    w_out_p = _perm_w_out(w_out)
    w_up = ffn_up.astype(BF16)
    w_dn = ffn_down.astype(BF16)
    w_cv = ffn_conv.astype(F32)

    h = x.reshape(bsz * seq, d)
    hc = ctx.reshape(bsz * nctx, d)
    for l in range(depth):
        emit_ctx = l < depth - 1
        lat = lambda k: mod[l, :bsz, k].reshape(bsz, 1, d)
        cx = lambda k: mod[l, bsz, k].reshape(1, 1, d)
        gain = norm_mix[l].reshape(1, d)
        p_lat = _in_proj(h, gain, lat(0), lat(1), w_in_p, l, seq)
        p_ctx = _in_proj(hc, gain, cx(0), cx(1), w_in_p, l, bsz * nctx)

        gna = jnp.tile(na_qk_gain[l].astype(F32), (1, GROUP // ATT_HD))
        gsw = jnp.tile(sw_qk_gain[l].astype(F32), (1, GROUP // ATT_HD))
        naq, nak, nav, swq, swk, swv = _attn_prep(p_lat, gna, gsw, bd, cos, sin, seq, True)
        cnaq, cnak, cnav, cswq, cswk, cswv = _attn_prep(p_ctx, gna, gsw, bd, cos, sin, nctx, False)
        sink = sw_sink[l].astype(F32)
        ya = _na_attention(naq, nak, nav, cnak, cnav, _na_bias_table(na_rpb[l], seq // GRID_W), bsz, seq, nctx)
        yb = _sw_attention(swq, swk, swv, cswk, cswv, sink, bsz, seq, nctx)

        qk_lat = _ml_prep(p_lat, ml_conv[l].astype(F32), seq)
        qk_ctx = _ml_prep(p_ctx, ml_conv[l].astype(F32), nctx)
        gb = ml_gate_bias[l].astype(F32).reshape(2, 8)
        brow = jnp.zeros((8, LANES), F32).at[0:2, 0:8].set(gb)
        ml0 = (jnp.zeros((bsz, 2, 2, LANES, 2 * LANES), F32), jnp.zeros((bsz, 2, 8, LANES), F32))
        hcf, hcr, ml_state = _mlstm_scan(qk_ctx, p_ctx, brow, ml0, bsz, nctx, ml_chunk_ctx, emit_ctx)
        hf, hr, _ = _mlstm_scan(qk_lat, p_lat, brow, ml_state, bsz, seq, ml_chunk, True)

        lb = lower[l]
        lbc = jnp.stack([jnp.maximum(jnp.log(lb), NEG), jnp.log1p(-lb), 1.0 - lb], axis=1)
        lbc = jnp.concatenate([lbc, jnp.zeros((2, 5, GROUP), F32)], axis=1)
        hg0 = jnp.zeros((bsz, 2, 4, LANES, LANES), F32)
        ocf, ocr, hg_state = _hgrn_scan(p_ctx, lbc, consts, hg0, bsz, nctx, chunk, emit_ctx, l == 0)
        of, orv, _ = _hgrn_scan(p_lat, lbc, consts, hg_state, bsz, seq, chunk, True, l == 0)

        gain2 = norm_ffn[l].reshape(1, d)
        h = _out_proj(h, ya, yb, hf, hr, of, orv, p_lat, w_out_p, l, lat(2), seq)
        h = _ffn(h, gain2, lat(3), lat(4), lat(5), w_up, w_cv, w_dn, l, seq)
        if emit_ctx:
            yac, ybc = _ctx_attention(cnaq, cnak, cnav, cswq, cswk, cswv, sink, bsz, nctx)
            hc = _out_proj(hc, yac, ybc, hcf, hcr, ocf, ocr, p_ctx, w_out_p, l, cx(2), bsz * nctx)
            hc = _ffn(hc, gain2, cx(3), cx(4), cx(5), w_up, w_cv, w_dn, l, nctx)
    return h.reshape(bsz, seq, d)
```

```python
import functools
import math

import numpy as np
import jax
import jax.numpy as jnp
from jax import lax
from jax.experimental import pallas as pl
from jax.experimental.pallas import tpu as pltpu

F32 = jnp.float32
BF16 = jnp.bfloat16

GRID_W = 64
ATT_HD = 64
NA_ROWS = 8
NA_COLS = 16
SW_BLOCK = 128
ROPE_BASE = 10000.0
EPS = 1e-6
NEG = -1e30
LOG2E = 1.4426950408889634
GROUP = 512
LANES = 128
HALO = 8
VMEM_LIMIT = 56 * 1024 * 1024

C_NAQ, C_NAK, C_NAV, C_SWQ = 0, 512, 1024, 1536
C_MLQK, C_MLV, C_MLO = 2048, 2560, 3072
C_HGQ, C_HGI, C_HGF, C_HGG = 3584, 4096, 4608, 5632
C_SWK, C_SWV = 6144, 6272
C_MLI, C_MLF = 6400, 6528
NP = 6656
ML_CHUNK = 256
ML_STEP_ROWS = 1024
HG_CHUNK = 64
SW_PERM = (0, 4, 1, 5, 2, 6, 3, 7)


def _cparams(sem):
    return pltpu.CompilerParams(dimension_semantics=sem, vmem_limit_bytes=VMEM_LIMIT)


def _sigmoid(x):
    return 1.0 / (1.0 + jnp.exp(-x))


def _log1p_unit(e):
    return jnp.log(1.0 + e)


def _log_sigmoid(x):
    return jnp.minimum(x, 0.0) - _log1p_unit(jnp.exp(-jnp.abs(x)))


def _split3(x):
    hi = x.astype(BF16)
    r = x - hi.astype(F32)
    mid = r.astype(BF16)
    lo = (r - mid.astype(F32)).astype(BF16)
    return hi, mid, lo


def _dot(a, b):
    return jnp.dot(a, b, preferred_element_type=F32)


def _dot_nt(a, b):
    return lax.dot_general(a, b, (((1,), (1,)), ((), ())), preferred_element_type=F32)


def _dot_tn(a, b):
    return lax.dot_general(a, b, (((0,), (0,)), ((), ())), preferred_element_type=F32)


def _exact_left(m01, x):
    hi, mid, lo = _split3(x)
    return _dot(m01, hi) + _dot(m01, mid) + _dot(m01, lo)


def _mod_kernel(c_ref, w_ref, b_ref, o_ref):
    c = c_ref[...]
    a = c * _sigmoid(c)
    a_hi = a.astype(BF16)
    a_lo = (a - a_hi.astype(F32)).astype(BF16)
    w = w_ref[...]
    w_hi = w.astype(BF16)
    w_lo = (w - w_hi.astype(F32)).astype(BF16)
    o_ref[...] = _dot(a_hi, w_hi) + _dot(a_hi, w_lo) + _dot(a_lo, w_hi) + b_ref[...]


def _modulation(c_all, w_mod, b_mod):
    depth, d, n = w_mod.shape
    tn = 1024 if n % 1024 == 0 else n
    return pl.pallas_call(
        _mod_kernel,
        grid=(depth, n // tn),
        in_specs=[pl.BlockSpec((8, d), lambda l, j: (0, 0)),
                  pl.BlockSpec((None, d, tn), lambda l, j: (l, 0, j)),
                  pl.BlockSpec((None, 1, tn), lambda l, j: (l, 0, j))],
        out_specs=pl.BlockSpec((None, 8, tn), lambda l, j: (l, 0, j)),
        out_shape=jax.ShapeDtypeStruct((depth, 8, n), F32),
        compiler_params=_cparams(("parallel", "parallel")),
    )(c_all, w_mod, b_mod.reshape(depth, 1, n))


def _norm_mod(x, gain, shift, scale):
    y = x * lax.rsqrt(jnp.mean(x * x, axis=-1, keepdims=True) + EPS) * gain
    return y * (1.0 + scale) + shift


NORM_ROWS = 16


def _norm_mod_rows(src_ref, dst_ref, nrows, gain, shift, scale):
    amp = gain * (1.0 + scale)

    def body(c, carry):
        rows = pl.ds(pl.multiple_of(c * NORM_ROWS, NORM_ROWS), NORM_ROWS)
        x = src_ref[rows, :]
        y = x * lax.rsqrt(jnp.mean(x * x, axis=-1, keepdims=True) + EPS) * amp + shift
        dst_ref[rows, :] = y.astype(BF16)
        return carry

    assert nrows % NORM_ROWS == 0
    lax.fori_loop(0, nrows // NORM_ROWS, body, 0, unroll=4)


def _inproj_kernel(x_ref, g_ref, sh_ref, sc_ref, w_ref, o_ref, xn_ref):
    @pl.when(pl.program_id(1) == 0)
    def _():
        _norm_mod_rows(x_ref, xn_ref, x_ref.shape[0], g_ref[...], sh_ref[...], sc_ref[...])

    o_ref[...] = _dot(xn_ref[...], w_ref[...])


def _row_tile(rows_per_mod, pref):
    tm = min(pref, rows_per_mod)
    assert rows_per_mod % tm == 0
    return tm


def _in_proj(x, gain, shift, scale, w_all, layer, rows_per_mod):
    t, d = x.shape
    n = w_all.shape[2]
    tm = _row_tile(rows_per_mod, 512)
    tn = next(c for c in (1664, 512, n) if n % c == 0)
    per = rows_per_mod // tm
    return pl.pallas_call(
        _inproj_kernel,
        grid=(t // tm, n // tn),
        in_specs=[pl.BlockSpec((tm, d), lambda i, j: (i, 0)),
                  pl.BlockSpec((1, d), lambda i, j: (0, 0)),
                  pl.BlockSpec((None, 1, d), lambda i, j: (i // per, 0, 0)),
                  pl.BlockSpec((None, 1, d), lambda i, j: (i // per, 0, 0)),
                  pl.BlockSpec((None, d, tn), lambda i, j: (layer, 0, j))],
        out_specs=pl.BlockSpec((tm, tn), lambda i, j: (i, j)),
        out_shape=jax.ShapeDtypeStruct((t, n), F32),
        scratch_shapes=[pltpu.VMEM((tm, d), BF16)],
        compiler_params=_cparams(("parallel", "arbitrary")),
    )(x, gain, shift, scale, w_all)


def _head_rms(x, gain_row, bd):
    xx = x * x
    hi = xx.astype(BF16)
    lo = (xx - hi.astype(F32)).astype(BF16)
    ss = _dot(hi, bd) + _dot(lo, bd)
    return x * lax.rsqrt(ss * (1.0 / ATT_HD) + EPS) * gain_row


def _rope(x, cos, sin, first):
    w = x.shape[-1]
    nxt = pltpu.roll(x, w - 16, axis=1)
    prv = pltpu.roll(x, 16, axis=1)
    return x * cos + jnp.where(first, nxt, prv) * sin


def _attn_prep_kernel(p_ref, pkv_ref, gna_ref, gsw_ref, bd_ref, cos_ref, sin_ref,
                      naq_ref, nak_ref, nav_ref, swq_ref, swk_ref, swv_ref, *, rope):
    bd = bd_ref[...]
    scale = ATT_HD ** -0.5
    naq_ref[...] = (_head_rms(p_ref[:, C_NAQ:C_NAQ + GROUP], gna_ref[0:1, :], bd) * scale).astype(BF16)
    nak_ref[...] = _head_rms(p_ref[:, C_NAK:C_NAK + GROUP], gna_ref[1:2, :], bd).astype(BF16)
    nav_ref[...] = p_ref[:, C_NAV:C_NAV + GROUP].astype(BF16)
    q = _head_rms(p_ref[:, C_SWQ:C_SWQ + GROUP], gsw_ref[0:1, :], bd)
    k = _head_rms(pkv_ref[:, 0:LANES], gsw_ref[1:2, 0:LANES], bd[0:LANES, 0:LANES])
    if rope:
        cos = cos_ref[...]
        sin = sin_ref[...]
        lane = lax.broadcasted_iota(jnp.int32, (1, GROUP), 1)
        first = (lane % 32) < 16
        q = _rope(q, jnp.concatenate([cos] * 4, axis=1), jnp.concatenate([sin] * 4, axis=1), first)
        k = _rope(k, cos, sin, first[:, 0:LANES])
    swq_ref[...] = (q * scale).astype(BF16)
    swk_ref[...] = k.astype(BF16)
    swv_ref[...] = pkv_ref[:, LANES:2 * LANES].astype(BF16)


def _attn_prep(p, gna, gsw, bd, cos, sin, seq, rope):
    t = p.shape[0]
    tm = _row_tile(seq, 512)
    per = seq // tm
    out = lambda w: jax.ShapeDtypeStruct((t, w), BF16)
    ospec = lambda w: pl.BlockSpec((tm, w), lambda i: (i, 0))
    return pl.pallas_call(
        functools.partial(_attn_prep_kernel, rope=rope),
        grid=(t // tm,),
        in_specs=[pl.BlockSpec((tm, C_MLQK), lambda i: (i, 0)),
                  pl.BlockSpec((tm, 2 * LANES), lambda i: (i, C_SWK // (2 * LANES))),
                  pl.BlockSpec((2, GROUP), lambda i: (0, 0)),
                  pl.BlockSpec((2, GROUP), lambda i: (0, 0)),
                  pl.BlockSpec((GROUP, GROUP), lambda i: (0, 0)),
                  pl.BlockSpec((tm, LANES), lambda i: (i % per, 0)),
                  pl.BlockSpec((tm, LANES), lambda i: (i % per, 0))],
        out_specs=[ospec(GROUP), ospec(GROUP), ospec(GROUP), ospec(GROUP), ospec(LANES), ospec(LANES)],
        out_shape=[out(GROUP), out(GROUP), out(GROUP), out(GROUP), out(LANES), out(LANES)],
        compiler_params=_cparams(("parallel",)),
    )(p, p, gna, gsw, bd, cos, sin)


def _pair_rows(q_pair):
    lane = lax.broadcasted_iota(jnp.int32, q_pair.shape, 1)
    zero = jnp.zeros_like(q_pair)
    return jnp.concatenate([jnp.where(lane < ATT_HD, q_pair, zero),
                            jnp.where(lane >= ATT_HD, q_pair, zero)], axis=0)


def _pair_merge(o2, inv_l):
    m = o2.shape[0] // 2
    lane = lax.broadcasted_iota(jnp.int32, (m, LANES), 1)
    return jnp.where(lane < ATT_HD, o2[0:m] * inv_l[0:m], o2[m:] * inv_l[m:])


NA_RPS = 4


def _softmax_pv(parts, sink=None):
    m = functools.reduce(jnp.maximum, [jnp.max(s, axis=-1, keepdims=True) for s, _ in parts])
    if sink is not None:
        m = jnp.maximum(m, sink)
    acc = None
    for s, v in parts:
        e = jnp.exp((s - m).astype(BF16))
        o = _dot(e, jnp.concatenate([v, jnp.ones_like(v)], axis=1))
        acc = o if acc is None else acc + o
    l = acc[:, LANES:LANES + 1]
    if sink is not None:
        l = l + jnp.exp(sink - m)
    return acc[:, 0:LANES], l


NA_WIN = NA_ROWS + NA_RPS


def _na_window(step, rows):
    return jnp.clip(step * NA_RPS - NA_ROWS // 2, 0, rows - NA_WIN)


def _na_kernel(q_ref, k_ref, v_ref, kc_ref, vc_ref, b_ref, o_ref, *, rows):
    start = pl.multiple_of(_na_window(pl.program_id(1), rows) * GRID_W, GRID_W)
    nkey = NA_WIN * GRID_W
    for p in range(GROUP // LANES):
        cs = slice(p * LANES, (p + 1) * LANES)
        qs = jnp.concatenate([_pair_rows(q_ref[rr * GRID_W:(rr + 1) * GRID_W, cs]) for rr in range(NA_RPS)],
                             axis=0)
        kw = k_ref[pl.ds(start, nkey), cs]
        vw = v_ref[pl.ds(start, nkey), cs]
        s_nb = _dot_nt(qs, kw) + b_ref[p]
        s_cx = _dot_nt(qs, kc_ref[:, cs])
        o2, l = _softmax_pv([(s_nb, vw), (s_cx, vc_ref[:, cs])])
        inv_l = 1.0 / l
        for rr in range(NA_RPS):
            sl = slice(rr * 2 * GRID_W, (rr + 1) * 2 * GRID_W)
            o_ref[rr * GRID_W:(rr + 1) * GRID_W, cs] = _pair_merge(o2[sl], inv_l[sl]).astype(BF16)


def _na_attention(q, k, v, kc, vc, bias, bsz, seq, ctx):
    rows = seq // GRID_W
    assert rows >= NA_WIN and rows % NA_RPS == 0
    steps = rows // NA_RPS
    k3, v3 = k.reshape(bsz, seq, GROUP), v.reshape(bsz, seq, GROUP)
    kc3, vc3 = kc.reshape(bsz, ctx, GROUP), vc.reshape(bsz, ctx, GROUP)
    qspec = pl.BlockSpec((NA_RPS * GRID_W, GROUP), lambda b, r: (b * steps + r, 0))
    variant = lambda b, r: ((r > 0).astype(jnp.int32) + (r == steps - 1).astype(jnp.int32), 0, 0, 0)
    return pl.pallas_call(
        functools.partial(_na_kernel, rows=rows),
        grid=(bsz, steps),
        in_specs=[qspec,
                  pl.BlockSpec((None, seq, GROUP), lambda b, r: (b, 0, 0)),
                  pl.BlockSpec((None, seq, GROUP), lambda b, r: (b, 0, 0)),
                  pl.BlockSpec((None, ctx, GROUP), lambda b, r: (b, 0, 0)),
                  pl.BlockSpec((None, ctx, GROUP), lambda b, r: (b, 0, 0)),
                  pl.BlockSpec((None,) + bias.shape[1:], variant)],
        out_specs=qspec,
        out_shape=jax.ShapeDtypeStruct((bsz * seq, GROUP), BF16),
        compiler_params=_cparams(("parallel", "arbitrary")),
    )(q, k3, v3, kc3, vc3, bias)


def _na_bias_table(rpb, rows):
    nh = rpb.shape[0]
    steps = rows // NA_RPS
    c = np.arange(GRID_W)
    kc = np.arange(GRID_W)
    cstart = np.clip(c - NA_COLS // 2, 0, GRID_W - NA_COLS)
    ok = (kc[None, :] >= cstart[:, None]) & (kc[None, :] < cstart[:, None] + NA_COLS)
    dc = np.clip(kc[None, :] - c[:, None] + NA_COLS - 1, 0, 2 * NA_COLS - 2)
    onehot = (dc.reshape(1, -1) == np.arange(2 * NA_COLS - 1)[:, None]).astype(np.float32)
    t = jnp.dot(rpb.astype(F32).reshape(-1, 2 * NA_COLS - 1), onehot, precision=lax.Precision.HIGHEST)
    t = jnp.where(ok[None, None], t.reshape(nh, -1, GRID_W, GRID_W), NEG)

    def step_layout(s):
        ws = int(np.clip(s * NA_RPS - NA_ROWS // 2, 0, rows - NA_WIN))
        r = s * NA_RPS + np.arange(NA_RPS)[:, None]
        rs = np.clip(r - NA_ROWS // 2, 0, rows - NA_ROWS)
        krow = ws + np.arange(NA_WIN)[None, :]
        return (krow - r + NA_ROWS - 1), (krow >= rs) & (krow < rs + NA_ROWS)

    layouts = [step_layout(s) for s in range(steps)]
    variants = [layouts[0], layouts[1], layouts[-1]]
    for s, (idx, valid) in enumerate(layouts):
        want = variants[(s > 0) + (s == steps - 1)]
        assert (valid == want[1]).all() and (idx[valid] == want[0][want[1]]).all()

    ndr = 2 * NA_ROWS - 1
    width = (ndr + 3) * GRID_W
    t2 = jnp.transpose(t, (0, 2, 1, 3)).reshape(nh, GRID_W, ndr * GRID_W)
    t2 = jnp.pad(t2, ((0, 0), (0, 0), (GRID_W, width - (ndr + 1) * GRID_W)), constant_values=NEG)
    t2s = jnp.pad(t2[:, :, GRID_W:], ((0, 0), (0, 0), (0, GRID_W)), constant_values=NEG)

    def build(ta_ref, tb_ref, o_ref):
        lane = lax.broadcasted_iota(jnp.int32, (GRID_W, LANES), 1)
        neg = jnp.full((GRID_W, LANES), NEG, F32)
        for v, (idx, valid) in enumerate(variants):
            for rr in range(NA_RPS):
                for e in range(2):
                    row0 = (rr * 2 + e) * GRID_W
                    for jj in range(NA_WIN // 2):
                        j = 2 * jj
                        ok0, ok1 = bool(valid[rr, j]), bool(valid[rr, j + 1])
                        if not (ok0 or ok1):
                            piece = neg
                        else:
                            blk = int(idx[rr, j] if ok0 else idx[rr, j + 1] - 1) + 1
                            src = ta_ref if blk % 2 == 0 else tb_ref
                            off = (blk - blk % 2) * GRID_W
                            piece = src[e, :, off:off + LANES]
                            if not ok0:
                                piece = jnp.where(lane >= GRID_W, piece, NEG)
                            if not ok1:
                                piece = jnp.where(lane < GRID_W, piece, NEG)
                        o_ref[v, row0:row0 + GRID_W, j * GRID_W:(j + 2) * GRID_W] = piece

    npair = nh // 2
    tspec = pl.BlockSpec((2, GRID_W, width), lambda p: (p, 0, 0))
    return pl.pallas_call(
        build,
        grid=(npair,),
        in_specs=[tspec, tspec],
        out_specs=pl.BlockSpec((3, None, NA_RPS * 2 * GRID_W, NA_WIN * GRID_W), lambda p: (0, p, 0, 0)),
        out_shape=jax.ShapeDtypeStruct((3, npair, NA_RPS * 2 * GRID_W, NA_WIN * GRID_W), F32),
        compiler_params=_cparams(("parallel",)),
    )(t2, t2s)


SW_BPS = 2


def _sw_kernel(sink_ref, q_ref, k_ref, v_ref, kc_ref, vc_ref, o_ref, *, seq):
    nq = SW_BPS * SW_BLOCK
    nwin = (SW_BPS + 2) * SW_BLOCK
    q0 = pl.program_id(1) * nq
    start = pl.multiple_of(jnp.clip(q0 - SW_BLOCK, 0, seq - nwin), SW_BLOCK)
    kw = k_ref[pl.ds(start, nwin), :]
    vw = v_ref[pl.ds(start, nwin), :]
    kc = kc_ref[...]
    vc = vc_ref[...]
    row = lax.broadcasted_iota(jnp.int32, (2 * nq, nwin), 0)
    col = lax.broadcasted_iota(jnp.int32, (2 * nq, nwin), 1)
    ok = jnp.abs((start + col) - (q0 + row % nq)) <= SW_BLOCK
    half = lax.broadcasted_iota(jnp.int32, (2 * nq, 1), 0) < nq
    for j in range(GROUP // LANES):
        cs = slice(j * LANES, (j + 1) * LANES)
        qs = _pair_rows(q_ref[:, cs])
        s_bd = jnp.where(ok, _dot_nt(qs, kw), NEG)
        s_cx = _dot_nt(qs, kc)
        sink = jnp.where(half, sink_ref[j], sink_ref[j + 4])
        o2, l = _softmax_pv([(s_bd, vw), (s_cx, vc)], sink)
        o_ref[:, cs] = _pair_merge(o2, 1.0 / l).astype(BF16)


def _sw_attention(q, k, v, kc, vc, sink, bsz, seq, ctx):
    assert seq % (SW_BPS * SW_BLOCK) == 0 and seq >= (SW_BPS + 2) * SW_BLOCK
    nb = seq // (SW_BPS * SW_BLOCK)
    k3, v3 = k.reshape(bsz, seq, LANES), v.reshape(bsz, seq, LANES)
    kc3, vc3 = kc.reshape(bsz, ctx, LANES), vc.reshape(bsz, ctx, LANES)
    return pl.pallas_call(
        functools.partial(_sw_kernel, seq=seq),
        grid=(bsz, nb),
        in_specs=[pl.BlockSpec(memory_space=pltpu.SMEM),
                  pl.BlockSpec((SW_BPS * SW_BLOCK, GROUP), lambda b, n: (b * nb + n, 0)),
                  pl.BlockSpec((None, seq, LANES), lambda b, n: (b, 0, 0)),
                  pl.BlockSpec((None, seq, LANES), lambda b, n: (b, 0, 0)),
                  pl.BlockSpec((None, ctx, LANES), lambda b, n: (b, 0, 0)),
                  pl.BlockSpec((None, ctx, LANES), lambda b, n: (b, 0, 0))],
        out_specs=pl.BlockSpec((SW_BPS * SW_BLOCK, GROUP), lambda b, n: (b * nb + n, 0)),
        out_shape=jax.ShapeDtypeStruct((bsz * seq, GROUP), BF16),
        compiler_params=_cparams(("parallel", "arbitrary")),
    )(sink, q, k3, v3, kc3, vc3)


def _ctx_attn_kernel(sink_ref, qa_ref, ka_ref, va_ref, qb_ref, kb_ref, vb_ref, oa_ref, ob_ref):
    for p in range(GROUP // LANES):
        cs = slice(p * LANES, (p + 1) * LANES)
        qs = _pair_rows(qa_ref[:, cs])
        o2, l = _softmax_pv([(_dot_nt(qs, ka_ref[:, cs]), va_ref[:, cs])])
        oa_ref[:, cs] = _pair_merge(o2, 1.0 / l).astype(BF16)
    ctx = qb_ref.shape[0]
    half = lax.broadcasted_iota(jnp.int32, (2 * ctx, 1), 0) < ctx
    for j in range(GROUP // LANES):
        cs = slice(j * LANES, (j + 1) * LANES)
        qs = _pair_rows(qb_ref[:, cs])
        sink = jnp.where(half, sink_ref[j], sink_ref[j + 4])
        o2, l = _softmax_pv([(_dot_nt(qs, kb_ref[...]), vb_ref[...])], sink)
        ob_ref[:, cs] = _pair_merge(o2, 1.0 / l).astype(BF16)


def _ctx_attention(qa, ka, va, qb, kb, vb, sink, bsz, ctx):
    big = pl.BlockSpec((ctx, GROUP), lambda b: (b, 0))
    small = pl.BlockSpec((ctx, LANES), lambda b: (b, 0))
    return pl.pallas_call(
        _ctx_attn_kernel,
        grid=(bsz,),
        in_specs=[pl.BlockSpec(memory_space=pltpu.SMEM), big, big, big, big, small, small],
        out_specs=[big, big],
        out_shape=[jax.ShapeDtypeStruct((bsz * ctx, GROUP), BF16)] * 2,
        compiler_params=_cparams(("parallel",)),
    )(sink, qa, ka, va, qb, kb, vb)


def _halo_specs(tm, width, col_block, nblk8):
    step = tm // HALO
    return [pl.BlockSpec((tm, width), lambda i: (i, col_block)),
            pl.BlockSpec((HALO, width), lambda i: (jnp.maximum(i * step - 1, 0), col_block)),
            pl.BlockSpec((HALO, width), lambda i: (jnp.minimum((i + 1) * step, nblk8 - 1), col_block))]


def _ml_prep_kernel(x_ref, xp_ref, xn_ref, w_ref, o_ref, *, per):
    i = pl.program_id(0)
    x = x_ref[...]
    tm = x.shape[0]
    row = lax.broadcasted_iota(jnp.int32, (tm, 1), 0)
    keep_p = jnp.where(i % per == 0, 0.0, 1.0)
    keep_n = jnp.where(i % per == per - 1, 0.0, 1.0)
    prv = jnp.where(row == 0, xp_ref[HALO - 1:HALO, :] * keep_p, pltpu.roll(x, 1, axis=0))
    nxt = jnp.where(row == tm - 1, xn_ref[0:1, :] * keep_n, pltpu.roll(x, tm - 1, axis=0))
    y = prv * w_ref[0:1, :] + x * w_ref[1:2, :] + nxt * w_ref[2:3, :]
    y = y * _sigmoid(y)
    lane = lax.broadcasted_iota(jnp.int32, (1, GROUP), 1)
    o_ref[...] = (y * jnp.where(lane < GROUP // 2, ATT_HD ** -0.5, 1.0)).astype(BF16)


def _ml_prep(p, conv_w, seq):
    t = p.shape[0]
    tm = _row_tile(seq, 512)
    per = seq // tm
    return pl.pallas_call(
        functools.partial(_ml_prep_kernel, per=per),
        grid=(t // tm,),
        in_specs=_halo_specs(tm, GROUP, C_MLQK // GROUP, t // HALO)
        + [pl.BlockSpec((3, GROUP), lambda i: (0, 0))],
        out_specs=pl.BlockSpec((tm, GROUP), lambda i: (i, 0)),
        out_shape=jax.ShapeDtypeStruct((t, GROUP), BF16),
        compiler_params=_cparams(("parallel",)),
    )(p, p, p, conv_w)


def _mlstm_kernel(*refs, chunk, emit):
    (qkf_ref, vf_ref, gif_ref, gff_ref, qkr_ref, vr_ref, gir_ref, gfr_ref,
     brow_ref, c0_ref, m0_ref) = refs[:11]
    if emit:
        hf_ref, hr_ref = refs[11:13]
        rest = refs[13:]
    else:
        hf_ref = hr_ref = None
        rest = refs[11:]
    c_out, m_out, c_s, m_s = rest
    i = pl.program_id(1)

    @pl.when(i == 0)
    def _():
        c_s[...] = c0_ref[...]
        m_s[...] = m0_ref[...]

    row = lax.broadcasted_iota(jnp.int32, (chunk, chunk), 0)
    col = lax.broadcasted_iota(jnp.int32, (chunk, chunk), 1)
    trow = lax.broadcasted_iota(jnp.int32, (chunk, 1), 0)
    lane = lax.broadcasted_iota(jnp.int32, (chunk, LANES), 1)
    srow = lax.broadcasted_iota(jnp.int32, (LANES, 1), 0)
    dirs = ((qkf_ref, vf_ref, gif_ref, gff_ref, hf_ref), (qkr_ref, vr_ref, gir_ref, gfr_ref, hr_ref))
    chains = [(b, d) for b in range(qkf_ref.shape[0]) for d in range(2)]
    old_c = {(b, d, pp): c_s[b, d, pp] for b, d in chains for pp in range(2)}
    old_m = {(b, d): m_s[b, d, 0:1, :] for b, d in chains}
    new_c, new_m = {}, {}
    for b, d in chains:
        qk_ref, v_ref, gi_ref, gf_ref, h_ref = [r if r is None else r.at[b] for r in dirs[d]]
        causal = (col <= row) if d == 0 else (col >= row)
        ipre = gi_ref[...] + brow_ref[0:1, :]
        fl = _log_sigmoid(gf_ref[...] + brow_ref[1:2, :])
        fcum = _exact_left(jnp.where(causal, 1.0, 0.0).astype(BF16), fl)
        ftot = jnp.sum(fl, axis=0, keepdims=True)
        m_prev = old_m[b, d]
        w_log = ftot - fcum + ipre
        m_new = jnp.maximum(ftot + m_prev, jnp.max(w_log, axis=0, keepdims=True))
        cd = jnp.exp(ftot + m_prev - m_new)
        w = jnp.exp(w_log - m_new)
        if emit:
            b_col = ipre - fcum
            run = b_col
            sh = 1
            while sh < chunk:
                if d == 0:
                    run = jnp.maximum(run, jnp.where(trow >= sh, pltpu.roll(run, sh, axis=0), NEG))
                else:
                    run = jnp.maximum(run, jnp.where(trow < chunk - sh, pltpu.roll(run, chunk - sh, axis=0), NEG))
                sh *= 2
            log_inter = fcum + m_prev
            m_row = jnp.maximum(log_inter, fcum + run)
            inter_w = jnp.exp(log_inter - m_row)
            fcm = fcum - m_row
            floor = jnp.exp(-m_row)
            b_rows = b_col.T[0:8, :]
        for pp in range(2):
            qp = qk_ref[:, pp * LANES:(pp + 1) * LANES]
            kp = qk_ref[:, GROUP // 2 + pp * LANES:GROUP // 2 + (pp + 1) * LANES]
            cn = old_c[b, d, pp]
            inc = None
            for e in range(2):
                h = 2 * pp + e
                c = 4 * d + h
                sel = (lane < ATT_HD) if e == 0 else (lane >= ATT_HD)
                vh = v_ref[:, h * LANES:(h + 1) * LANES]
                if emit:
                    qm = jnp.where(sel, qp, jnp.zeros_like(qp))
                    d_mat = jnp.exp(jnp.where(causal, fcm[:, c:c + 1] + b_rows[c:c + 1, :], NEG))
                    s_mat = (_dot_nt(qm, kp) * d_mat).astype(BF16)
                    vb = vh.astype(BF16)
                    acc = (inter_w[:, c:c + 1] * _dot(qm, cn.astype(BF16))
                           + _dot(s_mat, jnp.concatenate([vb, jnp.ones_like(vb)], axis=1)))
                    den = jnp.maximum(jnp.abs(acc[:, LANES:]), floor[:, c:c + 1])
                    h_ref[:, h * LANES:(h + 1) * LANES] = acc[:, 0:LANES] / den
                km = jnp.where(sel, kp, jnp.zeros_like(kp))
                wc = jnp.broadcast_to(w[:, c:c + 1], (chunk, LANES))
                upd = _dot_tn(km, jnp.concatenate([wc * vh, wc], axis=1).astype(BF16))
                inc = upd if inc is None else inc + upd
            c_lo, c_hi = 4 * d + 2 * pp, 4 * d + 2 * pp + 1
            new_c[b, d, pp] = (jnp.where(srow < ATT_HD, cd[:, c_lo:c_lo + 1], cd[:, c_hi:c_hi + 1]) * cn
                               + inc)
        new_m[b, d] = m_new
    for key, val in new_c.items():
        c_s[key] = val
    for (b, d), val in new_m.items():
        m_s[b, d, 0:1, :] = val

    @pl.when(i == pl.num_programs(1) - 1)
    def _():
        c_out[...] = c_s[...]
        m_out[...] = m_s[...]


def _mlstm_scan(qk, p, brow, state, bsz, seq, chunk, emit):
    nc = seq // chunk
    bps = bsz if bsz * chunk <= ML_STEP_ROWS else max(1, ML_STEP_ROWS // chunk)
    assert bsz % bps == 0
    fwd = lambda g, i: i
    rev = lambda g, i: nc - 1 - i
    qk3 = qk.reshape(bsz, seq, GROUP)
    p3 = p.reshape(bsz, seq, NP)

    def chunk_specs(idx):
        return [pl.BlockSpec((bps, chunk, GROUP), lambda g, i: (g, idx(g, i), 0)),
                pl.BlockSpec((bps, chunk, GROUP), lambda g, i: (g, idx(g, i), C_MLV // GROUP)),
                pl.BlockSpec((bps, chunk, LANES), lambda g, i: (g, idx(g, i), C_MLI // LANES)),
                pl.BlockSpec((bps, chunk, LANES), lambda g, i: (g, idx(g, i), C_MLF // LANES))]

    st_dims = [(2, 2, LANES, 2 * LANES), (2, 8, LANES)]
    st_specs = [pl.BlockSpec((bps,) + s, lambda g, i, n=len(s): (g,) + (0,) * n) for s in st_dims]
    st_shapes = [jax.ShapeDtypeStruct((bsz,) + s, F32) for s in st_dims]
    h_specs = [pl.BlockSpec((bps, chunk, GROUP), lambda g, i: (g, fwd(g, i), 0)),
               pl.BlockSpec((bps, chunk, GROUP), lambda g, i: (g, rev(g, i), 0))] if emit else []
    h_shapes = [jax.ShapeDtypeStruct((bsz, seq, GROUP), F32)] * 2 if emit else []
    outs = pl.pallas_call(
        functools.partial(_mlstm_kernel, chunk=chunk, emit=emit),
        grid=(bsz // bps, nc),
        in_specs=chunk_specs(fwd) + chunk_specs(rev)
        + [pl.BlockSpec((8, LANES), lambda g, i: (0, 0))] + st_specs,
        out_specs=h_specs + st_specs,
        out_shape=h_shapes + st_shapes,
        scratch_shapes=[pltpu.VMEM((bps,) + s, F32) for s in st_dims],
        compiler_params=_cparams(("parallel", "arbitrary")),
    )(qk3, p3, p3, p3, qk3, p3, p3, p3, brow, *state)
    if emit:
        return (outs[0].reshape(bsz * seq, GROUP), outs[1].reshape(bsz * seq, GROUP), tuple(outs[2:]))
    return None, None, tuple(outs)


def _hgrn_constants(chunk):
    nlev = int(np.log2(chunk))
    assert 2 ** nlev == chunk
    t = np.arange(chunk)
    mats = [(t[None, :] <= t[:, None]), (t[None, :] > t[:, None])]
    masks = []
    for lev in range(nlev):
        n = chunk >> (lev + 1)
        b0 = t - t % (2 * n)
        upper = (t % (2 * n)) >= n
        m_up = (t[None, :] >= (b0 + n)[:, None]) & (t[None, :] <= t[:, None])
        m_lo = (t[None, :] > t[:, None]) & (t[None, :] <= (b0 + n - 1)[:, None])
        mats.append(np.where(upper[:, None], m_up, m_lo))
        same = (t[:, None] // (2 * n)) == (t[None, :] // (2 * n))
        masks.append(same & upper[:, None] & ~upper[None, :])
    masks.append(t[:, None] == t[None, :])
    mstack = np.concatenate(mats, axis=0).astype(np.float32)
    lmask = np.stack([np.kron(np.eye(2), m.astype(np.float32)) for m in masks])
    nr = mstack.shape[0] // chunk
    mstack_rev = mstack.reshape(nr, chunk, chunk)[:, ::-1, ::-1].reshape(-1, chunk)
    lmask_rev = lmask.reshape(nlev + 1, 2, chunk, 2, chunk)[:, :, ::-1, :, ::-1].reshape(lmask.shape)
    return (jnp.asarray(np.stack([mstack, mstack_rev]), BF16),
            jnp.asarray(np.stack([lmask, lmask_rev]), F32), nlev)


def _stack_pair(x, pp):
    return jnp.concatenate([x[:, (2 * pp) * LANES:(2 * pp + 1) * LANES],
                            x[:, (2 * pp + 1) * LANES:(2 * pp + 2) * LANES]], axis=0)


def _hgrn_kernel(*refs, chunk, nlev, emit, zero_lb):
    (qf_ref, vf_ref, ff_ref, qr_ref, vr_ref, fr_ref, lb_ref, ms_ref, lm_ref, s0_ref) = refs[:10]
    if emit:
        of_ref, or_ref = refs[10:12]
        rest = refs[12:]
    else:
        of_ref = or_ref = None
        rest = refs[10:]
    s_out, s_s = rest
    i = pl.program_id(1)

    @pl.when(i == 0)
    def _():
        s_s[...] = s0_ref[...]

    dirs = ((qf_ref, vf_ref, ff_ref, of_ref), (qr_ref, vr_ref, fr_ref, or_ref))
    chains = [(b, d) for b in range(vf_ref.shape[0]) for d in range(2)]
    old_s = {(b, d, h): s_s[b, d, h] for b, d in chains for h in range(4)}
    new_s = {}

    for b, d in chains:
        q_ref, v_ref, f_ref, o_ref = [r if r is None else r.at[b] for r in dirs[d]]
        fpre = f_ref[...]
        vb = v_ref[...].astype(BF16)
        e_f = jnp.exp(-jnp.abs(fpre))
        ls = jnp.minimum(fpre, 0.0) - _log1p_unit(e_f)
        sig_neg = jnp.where(fpre >= 0.0, e_f, 1.0) / (1.0 + e_f)
        if zero_lb:
            logf, k = ls, sig_neg
        else:
            a = lb_ref[d, 0:1, :]
            bb = lb_ref[d, 1:2, :] + ls
            logf = jnp.maximum(a, bb) + _log1p_unit(jnp.exp(-jnp.abs(a - bb)))
            k = lb_ref[d, 2:3, :] * sig_neg
        lf2 = logf * LOG2E
        hi = lf2.astype(BF16)
        lo = (lf2 - hi.astype(F32)).astype(BF16)
        gtot = jnp.sum(lf2, axis=0, keepdims=True)
        if emit:
            qpre = q_ref[...]
            q = qpre * _sigmoid(qpre)
            qb = q.astype(BF16)
            kb = k.astype(BF16)
        ms = ms_ref[d]
        ex = _dot(ms, hi) + _dot(ms, lo)
        ktil = (k * jnp.exp2(ex[chunk:2 * chunk])).astype(BF16)
        if emit:
            qe = (q * jnp.exp2(ex[0:chunk])).astype(BF16)
            lev_ops = []
            for lev in range(nlev):
                e_l = jnp.exp2(ex[(2 + lev) * chunk:(3 + lev) * chunk].astype(BF16))
                lev_ops.append((qb * e_l, kb * e_l))
        if emit:
            att = [lm_ref[d, nlev] * _dot_nt(_stack_pair(qb, pp), _stack_pair(kb, pp)) for pp in range(2)]
            for lev, (q_l, k_l) in enumerate(lev_ops):
                for pp in range(2):
                    att[pp] = att[pp] + lm_ref[d, lev] * _dot_nt(_stack_pair(q_l, pp), _stack_pair(k_l, pp))
            o_pair = [_dot(att[pp].astype(BF16), _stack_pair(vb, pp)) for pp in range(2)]
        for h in range(4):
            cs = slice(h * LANES, (h + 1) * LANES)
            s_t = old_s[b, d, h]
            if emit:
                pp, e = divmod(h, 2)
                o_ref[:, cs] = o_pair[pp][e * chunk:(e + 1) * chunk] + _dot_nt(qe[:, cs], s_t.astype(BF16))
            new_s[b, d, h] = s_t * jnp.exp2(gtot[:, cs]) + _dot_tn(vb[:, cs], ktil[:, cs])

    for key, val in new_s.items():
        s_s[key] = val

    @pl.when(i == pl.num_programs(1) - 1)
    def _():
        s_out[...] = s_s[...]


def _hgrn_scan(p, lbc, consts, state, bsz, seq, chunk, emit, zero_lb):
    mstack, lmask, nlev = consts
    nc = seq // chunk
    bps = bsz
    fwd = lambda g, i: i
    rev = lambda g, i: nc - 1 - i
    p3 = p.reshape(bsz, seq, NP)

    def chunk_specs(idx, d):
        return [pl.BlockSpec((bps, chunk, GROUP), lambda g, i: (g, idx(g, i), C_HGQ // GROUP)),
                pl.BlockSpec((bps, chunk, GROUP), lambda g, i: (g, idx(g, i), C_HGI // GROUP)),
                pl.BlockSpec((bps, chunk, GROUP), lambda g, i: (g, idx(g, i), C_HGF // GROUP + d))]

    st_spec = pl.BlockSpec((bps, 2, 4, LANES, LANES), lambda g, i: (g, 0, 0, 0, 0))
    st_shape = jax.ShapeDtypeStruct((bsz, 2, 4, LANES, LANES), F32)
    o_specs = [pl.BlockSpec((bps, chunk, GROUP), lambda g, i: (g, fwd(g, i), 0)),
               pl.BlockSpec((bps, chunk, GROUP), lambda g, i: (g, rev(g, i), 0))] if emit else []
    o_shapes = [jax.ShapeDtypeStruct((bsz, seq, GROUP), F32)] * 2 if emit else []
    full = lambda a: pl.BlockSpec(a.shape, lambda g, i: (0,) * a.ndim)
    outs = pl.pallas_call(
        functools.partial(_hgrn_kernel, chunk=chunk, nlev=nlev, emit=emit, zero_lb=zero_lb),
        grid=(bsz // bps, nc),
        in_specs=chunk_specs(fwd, 0) + chunk_specs(rev, 1) + [full(lbc), full(mstack), full(lmask), st_spec],
        out_specs=o_specs + [st_spec],
        out_shape=o_shapes + [st_shape],
        scratch_shapes=[pltpu.VMEM((bps, 2, 4, LANES, LANES), F32)],
        compiler_params=_cparams(("parallel", "arbitrary")),
    )(p3, p3, p3, p3, p3, p3, lbc, mstack, lmask, state)
    if emit:
        return outs[0].reshape(bsz * seq, GROUP), outs[1].reshape(bsz * seq, GROUP), outs[2]
    return None, None, outs[0]


def _outproj_kernel(x_ref, ya_ref, yb_ref, hf_ref, hr_ref, mo_ref, of_ref, or_ref, hg_ref,
                    w_ref, g_ref, o_ref):
    acc = _dot(ya_ref[...], w_ref[0]) + _dot(yb_ref[...], w_ref[1])
    yc = _sigmoid(mo_ref[...]) * (hf_ref[...] + hr_ref[...])
    acc = acc + _dot(yc.astype(BF16), w_ref[2])
    o = of_ref[...] + or_ref[...]
    gate = _sigmoid(hg_ref[...])
    parts = []
    for h in range(4):
        cs = slice(h * LANES, (h + 1) * LANES)
        oh = o[:, cs]
        parts.append(oh * lax.rsqrt(jnp.mean(oh * oh, axis=-1, keepdims=True) + EPS) * gate[:, cs])
    yd = jnp.concatenate(parts, axis=1)
    acc = acc + _dot(yd.astype(BF16), w_ref[3])
    o_ref[...] = x_ref[...] + g_ref[...] * acc


def _out_proj(x, ya, yb, hf, hr, of, orv, p, w_all, layer, gate, rows_per_mod):
    t, d = x.shape
    tm = _row_tile(rows_per_mod, 512)
    per = rows_per_mod // tm
    grp = lambda cb=0: pl.BlockSpec((tm, GROUP), lambda i: (i, cb))
    return pl.pallas_call(
        _outproj_kernel,
        grid=(t // tm,),
        in_specs=[pl.BlockSpec((tm, d), lambda i: (i, 0)), grp(), grp(), grp(), grp(),
                  grp(C_MLO // GROUP), grp(), grp(), grp(C_HGG // GROUP),
                  pl.BlockSpec((None, 4, GROUP, d), lambda i: (layer, 0, 0, 0)),
                  pl.BlockSpec((None, 1, d), lambda i: (i // per, 0, 0))],
        out_specs=pl.BlockSpec((tm, d), lambda i: (i, 0)),
        out_shape=jax.ShapeDtypeStruct((t, d), F32),
        compiler_params=_cparams(("parallel",)),
    )(x, ya, yb, hf, hr, p, of, orv, p, w_all, gate)


def _ffn_kernel(x_ref, xp_ref, xn_ref, g_ref, sh_ref, sc_ref, gate_ref, wa_ref, wu_ref, wc_ref, wd_ref,
                o_ref, xs_ref, acc_ref, *, per, inner):
    i = pl.program_id(0)
    j = pl.program_id(1)
    tm = x_ref.shape[0]

    @pl.when(j == 0)
    def _():
        nm = lambda x: _norm_mod(x, g_ref[...], sh_ref[...], sc_ref[...])
        keep_p = jnp.where(i % per == 0, 0.0, 1.0)
        keep_n = jnp.where(i % per == per - 1, 0.0, 1.0)
        _norm_mod_rows(x_ref, xs_ref, tm, g_ref[...], sh_ref[...], sc_ref[...])
        xs_ref[tm:tm + HALO, :] = (nm(xn_ref[...]) * keep_n).astype(BF16)
        xs_ref[tm + HALO:, :] = (nm(xp_ref[...]) * keep_p).astype(BF16)
        acc_ref[...] = jnp.zeros_like(acc_ref)

    a = _dot(xs_ref[...], wa_ref[...])
    rows = a.shape[0]
    prv = pltpu.roll(a, 1, axis=0)[0:tm]
    nxt = pltpu.roll(a, rows - 1, axis=0)[0:tm]
    if inner:
        pos = lax.broadcasted_iota(jnp.int32, (tm, 1), 0) % inner
        prv = jnp.where(pos == 0, 0.0, prv)
        nxt = jnp.where(pos == inner - 1, 0.0, nxt)
    conv = prv * wc_ref[0:1, :] + a[0:tm] * wc_ref[1:2, :] + nxt * wc_ref[2:3, :]
    u = _dot(xs_ref[0:tm, :], wu_ref[...])
    act = (conv * _sigmoid(conv) * u).astype(BF16)
    acc_ref[...] += _dot(act, wd_ref[...])

    @pl.when(j == pl.num_programs(1) - 1)
    def _():
        o_ref[...] = x_ref[...] + gate_ref[...] * acc_ref[...]


def _ffn(x, gain, shift, scale, gate, w_up, w_conv, w_down, layer, seq):
    t, d = x.shape
    f = w_down.shape[1]
    tm = 512 if t % 512 == 0 else t
    assert seq % tm == 0 or (tm % seq == 0 and shift.shape[0] == 1)
    tf = 512 if f % 512 == 0 else f
    per = max(seq // tm, 1)
    inner = seq if seq < tm else 0
    nf = f // tf
    step = tm // HALO
    nblk8 = t // HALO
    shared = shift.shape[0] == 1
    mod = pl.BlockSpec((None, 1, d), lambda i, j: (0 if shared else i // per, 0, 0))
    return pl.pallas_call(
        functools.partial(_ffn_kernel, per=per, inner=inner),
        grid=(t // tm, nf),
        in_specs=[pl.BlockSpec((tm, d), lambda i, j: (i, 0)),
                  pl.BlockSpec((HALO, d), lambda i, j: (jnp.maximum(i * step - 1, 0), 0)),
                  pl.BlockSpec((HALO, d), lambda i, j: (jnp.minimum((i + 1) * step, nblk8 - 1), 0)),
                  pl.BlockSpec((1, d), lambda i, j: (0, 0)), mod, mod, mod,
                  pl.BlockSpec((None, d, tf), lambda i, j: (layer, 0, j)),
                  pl.BlockSpec((None, d, tf), lambda i, j: (layer, 0, nf + j)),
                  pl.BlockSpec((None, 3, tf), lambda i, j: (layer, 0, j)),
                  pl.BlockSpec((None, tf, d), lambda i, j: (layer, j, 0))],
        out_specs=pl.BlockSpec((tm, d), lambda i, j: (i, 0)),
        out_shape=jax.ShapeDtypeStruct((t, d), F32),
        scratch_shapes=[pltpu.VMEM((tm + 2 * HALO, d), BF16), pltpu.VMEM((tm, d), F32)],
        compiler_params=_cparams(("parallel", "arbitrary")),
    )(x, x, x, gain, shift, scale, gate, w_up, w_up, w_conv, w_down)


def _perm_w_in(w):
    n, d = w.shape[:2]
    swq = w[..., 1536:2048].reshape(n, d, 2, 4, ATT_HD).transpose(0, 1, 3, 2, 4).reshape(n, d, GROUP)
    pad = jnp.zeros((n, d, LANES - 8), w.dtype)
    cols = [w[..., 0:1536], swq, w[..., 2304:3840], w[..., 3856:6416], w[..., 2048:2304],
            w[..., 3840:3848], pad, w[..., 3848:3856], pad]
    out = jnp.concatenate(cols, axis=-1).astype(BF16)
    assert out.shape[-1] == NP
    return out


def _perm_w_out(w):
    n, _, d = w.shape
    w4 = w.reshape(n, 4, GROUP, d)
    swo = w4[:, 1].reshape(n, 2, 4, ATT_HD, d).transpose(0, 2, 1, 3, 4).reshape(n, GROUP, d)
    return jnp.stack([w4[:, 0], swo, w4[:, 2], w4[:, 3]], axis=1).astype(BF16)


def _rope_tables(seq):
    t = jnp.arange(seq)
    half = ATT_HD // 2
    inv = ROPE_BASE ** (-jnp.arange(0, half, 2, dtype=F32) / half)
    ang_r = (t // GRID_W).astype(F32)[:, None] * inv
    ang_c = (t % GRID_W).astype(F32)[:, None] * inv
    cr, sr, cc, sc = jnp.cos(ang_r), jnp.sin(ang_r), jnp.cos(ang_c), jnp.sin(ang_c)
    cos = jnp.concatenate([cr, cr, cc, cc] * 2, axis=-1)
    sin = jnp.concatenate([-sr, sr, -sc, sc] * 2, axis=-1)
    return cos, sin


def kernel(x, c, ctx, c_ctx, w_mod, b_mod, norm_mix, norm_ffn, w_in, w_out, na_qk_gain, na_rpb,
           sw_qk_gain, sw_sink, ml_conv, ml_gate_bias, hg_lb, ffn_up, ffn_conv, ffn_down):
    bsz, seq, d = x.shape
    nctx = ctx.shape[1]
    depth = w_mod.shape[0]
    ml_chunk_ctx, ml_chunk = math.gcd(ML_CHUNK, nctx), math.gcd(ML_CHUNK, seq)
    chunk = math.gcd(HG_CHUNK, math.gcd(seq, nctx))
    assert bsz + 1 <= 8

    c_all = jnp.concatenate([c, c_ctx[None], jnp.zeros((8 - bsz - 1, d), F32)], axis=0)
    mod = _modulation(c_all, w_mod, b_mod).reshape(depth, 8, 6, d)
    cos, sin = _rope_tables(seq)
    bd = jnp.asarray(np.kron(np.eye(GROUP // ATT_HD), np.ones((ATT_HD, ATT_HD))), BF16)
    lb_cum = jnp.cumsum(jax.nn.softmax(hg_lb.astype(F32), axis=0), axis=0)
    lower = lb_cum - lb_cum[:1]
    consts = _hgrn_constants(chunk)
    w_in_p = _perm_w_in(w_in)
    w_out_p = _perm_w_out(w_out)
    w_up = ffn_up.astype(BF16)
    w_dn = ffn_down.astype(BF16)
    w_cv = ffn_conv.astype(F32)

    h = x.reshape(bsz * seq, d)
    hc = ctx.reshape(bsz * nctx, d)
    for l in range(depth):
        emit_ctx = l < depth - 1
        lat = lambda k: mod[l, :bsz, k].reshape(bsz, 1, d)
        cx = lambda k: mod[l, bsz, k].reshape(1, 1, d)
        gain = norm_mix[l].reshape(1, d)
        p_lat = _in_proj(h, gain, lat(0), lat(1), w_in_p, l, seq)
        p_ctx = _in_proj(hc, gain, cx(0), cx(1), w_in_p, l, bsz * nctx)

        gna = jnp.tile(na_qk_gain[l].astype(F32), (1, GROUP // ATT_HD))
        gsw = jnp.tile(sw_qk_gain[l].astype(F32), (1, GROUP // ATT_HD))
        naq, nak, nav, swq, swk, swv = _attn_prep(p_lat, gna, gsw, bd, cos, sin, seq, True)
        cnaq, cnak, cnav, cswq, cswk, cswv = _attn_prep(p_ctx, gna, gsw, bd, cos, sin, nctx, False)
        sink = sw_sink[l].astype(F32)
        ya = _na_attention(naq, nak, nav, cnak, cnav, _na_bias_table(na_rpb[l], seq // GRID_W), bsz, seq, nctx)
        yb = _sw_attention(swq, swk, swv, cswk, cswv, sink, bsz, seq, nctx)

        qk_lat = _ml_prep(p_lat, ml_conv[l].astype(F32), seq)
        qk_ctx = _ml_prep(p_ctx, ml_conv[l].astype(F32), nctx)
        gb = ml_gate_bias[l].astype(F32).reshape(2, 8)
        brow = jnp.zeros((8, LANES), F32).at[0:2, 0:8].set(gb)
        ml0 = (jnp.zeros((bsz, 2, 2, LANES, 2 * LANES), F32), jnp.zeros((bsz, 2, 8, LANES), F32))
        hcf, hcr, ml_state = _mlstm_scan(qk_ctx, p_ctx, brow, ml0, bsz, nctx, ml_chunk_ctx, emit_ctx)
        hf, hr, _ = _mlstm_scan(qk_lat, p_lat, brow, ml_state, bsz, seq, ml_chunk, True)

        lb = lower[l]
        lbc = jnp.stack([jnp.maximum(jnp.log(lb), NEG), jnp.log1p(-lb), 1.0 - lb], axis=1)
        lbc = jnp.concatenate([lbc, jnp.zeros((2, 5, GROUP), F32)], axis=1)
        hg0 = jnp.zeros((bsz, 2, 4, LANES, LANES), F32)
        ocf, ocr, hg_state = _hgrn_scan(p_ctx, lbc, consts, hg0, bsz, nctx, chunk, emit_ctx, l == 0)
        of, orv, _ = _hgrn_scan(p_lat, lbc, consts, hg_state, bsz, seq, chunk, True, l == 0)

        gain2 = norm_ffn[l].reshape(1, d)
        h = _out_proj(h, ya, yb, hf, hr, of, orv, p_lat, w_out_p, l, lat(2), seq)
        h = _ffn(h, gain2, lat(3), lat(4), lat(5), w_up, w_cv, w_dn, l, seq)
        if emit_ctx:
            yac, ybc = _ctx_attention(cnaq, cnak, cnav, cswq, cswk, cswv, sink, bsz, nctx)
            hc = _out_proj(hc, yac, ybc, hcf, hcr, ocf, ocr, p_ctx, w_out_p, l, cx(2), bsz * nctx)
            hc = _ffn(hc, gain2, cx(3), cx(4), cx(5), w_up, w_cv, w_dn, l, nctx)
    return h.reshape(bsz, seq, d)
```

```python
import functools
import math

import numpy as np
import jax
import jax.numpy as jnp
from jax import lax
from jax.experimental import pallas as pl
from jax.experimental.pallas import tpu as pltpu

F32 = jnp.float32
BF16 = jnp.bfloat16

GRID_W = 64
ATT_HD = 64
NA_ROWS = 8
NA_COLS = 16
SW_BLOCK = 128
ROPE_BASE = 10000.0
EPS = 1e-6
NEG = -1e30
LOG2E = 1.4426950408889634
GROUP = 512
LANES = 128
HALO = 8
VMEM_LIMIT = 56 * 1024 * 1024

C_NAQ, C_NAK, C_NAV, C_SWQ = 0, 512, 1024, 1536
C_MLQK, C_MLV, C_MLO = 2048, 2560, 3072
C_HGQ, C_HGI, C_HGF, C_HGG = 3584, 4096, 4608, 5632
C_SWK, C_SWV = 6144, 6272
C_MLI, C_MLF = 6400, 6528
NP = 6656
ML_CHUNK = 256
ML_STEP_ROWS = 1024
HG_CHUNK = 64
SW_PERM = (0, 4, 1, 5, 2, 6, 3, 7)


def _cparams(sem):
    return pltpu.CompilerParams(dimension_semantics=sem, vmem_limit_bytes=VMEM_LIMIT)


def _sigmoid(x):
    return 1.0 / (1.0 + jnp.exp(-x))


def _log1p_unit(e):
    return jnp.log(1.0 + e)


def _log_sigmoid(x):
    return jnp.minimum(x, 0.0) - _log1p_unit(jnp.exp(-jnp.abs(x)))


def _split3(x):
    hi = x.astype(BF16)
    r = x - hi.astype(F32)
    mid = r.astype(BF16)
    lo = (r - mid.astype(F32)).astype(BF16)
    return hi, mid, lo


def _dot(a, b):
    return jnp.dot(a, b, preferred_element_type=F32)


def _dot_nt(a, b):
    return lax.dot_general(a, b, (((1,), (1,)), ((), ())), preferred_element_type=F32)


def _dot_tn(a, b):
    return lax.dot_general(a, b, (((0,), (0,)), ((), ())), preferred_element_type=F32)


def _exact_left(m01, x):
    hi, mid, lo = _split3(x)
    return _dot(m01, hi) + _dot(m01, mid) + _dot(m01, lo)


def _mod_kernel(c_ref, w_ref, b_ref, o_ref):
    c = c_ref[...]
    a = c * _sigmoid(c)
    a_hi = a.astype(BF16)
    a_lo = (a - a_hi.astype(F32)).astype(BF16)
    w = w_ref[...]
    w_hi = w.astype(BF16)
    w_lo = (w - w_hi.astype(F32)).astype(BF16)
    o_ref[...] = _dot(a_hi, w_hi) + _dot(a_hi, w_lo) + _dot(a_lo, w_hi) + b_ref[...]


def _modulation(c_all, w_mod, b_mod):
    depth, d, n = w_mod.shape
    tn = 1024 if n % 1024 == 0 else n
    return pl.pallas_call(
        _mod_kernel,
        grid=(depth, n // tn),
        in_specs=[pl.BlockSpec((8, d), lambda l, j: (0, 0)),
                  pl.BlockSpec((None, d, tn), lambda l, j: (l, 0, j)),
                  pl.BlockSpec((None, 1, tn), lambda l, j: (l, 0, j))],
        out_specs=pl.BlockSpec((None, 8, tn), lambda l, j: (l, 0, j)),
        out_shape=jax.ShapeDtypeStruct((depth, 8, n), F32),
        compiler_params=_cparams(("parallel", "parallel")),
    )(c_all, w_mod, b_mod.reshape(depth, 1, n))


def _norm_mod(x, gain, shift, scale):
    y = x * lax.rsqrt(jnp.mean(x * x, axis=-1, keepdims=True) + EPS) * gain
    return y * (1.0 + scale) + shift


NORM_ROWS = 16


def _norm_mod_rows(src_ref, dst_ref, nrows, gain, shift, scale):
    amp = gain * (1.0 + scale)

    def body(c, carry):
        rows = pl.ds(pl.multiple_of(c * NORM_ROWS, NORM_ROWS), NORM_ROWS)
        x = src_ref[rows, :]
        y = x * lax.rsqrt(jnp.mean(x * x, axis=-1, keepdims=True) + EPS) * amp + shift
        dst_ref[rows, :] = y.astype(BF16)
        return carry

    assert nrows % NORM_ROWS == 0
    lax.fori_loop(0, nrows // NORM_ROWS, body, 0, unroll=4)


def _inproj_kernel(x_ref, g_ref, sh_ref, sc_ref, w_ref, o_ref, xn_ref):
    @pl.when(pl.program_id(1) == 0)
    def _():
        _norm_mod_rows(x_ref, xn_ref, x_ref.shape[0], g_ref[...], sh_ref[...], sc_ref[...])

    o_ref[...] = _dot(xn_ref[...], w_ref[...])


def _row_tile(rows_per_mod, pref):
    tm = min(pref, rows_per_mod)
    assert rows_per_mod % tm == 0
    return tm


def _in_proj(x, gain, shift, scale, w_all, layer, rows_per_mod):
    t, d = x.shape
    n = w_all.shape[2]
    tm = _row_tile(rows_per_mod, 512)
    tn = next(c for c in (1664, 512, n) if n % c == 0)
    per = rows_per_mod // tm
    return pl.pallas_call(
        _inproj_kernel,
        grid=(t // tm, n // tn),
        in_specs=[pl.BlockSpec((tm, d), lambda i, j: (i, 0)),
                  pl.BlockSpec((1, d), lambda i, j: (0, 0)),
                  pl.BlockSpec((None, 1, d), lambda i, j: (i // per, 0, 0)),
                  pl.BlockSpec((None, 1, d), lambda i, j: (i // per, 0, 0)),
                  pl.BlockSpec((None, d, tn), lambda i, j: (layer, 0, j))],
        out_specs=pl.BlockSpec((tm, tn), lambda i, j: (i, j)),
        out_shape=jax.ShapeDtypeStruct((t, n), F32),
        scratch_shapes=[pltpu.VMEM((tm, d), BF16)],
        compiler_params=_cparams(("parallel", "arbitrary")),
    )(x, gain, shift, scale, w_all)


def _head_rms(x, gain_row, bd):
    parts = []
    for g in range(x.shape[1] // LANES):
        xg = x[:, g * LANES:(g + 1) * LANES]
        ss = _dot((xg * xg).astype(BF16), bd)
        parts.append(xg * lax.rsqrt(ss * (1.0 / ATT_HD) + EPS))
    return (parts[0] if len(parts) == 1 else jnp.concatenate(parts, axis=1)) * gain_row


def _rope(x, cos, sin, first):
    w = x.shape[-1]
    nxt = pltpu.roll(x, w - 16, axis=1)
    prv = pltpu.roll(x, 16, axis=1)
    return x * cos + jnp.where(first, nxt, prv) * sin


def _attn_prep_kernel(p_ref, pkv_ref, gna_ref, gsw_ref, bd_ref, cos_ref, sin_ref,
                      naq_ref, nak_ref, nav_ref, swq_ref, swk_ref, swv_ref, *, rope):
    bd = bd_ref[...]
    scale = ATT_HD ** -0.5
    naq_ref[...] = (_head_rms(p_ref[:, C_NAQ:C_NAQ + GROUP], gna_ref[0:1, :], bd) * scale).astype(BF16)
    nak_ref[...] = _head_rms(p_ref[:, C_NAK:C_NAK + GROUP], gna_ref[1:2, :], bd).astype(BF16)
    nav_ref[...] = p_ref[:, C_NAV:C_NAV + GROUP].astype(BF16)
    q = _head_rms(p_ref[:, C_SWQ:C_SWQ + GROUP], gsw_ref[0:1, :], bd)
    k = _head_rms(pkv_ref[:, 0:LANES], gsw_ref[1:2, 0:LANES], bd)
    if rope:
        cos = cos_ref[...]
        sin = sin_ref[...]
        lane = lax.broadcasted_iota(jnp.int32, (1, GROUP), 1)
        first = (lane % 32) < 16
        q = _rope(q, jnp.concatenate([cos] * 4, axis=1), jnp.concatenate([sin] * 4, axis=1), first)
        k = _rope(k, cos, sin, first[:, 0:LANES])
    swq_ref[...] = (q * scale).astype(BF16)
    swk_ref[...] = k.astype(BF16)
    swv_ref[...] = pkv_ref[:, LANES:2 * LANES].astype(BF16)


def _attn_prep(p, gna, gsw, bd, cos, sin, seq, rope):
    t = p.shape[0]
    tm = _row_tile(seq, 512)
    per = seq // tm
    out = lambda w: jax.ShapeDtypeStruct((t, w), BF16)
    ospec = lambda w: pl.BlockSpec((tm, w), lambda i: (i, 0))
    return pl.pallas_call(
        functools.partial(_attn_prep_kernel, rope=rope),
        grid=(t // tm,),
        in_specs=[pl.BlockSpec((tm, C_MLQK), lambda i: (i, 0)),
                  pl.BlockSpec((tm, 2 * LANES), lambda i: (i, C_SWK // (2 * LANES))),
                  pl.BlockSpec((2, GROUP), lambda i: (0, 0)),
                  pl.BlockSpec((2, GROUP), lambda i: (0, 0)),
                  pl.BlockSpec((LANES, LANES), lambda i: (0, 0)),
                  pl.BlockSpec((tm, LANES), lambda i: (i % per, 0)),
                  pl.BlockSpec((tm, LANES), lambda i: (i % per, 0))],
        out_specs=[ospec(GROUP), ospec(GROUP), ospec(GROUP), ospec(GROUP), ospec(LANES), ospec(LANES)],
        out_shape=[out(GROUP), out(GROUP), out(GROUP), out(GROUP), out(LANES), out(LANES)],
        compiler_params=_cparams(("parallel",)),
    )(p, p, gna, gsw, bd, cos, sin)


def _pair_rows(q_pair):
    lane = lax.broadcasted_iota(jnp.int32, q_pair.shape, 1)
    zero = jnp.zeros_like(q_pair)
    return jnp.concatenate([jnp.where(lane < ATT_HD, q_pair, zero),
                            jnp.where(lane >= ATT_HD, q_pair, zero)], axis=0)


def _pair_merge(o2, inv_l):
    m = o2.shape[0] // 2
    lane = lax.broadcasted_iota(jnp.int32, (m, LANES), 1)
    return jnp.where(lane < ATT_HD, o2[0:m] * inv_l[0:m], o2[m:] * inv_l[m:])


NA_RPS = 4


def _softmax_pv(parts, sink=None):
    m = functools.reduce(jnp.maximum, [jnp.max(s, axis=-1, keepdims=True) for s, _ in parts])
    if sink is not None:
        m = jnp.maximum(m, sink)
    acc = None
    for s, v in parts:
        e = jnp.exp((s - m).astype(BF16))
        o = _dot(e, jnp.concatenate([v, jnp.ones_like(v)], axis=1))
        acc = o if acc is None else acc + o
    l = acc[:, LANES:LANES + 1]
    if sink is not None:
        l = l + jnp.exp(sink - m)
    return acc[:, 0:LANES], l


NA_WIN = NA_ROWS + NA_RPS


def _na_window(step, rows):
    return jnp.clip(step * NA_RPS - NA_ROWS // 2, 0, rows - NA_WIN)


def _na_kernel(q_ref, k_ref, v_ref, kc_ref, vc_ref, b_ref, o_ref, *, rows):
    start = pl.multiple_of(_na_window(pl.program_id(1), rows) * GRID_W, GRID_W)
    nkey = NA_WIN * GRID_W
    for p in range(GROUP // LANES):
        cs = slice(p * LANES, (p + 1) * LANES)
        qs = jnp.concatenate([_pair_rows(q_ref[rr * GRID_W:(rr + 1) * GRID_W, cs]) for rr in range(NA_RPS)],
                             axis=0)
        kw = k_ref[pl.ds(start, nkey), cs]
        vw = v_ref[pl.ds(start, nkey), cs]
        s_nb = _dot_nt(qs, kw) + b_ref[p]
        s_cx = _dot_nt(qs, kc_ref[:, cs])
        o2, l = _softmax_pv([(s_nb, vw), (s_cx, vc_ref[:, cs])])
        inv_l = 1.0 / l
        for rr in range(NA_RPS):
            sl = slice(rr * 2 * GRID_W, (rr + 1) * 2 * GRID_W)
            o_ref[rr * GRID_W:(rr + 1) * GRID_W, cs] = _pair_merge(o2[sl], inv_l[sl]).astype(BF16)


def _na_attention(q, k, v, kc, vc, bias, bsz, seq, ctx):
    rows = seq // GRID_W
    assert rows >= NA_WIN and rows % NA_RPS == 0
    steps = rows // NA_RPS
    k3, v3 = k.reshape(bsz, seq, GROUP), v.reshape(bsz, seq, GROUP)
    kc3, vc3 = kc.reshape(bsz, ctx, GROUP), vc.reshape(bsz, ctx, GROUP)
    qspec = pl.BlockSpec((NA_RPS * GRID_W, GROUP), lambda b, r: (b * steps + r, 0))
    variant = lambda b, r: ((r > 0).astype(jnp.int32) + (r == steps - 1).astype(jnp.int32), 0, 0, 0)
    return pl.pallas_call(
        functools.partial(_na_kernel, rows=rows),
        grid=(bsz, steps),
        in_specs=[qspec,
                  pl.BlockSpec((None, seq, GROUP), lambda b, r: (b, 0, 0)),
                  pl.BlockSpec((None, seq, GROUP), lambda b, r: (b, 0, 0)),
                  pl.BlockSpec((None, ctx, GROUP), lambda b, r: (b, 0, 0)),
                  pl.BlockSpec((None, ctx, GROUP), lambda b, r: (b, 0, 0)),
                  pl.BlockSpec((None,) + bias.shape[1:], variant)],
        out_specs=qspec,
        out_shape=jax.ShapeDtypeStruct((bsz * seq, GROUP), BF16),
        compiler_params=_cparams(("parallel", "arbitrary")),
    )(q, k3, v3, kc3, vc3, bias)


def _na_bias_table(rpb, rows):
    nh = rpb.shape[0]
    steps = rows // NA_RPS
    c = np.arange(GRID_W)
    kc = np.arange(GRID_W)
    cstart = np.clip(c - NA_COLS // 2, 0, GRID_W - NA_COLS)
    ok = (kc[None, :] >= cstart[:, None]) & (kc[None, :] < cstart[:, None] + NA_COLS)
    dc = np.clip(kc[None, :] - c[:, None] + NA_COLS - 1, 0, 2 * NA_COLS - 2)
    onehot = (dc.reshape(1, -1) == np.arange(2 * NA_COLS - 1)[:, None]).astype(np.float32)
    t = jnp.dot(rpb.astype(F32).reshape(-1, 2 * NA_COLS - 1), onehot, precision=lax.Precision.HIGHEST)
    t = jnp.where(ok[None, None], t.reshape(nh, -1, GRID_W, GRID_W), NEG)

    def step_layout(s):
        ws = int(np.clip(s * NA_RPS - NA_ROWS // 2, 0, rows - NA_WIN))
        r = s * NA_RPS + np.arange(NA_RPS)[:, None]
        rs = np.clip(r - NA_ROWS // 2, 0, rows - NA_ROWS)
        krow = ws + np.arange(NA_WIN)[None, :]
        return (krow - r + NA_ROWS - 1), (krow >= rs) & (krow < rs + NA_ROWS)

    layouts = [step_layout(s) for s in range(steps)]
    variants = [layouts[0], layouts[1], layouts[-1]]
    for s, (idx, valid) in enumerate(layouts):
        want = variants[(s > 0) + (s == steps - 1)]
        assert (valid == want[1]).all() and (idx[valid] == want[0][want[1]]).all()

    ndr = 2 * NA_ROWS - 1
    width = (ndr + 3) * GRID_W
    t2 = jnp.transpose(t, (0, 2, 1, 3)).reshape(nh, GRID_W, ndr * GRID_W)
    t2 = jnp.pad(t2, ((0, 0), (0, 0), (GRID_W, width - (ndr + 1) * GRID_W)), constant_values=NEG)
    t2s = jnp.pad(t2[:, :, GRID_W:], ((0, 0), (0, 0), (0, GRID_W)), constant_values=NEG)

    def build(ta_ref, tb_ref, o_ref):
        lane = lax.broadcasted_iota(jnp.int32, (GRID_W, LANES), 1)
        neg = jnp.full((GRID_W, LANES), NEG, F32)
        for v, (idx, valid) in enumerate(variants):
            for rr in range(NA_RPS):
                for e in range(2):
                    row0 = (rr * 2 + e) * GRID_W
                    for jj in range(NA_WIN // 2):
                        j = 2 * jj
                        ok0, ok1 = bool(valid[rr, j]), bool(valid[rr, j + 1])
                        if not (ok0 or ok1):
                            piece = neg
                        else:
                            blk = int(idx[rr, j] if ok0 else idx[rr, j + 1] - 1) + 1
                            src = ta_ref if blk % 2 == 0 else tb_ref
                            off = (blk - blk % 2) * GRID_W
                            piece = src[e, :, off:off + LANES]
                            if not ok0:
                                piece = jnp.where(lane >= GRID_W, piece, NEG)
                            if not ok1:
                                piece = jnp.where(lane < GRID_W, piece, NEG)
                        o_ref[v, row0:row0 + GRID_W, j * GRID_W:(j + 2) * GRID_W] = piece

    npair = nh // 2
    tspec = pl.BlockSpec((2, GRID_W, width), lambda p: (p, 0, 0))
    return pl.pallas_call(
        build,
        grid=(npair,),
        in_specs=[tspec, tspec],
        out_specs=pl.BlockSpec((3, None, NA_RPS * 2 * GRID_W, NA_WIN * GRID_W), lambda p: (0, p, 0, 0)),
        out_shape=jax.ShapeDtypeStruct((3, npair, NA_RPS * 2 * GRID_W, NA_WIN * GRID_W), F32),
        compiler_params=_cparams(("parallel",)),
    )(t2, t2s)


SW_BPS = 2


def _sw_kernel(sink_ref, q_ref, k_ref, v_ref, kc_ref, vc_ref, o_ref, *, seq):
    nq = SW_BPS * SW_BLOCK
    nwin = (SW_BPS + 2) * SW_BLOCK
    q0 = pl.program_id(1) * nq
    start = pl.multiple_of(jnp.clip(q0 - SW_BLOCK, 0, seq - nwin), SW_BLOCK)
    kw = k_ref[pl.ds(start, nwin), :]
    vw = v_ref[pl.ds(start, nwin), :]
    kc = kc_ref[...]
    vc = vc_ref[...]
    row = lax.broadcasted_iota(jnp.int32, (2 * nq, nwin), 0)
    col = lax.broadcasted_iota(jnp.int32, (2 * nq, nwin), 1)
    ok = jnp.abs((start + col) - (q0 + row % nq)) <= SW_BLOCK
    half = lax.broadcasted_iota(jnp.int32, (2 * nq, 1), 0) < nq
    for j in range(GROUP // LANES):
        cs = slice(j * LANES, (j + 1) * LANES)
        qs = _pair_rows(q_ref[:, cs])
        s_bd = jnp.where(ok, _dot_nt(qs, kw), NEG)
        s_cx = _dot_nt(qs, kc)
        sink = jnp.where(half, sink_ref[j], sink_ref[j + 4])
        o2, l = _softmax_pv([(s_bd, vw), (s_cx, vc)], sink)
        o_ref[:, cs] = _pair_merge(o2, 1.0 / l).astype(BF16)


def _sw_attention(q, k, v, kc, vc, sink, bsz, seq, ctx):
    assert seq % (SW_BPS * SW_BLOCK) == 0 and seq >= (SW_BPS + 2) * SW_BLOCK
    nb = seq // (SW_BPS * SW_BLOCK)
    k3, v3 = k.reshape(bsz, seq, LANES), v.reshape(bsz, seq, LANES)
    kc3, vc3 = kc.reshape(bsz, ctx, LANES), vc.reshape(bsz, ctx, LANES)
    return pl.pallas_call(
        functools.partial(_sw_kernel, seq=seq),
        grid=(bsz, nb),
        in_specs=[pl.BlockSpec(memory_space=pltpu.SMEM),
                  pl.BlockSpec((SW_BPS * SW_BLOCK, GROUP), lambda b, n: (b * nb + n, 0)),
                  pl.BlockSpec((None, seq, LANES), lambda b, n: (b, 0, 0)),
                  pl.BlockSpec((None, seq, LANES), lambda b, n: (b, 0, 0)),
                  pl.BlockSpec((None, ctx, LANES), lambda b, n: (b, 0, 0)),
                  pl.BlockSpec((None, ctx, LANES), lambda b, n: (b, 0, 0))],
        out_specs=pl.BlockSpec((SW_BPS * SW_BLOCK, GROUP), lambda b, n: (b * nb + n, 0)),
        out_shape=jax.ShapeDtypeStruct((bsz * seq, GROUP), BF16),
        compiler_params=_cparams(("parallel", "arbitrary")),
    )(sink, q, k3, v3, kc3, vc3)


def _ctx_attn_kernel(sink_ref, qa_ref, ka_ref, va_ref, qb_ref, kb_ref, vb_ref, oa_ref, ob_ref):
    for p in range(GROUP // LANES):
        cs = slice(p * LANES, (p + 1) * LANES)
        qs = _pair_rows(qa_ref[:, cs])
        o2, l = _softmax_pv([(_dot_nt(qs, ka_ref[:, cs]), va_ref[:, cs])])
        oa_ref[:, cs] = _pair_merge(o2, 1.0 / l).astype(BF16)
    ctx = qb_ref.shape[0]
    half = lax.broadcasted_iota(jnp.int32, (2 * ctx, 1), 0) < ctx
    for j in range(GROUP // LANES):
        cs = slice(j * LANES, (j + 1) * LANES)
        qs = _pair_rows(qb_ref[:, cs])
        sink = jnp.where(half, sink_ref[j], sink_ref[j + 4])
        o2, l = _softmax_pv([(_dot_nt(qs, kb_ref[...]), vb_ref[...])], sink)
        ob_ref[:, cs] = _pair_merge(o2, 1.0 / l).astype(BF16)


def _ctx_attention(qa, ka, va, qb, kb, vb, sink, bsz, ctx):
    big = pl.BlockSpec((ctx, GROUP), lambda b: (b, 0))
    small = pl.BlockSpec((ctx, LANES), lambda b: (b, 0))
    return pl.pallas_call(
        _ctx_attn_kernel,
        grid=(bsz,),
        in_specs=[pl.BlockSpec(memory_space=pltpu.SMEM), big, big, big, big, small, small],
        out_specs=[big, big],
        out_shape=[jax.ShapeDtypeStruct((bsz * ctx, GROUP), BF16)] * 2,
        compiler_params=_cparams(("parallel",)),
    )(sink, qa, ka, va, qb, kb, vb)


def _halo_specs(tm, width, col_block, nblk8):
    step = tm // HALO
    return [pl.BlockSpec((tm, width), lambda i: (i, col_block)),
            pl.BlockSpec((HALO, width), lambda i: (jnp.maximum(i * step - 1, 0), col_block)),
            pl.BlockSpec((HALO, width), lambda i: (jnp.minimum((i + 1) * step, nblk8 - 1), col_block))]


def _ml_prep_kernel(x_ref, xp_ref, xn_ref, w_ref, o_ref, *, per):
    i = pl.program_id(0)
    x = x_ref[...]
    tm = x.shape[0]
    row = lax.broadcasted_iota(jnp.int32, (tm, 1), 0)
    keep_p = jnp.where(i % per == 0, 0.0, 1.0)
    keep_n = jnp.where(i % per == per - 1, 0.0, 1.0)
    prv = jnp.where(row == 0, xp_ref[HALO - 1:HALO, :] * keep_p, pltpu.roll(x, 1, axis=0))
    nxt = jnp.where(row == tm - 1, xn_ref[0:1, :] * keep_n, pltpu.roll(x, tm - 1, axis=0))
    y = prv * w_ref[0:1, :] + x * w_ref[1:2, :] + nxt * w_ref[2:3, :]
    y = y * _sigmoid(y)
    lane = lax.broadcasted_iota(jnp.int32, (1, GROUP), 1)
    o_ref[...] = (y * jnp.where(lane < GROUP // 2, ATT_HD ** -0.5, 1.0)).astype(BF16)


def _ml_prep(p, conv_w, seq):
    t = p.shape[0]
    tm = _row_tile(seq, 512)
    per = seq // tm
    return pl.pallas_call(
        functools.partial(_ml_prep_kernel, per=per),
        grid=(t // tm,),
        in_specs=_halo_specs(tm, GROUP, C_MLQK // GROUP, t // HALO)
        + [pl.BlockSpec((3, GROUP), lambda i: (0, 0))],
        out_specs=pl.BlockSpec((tm, GROUP), lambda i: (i, 0)),
        out_shape=jax.ShapeDtypeStruct((t, GROUP), BF16),
        compiler_params=_cparams(("parallel",)),
    )(p, p, p, conv_w)


def _mlstm_kernel(*refs, chunk, emit):
    (qkf_ref, vf_ref, gif_ref, gff_ref, qkr_ref, vr_ref, gir_ref, gfr_ref,
     brow_ref, c0_ref, m0_ref) = refs[:11]
    if emit:
        hf_ref, hr_ref = refs[11:13]
        rest = refs[13:]
    else:
        hf_ref = hr_ref = None
        rest = refs[11:]
    c_out, m_out, c_s, m_s = rest
    i = pl.program_id(1)

    @pl.when(i == 0)
    def _():
        c_s[...] = c0_ref[...]
        m_s[...] = m0_ref[...]

    row = lax.broadcasted_iota(jnp.int32, (chunk, chunk), 0)
    col = lax.broadcasted_iota(jnp.int32, (chunk, chunk), 1)
    trow = lax.broadcasted_iota(jnp.int32, (chunk, 1), 0)
    lane = lax.broadcasted_iota(jnp.int32, (chunk, LANES), 1)
    srow = lax.broadcasted_iota(jnp.int32, (LANES, 1), 0)
    dirs = ((qkf_ref, vf_ref, gif_ref, gff_ref, hf_ref), (qkr_ref, vr_ref, gir_ref, gfr_ref, hr_ref))
    chains = [(b, d) for b in range(qkf_ref.shape[0]) for d in range(2)]
    old_c = {(b, d, pp): c_s[b, d, pp] for b, d in chains for pp in range(2)}
    old_m = {(b, d): m_s[b, d, 0:1, :] for b, d in chains}
    new_c, new_m = {}, {}
    for b, d in chains:
        qk_ref, v_ref, gi_ref, gf_ref, h_ref = [r if r is None else r.at[b] for r in dirs[d]]
        causal = (col <= row) if d == 0 else (col >= row)
        ipre = gi_ref[...] + brow_ref[0:1, :]
        fl = _log_sigmoid(gf_ref[...] + brow_ref[1:2, :])
        fcum = _exact_left(jnp.where(causal, 1.0, 0.0).astype(BF16), fl)
        ftot = jnp.sum(fl, axis=0, keepdims=True)
        m_prev = old_m[b, d]
        w_log = ftot - fcum + ipre
        m_new = jnp.maximum(ftot + m_prev, jnp.max(w_log, axis=0, keepdims=True))
        cd = jnp.exp(ftot + m_prev - m_new)
        w = jnp.exp(w_log - m_new)
        if emit:
            b_col = ipre - fcum
            run = b_col
            sh = 1
            while sh < chunk:
                if d == 0:
                    run = jnp.maximum(run, jnp.where(trow >= sh, pltpu.roll(run, sh, axis=0), NEG))
                else:
                    run = jnp.maximum(run, jnp.where(trow < chunk - sh, pltpu.roll(run, chunk - sh, axis=0), NEG))
                sh *= 2
            log_inter = fcum + m_prev
            m_row = jnp.maximum(log_inter, fcum + run)
            inter_w = jnp.exp(log_inter - m_row)
            fcm = fcum - m_row
            floor = jnp.exp(-m_row)
            b_rows = b_col.T[0:8, :]
        for pp in range(2):
            qp = qk_ref[:, pp * LANES:(pp + 1) * LANES]
            kp = qk_ref[:, GROUP // 2 + pp * LANES:GROUP // 2 + (pp + 1) * LANES]
            cn = old_c[b, d, pp]
            inc = None
            for e in range(2):
                h = 2 * pp + e
                c = 4 * d + h
                sel = (lane < ATT_HD) if e == 0 else (lane >= ATT_HD)
                vh = v_ref[:, h * LANES:(h + 1) * LANES]
                if emit:
                    qm = jnp.where(sel, qp, jnp.zeros_like(qp))
                    d_mat = jnp.exp(jnp.where(causal, fcm[:, c:c + 1] + b_rows[c:c + 1, :], NEG))
                    s_mat = (_dot_nt(qm, kp) * d_mat).astype(BF16)
                    vb = vh.astype(BF16)
                    acc = (inter_w[:, c:c + 1] * _dot(qm, cn.astype(BF16))
                           + _dot(s_mat, jnp.concatenate([vb, jnp.ones_like(vb)], axis=1)))
                    den = jnp.maximum(jnp.abs(acc[:, LANES:]), floor[:, c:c + 1])
                    h_ref[:, h * LANES:(h + 1) * LANES] = acc[:, 0:LANES] / den
                km = jnp.where(sel, kp, jnp.zeros_like(kp))
                wc = jnp.broadcast_to(w[:, c:c + 1], (chunk, LANES))
                upd = _dot_tn(km, jnp.concatenate([wc * vh, wc], axis=1).astype(BF16))
                inc = upd if inc is None else inc + upd
            c_lo, c_hi = 4 * d + 2 * pp, 4 * d + 2 * pp + 1
            new_c[b, d, pp] = (jnp.where(srow < ATT_HD, cd[:, c_lo:c_lo + 1], cd[:, c_hi:c_hi + 1]) * cn
                               + inc)
        new_m[b, d] = m_new
    for key, val in new_c.items():
        c_s[key] = val
    for (b, d), val in new_m.items():
        m_s[b, d, 0:1, :] = val

    @pl.when(i == pl.num_programs(1) - 1)
    def _():
        c_out[...] = c_s[...]
        m_out[...] = m_s[...]


def _mlstm_scan(qk, p, brow, state, bsz, seq, chunk, emit):
    nc = seq // chunk
    bps = bsz if bsz * chunk <= ML_STEP_ROWS else max(1, ML_STEP_ROWS // chunk)
    assert bsz % bps == 0
    fwd = lambda g, i: i
    rev = lambda g, i: nc - 1 - i
    qk3 = qk.reshape(bsz, seq, GROUP)
    p3 = p.reshape(bsz, seq, NP)

    def chunk_specs(idx):
        return [pl.BlockSpec((bps, chunk, GROUP), lambda g, i: (g, idx(g, i), 0)),
                pl.BlockSpec((bps, chunk, GROUP), lambda g, i: (g, idx(g, i), C_MLV // GROUP)),
                pl.BlockSpec((bps, chunk, LANES), lambda g, i: (g, idx(g, i), C_MLI // LANES)),
                pl.BlockSpec((bps, chunk, LANES), lambda g, i: (g, idx(g, i), C_MLF // LANES))]

    st_dims = [(2, 2, LANES, 2 * LANES), (2, 8, LANES)]
    st_specs = [pl.BlockSpec((bps,) + s, lambda g, i, n=len(s): (g,) + (0,) * n) for s in st_dims]
    st_shapes = [jax.ShapeDtypeStruct((bsz,) + s, F32) for s in st_dims]
    h_specs = [pl.BlockSpec((bps, chunk, GROUP), lambda g, i: (g, fwd(g, i), 0)),
               pl.BlockSpec((bps, chunk, GROUP), lambda g, i: (g, rev(g, i), 0))] if emit else []
    h_shapes = [jax.ShapeDtypeStruct((bsz, seq, GROUP), F32)] * 2 if emit else []
    outs = pl.pallas_call(
        functools.partial(_mlstm_kernel, chunk=chunk, emit=emit),
        grid=(bsz // bps, nc),
        in_specs=chunk_specs(fwd) + chunk_specs(rev)
        + [pl.BlockSpec((8, LANES), lambda g, i: (0, 0))] + st_specs,
        out_specs=h_specs + st_specs,
        out_shape=h_shapes + st_shapes,
        scratch_shapes=[pltpu.VMEM((bps,) + s, F32) for s in st_dims],
        compiler_params=_cparams(("parallel", "arbitrary")),
    )(qk3, p3, p3, p3, qk3, p3, p3, p3, brow, *state)
    if emit:
        return (outs[0].reshape(bsz * seq, GROUP), outs[1].reshape(bsz * seq, GROUP), tuple(outs[2:]))
    return None, None, tuple(outs)


def _hgrn_constants(chunk):
    nlev = int(np.log2(chunk))
    assert 2 ** nlev == chunk
    t = np.arange(chunk)
    mats = [(t[None, :] <= t[:, None]), (t[None, :] > t[:, None])]
    masks = []
    for lev in range(nlev):
        n = chunk >> (lev + 1)
        b0 = t - t % (2 * n)
        upper = (t % (2 * n)) >= n
        m_up = (t[None, :] >= (b0 + n)[:, None]) & (t[None, :] <= t[:, None])
        m_lo = (t[None, :] > t[:, None]) & (t[None, :] <= (b0 + n - 1)[:, None])
        mats.append(np.where(upper[:, None], m_up, m_lo))
        same = (t[:, None] // (2 * n)) == (t[None, :] // (2 * n))
        masks.append(same & upper[:, None] & ~upper[None, :])
    masks.append(t[:, None] == t[None, :])
    mstack = np.concatenate(mats, axis=0).astype(np.float32)
    lmask = np.stack([np.kron(np.eye(2), m.astype(np.float32)) for m in masks])
    nr = mstack.shape[0] // chunk
    mstack_rev = mstack.reshape(nr, chunk, chunk)[:, ::-1, ::-1].reshape(-1, chunk)
    lmask_rev = lmask.reshape(nlev + 1, 2, chunk, 2, chunk)[:, :, ::-1, :, ::-1].reshape(lmask.shape)
    return (jnp.asarray(np.stack([mstack, mstack_rev]), BF16),
            jnp.asarray(np.stack([lmask, lmask_rev]), F32), nlev)


def _stack_pair(x, pp):
    return jnp.concatenate([x[:, (2 * pp) * LANES:(2 * pp + 1) * LANES],
                            x[:, (2 * pp + 1) * LANES:(2 * pp + 2) * LANES]], axis=0)


def _hgrn_kernel(*refs, chunk, nlev, emit, zero_lb):
    (qf_ref, vf_ref, ff_ref, qr_ref, vr_ref, fr_ref, lb_ref, ms_ref, lm_ref, s0_ref) = refs[:10]
    if emit:
        of_ref, or_ref = refs[10:12]
        rest = refs[12:]
    else:
        of_ref = or_ref = None
        rest = refs[10:]
    s_out, s_s = rest
    i = pl.program_id(1)

    @pl.when(i == 0)
    def _():
        s_s[...] = s0_ref[...]

    dirs = ((qf_ref, vf_ref, ff_ref, of_ref), (qr_ref, vr_ref, fr_ref, or_ref))
    chains = [(b, d) for b in range(vf_ref.shape[0]) for d in range(2)]
    old_s = {(b, d, h): s_s[b, d, h] for b, d in chains for h in range(4)}
    new_s = {}

    for b, d in chains:
        q_ref, v_ref, f_ref, o_ref = [r if r is None else r.at[b] for r in dirs[d]]
        fpre = f_ref[...]
        vb = v_ref[...].astype(BF16)
        e_f = jnp.exp(-jnp.abs(fpre))
        ls = jnp.minimum(fpre, 0.0) - _log1p_unit(e_f)
        sig_neg = jnp.where(fpre >= 0.0, e_f, 1.0) / (1.0 + e_f)
        if zero_lb:
            logf, k = ls, sig_neg
        else:
            a = lb_ref[d, 0:1, :]
            bb = lb_ref[d, 1:2, :] + ls
            logf = jnp.maximum(a, bb) + _log1p_unit(jnp.exp(-jnp.abs(a - bb)))
            k = lb_ref[d, 2:3, :] * sig_neg
        lf2 = logf * LOG2E
        hi = lf2.astype(BF16)
        lo = (lf2 - hi.astype(F32)).astype(BF16)
        gtot = jnp.sum(lf2, axis=0, keepdims=True)
        if emit:
            qpre = q_ref[...]
            q = qpre * _sigmoid(qpre)
            qb = q.astype(BF16)
            kb = k.astype(BF16)
        ms = ms_ref[d]
        ex = _dot(ms, hi) + _dot(ms, lo)
        ktil = (k * jnp.exp2(ex[chunk:2 * chunk])).astype(BF16)
        if emit:
            qe = (q * jnp.exp2(ex[0:chunk])).astype(BF16)
            lev_ops = []
            for lev in range(nlev):
                e_l = jnp.exp2(ex[(2 + lev) * chunk:(3 + lev) * chunk].astype(BF16))
                lev_ops.append((qb * e_l, kb * e_l))
        if emit:
            att = [lm_ref[d, nlev] * _dot_nt(_stack_pair(qb, pp), _stack_pair(kb, pp)) for pp in range(2)]
            for lev, (q_l, k_l) in enumerate(lev_ops):
                for pp in range(2):
                    att[pp] = att[pp] + lm_ref[d, lev] * _dot_nt(_stack_pair(q_l, pp), _stack_pair(k_l, pp))
            o_pair = [_dot(att[pp].astype(BF16), _stack_pair(vb, pp)) for pp in range(2)]
        for h in range(4):
            cs = slice(h * LANES, (h + 1) * LANES)
            s_t = old_s[b, d, h]
            if emit:
                pp, e = divmod(h, 2)
                o_ref[:, cs] = o_pair[pp][e * chunk:(e + 1) * chunk] + _dot_nt(qe[:, cs], s_t.astype(BF16))
            new_s[b, d, h] = s_t * jnp.exp2(gtot[:, cs]) + _dot_tn(vb[:, cs], ktil[:, cs])

    for key, val in new_s.items():
        s_s[key] = val

    @pl.when(i == pl.num_programs(1) - 1)
    def _():
        s_out[...] = s_s[...]


def _hgrn_scan(p, lbc, consts, state, bsz, seq, chunk, emit, zero_lb):
    mstack, lmask, nlev = consts
    nc = seq // chunk
    bps = bsz
    fwd = lambda g, i: i
    rev = lambda g, i: nc - 1 - i
    p3 = p.reshape(bsz, seq, NP)

    def chunk_specs(idx, d):
        return [pl.BlockSpec((bps, chunk, GROUP), lambda g, i: (g, idx(g, i), C_HGQ // GROUP)),
                pl.BlockSpec((bps, chunk, GROUP), lambda g, i: (g, idx(g, i), C_HGI // GROUP)),
                pl.BlockSpec((bps, chunk, GROUP), lambda g, i: (g, idx(g, i), C_HGF // GROUP + d))]

    st_spec = pl.BlockSpec((bps, 2, 4, LANES, LANES), lambda g, i: (g, 0, 0, 0, 0))
    st_shape = jax.ShapeDtypeStruct((bsz, 2, 4, LANES, LANES), F32)
    o_specs = [pl.BlockSpec((bps, chunk, GROUP), lambda g, i: (g, fwd(g, i), 0)),
               pl.BlockSpec((bps, chunk, GROUP), lambda g, i: (g, rev(g, i), 0))] if emit else []
    o_shapes = [jax.ShapeDtypeStruct((bsz, seq, GROUP), F32)] * 2 if emit else []
    full = lambda a: pl.BlockSpec(a.shape, lambda g, i: (0,) * a.ndim)
    outs = pl.pallas_call(
        functools.partial(_hgrn_kernel, chunk=chunk, nlev=nlev, emit=emit, zero_lb=zero_lb),
        grid=(bsz // bps, nc),
        in_specs=chunk_specs(fwd, 0) + chunk_specs(rev, 1) + [full(lbc), full(mstack), full(lmask), st_spec],
        out_specs=o_specs + [st_spec],
        out_shape=o_shapes + [st_shape],
        scratch_shapes=[pltpu.VMEM((bps, 2, 4, LANES, LANES), F32)],
        compiler_params=_cparams(("parallel", "arbitrary")),
    )(p3, p3, p3, p3, p3, p3, lbc, mstack, lmask, state)
    if emit:
        return outs[0].reshape(bsz * seq, GROUP), outs[1].reshape(bsz * seq, GROUP), outs[2]
    return None, None, outs[0]


def _outproj_kernel(x_ref, ya_ref, yb_ref, hf_ref, hr_ref, mo_ref, of_ref, or_ref, hg_ref,
                    w_ref, g_ref, o_ref):
    acc = _dot(ya_ref[...], w_ref[0]) + _dot(yb_ref[...], w_ref[1])
    yc = _sigmoid(mo_ref[...]) * (hf_ref[...] + hr_ref[...])
    acc = acc + _dot(yc.astype(BF16), w_ref[2])
    o = of_ref[...] + or_ref[...]
    gate = _sigmoid(hg_ref[...])
    parts = []
    for h in range(4):
        cs = slice(h * LANES, (h + 1) * LANES)
        oh = o[:, cs]
        parts.append(oh * lax.rsqrt(jnp.mean(oh * oh, axis=-1, keepdims=True) + EPS) * gate[:, cs])
    yd = jnp.concatenate(parts, axis=1)
    acc = acc + _dot(yd.astype(BF16), w_ref[3])
    o_ref[...] = x_ref[...] + g_ref[...] * acc


def _out_proj(x, ya, yb, hf, hr, of, orv, p, w_all, layer, gate, rows_per_mod):
    t, d = x.shape
    tm = _row_tile(rows_per_mod, 512)
    per = rows_per_mod // tm
    grp = lambda cb=0: pl.BlockSpec((tm, GROUP), lambda i: (i, cb))
    return pl.pallas_call(
        _outproj_kernel,
        grid=(t // tm,),
        in_specs=[pl.BlockSpec((tm, d), lambda i: (i, 0)), grp(), grp(), grp(), grp(),
                  grp(C_MLO // GROUP), grp(), grp(), grp(C_HGG // GROUP),
                  pl.BlockSpec((None, 4, GROUP, d), lambda i: (layer, 0, 0, 0)),
                  pl.BlockSpec((None, 1, d), lambda i: (i // per, 0, 0))],
        out_specs=pl.BlockSpec((tm, d), lambda i: (i, 0)),
        out_shape=jax.ShapeDtypeStruct((t, d), F32),
        compiler_params=_cparams(("parallel",)),
    )(x, ya, yb, hf, hr, p, of, orv, p, w_all, gate)


def _ffn_kernel(x_ref, xp_ref, xn_ref, g_ref, sh_ref, sc_ref, gate_ref, wa_ref, wu_ref, wc_ref, wd_ref,
                o_ref, xs_ref, acc_ref, *, per, inner):
    i = pl.program_id(0)
    j = pl.program_id(1)
    tm = x_ref.shape[0]

    @pl.when(j == 0)
    def _():
        nm = lambda x: _norm_mod(x, g_ref[...], sh_ref[...], sc_ref[...])
        keep_p = jnp.where(i % per == 0, 0.0, 1.0)
        keep_n = jnp.where(i % per == per - 1, 0.0, 1.0)
        _norm_mod_rows(x_ref, xs_ref, tm, g_ref[...], sh_ref[...], sc_ref[...])
        xs_ref[tm:tm + HALO, :] = (nm(xn_ref[...]) * keep_n).astype(BF16)
        xs_ref[tm + HALO:, :] = (nm(xp_ref[...]) * keep_p).astype(BF16)
        acc_ref[...] = jnp.zeros_like(acc_ref)

    a = _dot(xs_ref[...], wa_ref[...])
    rows = a.shape[0]
    prv = pltpu.roll(a, 1, axis=0)[0:tm]
    nxt = pltpu.roll(a, rows - 1, axis=0)[0:tm]
    if inner:
        pos = lax.broadcasted_iota(jnp.int32, (tm, 1), 0) % inner
        prv = jnp.where(pos == 0, 0.0, prv)
        nxt = jnp.where(pos == inner - 1, 0.0, nxt)
    conv = prv * wc_ref[0:1, :] + a[0:tm] * wc_ref[1:2, :] + nxt * wc_ref[2:3, :]
    u = _dot(xs_ref[0:tm, :], wu_ref[...])
    act = (conv * _sigmoid(conv) * u).astype(BF16)
    acc_ref[...] += _dot(act, wd_ref[...])

    @pl.when(j == pl.num_programs(1) - 1)
    def _():
        o_ref[...] = x_ref[...] + gate_ref[...] * acc_ref[...]


def _ffn(x, gain, shift, scale, gate, w_up, w_conv, w_down, layer, seq):
    t, d = x.shape
    f = w_down.shape[1]
    tm = 512 if t % 512 == 0 else t
    assert seq % tm == 0 or (tm % seq == 0 and shift.shape[0] == 1)
    tf = 512 if f % 512 == 0 else f
    per = max(seq // tm, 1)
    inner = seq if seq < tm else 0
    nf = f // tf
    step = tm // HALO
    nblk8 = t // HALO
    shared = shift.shape[0] == 1
    mod = pl.BlockSpec((None, 1, d), lambda i, j: (0 if shared else i // per, 0, 0))
    return pl.pallas_call(
        functools.partial(_ffn_kernel, per=per, inner=inner),
        grid=(t // tm, nf),
        in_specs=[pl.BlockSpec((tm, d), lambda i, j: (i, 0)),
                  pl.BlockSpec((HALO, d), lambda i, j: (jnp.maximum(i * step - 1, 0), 0)),
                  pl.BlockSpec((HALO, d), lambda i, j: (jnp.minimum((i + 1) * step, nblk8 - 1), 0)),
                  pl.BlockSpec((1, d), lambda i, j: (0, 0)), mod, mod, mod,
                  pl.BlockSpec((None, d, tf), lambda i, j: (layer, 0, j)),
                  pl.BlockSpec((None, d, tf), lambda i, j: (layer, 0, nf + j)),
                  pl.BlockSpec((None, 3, tf), lambda i, j: (layer, 0, j)),
                  pl.BlockSpec((None, tf, d), lambda i, j: (layer, j, 0))],
        out_specs=pl.BlockSpec((tm, d), lambda i, j: (i, 0)),
        out_shape=jax.ShapeDtypeStruct((t, d), F32),
        scratch_shapes=[pltpu.VMEM((tm + 2 * HALO, d), BF16), pltpu.VMEM((tm, d), F32)],
        compiler_params=_cparams(("parallel", "arbitrary")),
    )(x, x, x, gain, shift, scale, gate, w_up, w_up, w_conv, w_down)


def _perm_w_in(w):
    n, d = w.shape[:2]
    swq = w[..., 1536:2048].reshape(n, d, 2, 4, ATT_HD).transpose(0, 1, 3, 2, 4).reshape(n, d, GROUP)
    pad = jnp.zeros((n, d, LANES - 8), w.dtype)
    cols = [w[..., 0:1536], swq, w[..., 2304:3840], w[..., 3856:6416], w[..., 2048:2304],
            w[..., 3840:3848], pad, w[..., 3848:3856], pad]
    out = jnp.concatenate(cols, axis=-1).astype(BF16)
    assert out.shape[-1] == NP
    return out


def _perm_w_out(w):
    n, _, d = w.shape
    w4 = w.reshape(n, 4, GROUP, d)
    swo = w4[:, 1].reshape(n, 2, 4, ATT_HD, d).transpose(0, 2, 1, 3, 4).reshape(n, GROUP, d)
    return jnp.stack([w4[:, 0], swo, w4[:, 2], w4[:, 3]], axis=1).astype(BF16)


def _rope_tables(seq):
    t = jnp.arange(seq)
    half = ATT_HD // 2
    inv = ROPE_BASE ** (-jnp.arange(0, half, 2, dtype=F32) / half)
    ang_r = (t // GRID_W).astype(F32)[:, None] * inv
    ang_c = (t % GRID_W).astype(F32)[:, None] * inv
    cr, sr, cc, sc = jnp.cos(ang_r), jnp.sin(ang_r), jnp.cos(ang_c), jnp.sin(ang_c)
    cos = jnp.concatenate([cr, cr, cc, cc] * 2, axis=-1)
    sin = jnp.concatenate([-sr, sr, -sc, sc] * 2, axis=-1)
    return cos, sin


def kernel(x, c, ctx, c_ctx, w_mod, b_mod, norm_mix, norm_ffn, w_in, w_out, na_qk_gain, na_rpb,
           sw_qk_gain, sw_sink, ml_conv, ml_gate_bias, hg_lb, ffn_up, ffn_conv, ffn_down):
    bsz, seq, d = x.shape
    nctx = ctx.shape[1]
    depth = w_mod.shape[0]
    ml_chunk_ctx, ml_chunk = math.gcd(ML_CHUNK, nctx), math.gcd(ML_CHUNK, seq)
    chunk = math.gcd(HG_CHUNK, math.gcd(seq, nctx))
    assert bsz + 1 <= 8

    c_all = jnp.concatenate([c, c_ctx[None], jnp.zeros((8 - bsz - 1, d), F32)], axis=0)
    mod = _modulation(c_all, w_mod, b_mod).reshape(depth, 8, 6, d)
    cos, sin = _rope_tables(seq)
    bd = jnp.asarray(np.kron(np.eye(LANES // ATT_HD), np.ones((ATT_HD, ATT_HD))), BF16)
    lb_cum = jnp.cumsum(jax.nn.softmax(hg_lb.astype(F32), axis=0), axis=0)
    lower = lb_cum - lb_cum[:1]
    consts = _hgrn_constants(chunk)
    w_in_p = _perm_w_in(w_in)
    w_out_p = _perm_w_out(w_out)
    w_up = ffn_up.astype(BF16)
    w_dn = ffn_down.astype(BF16)
    w_cv = ffn_conv.astype(F32)

    h = x.reshape(bsz * seq, d)
    hc = ctx.reshape(bsz * nctx, d)
    for l in range(depth):
        emit_ctx = l < depth - 1
        lat = lambda k: mod[l, :bsz, k].reshape(bsz, 1, d)
        cx = lambda k: mod[l, bsz, k].reshape(1, 1, d)
        gain = norm_mix[l].reshape(1, d)
        p_lat = _in_proj(h, gain, lat(0), lat(1), w_in_p, l, seq)
        p_ctx = _in_proj(hc, gain, cx(0), cx(1), w_in_p, l, bsz * nctx)

        gna = jnp.tile(na_qk_gain[l].astype(F32), (1, GROUP // ATT_HD))
        gsw = jnp.tile(sw_qk_gain[l].astype(F32), (1, GROUP // ATT_HD))
        naq, nak, nav, swq, swk, swv = _attn_prep(p_lat, gna, gsw, bd, cos, sin, seq, True)
        cnaq, cnak, cnav, cswq, cswk, cswv = _attn_prep(p_ctx, gna, gsw, bd, cos, sin, nctx, False)
        sink = sw_sink[l].astype(F32)
        ya = _na_attention(naq, nak, nav, cnak, cnav, _na_bias_table(na_rpb[l], seq // GRID_W), bsz, seq, nctx)
        yb = _sw_attention(swq, swk, swv, cswk, cswv, sink, bsz, seq, nctx)

        qk_lat = _ml_prep(p_lat, ml_conv[l].astype(F32), seq)
        qk_ctx = _ml_prep(p_ctx, ml_conv[l].astype(F32), nctx)
        gb = ml_gate_bias[l].astype(F32).reshape(2, 8)
        brow = jnp.zeros((8, LANES), F32).at[0:2, 0:8].set(gb)
        ml0 = (jnp.zeros((bsz, 2, 2, LANES, 2 * LANES), F32), jnp.zeros((bsz, 2, 8, LANES), F32))
        hcf, hcr, ml_state = _mlstm_scan(qk_ctx, p_ctx, brow, ml0, bsz, nctx, ml_chunk_ctx, emit_ctx)
        hf, hr, _ = _mlstm_scan(qk_lat, p_lat, brow, ml_state, bsz, seq, ml_chunk, True)

        lb = lower[l]
        lbc = jnp.stack([jnp.maximum(jnp.log(lb), NEG), jnp.log1p(-lb), 1.0 - lb], axis=1)
        lbc = jnp.concatenate([lbc, jnp.zeros((2, 5, GROUP), F32)], axis=1)
        hg0 = jnp.zeros((bsz, 2, 4, LANES, LANES), F32)
        ocf, ocr, hg_state = _hgrn_scan(p_ctx, lbc, consts, hg0, bsz, nctx, chunk, emit_ctx, l == 0)
        of, orv, _ = _hgrn_scan(p_lat, lbc, consts, hg_state, bsz, seq, chunk, True, l == 0)

        gain2 = norm_ffn[l].reshape(1, d)
        h = _out_proj(h, ya, yb, hf, hr, of, orv, p_lat, w_out_p, l, lat(2), seq)
        h = _ffn(h, gain2, lat(3), lat(4), lat(5), w_up, w_cv, w_dn, l, seq)
        if emit_ctx:
            yac, ybc = _ctx_attention(cnaq, cnak, cnav, cswq, cswk, cswv, sink, bsz, nctx)
            hc = _out_proj(hc, yac, ybc, hcf, hcr, ocf, ocr, p_ctx, w_out_p, l, cx(2), bsz * nctx)
            hc = _ffn(hc, gain2, cx(3), cx(4), cx(5), w_up, w_cv, w_dn, l, nctx)
    return h.reshape(bsz, seq, d)
```

```python
import functools
import math

import numpy as np
import jax
import jax.numpy as jnp
from jax import lax
from jax.experimental import pallas as pl
from jax.experimental.pallas import tpu as pltpu

F32 = jnp.float32
BF16 = jnp.bfloat16

GRID_W = 64
ATT_HD = 64
NA_ROWS = 8
NA_COLS = 16
SW_BLOCK = 128
ROPE_BASE = 10000.0
EPS = 1e-6
NEG = -1e30
LOG2E = 1.4426950408889634
GROUP = 512
LANES = 128
HALO = 8
VMEM_LIMIT = 56 * 1024 * 1024

C_NAQ, C_NAK, C_NAV, C_SWQ = 0, 512, 1024, 1536
C_MLQK, C_MLV, C_MLO = 2048, 2560, 3072
C_HGQ, C_HGI, C_HGF, C_HGG = 3584, 4096, 4608, 5632
C_SWK, C_SWV = 6144, 6272
C_MLI, C_MLF = 6400, 6528
NP = 6656
ML_CHUNK = 256
ML_STEP_ROWS = 1024
HG_CHUNK = 64
SW_PERM = (0, 4, 1, 5, 2, 6, 3, 7)


def _cparams(sem):
    return pltpu.CompilerParams(dimension_semantics=sem, vmem_limit_bytes=VMEM_LIMIT)


def _sigmoid(x):
    return 1.0 / (1.0 + jnp.exp(-x))


def _log1p_unit(e):
    return jnp.log(1.0 + e)


def _log_sigmoid(x):
    return jnp.minimum(x, 0.0) - _log1p_unit(jnp.exp(-jnp.abs(x)))


def _split3(x):
    hi = x.astype(BF16)
    r = x - hi.astype(F32)
    mid = r.astype(BF16)
    lo = (r - mid.astype(F32)).astype(BF16)
    return hi, mid, lo


def _dot(a, b):
    return jnp.dot(a, b, preferred_element_type=F32)


def _dot_nt(a, b):
    return lax.dot_general(a, b, (((1,), (1,)), ((), ())), preferred_element_type=F32)


def _dot_tn(a, b):
    return lax.dot_general(a, b, (((0,), (0,)), ((), ())), preferred_element_type=F32)


def _exact_left(m01, x):
    hi, mid, lo = _split3(x)
    return _dot(m01, hi) + _dot(m01, mid) + _dot(m01, lo)


def _mod_kernel(c_ref, w_ref, b_ref, o_ref):
    c = c_ref[...]
    a = c * _sigmoid(c)
    a_hi = a.astype(BF16)
    a_lo = (a - a_hi.astype(F32)).astype(BF16)
    w = w_ref[...]
    w_hi = w.astype(BF16)
    w_lo = (w - w_hi.astype(F32)).astype(BF16)
    o_ref[...] = _dot(a_hi, w_hi) + _dot(a_hi, w_lo) + _dot(a_lo, w_hi) + b_ref[...]


def _modulation(c_all, w_mod, b_mod):
    depth, d, n = w_mod.shape
    tn = 1024 if n % 1024 == 0 else n
    return pl.pallas_call(
        _mod_kernel,
        grid=(depth, n // tn),
        in_specs=[pl.BlockSpec((8, d), lambda l, j: (0, 0)),
                  pl.BlockSpec((None, d, tn), lambda l, j: (l, 0, j)),
                  pl.BlockSpec((None, 1, tn), lambda l, j: (l, 0, j))],
        out_specs=pl.BlockSpec((None, 8, tn), lambda l, j: (l, 0, j)),
        out_shape=jax.ShapeDtypeStruct((depth, 8, n), F32),
        compiler_params=_cparams(("parallel", "parallel")),
    )(c_all, w_mod, b_mod.reshape(depth, 1, n))


def _norm_mod(x, gain, shift, scale):
    y = x * lax.rsqrt(jnp.mean(x * x, axis=-1, keepdims=True) + EPS) * gain
    return y * (1.0 + scale) + shift


NORM_ROWS = 16


def _norm_mod_rows(src_ref, dst_ref, nrows, gain, shift, scale):
    amp = gain * (1.0 + scale)

    def body(c, carry):
        rows = pl.ds(pl.multiple_of(c * NORM_ROWS, NORM_ROWS), NORM_ROWS)
        x = src_ref[rows, :]
        y = x * lax.rsqrt(jnp.mean(x * x, axis=-1, keepdims=True) + EPS) * amp + shift
        dst_ref[rows, :] = y.astype(BF16)
        return carry

    assert nrows % NORM_ROWS == 0
    lax.fori_loop(0, nrows // NORM_ROWS, body, 0, unroll=4)


def _inproj_kernel(x_ref, g_ref, sh_ref, sc_ref, w_ref, o_ref, xn_ref):
    @pl.when(pl.program_id(1) == 0)
    def _():
        _norm_mod_rows(x_ref, xn_ref, x_ref.shape[0], g_ref[...], sh_ref[...], sc_ref[...])

    o_ref[...] = _dot(xn_ref[...], w_ref[...])


def _row_tile(rows_per_mod, pref):
    tm = min(pref, rows_per_mod)
    assert rows_per_mod % tm == 0
    return tm


def _in_proj(x, gain, shift, scale, w_all, layer, rows_per_mod):
    t, d = x.shape
    n = w_all.shape[2]
    tm = _row_tile(rows_per_mod, 512)
    tn = next(c for c in (1664, 512, n) if n % c == 0)
    per = rows_per_mod // tm
    return pl.pallas_call(
        _inproj_kernel,
        grid=(t // tm, n // tn),
        in_specs=[pl.BlockSpec((tm, d), lambda i, j: (i, 0)),
                  pl.BlockSpec((1, d), lambda i, j: (0, 0)),
                  pl.BlockSpec((None, 1, d), lambda i, j: (i // per, 0, 0)),
                  pl.BlockSpec((None, 1, d), lambda i, j: (i // per, 0, 0)),
                  pl.BlockSpec((None, d, tn), lambda i, j: (layer, 0, j))],
        out_specs=pl.BlockSpec((tm, tn), lambda i, j: (i, j)),
        out_shape=jax.ShapeDtypeStruct((t, n), F32),
        scratch_shapes=[pltpu.VMEM((tm, d), BF16)],
        compiler_params=_cparams(("parallel", "arbitrary")),
    )(x, gain, shift, scale, w_all)


def _head_rms(x, gain_row, bd):
    parts = []
    for g in range(x.shape[1] // LANES):
        xg = x[:, g * LANES:(g + 1) * LANES]
        ss = _dot((xg * xg).astype(BF16), bd)
        parts.append(xg * lax.rsqrt(ss * (1.0 / ATT_HD) + EPS))
    return (parts[0] if len(parts) == 1 else jnp.concatenate(parts, axis=1)) * gain_row


def _rope(x, cos, sin, first):
    w = x.shape[-1]
    nxt = pltpu.roll(x, w - 16, axis=1)
    prv = pltpu.roll(x, 16, axis=1)
    return x * cos + jnp.where(first, nxt, prv) * sin


def _prep_kernel(p_ref, pkv_ref, mlp_ref, mln_ref, gna_ref, gsw_ref, bd_ref, cos_ref, sin_ref, wconv_ref,
                 naq_ref, nak_ref, nav_ref, swq_ref, swk_ref, swv_ref, qk_ref, *, rope, per):
    i = pl.program_id(0)
    x = p_ref[:, C_MLQK:C_MLQK + GROUP]
    tm = x.shape[0]
    row = lax.broadcasted_iota(jnp.int32, (tm, 1), 0)
    keep_p = jnp.where(i % per == 0, 0.0, 1.0)
    keep_n = jnp.where(i % per == per - 1, 0.0, 1.0)
    prv = jnp.where(row == 0, mlp_ref[HALO - 1:HALO, :] * keep_p, pltpu.roll(x, 1, axis=0))
    nxt = jnp.where(row == tm - 1, mln_ref[0:1, :] * keep_n, pltpu.roll(x, tm - 1, axis=0))
    y = prv * wconv_ref[0:1, :] + x * wconv_ref[1:2, :] + nxt * wconv_ref[2:3, :]
    y = y * _sigmoid(y)
    qlane = lax.broadcasted_iota(jnp.int32, (1, GROUP), 1) < GROUP // 2
    qk_ref[...] = (y * jnp.where(qlane, ATT_HD ** -0.5, 1.0)).astype(BF16)

    bd = bd_ref[...]
    scale = ATT_HD ** -0.5
    naq_ref[...] = (_head_rms(p_ref[:, C_NAQ:C_NAQ + GROUP], gna_ref[0:1, :], bd) * scale).astype(BF16)
    nak_ref[...] = _head_rms(p_ref[:, C_NAK:C_NAK + GROUP], gna_ref[1:2, :], bd).astype(BF16)
    nav_ref[...] = p_ref[:, C_NAV:C_NAV + GROUP].astype(BF16)
    q = _head_rms(p_ref[:, C_SWQ:C_SWQ + GROUP], gsw_ref[0:1, :], bd)
    k = _head_rms(pkv_ref[:, 0:LANES], gsw_ref[1:2, 0:LANES], bd)
    if rope:
        cos = cos_ref[...]
        sin = sin_ref[...]
        lane = lax.broadcasted_iota(jnp.int32, (1, GROUP), 1)
        first = (lane % 32) < 16
        q = _rope(q, jnp.concatenate([cos] * 4, axis=1), jnp.concatenate([sin] * 4, axis=1), first)
        k = _rope(k, cos, sin, first[:, 0:LANES])
    swq_ref[...] = (q * scale).astype(BF16)
    swk_ref[...] = k.astype(BF16)
    swv_ref[...] = pkv_ref[:, LANES:2 * LANES].astype(BF16)


def _prep(p, gna, gsw, bd, cos, sin, conv_w, seq, rope):
    t = p.shape[0]
    tm = _row_tile(seq, 512)
    per = seq // tm
    step = tm // HALO
    nblk8 = t // HALO
    mlcol = C_MLQK // GROUP
    out = lambda w: jax.ShapeDtypeStruct((t, w), BF16)
    ospec = lambda w: pl.BlockSpec((tm, w), lambda i: (i, 0))
    return pl.pallas_call(
        functools.partial(_prep_kernel, rope=rope, per=per),
        grid=(t // tm,),
        in_specs=[pl.BlockSpec((tm, C_MLV), lambda i: (i, 0)),
                  pl.BlockSpec((tm, 2 * LANES), lambda i: (i, C_SWK // (2 * LANES))),
                  pl.BlockSpec((HALO, GROUP), lambda i: (jnp.maximum(i * step - 1, 0), mlcol)),
                  pl.BlockSpec((HALO, GROUP), lambda i: (jnp.minimum((i + 1) * step, nblk8 - 1), mlcol)),
                  pl.BlockSpec((2, GROUP), lambda i: (0, 0)),
                  pl.BlockSpec((2, GROUP), lambda i: (0, 0)),
                  pl.BlockSpec((LANES, LANES), lambda i: (0, 0)),
                  pl.BlockSpec((tm, LANES), lambda i: (i % per, 0)),
                  pl.BlockSpec((tm, LANES), lambda i: (i % per, 0)),
                  pl.BlockSpec((3, GROUP), lambda i: (0, 0))],
        out_specs=[ospec(GROUP), ospec(GROUP), ospec(GROUP), ospec(GROUP), ospec(LANES), ospec(LANES),
                   ospec(GROUP)],
        out_shape=[out(GROUP), out(GROUP), out(GROUP), out(GROUP), out(LANES), out(LANES), out(GROUP)],
        compiler_params=_cparams(("parallel",)),
    )(p, p, p, p, gna, gsw, bd, cos, sin, conv_w)


def _pair_rows(q_pair):
    lane = lax.broadcasted_iota(jnp.int32, q_pair.shape, 1)
    zero = jnp.zeros_like(q_pair)
    return jnp.concatenate([jnp.where(lane < ATT_HD, q_pair, zero),
                            jnp.where(lane >= ATT_HD, q_pair, zero)], axis=0)


def _pair_merge(o2, inv_l):
    m = o2.shape[0] // 2
    lane = lax.broadcasted_iota(jnp.int32, (m, LANES), 1)
    return jnp.where(lane < ATT_HD, o2[0:m] * inv_l[0:m], o2[m:] * inv_l[m:])


NA_RPS = 4


def _softmax_pv(parts, sink=None):
    m = functools.reduce(jnp.maximum, [jnp.max(s, axis=-1, keepdims=True) for s, _ in parts])
    if sink is not None:
        m = jnp.maximum(m, sink)
    acc = None
    for s, v in parts:
        e = jnp.exp((s - m).astype(BF16))
        o = _dot(e, jnp.concatenate([v, jnp.ones_like(v)], axis=1))
        acc = o if acc is None else acc + o
    l = acc[:, LANES:LANES + 1]
    if sink is not None:
        l = l + jnp.exp(sink - m)
    return acc[:, 0:LANES], l


NA_WIN = NA_ROWS + NA_RPS


def _na_window(step, rows):
    return jnp.clip(step * NA_RPS - NA_ROWS // 2, 0, rows - NA_WIN)


def _na_kernel(q_ref, k_ref, v_ref, kc_ref, vc_ref, b_ref, o_ref, *, rows):
    start = pl.multiple_of(_na_window(pl.program_id(1), rows) * GRID_W, GRID_W)
    nkey = NA_WIN * GRID_W
    for p in range(GROUP // LANES):
        cs = slice(p * LANES, (p + 1) * LANES)
        qs = jnp.concatenate([_pair_rows(q_ref[rr * GRID_W:(rr + 1) * GRID_W, cs]) for rr in range(NA_RPS)],
                             axis=0)
        kw = k_ref[pl.ds(start, nkey), cs]
        vw = v_ref[pl.ds(start, nkey), cs]
        s_nb = _dot_nt(qs, kw) + b_ref[p]
        s_cx = _dot_nt(qs, kc_ref[:, cs])
        o2, l = _softmax_pv([(s_nb, vw), (s_cx, vc_ref[:, cs])])
        inv_l = 1.0 / l
        for rr in range(NA_RPS):
            sl = slice(rr * 2 * GRID_W, (rr + 1) * 2 * GRID_W)
            o_ref[rr * GRID_W:(rr + 1) * GRID_W, cs] = _pair_merge(o2[sl], inv_l[sl]).astype(BF16)


def _na_attention(q, k, v, kc, vc, bias, bsz, seq, ctx):
    rows = seq // GRID_W
    assert rows >= NA_WIN and rows % NA_RPS == 0
    steps = rows // NA_RPS
    k3, v3 = k.reshape(bsz, seq, GROUP), v.reshape(bsz, seq, GROUP)
    kc3, vc3 = kc.reshape(bsz, ctx, GROUP), vc.reshape(bsz, ctx, GROUP)
    qspec = pl.BlockSpec((NA_RPS * GRID_W, GROUP), lambda b, r: (b * steps + r, 0))
    variant = lambda b, r: ((r > 0).astype(jnp.int32) + (r == steps - 1).astype(jnp.int32), 0, 0, 0)
    return pl.pallas_call(
        functools.partial(_na_kernel, rows=rows),
        grid=(bsz, steps),
        in_specs=[qspec,
                  pl.BlockSpec((None, seq, GROUP), lambda b, r: (b, 0, 0)),
                  pl.BlockSpec((None, seq, GROUP), lambda b, r: (b, 0, 0)),
                  pl.BlockSpec((None, ctx, GROUP), lambda b, r: (b, 0, 0)),
                  pl.BlockSpec((None, ctx, GROUP), lambda b, r: (b, 0, 0)),
                  pl.BlockSpec((None,) + bias.shape[1:], variant)],
        out_specs=qspec,
        out_shape=jax.ShapeDtypeStruct((bsz * seq, GROUP), BF16),
        compiler_params=_cparams(("parallel", "arbitrary")),
    )(q, k3, v3, kc3, vc3, bias)


def _na_bias_table(rpb, rows):
    nh = rpb.shape[0]
    steps = rows // NA_RPS
    c = np.arange(GRID_W)
    kc = np.arange(GRID_W)
    cstart = np.clip(c - NA_COLS // 2, 0, GRID_W - NA_COLS)
    ok = (kc[None, :] >= cstart[:, None]) & (kc[None, :] < cstart[:, None] + NA_COLS)
    dc = np.clip(kc[None, :] - c[:, None] + NA_COLS - 1, 0, 2 * NA_COLS - 2)
    onehot = (dc.reshape(1, -1) == np.arange(2 * NA_COLS - 1)[:, None]).astype(np.float32)
    t = jnp.dot(rpb.astype(F32).reshape(-1, 2 * NA_COLS - 1), onehot, precision=lax.Precision.HIGHEST)
    t = jnp.where(ok[None, None], t.reshape(nh, -1, GRID_W, GRID_W), NEG)

    def step_layout(s):
        ws = int(np.clip(s * NA_RPS - NA_ROWS // 2, 0, rows - NA_WIN))
        r = s * NA_RPS + np.arange(NA_RPS)[:, None]
        rs = np.clip(r - NA_ROWS // 2, 0, rows - NA_ROWS)
        krow = ws + np.arange(NA_WIN)[None, :]
        return (krow - r + NA_ROWS - 1), (krow >= rs) & (krow < rs + NA_ROWS)

    layouts = [step_layout(s) for s in range(steps)]
    variants = [layouts[0], layouts[1], layouts[-1]]
    for s, (idx, valid) in enumerate(layouts):
        want = variants[(s > 0) + (s == steps - 1)]
        assert (valid == want[1]).all() and (idx[valid] == want[0][want[1]]).all()

    ndr = 2 * NA_ROWS - 1
    width = (ndr + 3) * GRID_W
    t2 = jnp.transpose(t, (0, 2, 1, 3)).reshape(nh, GRID_W, ndr * GRID_W)
    t2 = jnp.pad(t2, ((0, 0), (0, 0), (GRID_W, width - (ndr + 1) * GRID_W)), constant_values=NEG)
    t2s = jnp.pad(t2[:, :, GRID_W:], ((0, 0), (0, 0), (0, GRID_W)), constant_values=NEG)

    def build(ta_ref, tb_ref, o_ref):
        lane = lax.broadcasted_iota(jnp.int32, (GRID_W, LANES), 1)
        neg = jnp.full((GRID_W, LANES), NEG, F32)
        for v, (idx, valid) in enumerate(variants):
            for rr in range(NA_RPS):
                for e in range(2):
                    row0 = (rr * 2 + e) * GRID_W
                    for jj in range(NA_WIN // 2):
                        j = 2 * jj
                        ok0, ok1 = bool(valid[rr, j]), bool(valid[rr, j + 1])
                        if not (ok0 or ok1):
                            piece = neg
                        else:
                            blk = int(idx[rr, j] if ok0 else idx[rr, j + 1] - 1) + 1
                            src = ta_ref if blk % 2 == 0 else tb_ref
                            off = (blk - blk % 2) * GRID_W
                            piece = src[e, :, off:off + LANES]
                            if not ok0:
                                piece = jnp.where(lane >= GRID_W, piece, NEG)
                            if not ok1:
                                piece = jnp.where(lane < GRID_W, piece, NEG)
                        o_ref[v, row0:row0 + GRID_W, j * GRID_W:(j + 2) * GRID_W] = piece

    npair = nh // 2
    tspec = pl.BlockSpec((2, GRID_W, width), lambda p: (p, 0, 0))
    return pl.pallas_call(
        build,
        grid=(npair,),
        in_specs=[tspec, tspec],
        out_specs=pl.BlockSpec((3, None, NA_RPS * 2 * GRID_W, NA_WIN * GRID_W), lambda p: (0, p, 0, 0)),
        out_shape=jax.ShapeDtypeStruct((3, npair, NA_RPS * 2 * GRID_W, NA_WIN * GRID_W), F32),
        compiler_params=_cparams(("parallel",)),
    )(t2, t2s)


SW_BPS = 2


def _sw_kernel(sink_ref, q_ref, k_ref, v_ref, kc_ref, vc_ref, o_ref, *, seq):
    nq = SW_BPS * SW_BLOCK
    nwin = (SW_BPS + 2) * SW_BLOCK
    q0 = pl.program_id(1) * nq
    start = pl.multiple_of(jnp.clip(q0 - SW_BLOCK, 0, seq - nwin), SW_BLOCK)
    kw = k_ref[pl.ds(start, nwin), :]
    vw = v_ref[pl.ds(start, nwin), :]
    kc = kc_ref[...]
    vc = vc_ref[...]
    row = lax.broadcasted_iota(jnp.int32, (2 * nq, nwin), 0)
    col = lax.broadcasted_iota(jnp.int32, (2 * nq, nwin), 1)
    ok = jnp.abs((start + col) - (q0 + row % nq)) <= SW_BLOCK
    half = lax.broadcasted_iota(jnp.int32, (2 * nq, 1), 0) < nq
    for j in range(GROUP // LANES):
        cs = slice(j * LANES, (j + 1) * LANES)
        qs = _pair_rows(q_ref[:, cs])
        s_bd = jnp.where(ok, _dot_nt(qs, kw), NEG)
        s_cx = _dot_nt(qs, kc)
        sink = jnp.where(half, sink_ref[j], sink_ref[j + 4])
        o2, l = _softmax_pv([(s_bd, vw), (s_cx, vc)], sink)
        o_ref[:, cs] = _pair_merge(o2, 1.0 / l).astype(BF16)


def _sw_attention(q, k, v, kc, vc, sink, bsz, seq, ctx):
    assert seq % (SW_BPS * SW_BLOCK) == 0 and seq >= (SW_BPS + 2) * SW_BLOCK
    nb = seq // (SW_BPS * SW_BLOCK)
    k3, v3 = k.reshape(bsz, seq, LANES), v.reshape(bsz, seq, LANES)
    kc3, vc3 = kc.reshape(bsz, ctx, LANES), vc.reshape(bsz, ctx, LANES)
    return pl.pallas_call(
        functools.partial(_sw_kernel, seq=seq),
        grid=(bsz, nb),
        in_specs=[pl.BlockSpec(memory_space=pltpu.SMEM),
                  pl.BlockSpec((SW_BPS * SW_BLOCK, GROUP), lambda b, n: (b * nb + n, 0)),
                  pl.BlockSpec((None, seq, LANES), lambda b, n: (b, 0, 0)),
                  pl.BlockSpec((None, seq, LANES), lambda b, n: (b, 0, 0)),
                  pl.BlockSpec((None, ctx, LANES), lambda b, n: (b, 0, 0)),
                  pl.BlockSpec((None, ctx, LANES), lambda b, n: (b, 0, 0))],
        out_specs=pl.BlockSpec((SW_BPS * SW_BLOCK, GROUP), lambda b, n: (b * nb + n, 0)),
        out_shape=jax.ShapeDtypeStruct((bsz * seq, GROUP), BF16),
        compiler_params=_cparams(("parallel", "arbitrary")),
    )(sink, q, k3, v3, kc3, vc3)


def _ctx_attn_kernel(sink_ref, qa_ref, ka_ref, va_ref, qb_ref, kb_ref, vb_ref, oa_ref, ob_ref):
    for p in range(GROUP // LANES):
        cs = slice(p * LANES, (p + 1) * LANES)
        qs = _pair_rows(qa_ref[:, cs])
        o2, l = _softmax_pv([(_dot_nt(qs, ka_ref[:, cs]), va_ref[:, cs])])
        oa_ref[:, cs] = _pair_merge(o2, 1.0 / l).astype(BF16)
    ctx = qb_ref.shape[0]
    half = lax.broadcasted_iota(jnp.int32, (2 * ctx, 1), 0) < ctx
    for j in range(GROUP // LANES):
        cs = slice(j * LANES, (j + 1) * LANES)
        qs = _pair_rows(qb_ref[:, cs])
        sink = jnp.where(half, sink_ref[j], sink_ref[j + 4])
        o2, l = _softmax_pv([(_dot_nt(qs, kb_ref[...]), vb_ref[...])], sink)
        ob_ref[:, cs] = _pair_merge(o2, 1.0 / l).astype(BF16)


def _ctx_attention(qa, ka, va, qb, kb, vb, sink, bsz, ctx):
    big = pl.BlockSpec((ctx, GROUP), lambda b: (b, 0))
    small = pl.BlockSpec((ctx, LANES), lambda b: (b, 0))
    return pl.pallas_call(
        _ctx_attn_kernel,
        grid=(bsz,),
        in_specs=[pl.BlockSpec(memory_space=pltpu.SMEM), big, big, big, big, small, small],
        out_specs=[big, big],
        out_shape=[jax.ShapeDtypeStruct((bsz * ctx, GROUP), BF16)] * 2,
        compiler_params=_cparams(("parallel",)),
    )(sink, qa, ka, va, qb, kb, vb)


def _mlstm_kernel(*refs, chunk, emit):
    (qkf_ref, vf_ref, gif_ref, gff_ref, qkr_ref, vr_ref, gir_ref, gfr_ref,
     brow_ref, c0_ref, m0_ref) = refs[:11]
    if emit:
        hf_ref, hr_ref = refs[11:13]
        rest = refs[13:]
    else:
        hf_ref = hr_ref = None
        rest = refs[11:]
    c_out, m_out, c_s, m_s = rest
    i = pl.program_id(1)

    @pl.when(i == 0)
    def _():
        c_s[...] = c0_ref[...]
        m_s[...] = m0_ref[...]

    row = lax.broadcasted_iota(jnp.int32, (chunk, chunk), 0)
    col = lax.broadcasted_iota(jnp.int32, (chunk, chunk), 1)
    trow = lax.broadcasted_iota(jnp.int32, (chunk, 1), 0)
    lane = lax.broadcasted_iota(jnp.int32, (chunk, LANES), 1)
    srow = lax.broadcasted_iota(jnp.int32, (LANES, 1), 0)
    dirs = ((qkf_ref, vf_ref, gif_ref, gff_ref, hf_ref), (qkr_ref, vr_ref, gir_ref, gfr_ref, hr_ref))
    chains = [(b, d) for b in range(qkf_ref.shape[0]) for d in range(2)]
    old_c = {(b, d, pp): c_s[b, d, pp] for b, d in chains for pp in range(2)}
    old_m = {(b, d): m_s[b, d, 0:1, :] for b, d in chains}
    new_c, new_m = {}, {}
    for b, d in chains:
        qk_ref, v_ref, gi_ref, gf_ref, h_ref = [r if r is None else r.at[b] for r in dirs[d]]
        causal = (col <= row) if d == 0 else (col >= row)
        ipre = gi_ref[...] + brow_ref[0:1, :]
        fl = _log_sigmoid(gf_ref[...] + brow_ref[1:2, :])
        fcum = _exact_left(jnp.where(causal, 1.0, 0.0).astype(BF16), fl)
        ftot = jnp.sum(fl, axis=0, keepdims=True)
        m_prev = old_m[b, d]
        w_log = ftot - fcum + ipre
        m_new = jnp.maximum(ftot + m_prev, jnp.max(w_log, axis=0, keepdims=True))
        cd = jnp.exp(ftot + m_prev - m_new)
        w = jnp.exp(w_log - m_new)
        if emit:
            b_col = ipre - fcum
            run = b_col
            sh = 1
            while sh < chunk:
                if d == 0:
                    run = jnp.maximum(run, jnp.where(trow >= sh, pltpu.roll(run, sh, axis=0), NEG))
                else:
                    run = jnp.maximum(run, jnp.where(trow < chunk - sh, pltpu.roll(run, chunk - sh, axis=0), NEG))
                sh *= 2
            log_inter = fcum + m_prev
            m_row = jnp.maximum(log_inter, fcum + run)
            inter_w = jnp.exp(log_inter - m_row)
            fcm = fcum - m_row
            floor = jnp.exp(-m_row)
            b_rows = b_col.T[0:8, :]
        for pp in range(2):
            qp = qk_ref[:, pp * LANES:(pp + 1) * LANES]
            kp = qk_ref[:, GROUP // 2 + pp * LANES:GROUP // 2 + (pp + 1) * LANES]
            cn = old_c[b, d, pp]
            inc = None
            for e in range(2):
                h = 2 * pp + e
                c = 4 * d + h
                sel = (lane < ATT_HD) if e == 0 else (lane >= ATT_HD)
                vh = v_ref[:, h * LANES:(h + 1) * LANES]
                if emit:
                    qm = jnp.where(sel, qp, jnp.zeros_like(qp))
                    d_mat = jnp.exp(jnp.where(causal, fcm[:, c:c + 1] + b_rows[c:c + 1, :], NEG))
                    s_mat = (_dot_nt(qm, kp) * d_mat).astype(BF16)
                    vb = vh.astype(BF16)
                    acc = (inter_w[:, c:c + 1] * _dot(qm, cn.astype(BF16))
                           + _dot(s_mat, jnp.concatenate([vb, jnp.ones_like(vb)], axis=1)))
                    den = jnp.maximum(jnp.abs(acc[:, LANES:]), floor[:, c:c + 1])
                    h_ref[:, h * LANES:(h + 1) * LANES] = acc[:, 0:LANES] / den
                km = jnp.where(sel, kp, jnp.zeros_like(kp))
                wc = jnp.broadcast_to(w[:, c:c + 1], (chunk, LANES))
                upd = _dot_tn(km, jnp.concatenate([wc * vh, wc], axis=1).astype(BF16))
                inc = upd if inc is None else inc + upd
            c_lo, c_hi = 4 * d + 2 * pp, 4 * d + 2 * pp + 1
            new_c[b, d, pp] = (jnp.where(srow < ATT_HD, cd[:, c_lo:c_lo + 1], cd[:, c_hi:c_hi + 1]) * cn
                               + inc)
        new_m[b, d] = m_new
    for key, val in new_c.items():
        c_s[key] = val
    for (b, d), val in new_m.items():
        m_s[b, d, 0:1, :] = val

    @pl.when(i == pl.num_programs(1) - 1)
    def _():
        c_out[...] = c_s[...]
        m_out[...] = m_s[...]


def _mlstm_scan(qk, p, brow, state, bsz, seq, chunk, emit):
    nc = seq // chunk
    bps = bsz if bsz * chunk <= ML_STEP_ROWS else max(1, ML_STEP_ROWS // chunk)
    assert bsz % bps == 0
    fwd = lambda g, i: i
    rev = lambda g, i: nc - 1 - i
    qk3 = qk.reshape(bsz, seq, GROUP)
    p3 = p.reshape(bsz, seq, NP)

    def chunk_specs(idx):
        return [pl.BlockSpec((bps, chunk, GROUP), lambda g, i: (g, idx(g, i), 0)),
                pl.BlockSpec((bps, chunk, GROUP), lambda g, i: (g, idx(g, i), C_MLV // GROUP)),
                pl.BlockSpec((bps, chunk, LANES), lambda g, i: (g, idx(g, i), C_MLI // LANES)),
                pl.BlockSpec((bps, chunk, LANES), lambda g, i: (g, idx(g, i), C_MLF // LANES))]

    st_dims = [(2, 2, LANES, 2 * LANES), (2, 8, LANES)]
    st_specs = [pl.BlockSpec((bps,) + s, lambda g, i, n=len(s): (g,) + (0,) * n) for s in st_dims]
    st_shapes = [jax.ShapeDtypeStruct((bsz,) + s, F32) for s in st_dims]
    h_specs = [pl.BlockSpec((bps, chunk, GROUP), lambda g, i: (g, fwd(g, i), 0)),
               pl.BlockSpec((bps, chunk, GROUP), lambda g, i: (g, rev(g, i), 0))] if emit else []
    h_shapes = [jax.ShapeDtypeStruct((bsz, seq, GROUP), F32)] * 2 if emit else []
    outs = pl.pallas_call(
        functools.partial(_mlstm_kernel, chunk=chunk, emit=emit),
        grid=(bsz // bps, nc),
        in_specs=chunk_specs(fwd) + chunk_specs(rev)
        + [pl.BlockSpec((8, LANES), lambda g, i: (0, 0))] + st_specs,
        out_specs=h_specs + st_specs,
        out_shape=h_shapes + st_shapes,
        scratch_shapes=[pltpu.VMEM((bps,) + s, F32) for s in st_dims],
        compiler_params=_cparams(("parallel", "arbitrary")),
    )(qk3, p3, p3, p3, qk3, p3, p3, p3, brow, *state)
    if emit:
        return (outs[0].reshape(bsz * seq, GROUP), outs[1].reshape(bsz * seq, GROUP), tuple(outs[2:]))
    return None, None, tuple(outs)


def _hgrn_constants(chunk):
    nlev = int(np.log2(chunk))
    assert 2 ** nlev == chunk
    t = np.arange(chunk)
    mats = [(t[None, :] <= t[:, None])]
    masks = []
    for lev in range(nlev):
        n = chunk >> (lev + 1)
        b0 = t - t % (2 * n)
        upper = (t % (2 * n)) >= n
        m_up = (t[None, :] >= (b0 + n)[:, None]) & (t[None, :] <= t[:, None])
        m_lo = (t[None, :] > t[:, None]) & (t[None, :] <= (b0 + n - 1)[:, None])
        mats.append(np.where(upper[:, None], m_up, m_lo))
        same = (t[:, None] // (2 * n)) == (t[None, :] // (2 * n))
        masks.append(same & upper[:, None] & ~upper[None, :])
    masks.append(t[:, None] == t[None, :])
    mstack = np.concatenate(mats, axis=0).astype(np.float32)
    lmask = np.stack([np.kron(np.eye(2), m.astype(np.float32)) for m in masks])
    nr = mstack.shape[0] // chunk
    mstack_rev = mstack.reshape(nr, chunk, chunk)[:, ::-1, ::-1].reshape(-1, chunk)
    lmask_rev = lmask.reshape(nlev + 1, 2, chunk, 2, chunk)[:, :, ::-1, :, ::-1].reshape(lmask.shape)
    return (jnp.asarray(np.stack([mstack, mstack_rev]), BF16),
            jnp.asarray(np.stack([lmask, lmask_rev]), F32), nlev)


def _stack_pair(x, pp):
    return jnp.concatenate([x[:, (2 * pp) * LANES:(2 * pp + 1) * LANES],
                            x[:, (2 * pp + 1) * LANES:(2 * pp + 2) * LANES]], axis=0)


def _hgrn_kernel(*refs, chunk, nlev, emit, zero_lb):
    (qf_ref, vf_ref, ff_ref, qr_ref, vr_ref, fr_ref, lb_ref, ms_ref, lm_ref, s0_ref) = refs[:10]
    if emit:
        of_ref, or_ref = refs[10:12]
        rest = refs[12:]
    else:
        of_ref = or_ref = None
        rest = refs[10:]
    s_out, s_s = rest
    i = pl.program_id(1)

    @pl.when(i == 0)
    def _():
        s_s[...] = s0_ref[...]

    dirs = ((qf_ref, vf_ref, ff_ref, of_ref), (qr_ref, vr_ref, fr_ref, or_ref))
    chains = [(b, d) for b in range(vf_ref.shape[0]) for d in range(2)]
    old_s = {(b, d, h): s_s[b, d, h] for b, d in chains for h in range(4)}
    new_s = {}

    for b, d in chains:
        q_ref, v_ref, f_ref, o_ref = [r if r is None else r.at[b] for r in dirs[d]]
        fpre = f_ref[...]
        vb = v_ref[...].astype(BF16)
        e_f = jnp.exp(-jnp.abs(fpre))
        ls = jnp.minimum(fpre, 0.0) - _log1p_unit(e_f)
        sig_neg = jnp.where(fpre >= 0.0, e_f, 1.0) / (1.0 + e_f)
        if zero_lb:
            logf, k = ls, sig_neg
        else:
            a = lb_ref[d, 0:1, :]
            bb = lb_ref[d, 1:2, :] + ls
            logf = jnp.maximum(a, bb) + _log1p_unit(jnp.exp(-jnp.abs(a - bb)))
            k = lb_ref[d, 2:3, :] * sig_neg
        lf2 = logf * LOG2E
        hi = lf2.astype(BF16)
        lo = (lf2 - hi.astype(F32)).astype(BF16)
        gtot = jnp.sum(lf2, axis=0, keepdims=True)
        if emit:
            qpre = q_ref[...]
            q = qpre * _sigmoid(qpre)
            qb = q.astype(BF16)
            kb = k.astype(BF16)
        ms = ms_ref[d] if emit else ms_ref[d, 0:chunk]
        ex = _dot(ms, hi) + _dot(ms, lo)
        ktil = (k * jnp.exp2(gtot - ex[0:chunk])).astype(BF16)
        if emit:
            qe = (q * jnp.exp2(ex[0:chunk])).astype(BF16)
            lev_ops = []
            for lev in range(nlev):
                e_l = jnp.exp2(ex[(1 + lev) * chunk:(2 + lev) * chunk].astype(BF16))
                lev_ops.append((qb * e_l, kb * e_l))
        if emit:
            att = [lm_ref[d, nlev] * _dot_nt(_stack_pair(qb, pp), _stack_pair(kb, pp)) for pp in range(2)]
            for lev, (q_l, k_l) in enumerate(lev_ops):
                for pp in range(2):
                    att[pp] = att[pp] + lm_ref[d, lev] * _dot_nt(_stack_pair(q_l, pp), _stack_pair(k_l, pp))
            o_pair = [_dot(att[pp].astype(BF16), _stack_pair(vb, pp)) for pp in range(2)]
        for h in range(4):
            cs = slice(h * LANES, (h + 1) * LANES)
            s_t = old_s[b, d, h]
            if emit:
                pp, e = divmod(h, 2)
                o_ref[:, cs] = o_pair[pp][e * chunk:(e + 1) * chunk] + _dot_nt(qe[:, cs], s_t.astype(BF16))
            new_s[b, d, h] = s_t * jnp.exp2(gtot[:, cs]) + _dot_tn(vb[:, cs], ktil[:, cs])

    for key, val in new_s.items():
        s_s[key] = val

    @pl.when(i == pl.num_programs(1) - 1)
    def _():
        s_out[...] = s_s[...]


def _hgrn_scan(p, lbc, consts, state, bsz, seq, chunk, emit, zero_lb):
    mstack, lmask, nlev = consts
    nc = seq // chunk
    bps = bsz
    fwd = lambda g, i: i
    rev = lambda g, i: nc - 1 - i
    p3 = p.reshape(bsz, seq, NP)

    def chunk_specs(idx, d):
        return [pl.BlockSpec((bps, chunk, GROUP), lambda g, i: (g, idx(g, i), C_HGQ // GROUP)),
                pl.BlockSpec((bps, chunk, GROUP), lambda g, i: (g, idx(g, i), C_HGI // GROUP)),
                pl.BlockSpec((bps, chunk, GROUP), lambda g, i: (g, idx(g, i), C_HGF // GROUP + d))]

    st_spec = pl.BlockSpec((bps, 2, 4, LANES, LANES), lambda g, i: (g, 0, 0, 0, 0))
    st_shape = jax.ShapeDtypeStruct((bsz, 2, 4, LANES, LANES), F32)
    o_specs = [pl.BlockSpec((bps, chunk, GROUP), lambda g, i: (g, fwd(g, i), 0)),
               pl.BlockSpec((bps, chunk, GROUP), lambda g, i: (g, rev(g, i), 0))] if emit else []
    o_shapes = [jax.ShapeDtypeStruct((bsz, seq, GROUP), F32)] * 2 if emit else []
    full = lambda a: pl.BlockSpec(a.shape, lambda g, i: (0,) * a.ndim)
    outs = pl.pallas_call(
        functools.partial(_hgrn_kernel, chunk=chunk, nlev=nlev, emit=emit, zero_lb=zero_lb),
        grid=(bsz // bps, nc),
        in_specs=chunk_specs(fwd, 0) + chunk_specs(rev, 1) + [full(lbc), full(mstack), full(lmask), st_spec],
        out_specs=o_specs + [st_spec],
        out_shape=o_shapes + [st_shape],
        scratch_shapes=[pltpu.VMEM((bps, 2, 4, LANES, LANES), F32)],
        compiler_params=_cparams(("parallel", "arbitrary")),
    )(p3, p3, p3, p3, p3, p3, lbc, mstack, lmask, state)
    if emit:
        return outs[0].reshape(bsz * seq, GROUP), outs[1].reshape(bsz * seq, GROUP), outs[2]
    return None, None, outs[0]


def _outproj_kernel(x_ref, ya_ref, yb_ref, hf_ref, hr_ref, mo_ref, of_ref, or_ref, hg_ref,
                    w_ref, g_ref, o_ref):
    acc = _dot(ya_ref[...], w_ref[0]) + _dot(yb_ref[...], w_ref[1])
    yc = _sigmoid(mo_ref[...]) * (hf_ref[...] + hr_ref[...])
    acc = acc + _dot(yc.astype(BF16), w_ref[2])
    o = of_ref[...] + or_ref[...]
    gate = _sigmoid(hg_ref[...])
    parts = []
    for h in range(4):
        cs = slice(h * LANES, (h + 1) * LANES)
        oh = o[:, cs]
        parts.append(oh * lax.rsqrt(jnp.mean(oh * oh, axis=-1, keepdims=True) + EPS) * gate[:, cs])
    yd = jnp.concatenate(parts, axis=1)
    acc = acc + _dot(yd.astype(BF16), w_ref[3])
    o_ref[...] = x_ref[...] + g_ref[...] * acc


def _out_proj(x, ya, yb, hf, hr, of, orv, p, w_all, layer, gate, rows_per_mod):
    t, d = x.shape
    tm = _row_tile(rows_per_mod, 512)
    per = rows_per_mod // tm
    grp = lambda cb=0: pl.BlockSpec((tm, GROUP), lambda i: (i, cb))
    return pl.pallas_call(
        _outproj_kernel,
        grid=(t // tm,),
        in_specs=[pl.BlockSpec((tm, d), lambda i: (i, 0)), grp(), grp(), grp(), grp(),
                  grp(C_MLO // GROUP), grp(), grp(), grp(C_HGG // GROUP),
                  pl.BlockSpec((None, 4, GROUP, d), lambda i: (layer, 0, 0, 0)),
                  pl.BlockSpec((None, 1, d), lambda i: (i // per, 0, 0))],
        out_specs=pl.BlockSpec((tm, d), lambda i: (i, 0)),
        out_shape=jax.ShapeDtypeStruct((t, d), F32),
        compiler_params=_cparams(("parallel",)),
    )(x, ya, yb, hf, hr, p, of, orv, p, w_all, gate)


def _ffn_kernel(x_ref, xp_ref, xn_ref, g_ref, sh_ref, sc_ref, gate_ref, wa_ref, wu_ref, wc_ref, wd_ref,
                o_ref, xs_ref, acc_ref, *, per, inner):
    i = pl.program_id(0)
    j = pl.program_id(1)
    tm = x_ref.shape[0]

    @pl.when(j == 0)
    def _():
        nm = lambda x: _norm_mod(x, g_ref[...], sh_ref[...], sc_ref[...])
        keep_p = jnp.where(i % per == 0, 0.0, 1.0)
        keep_n = jnp.where(i % per == per - 1, 0.0, 1.0)
        _norm_mod_rows(x_ref, xs_ref, tm, g_ref[...], sh_ref[...], sc_ref[...])
        xs_ref[tm:tm + HALO, :] = (nm(xn_ref[...]) * keep_n).astype(BF16)
        xs_ref[tm + HALO:, :] = (nm(xp_ref[...]) * keep_p).astype(BF16)
        acc_ref[...] = jnp.zeros_like(acc_ref)

    a = _dot(xs_ref[...], wa_ref[...])
    rows = a.shape[0]
    prv = pltpu.roll(a, 1, axis=0)[0:tm]
    nxt = pltpu.roll(a, rows - 1, axis=0)[0:tm]
    if inner:
        pos = lax.broadcasted_iota(jnp.int32, (tm, 1), 0) % inner
        prv = jnp.where(pos == 0, 0.0, prv)
        nxt = jnp.where(pos == inner - 1, 0.0, nxt)
    conv = prv * wc_ref[0:1, :] + a[0:tm] * wc_ref[1:2, :] + nxt * wc_ref[2:3, :]
    u = _dot(xs_ref[0:tm, :], wu_ref[...])
    act = (conv * _sigmoid(conv) * u).astype(BF16)
    acc_ref[...] += _dot(act, wd_ref[...])

    @pl.when(j == pl.num_programs(1) - 1)
    def _():
        o_ref[...] = x_ref[...] + gate_ref[...] * acc_ref[...]


def _ffn(x, gain, shift, scale, gate, w_up, w_conv, w_down, layer, seq):
    t, d = x.shape
    f = w_down.shape[1]
    tm = 512 if t % 512 == 0 else t
    assert seq % tm == 0 or (tm % seq == 0 and shift.shape[0] == 1)
    tf = 512 if f % 512 == 0 else f
    per = max(seq // tm, 1)
    inner = seq if seq < tm else 0
    nf = f // tf
    step = tm // HALO
    nblk8 = t // HALO
    shared = shift.shape[0] == 1
    mod = pl.BlockSpec((None, 1, d), lambda i, j: (0 if shared else i // per, 0, 0))
    return pl.pallas_call(
        functools.partial(_ffn_kernel, per=per, inner=inner),
        grid=(t // tm, nf),
        in_specs=[pl.BlockSpec((tm, d), lambda i, j: (i, 0)),
                  pl.BlockSpec((HALO, d), lambda i, j: (jnp.maximum(i * step - 1, 0), 0)),
                  pl.BlockSpec((HALO, d), lambda i, j: (jnp.minimum((i + 1) * step, nblk8 - 1), 0)),
                  pl.BlockSpec((1, d), lambda i, j: (0, 0)), mod, mod, mod,
                  pl.BlockSpec((None, d, tf), lambda i, j: (layer, 0, j)),
                  pl.BlockSpec((None, d, tf), lambda i, j: (layer, 0, nf + j)),
                  pl.BlockSpec((None, 3, tf), lambda i, j: (layer, 0, j)),
                  pl.BlockSpec((None, tf, d), lambda i, j: (layer, j, 0))],
        out_specs=pl.BlockSpec((tm, d), lambda i, j: (i, 0)),
        out_shape=jax.ShapeDtypeStruct((t, d), F32),
        scratch_shapes=[pltpu.VMEM((tm + 2 * HALO, d), BF16), pltpu.VMEM((tm, d), F32)],
        compiler_params=_cparams(("parallel", "arbitrary")),
    )(x, x, x, gain, shift, scale, gate, w_up, w_up, w_conv, w_down)


def _perm_w_in(w):
    n, d = w.shape[:2]
    swq = w[..., 1536:2048].reshape(n, d, 2, 4, ATT_HD).transpose(0, 1, 3, 2, 4).reshape(n, d, GROUP)
    pad = jnp.zeros((n, d, LANES - 8), w.dtype)
    cols = [w[..., 0:1536], swq, w[..., 2304:3840], w[..., 3856:6416], w[..., 2048:2304],
            w[..., 3840:3848], pad, w[..., 3848:3856], pad]
    out = jnp.concatenate(cols, axis=-1).astype(BF16)
    assert out.shape[-1] == NP
    return out


def _perm_w_out(w):
    n, _, d = w.shape
    w4 = w.reshape(n, 4, GROUP, d)
    swo = w4[:, 1].reshape(n, 2, 4, ATT_HD, d).transpose(0, 2, 1, 3, 4).reshape(n, GROUP, d)
    return jnp.stack([w4[:, 0], swo, w4[:, 2], w4[:, 3]], axis=1).astype(BF16)


def _rope_tables(seq):
    t = jnp.arange(seq)
    half = ATT_HD // 2
    inv = ROPE_BASE ** (-jnp.arange(0, half, 2, dtype=F32) / half)
    ang_r = (t // GRID_W).astype(F32)[:, None] * inv
    ang_c = (t % GRID_W).astype(F32)[:, None] * inv
    cr, sr, cc, sc = jnp.cos(ang_r), jnp.sin(ang_r), jnp.cos(ang_c), jnp.sin(ang_c)
    cos = jnp.concatenate([cr, cr, cc, cc] * 2, axis=-1)
    sin = jnp.concatenate([-sr, sr, -sc, sc] * 2, axis=-1)
    return cos, sin


def kernel(x, c, ctx, c_ctx, w_mod, b_mod, norm_mix, norm_ffn, w_in, w_out, na_qk_gain, na_rpb,
           sw_qk_gain, sw_sink, ml_conv, ml_gate_bias, hg_lb, ffn_up, ffn_conv, ffn_down):
    bsz, seq, d = x.shape
    nctx = ctx.shape[1]
    depth = w_mod.shape[0]
    ml_chunk_ctx, ml_chunk = math.gcd(ML_CHUNK, nctx), math.gcd(ML_CHUNK, seq)
    chunk = math.gcd(HG_CHUNK, math.gcd(seq, nctx))
    assert bsz + 1 <= 8

    c_all = jnp.concatenate([c, c_ctx[None], jnp.zeros((8 - bsz - 1, d), F32)], axis=0)
    mod = _modulation(c_all, w_mod, b_mod).reshape(depth, 8, 6, d)
    cos, sin = _rope_tables(seq)
    bd = jnp.asarray(np.kron(np.eye(LANES // ATT_HD), np.ones((ATT_HD, ATT_HD))), BF16)
    lb_cum = jnp.cumsum(jax.nn.softmax(hg_lb.astype(F32), axis=0), axis=0)
    lower = lb_cum - lb_cum[:1]
    consts = _hgrn_constants(chunk)
    w_in_p = _perm_w_in(w_in)
    w_out_p = _perm_w_out(w_out)
    w_up = ffn_up.astype(BF16)
    w_dn = ffn_down.astype(BF16)
    w_cv = ffn_conv.astype(F32)

    h = x.reshape(bsz * seq, d)
    hc = ctx.reshape(bsz * nctx, d)
    for l in range(depth):
        emit_ctx = l < depth - 1
        lat = lambda k: mod[l, :bsz, k].reshape(bsz, 1, d)
        cx = lambda k: mod[l, bsz, k].reshape(1, 1, d)
        gain = norm_mix[l].reshape(1, d)
        p_lat = _in_proj(h, gain, lat(0), lat(1), w_in_p, l, seq)
        p_ctx = _in_proj(hc, gain, cx(0), cx(1), w_in_p, l, bsz * nctx)

        gna = jnp.tile(na_qk_gain[l].astype(F32), (1, GROUP // ATT_HD))
        gsw = jnp.tile(sw_qk_gain[l].astype(F32), (1, GROUP // ATT_HD))
        conv_w = ml_conv[l].astype(F32)
        naq, nak, nav, swq, swk, swv, qk_lat = _prep(p_lat, gna, gsw, bd, cos, sin, conv_w, seq, True)
        cnaq, cnak, cnav, cswq, cswk, cswv, qk_ctx = _prep(p_ctx, gna, gsw, bd, cos, sin, conv_w, nctx, False)
        sink = sw_sink[l].astype(F32)
        ya = _na_attention(naq, nak, nav, cnak, cnav, _na_bias_table(na_rpb[l], seq // GRID_W), bsz, seq, nctx)
        yb = _sw_attention(swq, swk, swv, cswk, cswv, sink, bsz, seq, nctx)

        gb = ml_gate_bias[l].astype(F32).reshape(2, 8)
        brow = jnp.zeros((8, LANES), F32).at[0:2, 0:8].set(gb)
        ml0 = (jnp.zeros((bsz, 2, 2, LANES, 2 * LANES), F32), jnp.zeros((bsz, 2, 8, LANES), F32))
        hcf, hcr, ml_state = _mlstm_scan(qk_ctx, p_ctx, brow, ml0, bsz, nctx, ml_chunk_ctx, emit_ctx)
        hf, hr, _ = _mlstm_scan(qk_lat, p_lat, brow, ml_state, bsz, seq, ml_chunk, True)

        lb = lower[l]
        lbc = jnp.stack([jnp.maximum(jnp.log(lb), NEG), jnp.log1p(-lb), 1.0 - lb], axis=1)
        lbc = jnp.concatenate([lbc, jnp.zeros((2, 5, GROUP), F32)], axis=1)
        hg0 = jnp.zeros((bsz, 2, 4, LANES, LANES), F32)
        ocf, ocr, hg_state = _hgrn_scan(p_ctx, lbc, consts, hg0, bsz, nctx, chunk, emit_ctx, l == 0)
        of, orv, _ = _hgrn_scan(p_lat, lbc, consts, hg_state, bsz, seq, chunk, True, l == 0)

        gain2 = norm_ffn[l].reshape(1, d)
        h = _out_proj(h, ya, yb, hf, hr, of, orv, p_lat, w_out_p, l, lat(2), seq)
        h = _ffn(h, gain2, lat(3), lat(4), lat(5), w_up, w_cv, w_dn, l, seq)
        if emit_ctx:
            yac, ybc = _ctx_attention(cnaq, cnak, cnav, cswq, cswk, cswv, sink, bsz, nctx)
            hc = _out_proj(hc, yac, ybc, hcf, hcr, ocf, ocr, p_ctx, w_out_p, l, cx(2), bsz * nctx)
            hc = _ffn(hc, gain2, cx(3), cx(4), cx(5), w_up, w_cv, w_dn, l, nctx)
    return h.reshape(bsz, seq, d)
```

```python
import functools
import math

import numpy as np
import jax
import jax.numpy as jnp
from jax import lax
from jax.experimental import pallas as pl
from jax.experimental.pallas import tpu as pltpu

F32 = jnp.float32
BF16 = jnp.bfloat16

GRID_W = 64
ATT_HD = 64
NA_ROWS = 8
NA_COLS = 16
SW_BLOCK = 128
ROPE_BASE = 10000.0
EPS = 1e-6
NEG = -1e30
LOG2E = 1.4426950408889634
GROUP = 512
LANES = 128
HALO = 8
VMEM_LIMIT = 56 * 1024 * 1024

C_NAQ, C_NAK, C_NAV, C_SWQ = 0, 512, 1024, 1536
C_MLQK, C_MLV, C_MLO = 2048, 2560, 3072
C_HGQ, C_HGI, C_HGF, C_HGG = 3584, 4096, 4608, 5632
C_SWK, C_SWV = 6144, 6272
C_MLI, C_MLF = 6400, 6528
NP = 6656
ML_CHUNK = 256
ML_STEP_ROWS = 1024
HG_CHUNK = 64
SW_PERM = (0, 4, 1, 5, 2, 6, 3, 7)


def _cparams(sem):
    return pltpu.CompilerParams(dimension_semantics=sem, vmem_limit_bytes=VMEM_LIMIT)


def _sigmoid(x):
    return 1.0 / (1.0 + jnp.exp(-x))


def _log1p_unit(e):
    return jnp.log(1.0 + e)


def _log_sigmoid(x):
    return jnp.minimum(x, 0.0) - _log1p_unit(jnp.exp(-jnp.abs(x)))


def _split3(x):
    hi = x.astype(BF16)
    r = x - hi.astype(F32)
    mid = r.astype(BF16)
    lo = (r - mid.astype(F32)).astype(BF16)
    return hi, mid, lo


def _dot(a, b):
    return jnp.dot(a, b, preferred_element_type=F32)


def _dot_nt(a, b):
    return lax.dot_general(a, b, (((1,), (1,)), ((), ())), preferred_element_type=F32)


def _dot_tn(a, b):
    return lax.dot_general(a, b, (((0,), (0,)), ((), ())), preferred_element_type=F32)


def _exact_left(m01, x):
    hi, mid, lo = _split3(x)
    return _dot(m01, hi) + _dot(m01, mid) + _dot(m01, lo)


def _mod_kernel(c_ref, w_ref, b_ref, o_ref):
    c = c_ref[...]
    a = c * _sigmoid(c)
    a_hi = a.astype(BF16)
    a_lo = (a - a_hi.astype(F32)).astype(BF16)
    w = w_ref[...]
    w_hi = w.astype(BF16)
    w_lo = (w - w_hi.astype(F32)).astype(BF16)
    o_ref[...] = _dot(a_hi, w_hi) + _dot(a_hi, w_lo) + _dot(a_lo, w_hi) + b_ref[...]


def _modulation(c_all, w_mod, b_mod):
    depth, d, n = w_mod.shape
    tn = 1024 if n % 1024 == 0 else n
    return pl.pallas_call(
        _mod_kernel,
        grid=(depth, n // tn),
        in_specs=[pl.BlockSpec((8, d), lambda l, j: (0, 0)),
                  pl.BlockSpec((None, d, tn), lambda l, j: (l, 0, j)),
                  pl.BlockSpec((None, 1, tn), lambda l, j: (l, 0, j))],
        out_specs=pl.BlockSpec((None, 8, tn), lambda l, j: (l, 0, j)),
        out_shape=jax.ShapeDtypeStruct((depth, 8, n), F32),
        compiler_params=_cparams(("parallel", "parallel")),
    )(c_all, w_mod, b_mod.reshape(depth, 1, n))


def _norm_mod(x, gain, shift, scale):
    y = x * lax.rsqrt(jnp.mean(x * x, axis=-1, keepdims=True) + EPS) * gain
    return y * (1.0 + scale) + shift


NORM_ROWS = 16


def _norm_mod_rows(src_ref, dst_ref, nrows, gain, shift, scale):
    amp = gain * (1.0 + scale)

    def body(c, carry):
        rows = pl.ds(pl.multiple_of(c * NORM_ROWS, NORM_ROWS), NORM_ROWS)
        x = src_ref[rows, :]
        y = x * lax.rsqrt(jnp.mean(x * x, axis=-1, keepdims=True) + EPS) * amp + shift
        dst_ref[rows, :] = y.astype(BF16)
        return carry

    assert nrows % NORM_ROWS == 0
    lax.fori_loop(0, nrows // NORM_ROWS, body, 0, unroll=4)


def _inproj_kernel(x_ref, g_ref, sh_ref, sc_ref, w_ref, o_ref, xn_ref):
    @pl.when(pl.program_id(1) == 0)
    def _():
        _norm_mod_rows(x_ref, xn_ref, x_ref.shape[0], g_ref[...], sh_ref[...], sc_ref[...])

    o_ref[...] = _dot(xn_ref[...], w_ref[...])


def _row_tile(rows_per_mod, pref):
    tm = min(pref, rows_per_mod)
    assert rows_per_mod % tm == 0
    return tm


def _in_proj(x, gain, shift, scale, w_all, layer, rows_per_mod):
    t, d = x.shape
    n = w_all.shape[2]
    tm = _row_tile(rows_per_mod, 512)
    tn = next(c for c in (1664, 512, n) if n % c == 0)
    per = rows_per_mod // tm
    return pl.pallas_call(
        _inproj_kernel,
        grid=(t // tm, n // tn),
        in_specs=[pl.BlockSpec((tm, d), lambda i, j: (i, 0)),
                  pl.BlockSpec((1, d), lambda i, j: (0, 0)),
                  pl.BlockSpec((None, 1, d), lambda i, j: (i // per, 0, 0)),
                  pl.BlockSpec((None, 1, d), lambda i, j: (i // per, 0, 0)),
                  pl.BlockSpec((None, d, tn), lambda i, j: (layer, 0, j))],
        out_specs=pl.BlockSpec((tm, tn), lambda i, j: (i, j)),
        out_shape=jax.ShapeDtypeStruct((t, n), F32),
        scratch_shapes=[pltpu.VMEM((tm, d), BF16)],
        compiler_params=_cparams(("parallel", "arbitrary")),
    )(x, gain, shift, scale, w_all)


def _head_rms(x, gain_row, bd):
    parts = []
    for g in range(x.shape[1] // LANES):
        xg = x[:, g * LANES:(g + 1) * LANES]
        ss = _dot((xg * xg).astype(BF16), bd)
        parts.append(xg * lax.rsqrt(ss * (1.0 / ATT_HD) + EPS))
    return (parts[0] if len(parts) == 1 else jnp.concatenate(parts, axis=1)) * gain_row


def _rope(x, cos, sin, first):
    w = x.shape[-1]
    nxt = pltpu.roll(x, w - 16, axis=1)
    prv = pltpu.roll(x, 16, axis=1)
    return x * cos + jnp.where(first, nxt, prv) * sin


def _prep_kernel(p_ref, pkv_ref, mlp_ref, mln_ref, gna_ref, gsw_ref, bd_ref, cos_ref, sin_ref, wconv_ref,
                 naq_ref, nak_ref, nav_ref, swq_ref, swk_ref, swv_ref, qk_ref, *, rope, per):
    i = pl.program_id(0)
    x = p_ref[:, C_MLQK:C_MLQK + GROUP]
    tm = x.shape[0]
    row = lax.broadcasted_iota(jnp.int32, (tm, 1), 0)
    keep_p = jnp.where(i % per == 0, 0.0, 1.0)
    keep_n = jnp.where(i % per == per - 1, 0.0, 1.0)
    prv = jnp.where(row == 0, mlp_ref[HALO - 1:HALO, :] * keep_p, pltpu.roll(x, 1, axis=0))
    nxt = jnp.where(row == tm - 1, mln_ref[0:1, :] * keep_n, pltpu.roll(x, tm - 1, axis=0))
    y = prv * wconv_ref[0:1, :] + x * wconv_ref[1:2, :] + nxt * wconv_ref[2:3, :]
    y = y * _sigmoid(y)
    qlane = lax.broadcasted_iota(jnp.int32, (1, GROUP), 1) < GROUP // 2
    qk_ref[...] = (y * jnp.where(qlane, ATT_HD ** -0.5, 1.0)).astype(BF16)

    bd = bd_ref[...]
    scale = ATT_HD ** -0.5
    naq_ref[...] = (_head_rms(p_ref[:, C_NAQ:C_NAQ + GROUP], gna_ref[0:1, :], bd) * scale).astype(BF16)
    nak_ref[...] = _head_rms(p_ref[:, C_NAK:C_NAK + GROUP], gna_ref[1:2, :], bd).astype(BF16)
    nav_ref[...] = p_ref[:, C_NAV:C_NAV + GROUP].astype(BF16)
    q = _head_rms(p_ref[:, C_SWQ:C_SWQ + GROUP], gsw_ref[0:1, :], bd)
    k = _head_rms(pkv_ref[:, 0:LANES], gsw_ref[1:2, 0:LANES], bd)
    if rope:
        cos = cos_ref[...]
        sin = sin_ref[...]
        lane = lax.broadcasted_iota(jnp.int32, (1, GROUP), 1)
        first = (lane % 32) < 16
        q = _rope(q, jnp.concatenate([cos] * 4, axis=1), jnp.concatenate([sin] * 4, axis=1), first)
        k = _rope(k, cos, sin, first[:, 0:LANES])
    swq_ref[...] = (q * scale).astype(BF16)
    swk_ref[...] = k.astype(BF16)
    swv_ref[...] = pkv_ref[:, LANES:2 * LANES].astype(BF16)


def _prep(p, gna, gsw, bd, cos, sin, conv_w, seq, rope):
    t = p.shape[0]
    tm = _row_tile(seq, 512)
    per = seq // tm
    step = tm // HALO
    nblk8 = t // HALO
    mlcol = C_MLQK // GROUP
    out = lambda w: jax.ShapeDtypeStruct((t, w), BF16)
    ospec = lambda w: pl.BlockSpec((tm, w), lambda i: (i, 0))
    return pl.pallas_call(
        functools.partial(_prep_kernel, rope=rope, per=per),
        grid=(t // tm,),
        in_specs=[pl.BlockSpec((tm, C_MLV), lambda i: (i, 0)),
                  pl.BlockSpec((tm, 2 * LANES), lambda i: (i, C_SWK // (2 * LANES))),
                  pl.BlockSpec((HALO, GROUP), lambda i: (jnp.maximum(i * step - 1, 0), mlcol)),
                  pl.BlockSpec((HALO, GROUP), lambda i: (jnp.minimum((i + 1) * step, nblk8 - 1), mlcol)),
                  pl.BlockSpec((2, GROUP), lambda i: (0, 0)),
                  pl.BlockSpec((2, GROUP), lambda i: (0, 0)),
                  pl.BlockSpec((LANES, LANES), lambda i: (0, 0)),
                  pl.BlockSpec((tm, LANES), lambda i: (i % per, 0)),
                  pl.BlockSpec((tm, LANES), lambda i: (i % per, 0)),
                  pl.BlockSpec((3, GROUP), lambda i: (0, 0))],
        out_specs=[ospec(GROUP), ospec(GROUP), ospec(GROUP), ospec(GROUP), ospec(LANES), ospec(LANES),
                   ospec(GROUP)],
        out_shape=[out(GROUP), out(GROUP), out(GROUP), out(GROUP), out(LANES), out(LANES), out(GROUP)],
        compiler_params=_cparams(("parallel",)),
    )(p, p, p, p, gna, gsw, bd, cos, sin, conv_w)


def _pair_rows(q_pair):
    lane = lax.broadcasted_iota(jnp.int32, q_pair.shape, 1)
    zero = jnp.zeros_like(q_pair)
    return jnp.concatenate([jnp.where(lane < ATT_HD, q_pair, zero),
                            jnp.where(lane >= ATT_HD, q_pair, zero)], axis=0)


def _pair_merge(o2, inv_l):
    m = o2.shape[0] // 2
    lane = lax.broadcasted_iota(jnp.int32, (m, LANES), 1)
    return jnp.where(lane < ATT_HD, o2[0:m] * inv_l[0:m], o2[m:] * inv_l[m:])


NA_RPS = 4


def _softmax_pv(parts, sink=None):
    m = functools.reduce(jnp.maximum, [jnp.max(s, axis=-1, keepdims=True) for s, _ in parts])
    if sink is not None:
        m = jnp.maximum(m, sink)
    acc = None
    for s, v in parts:
        e = jnp.exp((s - m).astype(BF16))
        o = _dot(e, jnp.concatenate([v, jnp.ones_like(v)], axis=1))
        acc = o if acc is None else acc + o
    l = acc[:, LANES:LANES + 1]
    if sink is not None:
        l = l + jnp.exp(sink - m)
    return acc[:, 0:LANES], l


NA_WIN = NA_ROWS + NA_RPS


def _na_window(step, rows):
    return jnp.clip(step * NA_RPS - NA_ROWS // 2, 0, rows - NA_WIN)


def _na_kernel(q_ref, k_ref, v_ref, kc_ref, vc_ref, b_ref, o_ref, *, rows):
    start = pl.multiple_of(_na_window(pl.program_id(1), rows) * GRID_W, GRID_W)
    nkey = NA_WIN * GRID_W
    for p in range(GROUP // LANES):
        cs = slice(p * LANES, (p + 1) * LANES)
        qs = jnp.concatenate([_pair_rows(q_ref[rr * GRID_W:(rr + 1) * GRID_W, cs]) for rr in range(NA_RPS)],
                             axis=0)
        kw = k_ref[pl.ds(start, nkey), cs]
        vw = v_ref[pl.ds(start, nkey), cs]
        s_nb = _dot_nt(qs, kw) + b_ref[p]
        s_cx = _dot_nt(qs, kc_ref[:, cs])
        o2, l = _softmax_pv([(s_nb, vw), (s_cx, vc_ref[:, cs])])
        inv_l = 1.0 / l
        for rr in range(NA_RPS):
            sl = slice(rr * 2 * GRID_W, (rr + 1) * 2 * GRID_W)
            o_ref[rr * GRID_W:(rr + 1) * GRID_W, cs] = _pair_merge(o2[sl], inv_l[sl]).astype(BF16)


def _na_attention(q, k, v, kc, vc, bias, bsz, seq, ctx):
    rows = seq // GRID_W
    assert rows >= NA_WIN and rows % NA_RPS == 0
    steps = rows // NA_RPS
    k3, v3 = k.reshape(bsz, seq, GROUP), v.reshape(bsz, seq, GROUP)
    kc3, vc3 = kc.reshape(bsz, ctx, GROUP), vc.reshape(bsz, ctx, GROUP)
    qspec = pl.BlockSpec((NA_RPS * GRID_W, GROUP), lambda b, r: (b * steps + r, 0))
    variant = lambda b, r: ((r > 0).astype(jnp.int32) + (r == steps - 1).astype(jnp.int32), 0, 0, 0)
    return pl.pallas_call(
        functools.partial(_na_kernel, rows=rows),
        grid=(bsz, steps),
        in_specs=[qspec,
                  pl.BlockSpec((None, seq, GROUP), lambda b, r: (b, 0, 0)),
                  pl.BlockSpec((None, seq, GROUP), lambda b, r: (b, 0, 0)),
                  pl.BlockSpec((None, ctx, GROUP), lambda b, r: (b, 0, 0)),
                  pl.BlockSpec((None, ctx, GROUP), lambda b, r: (b, 0, 0)),
                  pl.BlockSpec((None,) + bias.shape[1:], variant)],
        out_specs=qspec,
        out_shape=jax.ShapeDtypeStruct((bsz * seq, GROUP), BF16),
        compiler_params=_cparams(("parallel", "arbitrary")),
    )(q, k3, v3, kc3, vc3, bias)


def _na_bias_table(rpb, rows):
    nh = rpb.shape[0]
    steps = rows // NA_RPS
    c = np.arange(GRID_W)
    kc = np.arange(GRID_W)
    cstart = np.clip(c - NA_COLS // 2, 0, GRID_W - NA_COLS)
    ok = (kc[None, :] >= cstart[:, None]) & (kc[None, :] < cstart[:, None] + NA_COLS)
    dc = np.clip(kc[None, :] - c[:, None] + NA_COLS - 1, 0, 2 * NA_COLS - 2)
    onehot = (dc.reshape(1, -1) == np.arange(2 * NA_COLS - 1)[:, None]).astype(np.float32)
    t = jnp.dot(rpb.astype(F32).reshape(-1, 2 * NA_COLS - 1), onehot, precision=lax.Precision.HIGHEST)
    t = jnp.where(ok[None, None], t.reshape(nh, -1, GRID_W, GRID_W), NEG)

    def step_layout(s):
        ws = int(np.clip(s * NA_RPS - NA_ROWS // 2, 0, rows - NA_WIN))
        r = s * NA_RPS + np.arange(NA_RPS)[:, None]
        rs = np.clip(r - NA_ROWS // 2, 0, rows - NA_ROWS)
        krow = ws + np.arange(NA_WIN)[None, :]
        return (krow - r + NA_ROWS - 1), (krow >= rs) & (krow < rs + NA_ROWS)

    layouts = [step_layout(s) for s in range(steps)]
    variants = [layouts[0], layouts[1], layouts[-1]]
    for s, (idx, valid) in enumerate(layouts):
        want = variants[(s > 0) + (s == steps - 1)]
        assert (valid == want[1]).all() and (idx[valid] == want[0][want[1]]).all()

    ndr = 2 * NA_ROWS - 1
    width = (ndr + 3) * GRID_W
    t2 = jnp.transpose(t, (0, 2, 1, 3)).reshape(nh, GRID_W, ndr * GRID_W)
    t2 = jnp.pad(t2, ((0, 0), (0, 0), (GRID_W, width - (ndr + 1) * GRID_W)), constant_values=NEG)
    t2s = jnp.pad(t2[:, :, GRID_W:], ((0, 0), (0, 0), (0, GRID_W)), constant_values=NEG)

    def build(ta_ref, tb_ref, o_ref):
        lane = lax.broadcasted_iota(jnp.int32, (GRID_W, LANES), 1)
        neg = jnp.full((GRID_W, LANES), NEG, F32)
        for v, (idx, valid) in enumerate(variants):
            for rr in range(NA_RPS):
                for e in range(2):
                    row0 = (rr * 2 + e) * GRID_W
                    for jj in range(NA_WIN // 2):
                        j = 2 * jj
                        ok0, ok1 = bool(valid[rr, j]), bool(valid[rr, j + 1])
                        if not (ok0 or ok1):
                            piece = neg
                        else:
                            blk = int(idx[rr, j] if ok0 else idx[rr, j + 1] - 1) + 1
                            src = ta_ref if blk % 2 == 0 else tb_ref
                            off = (blk - blk % 2) * GRID_W
                            piece = src[e, :, off:off + LANES]
                            if not ok0:
                                piece = jnp.where(lane >= GRID_W, piece, NEG)
                            if not ok1:
                                piece = jnp.where(lane < GRID_W, piece, NEG)
                        o_ref[v, row0:row0 + GRID_W, j * GRID_W:(j + 2) * GRID_W] = piece

    npair = nh // 2
    tspec = pl.BlockSpec((2, GRID_W, width), lambda p: (p, 0, 0))
    return pl.pallas_call(
        build,
        grid=(npair,),
        in_specs=[tspec, tspec],
        out_specs=pl.BlockSpec((3, None, NA_RPS * 2 * GRID_W, NA_WIN * GRID_W), lambda p: (0, p, 0, 0)),
        out_shape=jax.ShapeDtypeStruct((3, npair, NA_RPS * 2 * GRID_W, NA_WIN * GRID_W), F32),
        compiler_params=_cparams(("parallel",)),
    )(t2, t2s)


SW_BPS = 2


def _sw_kernel(sink_ref, q_ref, k_ref, v_ref, kc_ref, vc_ref, o_ref, *, seq):
    nq = SW_BPS * SW_BLOCK
    nwin = (SW_BPS + 2) * SW_BLOCK
    q0 = pl.program_id(1) * nq
    start = pl.multiple_of(jnp.clip(q0 - SW_BLOCK, 0, seq - nwin), SW_BLOCK)
    kw = k_ref[pl.ds(start, nwin), :]
    vw = v_ref[pl.ds(start, nwin), :]
    kc = kc_ref[...]
    vc = vc_ref[...]
    row = lax.broadcasted_iota(jnp.int32, (2 * nq, nwin), 0)
    col = lax.broadcasted_iota(jnp.int32, (2 * nq, nwin), 1)
    ok = jnp.abs((start + col) - (q0 + row % nq)) <= SW_BLOCK
    half = lax.broadcasted_iota(jnp.int32, (2 * nq, 1), 0) < nq
    for j in range(GROUP // LANES):
        cs = slice(j * LANES, (j + 1) * LANES)
        qs = _pair_rows(q_ref[:, cs])
        s_bd = jnp.where(ok, _dot_nt(qs, kw), NEG)
        s_cx = _dot_nt(qs, kc)
        sink = jnp.where(half, sink_ref[j], sink_ref[j + 4])
        o2, l = _softmax_pv([(s_bd, vw), (s_cx, vc)], sink)
        o_ref[:, cs] = _pair_merge(o2, 1.0 / l).astype(BF16)


def _sw_attention(q, k, v, kc, vc, sink, bsz, seq, ctx):
    assert seq % (SW_BPS * SW_BLOCK) == 0 and seq >= (SW_BPS + 2) * SW_BLOCK
    nb = seq // (SW_BPS * SW_BLOCK)
    k3, v3 = k.reshape(bsz, seq, LANES), v.reshape(bsz, seq, LANES)
    kc3, vc3 = kc.reshape(bsz, ctx, LANES), vc.reshape(bsz, ctx, LANES)
    return pl.pallas_call(
        functools.partial(_sw_kernel, seq=seq),
        grid=(bsz, nb),
        in_specs=[pl.BlockSpec(memory_space=pltpu.SMEM),
                  pl.BlockSpec((SW_BPS * SW_BLOCK, GROUP), lambda b, n: (b * nb + n, 0)),
                  pl.BlockSpec((None, seq, LANES), lambda b, n: (b, 0, 0)),
                  pl.BlockSpec((None, seq, LANES), lambda b, n: (b, 0, 0)),
                  pl.BlockSpec((None, ctx, LANES), lambda b, n: (b, 0, 0)),
                  pl.BlockSpec((None, ctx, LANES), lambda b, n: (b, 0, 0))],
        out_specs=pl.BlockSpec((SW_BPS * SW_BLOCK, GROUP), lambda b, n: (b * nb + n, 0)),
        out_shape=jax.ShapeDtypeStruct((bsz * seq, GROUP), BF16),
        compiler_params=_cparams(("parallel", "arbitrary")),
    )(sink, q, k3, v3, kc3, vc3)


def _ctx_attn_kernel(sink_ref, qa_ref, ka_ref, va_ref, qb_ref, kb_ref, vb_ref, oa_ref, ob_ref):
    for p in range(GROUP // LANES):
        cs = slice(p * LANES, (p + 1) * LANES)
        qs = _pair_rows(qa_ref[:, cs])
        o2, l = _softmax_pv([(_dot_nt(qs, ka_ref[:, cs]), va_ref[:, cs])])
        oa_ref[:, cs] = _pair_merge(o2, 1.0 / l).astype(BF16)
    ctx = qb_ref.shape[0]
    half = lax.broadcasted_iota(jnp.int32, (2 * ctx, 1), 0) < ctx
    for j in range(GROUP // LANES):
        cs = slice(j * LANES, (j + 1) * LANES)
        qs = _pair_rows(qb_ref[:, cs])
        sink = jnp.where(half, sink_ref[j], sink_ref[j + 4])
        o2, l = _softmax_pv([(_dot_nt(qs, kb_ref[...]), vb_ref[...])], sink)
        ob_ref[:, cs] = _pair_merge(o2, 1.0 / l).astype(BF16)


def _ctx_attention(qa, ka, va, qb, kb, vb, sink, bsz, ctx):
    big = pl.BlockSpec((ctx, GROUP), lambda b: (b, 0))
    small = pl.BlockSpec((ctx, LANES), lambda b: (b, 0))
    return pl.pallas_call(
        _ctx_attn_kernel,
        grid=(bsz,),
        in_specs=[pl.BlockSpec(memory_space=pltpu.SMEM), big, big, big, big, small, small],
        out_specs=[big, big],
        out_shape=[jax.ShapeDtypeStruct((bsz * ctx, GROUP), BF16)] * 2,
        compiler_params=_cparams(("parallel",)),
    )(sink, qa, ka, va, qb, kb, vb)


def _mlstm_kernel(*refs, chunk, emit):
    (qkf_ref, vf_ref, gif_ref, gff_ref, qkr_ref, vr_ref, gir_ref, gfr_ref,
     brow_ref, c0_ref, m0_ref) = refs[:11]
    if emit:
        hf_ref, hr_ref = refs[11:13]
        rest = refs[13:]
    else:
        hf_ref = hr_ref = None
        rest = refs[11:]
    c_out, m_out, c_s, m_s = rest
    i = pl.program_id(1)

    @pl.when(i == 0)
    def _():
        c_s[...] = c0_ref[...]
        m_s[...] = m0_ref[...]

    row = lax.broadcasted_iota(jnp.int32, (chunk, chunk), 0)
    col = lax.broadcasted_iota(jnp.int32, (chunk, chunk), 1)
    trow = lax.broadcasted_iota(jnp.int32, (chunk, 1), 0)
    lane = lax.broadcasted_iota(jnp.int32, (chunk, LANES), 1)
    srow = lax.broadcasted_iota(jnp.int32, (LANES, 1), 0)
    dirs = ((qkf_ref, vf_ref, gif_ref, gff_ref, hf_ref), (qkr_ref, vr_ref, gir_ref, gfr_ref, hr_ref))
    chains = [(b, d) for b in range(qkf_ref.shape[0]) for d in range(2)]
    old_c = {(b, d, pp): c_s[b, d, pp] for b, d in chains for pp in range(2)}
    old_m = {(b, d): m_s[b, d, 0:1, :] for b, d in chains}
    new_c, new_m = {}, {}
    for b, d in chains:
        qk_ref, v_ref, gi_ref, gf_ref, h_ref = [r if r is None else r.at[b] for r in dirs[d]]
        causal = (col <= row) if d == 0 else (col >= row)
        ipre = gi_ref[...] + brow_ref[0:1, :]
        fl = _log_sigmoid(gf_ref[...] + brow_ref[1:2, :])
        fcum = _exact_left(jnp.where(causal, 1.0, 0.0).astype(BF16), fl)
        ftot = jnp.sum(fl, axis=0, keepdims=True)
        m_prev = old_m[b, d]
        w_log = ftot - fcum + ipre
        m_new = jnp.maximum(ftot + m_prev, jnp.max(w_log, axis=0, keepdims=True))
        cd = jnp.exp(ftot + m_prev - m_new)
        w = jnp.exp(w_log - m_new)
        if emit:
            b_col = ipre - fcum
            run = b_col
            sh = 1
            while sh < chunk:
                if d == 0:
                    run = jnp.maximum(run, jnp.where(trow >= sh, pltpu.roll(run, sh, axis=0), NEG))
                else:
                    run = jnp.maximum(run, jnp.where(trow < chunk - sh, pltpu.roll(run, chunk - sh, axis=0), NEG))
                sh *= 2
            log_inter = fcum + m_prev
            m_row = jnp.maximum(log_inter, fcum + run)
            inter_w = jnp.exp(log_inter - m_row)
            fcm = fcum - m_row
            floor = jnp.exp(-m_row)
            b_rows = b_col.T[0:8, :]
        for pp in range(2):
            qp = qk_ref[:, pp * LANES:(pp + 1) * LANES]
            kp = qk_ref[:, GROUP // 2 + pp * LANES:GROUP // 2 + (pp + 1) * LANES]
            cn = old_c[b, d, pp]
            inc = None
            for e in range(2):
                h = 2 * pp + e
                c = 4 * d + h
                sel = (lane < ATT_HD) if e == 0 else (lane >= ATT_HD)
                vh = v_ref[:, h * LANES:(h + 1) * LANES]
                if emit:
                    qm = jnp.where(sel, qp, jnp.zeros_like(qp))
                    d_mat = jnp.exp(jnp.where(causal, fcm[:, c:c + 1] + b_rows[c:c + 1, :], NEG))
                    s_mat = (_dot_nt(qm, kp) * d_mat).astype(BF16)
                    vb = vh.astype(BF16)
                    acc = (inter_w[:, c:c + 1] * _dot(qm, cn.astype(BF16))
                           + _dot(s_mat, jnp.concatenate([vb, jnp.ones_like(vb)], axis=1)))
                    den = jnp.maximum(jnp.abs(acc[:, LANES:]), floor[:, c:c + 1])
                    h_ref[:, h * LANES:(h + 1) * LANES] = acc[:, 0:LANES] / den
                km = jnp.where(sel, kp, jnp.zeros_like(kp))
                wc = jnp.broadcast_to(w[:, c:c + 1], (chunk, LANES))
                upd = _dot_tn(km, jnp.concatenate([wc * vh, wc], axis=1).astype(BF16))
                inc = upd if inc is None else inc + upd
            c_lo, c_hi = 4 * d + 2 * pp, 4 * d + 2 * pp + 1
            new_c[b, d, pp] = (jnp.where(srow < ATT_HD, cd[:, c_lo:c_lo + 1], cd[:, c_hi:c_hi + 1]) * cn
                               + inc)
        new_m[b, d] = m_new
    for key, val in new_c.items():
        c_s[key] = val
    for (b, d), val in new_m.items():
        m_s[b, d, 0:1, :] = val

    @pl.when(i == pl.num_programs(1) - 1)
    def _():
        c_out[...] = c_s[...]
        m_out[...] = m_s[...]


def _mlstm_scan(qk, p, brow, state, bsz, seq, chunk, emit):
    nc = seq // chunk
    bps = bsz if bsz * chunk <= ML_STEP_ROWS else max(1, ML_STEP_ROWS // chunk)
    assert bsz % bps == 0
    fwd = lambda g, i: i
    rev = lambda g, i: nc - 1 - i
    qk3 = qk.reshape(bsz, seq, GROUP)
    p3 = p.reshape(bsz, seq, NP)

    def chunk_specs(idx):
        return [pl.BlockSpec((bps, chunk, GROUP), lambda g, i: (g, idx(g, i), 0)),
                pl.BlockSpec((bps, chunk, GROUP), lambda g, i: (g, idx(g, i), C_MLV // GROUP)),
                pl.BlockSpec((bps, chunk, LANES), lambda g, i: (g, idx(g, i), C_MLI // LANES)),
                pl.BlockSpec((bps, chunk, LANES), lambda g, i: (g, idx(g, i), C_MLF // LANES))]

    st_dims = [(2, 2, LANES, 2 * LANES), (2, 8, LANES)]
    st_specs = [pl.BlockSpec((bps,) + s, lambda g, i, n=len(s): (g,) + (0,) * n) for s in st_dims]
    st_shapes = [jax.ShapeDtypeStruct((bsz,) + s, F32) for s in st_dims]
    h_specs = [pl.BlockSpec((bps, chunk, GROUP), lambda g, i: (g, fwd(g, i), 0)),
               pl.BlockSpec((bps, chunk, GROUP), lambda g, i: (g, rev(g, i), 0))] if emit else []
    h_shapes = [jax.ShapeDtypeStruct((bsz, seq, GROUP), F32)] * 2 if emit else []
    outs = pl.pallas_call(
        functools.partial(_mlstm_kernel, chunk=chunk, emit=emit),
        grid=(bsz // bps, nc),
        in_specs=chunk_specs(fwd) + chunk_specs(rev)
        + [pl.BlockSpec((8, LANES), lambda g, i: (0, 0))] + st_specs,
        out_specs=h_specs + st_specs,
        out_shape=h_shapes + st_shapes,
        scratch_shapes=[pltpu.VMEM((bps,) + s, F32) for s in st_dims],
        compiler_params=_cparams(("parallel", "arbitrary")),
    )(qk3, p3, p3, p3, qk3, p3, p3, p3, brow, *state)
    if emit:
        return (outs[0].reshape(bsz * seq, GROUP), outs[1].reshape(bsz * seq, GROUP), tuple(outs[2:]))
    return None, None, tuple(outs)


def _hgrn_constants(chunk):
    nlev = int(np.log2(chunk))
    assert 2 ** nlev == chunk
    t = np.arange(chunk)
    mats = [(t[None, :] <= t[:, None])]
    masks = []
    for lev in range(nlev):
        n = chunk >> (lev + 1)
        b0 = t - t % (2 * n)
        upper = (t % (2 * n)) >= n
        m_up = (t[None, :] >= (b0 + n)[:, None]) & (t[None, :] <= t[:, None])
        m_lo = (t[None, :] > t[:, None]) & (t[None, :] <= (b0 + n - 1)[:, None])
        mats.append(np.where(upper[:, None], m_up, m_lo))
        same = (t[:, None] // (2 * n)) == (t[None, :] // (2 * n))
        masks.append(same & upper[:, None] & ~upper[None, :])
    masks.append(t[:, None] == t[None, :])
    mstack = np.concatenate(mats, axis=0).astype(np.float32)
    lmask = np.stack([np.kron(np.eye(2), m.astype(np.float32)) for m in masks])
    nr = mstack.shape[0] // chunk
    mstack_rev = mstack.reshape(nr, chunk, chunk)[:, ::-1, ::-1].reshape(-1, chunk)
    lmask_rev = lmask.reshape(nlev + 1, 2, chunk, 2, chunk)[:, :, ::-1, :, ::-1].reshape(lmask.shape)
    return (jnp.asarray(np.stack([mstack, mstack_rev]), BF16),
            jnp.asarray(np.stack([lmask, lmask_rev]), F32), nlev)


def _stack_pair(x, pp):
    return jnp.concatenate([x[:, (2 * pp) * LANES:(2 * pp + 1) * LANES],
                            x[:, (2 * pp + 1) * LANES:(2 * pp + 2) * LANES]], axis=0)


def _hgrn_kernel(*refs, chunk, nlev, emit, zero_lb):
    (qf_ref, vf_ref, ff_ref, qr_ref, vr_ref, fr_ref, lb_ref, ms_ref, lm_ref, s0_ref) = refs[:10]
    if emit:
        of_ref, or_ref = refs[10:12]
        rest = refs[12:]
    else:
        of_ref = or_ref = None
        rest = refs[10:]
    s_out, s_s = rest
    i = pl.program_id(1)

    @pl.when(i == 0)
    def _():
        s_s[...] = s0_ref[...]

    dirs = ((qf_ref, vf_ref, ff_ref, of_ref), (qr_ref, vr_ref, fr_ref, or_ref))
    chains = [(b, d) for b in range(vf_ref.shape[0]) for d in range(2)]
    old_s = {(b, d, h): s_s[b, d, h] for b, d in chains for h in range(4)}
    new_s = {}

    for b, d in chains:
        q_ref, v_ref, f_ref, o_ref = [r if r is None else r.at[b] for r in dirs[d]]
        fpre = f_ref[...]
        vb = v_ref[...].astype(BF16)
        e_f = jnp.exp(-jnp.abs(fpre))
        ls = jnp.minimum(fpre, 0.0) - _log1p_unit(e_f)
        sig_neg = jnp.where(fpre >= 0.0, e_f, 1.0) / (1.0 + e_f)
        if zero_lb:
            logf, k = ls, sig_neg
        else:
            a = lb_ref[d, 0:1, :]
            bb = lb_ref[d, 1:2, :] + ls
            logf = jnp.maximum(a, bb) + _log1p_unit(jnp.exp(-jnp.abs(a - bb)))
            k = lb_ref[d, 2:3, :] * sig_neg
        lf2 = logf * LOG2E
        hi = lf2.astype(BF16)
        lo = (lf2 - hi.astype(F32)).astype(BF16)
        gtot = jnp.sum(lf2, axis=0, keepdims=True)
        if emit:
            qpre = q_ref[...]
            q = qpre * _sigmoid(qpre)
            qb = q.astype(BF16)
            kb = k.astype(BF16)
        ms = ms_ref[d] if emit else ms_ref[d, 0:chunk]
        ex = _dot(ms, hi) + _dot(ms, lo)
        ktil = (k * jnp.exp2(gtot - ex[0:chunk])).astype(BF16)
        if emit:
            qe = (q * jnp.exp2(ex[0:chunk])).astype(BF16)
            lev_ops = []
            for lev in range(nlev):
                e_l = jnp.exp2(ex[(1 + lev) * chunk:(2 + lev) * chunk].astype(BF16))
                lev_ops.append((qb * e_l, kb * e_l))
        if emit:
            att = [lm_ref[d, nlev] * _dot_nt(_stack_pair(qb, pp), _stack_pair(kb, pp)) for pp in range(2)]
            for lev, (q_l, k_l) in enumerate(lev_ops):
                for pp in range(2):
                    att[pp] = att[pp] + lm_ref[d, lev] * _dot_nt(_stack_pair(q_l, pp), _stack_pair(k_l, pp))
            o_pair = [_dot(att[pp].astype(BF16), _stack_pair(vb, pp)) for pp in range(2)]
        for h in range(4):
            cs = slice(h * LANES, (h + 1) * LANES)
            s_t = old_s[b, d, h]
            if emit:
                pp, e = divmod(h, 2)
                o_ref[:, cs] = o_pair[pp][e * chunk:(e + 1) * chunk] + _dot_nt(qe[:, cs], s_t.astype(BF16))
            new_s[b, d, h] = s_t * jnp.exp2(gtot[:, cs]) + _dot_tn(vb[:, cs], ktil[:, cs])

    for key, val in new_s.items():
        s_s[key] = val

    @pl.when(i == pl.num_programs(1) - 1)
    def _():
        s_out[...] = s_s[...]


def _hgrn_scan(p, lbc, consts, state, bsz, seq, chunk, emit, zero_lb):
    mstack, lmask, nlev = consts
    nc = seq // chunk
    bps = bsz
    fwd = lambda g, i: i
    rev = lambda g, i: nc - 1 - i
    p3 = p.reshape(bsz, seq, NP)

    def chunk_specs(idx, d):
        return [pl.BlockSpec((bps, chunk, GROUP), lambda g, i: (g, idx(g, i), C_HGQ // GROUP)),
                pl.BlockSpec((bps, chunk, GROUP), lambda g, i: (g, idx(g, i), C_HGI // GROUP)),
                pl.BlockSpec((bps, chunk, GROUP), lambda g, i: (g, idx(g, i), C_HGF // GROUP + d))]

    st_spec = pl.BlockSpec((bps, 2, 4, LANES, LANES), lambda g, i: (g, 0, 0, 0, 0))
    st_shape = jax.ShapeDtypeStruct((bsz, 2, 4, LANES, LANES), F32)
    o_specs = [pl.BlockSpec((bps, chunk, GROUP), lambda g, i: (g, fwd(g, i), 0)),
               pl.BlockSpec((bps, chunk, GROUP), lambda g, i: (g, rev(g, i), 0))] if emit else []
    o_shapes = [jax.ShapeDtypeStruct((bsz, seq, GROUP), F32)] * 2 if emit else []
    full = lambda a: pl.BlockSpec(a.shape, lambda g, i: (0,) * a.ndim)
    outs = pl.pallas_call(
        functools.partial(_hgrn_kernel, chunk=chunk, nlev=nlev, emit=emit, zero_lb=zero_lb),
        grid=(bsz // bps, nc),
        in_specs=chunk_specs(fwd, 0) + chunk_specs(rev, 1) + [full(lbc), full(mstack), full(lmask), st_spec],
        out_specs=o_specs + [st_spec],
        out_shape=o_shapes + [st_shape],
        scratch_shapes=[pltpu.VMEM((bps, 2, 4, LANES, LANES), F32)],
        compiler_params=_cparams(("parallel", "arbitrary")),
    )(p3, p3, p3, p3, p3, p3, lbc, mstack, lmask, state)
    if emit:
        return outs[0].reshape(bsz * seq, GROUP), outs[1].reshape(bsz * seq, GROUP), outs[2]
    return None, None, outs[0]


def _outproj_kernel(x_ref, ya_ref, yb_ref, hf_ref, hr_ref, mo_ref, of_ref, or_ref, hg_ref,
                    w_ref, g_ref, o_ref):
    acc = _dot(ya_ref[...], w_ref[0]) + _dot(yb_ref[...], w_ref[1])
    yc = _sigmoid(mo_ref[...]) * (hf_ref[...] + hr_ref[...])
    acc = acc + _dot(yc.astype(BF16), w_ref[2])
    o = of_ref[...] + or_ref[...]
    gate = _sigmoid(hg_ref[...])
    parts = []
    for h in range(4):
        cs = slice(h * LANES, (h + 1) * LANES)
        oh = o[:, cs]
        parts.append(oh * lax.rsqrt(jnp.mean(oh * oh, axis=-1, keepdims=True) + EPS) * gate[:, cs])
    yd = jnp.concatenate(parts, axis=1)
    acc = acc + _dot(yd.astype(BF16), w_ref[3])
    o_ref[...] = x_ref[...] + g_ref[...] * acc


def _out_proj(x, ya, yb, hf, hr, of, orv, p, w_all, layer, gate, rows_per_mod):
    t, d = x.shape
    tm = _row_tile(rows_per_mod, 512)
    per = rows_per_mod // tm
    grp = lambda cb=0: pl.BlockSpec((tm, GROUP), lambda i: (i, cb))
    return pl.pallas_call(
        _outproj_kernel,
        grid=(t // tm,),
        in_specs=[pl.BlockSpec((tm, d), lambda i: (i, 0)), grp(), grp(), grp(), grp(),
                  grp(C_MLO // GROUP), grp(), grp(), grp(C_HGG // GROUP),
                  pl.BlockSpec((None, 4, GROUP, d), lambda i: (layer, 0, 0, 0)),
                  pl.BlockSpec((None, 1, d), lambda i: (i // per, 0, 0))],
        out_specs=pl.BlockSpec((tm, d), lambda i: (i, 0)),
        out_shape=jax.ShapeDtypeStruct((t, d), F32),
        compiler_params=_cparams(("parallel",)),
    )(x, ya, yb, hf, hr, p, of, orv, p, w_all, gate)


def _ffn_kernel(x_ref, xp_ref, xn_ref, g_ref, sh_ref, sc_ref, gate_ref, wa_ref, wu_ref, wc_ref, wd_ref,
                o_ref, xs_ref, acc_ref, act_ref, *, per, inner):
    i = pl.program_id(0)
    j = pl.program_id(1)
    nf = pl.num_programs(1) - 1
    tm = x_ref.shape[0]

    def up(slot):
        a = _dot(xs_ref[...], wa_ref[...])
        rows = a.shape[0]
        prv = pltpu.roll(a, 1, axis=0)[0:tm]
        nxt = pltpu.roll(a, rows - 1, axis=0)[0:tm]
        if inner:
            pos = lax.broadcasted_iota(jnp.int32, (tm, 1), 0) % inner
            prv = jnp.where(pos == 0, 0.0, prv)
            nxt = jnp.where(pos == inner - 1, 0.0, nxt)
        conv = prv * wc_ref[0:1, :] + a[0:tm] * wc_ref[1:2, :] + nxt * wc_ref[2:3, :]
        u = _dot(xs_ref[0:tm, :], wu_ref[...])
        act_ref[slot] = (conv * _sigmoid(conv) * u).astype(BF16)

    def down(slot):
        acc_ref[...] += _dot(act_ref[slot], wd_ref[...])

    @pl.when(j == 0)
    def _():
        nm = lambda x: _norm_mod(x, g_ref[...], sh_ref[...], sc_ref[...])
        keep_p = jnp.where(i % per == 0, 0.0, 1.0)
        keep_n = jnp.where(i % per == per - 1, 0.0, 1.0)
        _norm_mod_rows(x_ref, xs_ref, tm, g_ref[...], sh_ref[...], sc_ref[...])
        xs_ref[tm:tm + HALO, :] = (nm(xn_ref[...]) * keep_n).astype(BF16)
        xs_ref[tm + HALO:, :] = (nm(xp_ref[...]) * keep_p).astype(BF16)
        acc_ref[...] = jnp.zeros_like(acc_ref)
        up(0)

    @pl.when((j > 0) & (j < nf))
    def _():
        down((j - 1) % 2)
        up(j % 2)

    @pl.when(j == nf)
    def _():
        down((j - 1) % 2)
        o_ref[...] = x_ref[...] + gate_ref[...] * acc_ref[...]


def _ffn(x, gain, shift, scale, gate, w_up, w_conv, w_down, layer, seq):
    t, d = x.shape
    f = w_down.shape[1]
    tm = 512 if t % 512 == 0 else t
    assert seq % tm == 0 or (tm % seq == 0 and shift.shape[0] == 1)
    tf = 512 if f % 512 == 0 else f
    per = max(seq // tm, 1)
    inner = seq if seq < tm else 0
    nf = f // tf
    step = tm // HALO
    nblk8 = t // HALO
    shared = shift.shape[0] == 1
    mod = pl.BlockSpec((None, 1, d), lambda i, j: (0 if shared else i // per, 0, 0))
    return pl.pallas_call(
        functools.partial(_ffn_kernel, per=per, inner=inner),
        grid=(t // tm, nf + 1),
        in_specs=[pl.BlockSpec((tm, d), lambda i, j: (i, 0)),
                  pl.BlockSpec((HALO, d), lambda i, j: (jnp.maximum(i * step - 1, 0), 0)),
                  pl.BlockSpec((HALO, d), lambda i, j: (jnp.minimum((i + 1) * step, nblk8 - 1), 0)),
                  pl.BlockSpec((1, d), lambda i, j: (0, 0)), mod, mod, mod,
                  pl.BlockSpec((None, d, tf), lambda i, j: (layer, 0, jnp.minimum(j, nf - 1))),
                  pl.BlockSpec((None, d, tf), lambda i, j: (layer, 0, nf + jnp.minimum(j, nf - 1))),
                  pl.BlockSpec((None, 3, tf), lambda i, j: (layer, 0, jnp.minimum(j, nf - 1))),
                  pl.BlockSpec((None, tf, d), lambda i, j: (layer, jnp.maximum(j - 1, 0), 0))],
        out_specs=pl.BlockSpec((tm, d), lambda i, j: (i, 0)),
        out_shape=jax.ShapeDtypeStruct((t, d), F32),
        scratch_shapes=[pltpu.VMEM((tm + 2 * HALO, d), BF16), pltpu.VMEM((tm, d), F32),
                        pltpu.VMEM((2, tm, tf), BF16)],
        compiler_params=_cparams(("parallel", "arbitrary")),
    )(x, x, x, gain, shift, scale, gate, w_up, w_up, w_conv, w_down)


def _perm_w_in(w):
    n, d = w.shape[:2]
    swq = w[..., 1536:2048].reshape(n, d, 2, 4, ATT_HD).transpose(0, 1, 3, 2, 4).reshape(n, d, GROUP)
    pad = jnp.zeros((n, d, LANES - 8), w.dtype)
    cols = [w[..., 0:1536], swq, w[..., 2304:3840], w[..., 3856:6416], w[..., 2048:2304],
            w[..., 3840:3848], pad, w[..., 3848:3856], pad]
    out = jnp.concatenate(cols, axis=-1).astype(BF16)
    assert out.shape[-1] == NP
    return out


def _perm_w_out(w):
    n, _, d = w.shape
    w4 = w.reshape(n, 4, GROUP, d)
    swo = w4[:, 1].reshape(n, 2, 4, ATT_HD, d).transpose(0, 2, 1, 3, 4).reshape(n, GROUP, d)
    return jnp.stack([w4[:, 0], swo, w4[:, 2], w4[:, 3]], axis=1).astype(BF16)


def _rope_tables(seq):
    t = jnp.arange(seq)
    half = ATT_HD // 2
    inv = ROPE_BASE ** (-jnp.arange(0, half, 2, dtype=F32) / half)
    ang_r = (t // GRID_W).astype(F32)[:, None] * inv
    ang_c = (t % GRID_W).astype(F32)[:, None] * inv
    cr, sr, cc, sc = jnp.cos(ang_r), jnp.sin(ang_r), jnp.cos(ang_c), jnp.sin(ang_c)
    cos = jnp.concatenate([cr, cr, cc, cc] * 2, axis=-1)
    sin = jnp.concatenate([-sr, sr, -sc, sc] * 2, axis=-1)
    return cos, sin


def kernel(x, c, ctx, c_ctx, w_mod, b_mod, norm_mix, norm_ffn, w_in, w_out, na_qk_gain, na_rpb,
           sw_qk_gain, sw_sink, ml_conv, ml_gate_bias, hg_lb, ffn_up, ffn_conv, ffn_down):
    bsz, seq, d = x.shape
    nctx = ctx.shape[1]
    depth = w_mod.shape[0]
    ml_chunk_ctx, ml_chunk = math.gcd(ML_CHUNK, nctx), math.gcd(ML_CHUNK, seq)
    chunk = math.gcd(HG_CHUNK, math.gcd(seq, nctx))
    assert bsz + 1 <= 8

    c_all = jnp.concatenate([c, c_ctx[None], jnp.zeros((8 - bsz - 1, d), F32)], axis=0)
    mod = _modulation(c_all, w_mod, b_mod).reshape(depth, 8, 6, d)
    cos, sin = _rope_tables(seq)
    bd = jnp.asarray(np.kron(np.eye(LANES // ATT_HD), np.ones((ATT_HD, ATT_HD))), BF16)
    lb_cum = jnp.cumsum(jax.nn.softmax(hg_lb.astype(F32), axis=0), axis=0)
    lower = lb_cum - lb_cum[:1]
    consts = _hgrn_constants(chunk)
    w_in_p = _perm_w_in(w_in)
    w_out_p = _perm_w_out(w_out)
    w_up = ffn_up.astype(BF16)
    w_dn = ffn_down.astype(BF16)
    w_cv = ffn_conv.astype(F32)

    h = x.reshape(bsz * seq, d)
    hc = ctx.reshape(bsz * nctx, d)
    for l in range(depth):
        emit_ctx = l < depth - 1
        lat = lambda k: mod[l, :bsz, k].reshape(bsz, 1, d)
        cx = lambda k: mod[l, bsz, k].reshape(1, 1, d)
        gain = norm_mix[l].reshape(1, d)
        p_lat = _in_proj(h, gain, lat(0), lat(1), w_in_p, l, seq)
        p_ctx = _in_proj(hc, gain, cx(0), cx(1), w_in_p, l, bsz * nctx)

        gna = jnp.tile(na_qk_gain[l].astype(F32), (1, GROUP // ATT_HD))
        gsw = jnp.tile(sw_qk_gain[l].astype(F32), (1, GROUP // ATT_HD))
        conv_w = ml_conv[l].astype(F32)
        naq, nak, nav, swq, swk, swv, qk_lat = _prep(p_lat, gna, gsw, bd, cos, sin, conv_w, seq, True)
        cnaq, cnak, cnav, cswq, cswk, cswv, qk_ctx = _prep(p_ctx, gna, gsw, bd, cos, sin, conv_w, nctx, False)
        sink = sw_sink[l].astype(F32)
        ya = _na_attention(naq, nak, nav, cnak, cnav, _na_bias_table(na_rpb[l], seq // GRID_W), bsz, seq, nctx)
        yb = _sw_attention(swq, swk, swv, cswk, cswv, sink, bsz, seq, nctx)

        gb = ml_gate_bias[l].astype(F32).reshape(2, 8)
        brow = jnp.zeros((8, LANES), F32).at[0:2, 0:8].set(gb)
        ml0 = (jnp.zeros((bsz, 2, 2, LANES, 2 * LANES), F32), jnp.zeros((bsz, 2, 8, LANES), F32))
        hcf, hcr, ml_state = _mlstm_scan(qk_ctx, p_ctx, brow, ml0, bsz, nctx, ml_chunk_ctx, emit_ctx)
        hf, hr, _ = _mlstm_scan(qk_lat, p_lat, brow, ml_state, bsz, seq, ml_chunk, True)

        lb = lower[l]
        lbc = jnp.stack([jnp.maximum(jnp.log(lb), NEG), jnp.log1p(-lb), 1.0 - lb], axis=1)
        lbc = jnp.concatenate([lbc, jnp.zeros((2, 5, GROUP), F32)], axis=1)
        hg0 = jnp.zeros((bsz, 2, 4, LANES, LANES), F32)
        ocf, ocr, hg_state = _hgrn_scan(p_ctx, lbc, consts, hg0, bsz, nctx, chunk, emit_ctx, l == 0)
        of, orv, _ = _hgrn_scan(p_lat, lbc, consts, hg_state, bsz, seq, chunk, True, l == 0)

        gain2 = norm_ffn[l].reshape(1, d)
        h = _out_proj(h, ya, yb, hf, hr, of, orv, p_lat, w_out_p, l, lat(2), seq)
        h = _ffn(h, gain2, lat(3), lat(4), lat(5), w_up, w_cv, w_dn, l, seq)
        if emit_ctx:
            yac, ybc = _ctx_attention(cnaq, cnak, cnav, cswq, cswk, cswv, sink, bsz, nctx)
            hc = _out_proj(hc, yac, ybc, hcf, hcr, ocf, ocr, p_ctx, w_out_p, l, cx(2), bsz * nctx)
            hc = _ffn(hc, gain2, cx(3), cx(4), cx(5), w_up, w_cv, w_dn, l, nctx)
    return h.reshape(bsz, seq, d)
```

```python
import functools
import math

import numpy as np
import jax
import jax.numpy as jnp
from jax import lax
from jax.experimental import pallas as pl
from jax.experimental.pallas import tpu as pltpu

F32 = jnp.float32
BF16 = jnp.bfloat16

GRID_W = 64
ATT_HD = 64
NA_ROWS = 8
NA_COLS = 16
SW_BLOCK = 128
ROPE_BASE = 10000.0
EPS = 1e-6
NEG = -1e30
LOG2E = 1.4426950408889634
GROUP = 512
LANES = 128
HALO = 8
VMEM_LIMIT = 56 * 1024 * 1024

C_NAQ, C_NAK, C_NAV, C_SWQ = 0, 512, 1024, 1536
C_MLQK, C_MLV, C_MLO = 2048, 2560, 3072
C_HGQ, C_HGI, C_HGF, C_HGG = 3584, 4096, 4608, 5632
C_SWK, C_SWV = 6144, 6272
C_MLI, C_MLF = 6400, 6528
NP = 6656
ML_CHUNK = 256
ML_STEP_ROWS = 1024
HG_CHUNK = 64
SW_PERM = (0, 4, 1, 5, 2, 6, 3, 7)


def _cparams(sem):
    return pltpu.CompilerParams(dimension_semantics=sem, vmem_limit_bytes=VMEM_LIMIT)


def _sigmoid(x):
    return 1.0 / (1.0 + jnp.exp(-x))


def _log1p_unit(e):
    return jnp.log(1.0 + e)


def _log_sigmoid(x):
    return jnp.minimum(x, 0.0) - _log1p_unit(jnp.exp(-jnp.abs(x)))


def _split3(x):
    hi = x.astype(BF16)
    r = x - hi.astype(F32)
    mid = r.astype(BF16)
    lo = (r - mid.astype(F32)).astype(BF16)
    return hi, mid, lo


def _dot(a, b):
    return jnp.dot(a, b, preferred_element_type=F32)


def _dot_nt(a, b):
    return lax.dot_general(a, b, (((1,), (1,)), ((), ())), preferred_element_type=F32)


def _dot_tn(a, b):
    return lax.dot_general(a, b, (((0,), (0,)), ((), ())), preferred_element_type=F32)


def _exact_left(m01, x):
    hi, mid, lo = _split3(x)
    return _dot(m01, hi) + _dot(m01, mid) + _dot(m01, lo)


def _mod_kernel(c_ref, w_ref, b_ref, o_ref):
    c = c_ref[...]
    a = c * _sigmoid(c)
    a_hi = a.astype(BF16)
    a_lo = (a - a_hi.astype(F32)).astype(BF16)
    w = w_ref[...]
    w_hi = w.astype(BF16)
    w_lo = (w - w_hi.astype(F32)).astype(BF16)
    o_ref[...] = _dot(a_hi, w_hi) + _dot(a_hi, w_lo) + _dot(a_lo, w_hi) + b_ref[...]


def _modulation(c_all, w_mod, b_mod):
    depth, d, n = w_mod.shape
    tn = 1024 if n % 1024 == 0 else n
    return pl.pallas_call(
        _mod_kernel,
        grid=(depth, n // tn),
        in_specs=[pl.BlockSpec((8, d), lambda l, j: (0, 0)),
                  pl.BlockSpec((None, d, tn), lambda l, j: (l, 0, j)),
                  pl.BlockSpec((None, 1, tn), lambda l, j: (l, 0, j))],
        out_specs=pl.BlockSpec((None, 8, tn), lambda l, j: (l, 0, j)),
        out_shape=jax.ShapeDtypeStruct((depth, 8, n), F32),
        compiler_params=_cparams(("parallel", "parallel")),
    )(c_all, w_mod, b_mod.reshape(depth, 1, n))


def _norm_mod(x, gain, shift, scale):
    y = x * lax.rsqrt(jnp.mean(x * x, axis=-1, keepdims=True) + EPS) * gain
    return y * (1.0 + scale) + shift


NORM_ROWS = 16


def _norm_mod_rows(src_ref, dst_ref, nrows, gain, shift, scale):
    amp = gain * (1.0 + scale)

    def body(c, carry):
        rows = pl.ds(pl.multiple_of(c * NORM_ROWS, NORM_ROWS), NORM_ROWS)
        x = src_ref[rows, :]
        y = x * lax.rsqrt(jnp.mean(x * x, axis=-1, keepdims=True) + EPS) * amp + shift
        dst_ref[rows, :] = y.astype(BF16)
        return carry

    assert nrows % NORM_ROWS == 0
    lax.fori_loop(0, nrows // NORM_ROWS, body, 0, unroll=4)


def _inproj_kernel(x_ref, g_ref, sh_ref, sc_ref, w_ref, o_ref, xn_ref):
    @pl.when(pl.program_id(1) == 0)
    def _():
        _norm_mod_rows(x_ref, xn_ref, x_ref.shape[0], g_ref[...], sh_ref[...], sc_ref[...])

    o_ref[...] = _dot(xn_ref[...], w_ref[...])


def _row_tile(rows_per_mod, pref):
    tm = min(pref, rows_per_mod)
    assert rows_per_mod % tm == 0
    return tm


def _in_proj(x, gain, shift, scale, w_all, layer, rows_per_mod):
    t, d = x.shape
    n = w_all.shape[2]
    tm = _row_tile(rows_per_mod, 512)
    tn = next(c for c in (1664, 512, n) if n % c == 0)
    per = rows_per_mod // tm
    return pl.pallas_call(
        _inproj_kernel,
        grid=(t // tm, n // tn),
        in_specs=[pl.BlockSpec((tm, d), lambda i, j: (i, 0)),
                  pl.BlockSpec((1, d), lambda i, j: (0, 0)),
                  pl.BlockSpec((None, 1, d), lambda i, j: (i // per, 0, 0)),
                  pl.BlockSpec((None, 1, d), lambda i, j: (i // per, 0, 0)),
                  pl.BlockSpec((None, d, tn), lambda i, j: (layer, 0, j))],
        out_specs=pl.BlockSpec((tm, tn), lambda i, j: (i, j)),
        out_shape=jax.ShapeDtypeStruct((t, n), F32),
        scratch_shapes=[pltpu.VMEM((tm, d), BF16)],
        compiler_params=_cparams(("parallel", "arbitrary")),
    )(x, gain, shift, scale, w_all)


def _head_rms(x, gain_row, bd):
    parts = []
    for g in range(x.shape[1] // LANES):
        xg = x[:, g * LANES:(g + 1) * LANES]
        ss = _dot((xg * xg).astype(BF16), bd)
        parts.append(xg * lax.rsqrt(ss * (1.0 / ATT_HD) + EPS))
    return (parts[0] if len(parts) == 1 else jnp.concatenate(parts, axis=1)) * gain_row


def _rope(x, cos, sin, first):
    w = x.shape[-1]
    nxt = pltpu.roll(x, w - 16, axis=1)
    prv = pltpu.roll(x, 16, axis=1)
    return x * cos + jnp.where(first, nxt, prv) * sin


def _prep_kernel(p_ref, pkv_ref, mlp_ref, mln_ref, gna_ref, gsw_ref, bd_ref, cos_ref, sin_ref, wconv_ref,
                 naq_ref, nak_ref, nav_ref, swq_ref, swk_ref, swv_ref, qk_ref, *, rope, per):
    i = pl.program_id(0)
    x = p_ref[:, C_MLQK:C_MLQK + GROUP]
    tm = x.shape[0]
    row = lax.broadcasted_iota(jnp.int32, (tm, 1), 0)
    keep_p = jnp.where(i % per == 0, 0.0, 1.0)
    keep_n = jnp.where(i % per == per - 1, 0.0, 1.0)
    prv = jnp.where(row == 0, mlp_ref[HALO - 1:HALO, :] * keep_p, pltpu.roll(x, 1, axis=0))
    nxt = jnp.where(row == tm - 1, mln_ref[0:1, :] * keep_n, pltpu.roll(x, tm - 1, axis=0))
    y = prv * wconv_ref[0:1, :] + x * wconv_ref[1:2, :] + nxt * wconv_ref[2:3, :]
    y = y * _sigmoid(y)
    qlane = lax.broadcasted_iota(jnp.int32, (1, GROUP), 1) < GROUP // 2
    qk_ref[...] = (y * jnp.where(qlane, ATT_HD ** -0.5, 1.0)).astype(BF16)

    bd = bd_ref[...]
    scale = ATT_HD ** -0.5
    naq_ref[...] = (_head_rms(p_ref[:, C_NAQ:C_NAQ + GROUP], gna_ref[0:1, :], bd) * scale).astype(BF16)
    nak_ref[...] = _head_rms(p_ref[:, C_NAK:C_NAK + GROUP], gna_ref[1:2, :], bd).astype(BF16)
    nav_ref[...] = p_ref[:, C_NAV:C_NAV + GROUP].astype(BF16)
    q = _head_rms(p_ref[:, C_SWQ:C_SWQ + GROUP], gsw_ref[0:1, :], bd)
    k = _head_rms(pkv_ref[:, 0:LANES], gsw_ref[1:2, 0:LANES], bd)
    if rope:
        cos = cos_ref[...]
        sin = sin_ref[...]
        lane = lax.broadcasted_iota(jnp.int32, (1, GROUP), 1)
        first = (lane % 32) < 16
        q = _rope(q, jnp.concatenate([cos] * 4, axis=1), jnp.concatenate([sin] * 4, axis=1), first)
        k = _rope(k, cos, sin, first[:, 0:LANES])
    swq_ref[...] = (q * scale).astype(BF16)
    swk_ref[...] = k.astype(BF16)
    swv_ref[...] = pkv_ref[:, LANES:2 * LANES].astype(BF16)


def _prep(p, gna, gsw, bd, cos, sin, conv_w, seq, rope):
    t = p.shape[0]
    tm = _row_tile(seq, 512)
    per = seq // tm
    step = tm // HALO
    nblk8 = t // HALO
    mlcol = C_MLQK // GROUP
    out = lambda w: jax.ShapeDtypeStruct((t, w), BF16)
    ospec = lambda w: pl.BlockSpec((tm, w), lambda i: (i, 0))
    return pl.pallas_call(
        functools.partial(_prep_kernel, rope=rope, per=per),
        grid=(t // tm,),
        in_specs=[pl.BlockSpec((tm, C_MLV), lambda i: (i, 0)),
                  pl.BlockSpec((tm, 2 * LANES), lambda i: (i, C_SWK // (2 * LANES))),
                  pl.BlockSpec((HALO, GROUP), lambda i: (jnp.maximum(i * step - 1, 0), mlcol)),
                  pl.BlockSpec((HALO, GROUP), lambda i: (jnp.minimum((i + 1) * step, nblk8 - 1), mlcol)),
                  pl.BlockSpec((2, GROUP), lambda i: (0, 0)),
                  pl.BlockSpec((2, GROUP), lambda i: (0, 0)),
                  pl.BlockSpec((LANES, LANES), lambda i: (0, 0)),
                  pl.BlockSpec((tm, LANES), lambda i: (i % per, 0)),
                  pl.BlockSpec((tm, LANES), lambda i: (i % per, 0)),
                  pl.BlockSpec((3, GROUP), lambda i: (0, 0))],
        out_specs=[ospec(GROUP), ospec(GROUP), ospec(GROUP), ospec(GROUP), ospec(LANES), ospec(LANES),
                   ospec(GROUP)],
        out_shape=[out(GROUP), out(GROUP), out(GROUP), out(GROUP), out(LANES), out(LANES), out(GROUP)],
        compiler_params=_cparams(("parallel",)),
    )(p, p, p, p, gna, gsw, bd, cos, sin, conv_w)


def _pair_rows(q_pair):
    lane = lax.broadcasted_iota(jnp.int32, q_pair.shape, 1)
    zero = jnp.zeros_like(q_pair)
    return jnp.concatenate([jnp.where(lane < ATT_HD, q_pair, zero),
                            jnp.where(lane >= ATT_HD, q_pair, zero)], axis=0)


def _pair_merge(o2, inv_l):
    m = o2.shape[0] // 2
    lane = lax.broadcasted_iota(jnp.int32, (m, LANES), 1)
    return jnp.where(lane < ATT_HD, o2[0:m] * inv_l[0:m], o2[m:] * inv_l[m:])


NA_RPS = 4


def _softmax_pv(parts, sink=None):
    m = functools.reduce(jnp.maximum, [jnp.max(s, axis=-1, keepdims=True) for s, _ in parts])
    if sink is not None:
        m = jnp.maximum(m, sink)
    acc = None
    for s, v in parts:
        e = jnp.exp((s - m).astype(BF16))
        o = _dot(e, jnp.concatenate([v, jnp.ones_like(v)], axis=1))
        acc = o if acc is None else acc + o
    l = acc[:, LANES:LANES + 1]
    if sink is not None:
        l = l + jnp.exp(sink - m)
    return acc[:, 0:LANES], l


NA_WIN = NA_ROWS + NA_RPS


def _na_window(step, rows):
    return jnp.clip(step * NA_RPS - NA_ROWS // 2, 0, rows - NA_WIN)


def _na_kernel(q_ref, k_ref, v_ref, kc_ref, vc_ref, b_ref, o_ref, *, rows):
    start = pl.multiple_of(_na_window(pl.program_id(1), rows) * GRID_W, GRID_W)
    nkey = NA_WIN * GRID_W
    for p in range(GROUP // LANES):
        cs = slice(p * LANES, (p + 1) * LANES)
        qs = jnp.concatenate([_pair_rows(q_ref[rr * GRID_W:(rr + 1) * GRID_W, cs]) for rr in range(NA_RPS)],
                             axis=0)
        kw = k_ref[pl.ds(start, nkey), cs]
        vw = v_ref[pl.ds(start, nkey), cs]
        s_nb = _dot_nt(qs, kw) + b_ref[p]
        s_cx = _dot_nt(qs, kc_ref[:, cs])
        o2, l = _softmax_pv([(s_nb, vw), (s_cx, vc_ref[:, cs])])
        inv_l = 1.0 / l
        for rr in range(NA_RPS):
            sl = slice(rr * 2 * GRID_W, (rr + 1) * 2 * GRID_W)
            o_ref[rr * GRID_W:(rr + 1) * GRID_W, cs] = _pair_merge(o2[sl], inv_l[sl]).astype(BF16)


def _na_attention(q, k, v, kc, vc, bias, bsz, seq, ctx):
    rows = seq // GRID_W
    assert rows >= NA_WIN and rows % NA_RPS == 0
    steps = rows // NA_RPS
    k3, v3 = k.reshape(bsz, seq, GROUP), v.reshape(bsz, seq, GROUP)
    kc3, vc3 = kc.reshape(bsz, ctx, GROUP), vc.reshape(bsz, ctx, GROUP)
    qspec = pl.BlockSpec((NA_RPS * GRID_W, GROUP), lambda b, r: (b * steps + r, 0))
    variant = lambda b, r: ((r > 0).astype(jnp.int32) + (r == steps - 1).astype(jnp.int32), 0, 0, 0)
    return pl.pallas_call(
        functools.partial(_na_kernel, rows=rows),
        grid=(bsz, steps),
        in_specs=[qspec,
                  pl.BlockSpec((None, seq, GROUP), lambda b, r: (b, 0, 0)),
                  pl.BlockSpec((None, seq, GROUP), lambda b, r: (b, 0, 0)),
                  pl.BlockSpec((None, ctx, GROUP), lambda b, r: (b, 0, 0)),
                  pl.BlockSpec((None, ctx, GROUP), lambda b, r: (b, 0, 0)),
                  pl.BlockSpec((None,) + bias.shape[1:], variant)],
        out_specs=qspec,
        out_shape=jax.ShapeDtypeStruct((bsz * seq, GROUP), BF16),
        compiler_params=_cparams(("parallel", "arbitrary")),
    )(q, k3, v3, kc3, vc3, bias)


def _na_bias_table(rpb, rows):
    nh = rpb.shape[0]
    steps = rows // NA_RPS
    c = np.arange(GRID_W)
    kc = np.arange(GRID_W)
    cstart = np.clip(c - NA_COLS // 2, 0, GRID_W - NA_COLS)
    ok = (kc[None, :] >= cstart[:, None]) & (kc[None, :] < cstart[:, None] + NA_COLS)
    dc = np.clip(kc[None, :] - c[:, None] + NA_COLS - 1, 0, 2 * NA_COLS - 2)
    onehot = (dc.reshape(1, -1) == np.arange(2 * NA_COLS - 1)[:, None]).astype(np.float32)
    t = jnp.dot(rpb.astype(F32).reshape(-1, 2 * NA_COLS - 1), onehot, precision=lax.Precision.HIGHEST)
    t = jnp.where(ok[None, None], t.reshape(nh, -1, GRID_W, GRID_W), NEG)

    def step_layout(s):
        ws = int(np.clip(s * NA_RPS - NA_ROWS // 2, 0, rows - NA_WIN))
        r = s * NA_RPS + np.arange(NA_RPS)[:, None]
        rs = np.clip(r - NA_ROWS // 2, 0, rows - NA_ROWS)
        krow = ws + np.arange(NA_WIN)[None, :]
        return (krow - r + NA_ROWS - 1), (krow >= rs) & (krow < rs + NA_ROWS)

    layouts = [step_layout(s) for s in range(steps)]
    variants = [layouts[0], layouts[1], layouts[-1]]
    for s, (idx, valid) in enumerate(layouts):
        want = variants[(s > 0) + (s == steps - 1)]
        assert (valid == want[1]).all() and (idx[valid] == want[0][want[1]]).all()

    ndr = 2 * NA_ROWS - 1
    width = (ndr + 3) * GRID_W
    t2 = jnp.transpose(t, (0, 2, 1, 3)).reshape(nh, GRID_W, ndr * GRID_W)
    t2 = jnp.pad(t2, ((0, 0), (0, 0), (GRID_W, width - (ndr + 1) * GRID_W)), constant_values=NEG)
    t2s = jnp.pad(t2[:, :, GRID_W:], ((0, 0), (0, 0), (0, GRID_W)), constant_values=NEG)

    def build(ta_ref, tb_ref, o_ref):
        lane = lax.broadcasted_iota(jnp.int32, (GRID_W, LANES), 1)
        neg = jnp.full((GRID_W, LANES), NEG, F32)
        for v, (idx, valid) in enumerate(variants):
            for rr in range(NA_RPS):
                for e in range(2):
                    row0 = (rr * 2 + e) * GRID_W
                    for jj in range(NA_WIN // 2):
                        j = 2 * jj
                        ok0, ok1 = bool(valid[rr, j]), bool(valid[rr, j + 1])
                        if not (ok0 or ok1):
                            piece = neg
                        else:
                            blk = int(idx[rr, j] if ok0 else idx[rr, j + 1] - 1) + 1
                            src = ta_ref if blk % 2 == 0 else tb_ref
                            off = (blk - blk % 2) * GRID_W
                            piece = src[e, :, off:off + LANES]
                            if not ok0:
                                piece = jnp.where(lane >= GRID_W, piece, NEG)
                            if not ok1:
                                piece = jnp.where(lane < GRID_W, piece, NEG)
                        o_ref[v, row0:row0 + GRID_W, j * GRID_W:(j + 2) * GRID_W] = piece

    npair = nh // 2
    tspec = pl.BlockSpec((2, GRID_W, width), lambda p: (p, 0, 0))
    return pl.pallas_call(
        build,
        grid=(npair,),
        in_specs=[tspec, tspec],
        out_specs=pl.BlockSpec((3, None, NA_RPS * 2 * GRID_W, NA_WIN * GRID_W), lambda p: (0, p, 0, 0)),
        out_shape=jax.ShapeDtypeStruct((3, npair, NA_RPS * 2 * GRID_W, NA_WIN * GRID_W), F32),
        compiler_params=_cparams(("parallel",)),
    )(t2, t2s)


SW_BPS = 2


def _sw_kernel(sink_ref, q_ref, k_ref, v_ref, kc_ref, vc_ref, o_ref, *, seq):
    nq = SW_BPS * SW_BLOCK
    nwin = (SW_BPS + 2) * SW_BLOCK
    q0 = pl.program_id(1) * nq
    start = pl.multiple_of(jnp.clip(q0 - SW_BLOCK, 0, seq - nwin), SW_BLOCK)
    kw = k_ref[pl.ds(start, nwin), :]
    vw = v_ref[pl.ds(start, nwin), :]
    kc = kc_ref[...]
    vc = vc_ref[...]
    row = lax.broadcasted_iota(jnp.int32, (2 * nq, nwin), 0)
    col = lax.broadcasted_iota(jnp.int32, (2 * nq, nwin), 1)
    ok = jnp.abs((start + col) - (q0 + row % nq)) <= SW_BLOCK
    half = lax.broadcasted_iota(jnp.int32, (2 * nq, 1), 0) < nq
    for j in range(GROUP // LANES):
        cs = slice(j * LANES, (j + 1) * LANES)
        qs = _pair_rows(q_ref[:, cs])
        s_bd = jnp.where(ok, _dot_nt(qs, kw), NEG)
        s_cx = _dot_nt(qs, kc)
        sink = jnp.where(half, sink_ref[j], sink_ref[j + 4])
        o2, l = _softmax_pv([(s_bd, vw), (s_cx, vc)], sink)
        o_ref[:, cs] = _pair_merge(o2, 1.0 / l).astype(BF16)


def _sw_attention(q, k, v, kc, vc, sink, bsz, seq, ctx):
    assert seq % (SW_BPS * SW_BLOCK) == 0 and seq >= (SW_BPS + 2) * SW_BLOCK
    nb = seq // (SW_BPS * SW_BLOCK)
    k3, v3 = k.reshape(bsz, seq, LANES), v.reshape(bsz, seq, LANES)
    kc3, vc3 = kc.reshape(bsz, ctx, LANES), vc.reshape(bsz, ctx, LANES)
    return pl.pallas_call(
        functools.partial(_sw_kernel, seq=seq),
        grid=(bsz, nb),
        in_specs=[pl.BlockSpec(memory_space=pltpu.SMEM),
                  pl.BlockSpec((SW_BPS * SW_BLOCK, GROUP), lambda b, n: (b * nb + n, 0)),
                  pl.BlockSpec((None, seq, LANES), lambda b, n: (b, 0, 0)),
                  pl.BlockSpec((None, seq, LANES), lambda b, n: (b, 0, 0)),
                  pl.BlockSpec((None, ctx, LANES), lambda b, n: (b, 0, 0)),
                  pl.BlockSpec((None, ctx, LANES), lambda b, n: (b, 0, 0))],
        out_specs=pl.BlockSpec((SW_BPS * SW_BLOCK, GROUP), lambda b, n: (b * nb + n, 0)),
        out_shape=jax.ShapeDtypeStruct((bsz * seq, GROUP), BF16),
        compiler_params=_cparams(("parallel", "arbitrary")),
    )(sink, q, k3, v3, kc3, vc3)


def _ctx_attn_kernel(sink_ref, qa_ref, ka_ref, va_ref, qb_ref, kb_ref, vb_ref, oa_ref, ob_ref):
    for p in range(GROUP // LANES):
        cs = slice(p * LANES, (p + 1) * LANES)
        qs = _pair_rows(qa_ref[:, cs])
        o2, l = _softmax_pv([(_dot_nt(qs, ka_ref[:, cs]), va_ref[:, cs])])
        oa_ref[:, cs] = _pair_merge(o2, 1.0 / l).astype(BF16)
    ctx = qb_ref.shape[0]
    half = lax.broadcasted_iota(jnp.int32, (2 * ctx, 1), 0) < ctx
    for j in range(GROUP // LANES):
        cs = slice(j * LANES, (j + 1) * LANES)
        qs = _pair_rows(qb_ref[:, cs])
        sink = jnp.where(half, sink_ref[j], sink_ref[j + 4])
        o2, l = _softmax_pv([(_dot_nt(qs, kb_ref[...]), vb_ref[...])], sink)
        ob_ref[:, cs] = _pair_merge(o2, 1.0 / l).astype(BF16)


def _ctx_attention(qa, ka, va, qb, kb, vb, sink, bsz, ctx):
    big = pl.BlockSpec((ctx, GROUP), lambda b: (b, 0))
    small = pl.BlockSpec((ctx, LANES), lambda b: (b, 0))
    return pl.pallas_call(
        _ctx_attn_kernel,
        grid=(bsz,),
        in_specs=[pl.BlockSpec(memory_space=pltpu.SMEM), big, big, big, big, small, small],
        out_specs=[big, big],
        out_shape=[jax.ShapeDtypeStruct((bsz * ctx, GROUP), BF16)] * 2,
        compiler_params=_cparams(("parallel",)),
    )(sink, qa, ka, va, qb, kb, vb)


def _mlstm_kernel(*refs, chunk, emit):
    (qkf_ref, vf_ref, gif_ref, gff_ref, qkr_ref, vr_ref, gir_ref, gfr_ref,
     brow_ref, c0_ref, m0_ref) = refs[:11]
    if emit:
        hf_ref, hr_ref = refs[11:13]
        rest = refs[13:]
    else:
        hf_ref = hr_ref = None
        rest = refs[11:]
    c_out, m_out, c_s, m_s = rest
    i = pl.program_id(1)

    @pl.when(i == 0)
    def _():
        c_s[...] = c0_ref[...]
        m_s[...] = m0_ref[...]

    row = lax.broadcasted_iota(jnp.int32, (chunk, chunk), 0)
    col = lax.broadcasted_iota(jnp.int32, (chunk, chunk), 1)
    trow = lax.broadcasted_iota(jnp.int32, (chunk, 1), 0)
    lane = lax.broadcasted_iota(jnp.int32, (chunk, LANES), 1)
    srow = lax.broadcasted_iota(jnp.int32, (LANES, 1), 0)
    dirs = ((qkf_ref, vf_ref, gif_ref, gff_ref, hf_ref), (qkr_ref, vr_ref, gir_ref, gfr_ref, hr_ref))
    chains = [(b, d) for b in range(qkf_ref.shape[0]) for d in range(2)]
    old_c = {(b, d, pp): c_s[b, d, pp] for b, d in chains for pp in range(2)}
    old_m = {(b, d): m_s[b, d, 0:1, :] for b, d in chains}
    new_c, new_m = {}, {}
    for b, d in chains:
        qk_ref, v_ref, gi_ref, gf_ref, h_ref = [r if r is None else r.at[b] for r in dirs[d]]
        causal = (col <= row) if d == 0 else (col >= row)
        ipre = gi_ref[...] + brow_ref[0:1, :]
        fl = _log_sigmoid(gf_ref[...] + brow_ref[1:2, :])
        fcum = _exact_left(jnp.where(causal, 1.0, 0.0).astype(BF16), fl)
        ftot = jnp.sum(fl, axis=0, keepdims=True)
        m_prev = old_m[b, d]
        w_log = ftot - fcum + ipre
        m_new = jnp.maximum(ftot + m_prev, jnp.max(w_log, axis=0, keepdims=True))
        cd = jnp.exp(ftot + m_prev - m_new)
        w = jnp.exp(w_log - m_new)
        if emit:
            b_col = ipre - fcum
            run = b_col
            sh = 1
            while sh < chunk:
                if d == 0:
                    run = jnp.maximum(run, jnp.where(trow >= sh, pltpu.roll(run, sh, axis=0), NEG))
                else:
                    run = jnp.maximum(run, jnp.where(trow < chunk - sh, pltpu.roll(run, chunk - sh, axis=0), NEG))
                sh *= 2
            log_inter = fcum + m_prev
            m_row = jnp.maximum(log_inter, fcum + run)
            inter_w = jnp.exp(log_inter - m_row)
            fcm = fcum - m_row
            floor = jnp.exp(-m_row)
            b_rows = b_col.T[0:8, :]
        for pp in range(2):
            qp = qk_ref[:, pp * LANES:(pp + 1) * LANES]
            kp = qk_ref[:, GROUP // 2 + pp * LANES:GROUP // 2 + (pp + 1) * LANES]
            cn = old_c[b, d, pp]
            inc = None
            for e in range(2):
                h = 2 * pp + e
                c = 4 * d + h
                sel = (lane < ATT_HD) if e == 0 else (lane >= ATT_HD)
                vh = v_ref[:, h * LANES:(h + 1) * LANES]
                if emit:
                    qm = jnp.where(sel, qp, jnp.zeros_like(qp))
                    d_mat = jnp.exp(jnp.where(causal, fcm[:, c:c + 1] + b_rows[c:c + 1, :], NEG))
                    s_mat = (_dot_nt(qm, kp) * d_mat).astype(BF16)
                    vb = vh.astype(BF16)
                    acc = (inter_w[:, c:c + 1] * _dot(qm, cn.astype(BF16))
                           + _dot(s_mat, jnp.concatenate([vb, jnp.ones_like(vb)], axis=1)))
                    den = jnp.maximum(jnp.abs(acc[:, LANES:]), floor[:, c:c + 1])
                    h_ref[:, h * LANES:(h + 1) * LANES] = acc[:, 0:LANES] / den
                km = jnp.where(sel, kp, jnp.zeros_like(kp))
                wc = jnp.broadcast_to(w[:, c:c + 1], (chunk, LANES))
                upd = _dot_tn(km, jnp.concatenate([wc * vh, wc], axis=1).astype(BF16))
                inc = upd if inc is None else inc + upd
            c_lo, c_hi = 4 * d + 2 * pp, 4 * d + 2 * pp + 1
            new_c[b, d, pp] = (jnp.where(srow < ATT_HD, cd[:, c_lo:c_lo + 1], cd[:, c_hi:c_hi + 1]) * cn
                               + inc)
        new_m[b, d] = m_new
    for key, val in new_c.items():
        c_s[key] = val
    for (b, d), val in new_m.items():
        m_s[b, d, 0:1, :] = val

    @pl.when(i == pl.num_programs(1) - 1)
    def _():
        c_out[...] = c_s[...]
        m_out[...] = m_s[...]


def _mlstm_scan(qk, p, brow, state, bsz, seq, chunk, emit):
    nc = seq // chunk
    bps = bsz if bsz * chunk <= ML_STEP_ROWS else max(1, ML_STEP_ROWS // chunk)
    assert bsz % bps == 0
    fwd = lambda g, i: i
    rev = lambda g, i: nc - 1 - i
    qk3 = qk.reshape(bsz, seq, GROUP)
    p3 = p.reshape(bsz, seq, NP)

    def chunk_specs(idx):
        return [pl.BlockSpec((bps, chunk, GROUP), lambda g, i: (g, idx(g, i), 0)),
                pl.BlockSpec((bps, chunk, GROUP), lambda g, i: (g, idx(g, i), C_MLV // GROUP)),
                pl.BlockSpec((bps, chunk, LANES), lambda g, i: (g, idx(g, i), C_MLI // LANES)),
                pl.BlockSpec((bps, chunk, LANES), lambda g, i: (g, idx(g, i), C_MLF // LANES))]

    st_dims = [(2, 2, LANES, 2 * LANES), (2, 8, LANES)]
    st_specs = [pl.BlockSpec((bps,) + s, lambda g, i, n=len(s): (g,) + (0,) * n) for s in st_dims]
    st_shapes = [jax.ShapeDtypeStruct((bsz,) + s, F32) for s in st_dims]
    h_specs = [pl.BlockSpec((bps, chunk, GROUP), lambda g, i: (g, fwd(g, i), 0)),
               pl.BlockSpec((bps, chunk, GROUP), lambda g, i: (g, rev(g, i), 0))] if emit else []
    h_shapes = [jax.ShapeDtypeStruct((bsz, seq, GROUP), F32)] * 2 if emit else []
    outs = pl.pallas_call(
        functools.partial(_mlstm_kernel, chunk=chunk, emit=emit),
        grid=(bsz // bps, nc),
        in_specs=chunk_specs(fwd) + chunk_specs(rev)
        + [pl.BlockSpec((8, LANES), lambda g, i: (0, 0))] + st_specs,
        out_specs=h_specs + st_specs,
        out_shape=h_shapes + st_shapes,
        scratch_shapes=[pltpu.VMEM((bps,) + s, F32) for s in st_dims],
        compiler_params=_cparams(("parallel", "arbitrary")),
    )(qk3, p3, p3, p3, qk3, p3, p3, p3, brow, *state)
    if emit:
        return (outs[0].reshape(bsz * seq, GROUP), outs[1].reshape(bsz * seq, GROUP), tuple(outs[2:]))
    return None, None, tuple(outs)


def _hgrn_constants(chunk):
    nlev = int(np.log2(chunk))
    assert 2 ** nlev == chunk
    t = np.arange(chunk)
    mats = [(t[None, :] <= t[:, None])]
    masks = []
    for lev in range(nlev):
        n = chunk >> (lev + 1)
        b0 = t - t % (2 * n)
        upper = (t % (2 * n)) >= n
        m_up = (t[None, :] >= (b0 + n)[:, None]) & (t[None, :] <= t[:, None])
        m_lo = (t[None, :] > t[:, None]) & (t[None, :] <= (b0 + n - 1)[:, None])
        mats.append(np.where(upper[:, None], m_up, m_lo))
        same = (t[:, None] // (2 * n)) == (t[None, :] // (2 * n))
        masks.append(same & upper[:, None] & ~upper[None, :])
    masks.append(t[:, None] == t[None, :])
    mstack = np.concatenate(mats, axis=0).astype(np.float32)
    lmask = np.stack([np.kron(np.eye(2), m.astype(np.float32)) for m in masks])
    nr = mstack.shape[0] // chunk
    mstack_rev = mstack.reshape(nr, chunk, chunk)[:, ::-1, ::-1].reshape(-1, chunk)
    lmask_rev = lmask.reshape(nlev + 1, 2, chunk, 2, chunk)[:, :, ::-1, :, ::-1].reshape(lmask.shape)
    return (jnp.asarray(np.stack([mstack, mstack_rev]), BF16),
            jnp.asarray(np.stack([lmask, lmask_rev]), F32), nlev)


def _stack_pair(x, pp):
    return jnp.concatenate([x[:, (2 * pp) * LANES:(2 * pp + 1) * LANES],
                            x[:, (2 * pp + 1) * LANES:(2 * pp + 2) * LANES]], axis=0)


def _hgrn_kernel(*refs, chunk, nlev, emit, zero_lb):
    (qf_ref, vf_ref, ff_ref, qr_ref, vr_ref, fr_ref, lb_ref, ms_ref, lm_ref, s0_ref) = refs[:10]
    if emit:
        of_ref, or_ref = refs[10:12]
        rest = refs[12:]
    else:
        of_ref = or_ref = None
        rest = refs[10:]
    s_out, s_s = rest
    i = pl.program_id(1)

    @pl.when(i == 0)
    def _():
        s_s[...] = s0_ref[...]

    dirs = ((qf_ref, vf_ref, ff_ref, of_ref), (qr_ref, vr_ref, fr_ref, or_ref))
    chains = [(b, d) for b in range(vf_ref.shape[0]) for d in range(2)]
    old_s = {(b, d, h): s_s[b, d, h] for b, d in chains for h in range(4)}
    new_s = {}

    for b, d in chains:
        q_ref, v_ref, f_ref, o_ref = [r if r is None else r.at[b] for r in dirs[d]]
        fpre = f_ref[...]
        vb = v_ref[...].astype(BF16)
        e_f = jnp.exp(-jnp.abs(fpre))
        ls = jnp.minimum(fpre, 0.0) - _log1p_unit(e_f)
        sig_neg = jnp.where(fpre >= 0.0, e_f, 1.0) / (1.0 + e_f)
        if zero_lb:
            logf, k = ls, sig_neg
        else:
            a = lb_ref[d, 0:1, :]
            bb = lb_ref[d, 1:2, :] + ls
            logf = jnp.maximum(a, bb) + _log1p_unit(jnp.exp(-jnp.abs(a - bb)))
            k = lb_ref[d, 2:3, :] * sig_neg
        lf2 = logf * LOG2E
        hi = lf2.astype(BF16)
        lo = (lf2 - hi.astype(F32)).astype(BF16)
        gtot = jnp.sum(lf2, axis=0, keepdims=True)
        if emit:
            qpre = q_ref[...]
            q = qpre * _sigmoid(qpre)
            qb = q.astype(BF16)
            kb = k.astype(BF16)
        ms = ms_ref[d] if emit else ms_ref[d, 0:chunk]
        ex = _dot(ms, hi) + _dot(ms, lo)
        ktil = (k * jnp.exp2(gtot - ex[0:chunk])).astype(BF16)
        if emit:
            qe = (q * jnp.exp2(ex[0:chunk])).astype(BF16)
            lev_ops = []
            for lev in range(nlev):
                e_l = jnp.exp2(ex[(1 + lev) * chunk:(2 + lev) * chunk].astype(BF16))
                lev_ops.append((qb * e_l, kb * e_l))
        if emit:
            att = [lm_ref[d, nlev] * _dot_nt(_stack_pair(qb, pp), _stack_pair(kb, pp)) for pp in range(2)]
            for lev, (q_l, k_l) in enumerate(lev_ops):
                for pp in range(2):
                    att[pp] = att[pp] + lm_ref[d, lev] * _dot_nt(_stack_pair(q_l, pp), _stack_pair(k_l, pp))
            o_pair = [_dot(att[pp].astype(BF16), _stack_pair(vb, pp)) for pp in range(2)]
        for h in range(4):
            cs = slice(h * LANES, (h + 1) * LANES)
            s_t = old_s[b, d, h]
            if emit:
                pp, e = divmod(h, 2)
                o_ref[:, cs] = o_pair[pp][e * chunk:(e + 1) * chunk] + _dot_nt(qe[:, cs], s_t.astype(BF16))
            new_s[b, d, h] = s_t * jnp.exp2(gtot[:, cs]) + _dot_tn(vb[:, cs], ktil[:, cs])

    for key, val in new_s.items():
        s_s[key] = val

    @pl.when(i == pl.num_programs(1) - 1)
    def _():
        s_out[...] = s_s[...]


def _hgrn_scan(p, lbc, consts, state, bsz, seq, chunk, emit, zero_lb):
    mstack, lmask, nlev = consts
    nc = seq // chunk
    bps = bsz
    fwd = lambda g, i: i
    rev = lambda g, i: nc - 1 - i
    p3 = p.reshape(bsz, seq, NP)

    def chunk_specs(idx, d):
        return [pl.BlockSpec((bps, chunk, GROUP), lambda g, i: (g, idx(g, i), C_HGQ // GROUP)),
                pl.BlockSpec((bps, chunk, GROUP), lambda g, i: (g, idx(g, i), C_HGI // GROUP)),
                pl.BlockSpec((bps, chunk, GROUP), lambda g, i: (g, idx(g, i), C_HGF // GROUP + d))]

    st_spec = pl.BlockSpec((bps, 2, 4, LANES, LANES), lambda g, i: (g, 0, 0, 0, 0))
    st_shape = jax.ShapeDtypeStruct((bsz, 2, 4, LANES, LANES), F32)
    o_specs = [pl.BlockSpec((bps, chunk, GROUP), lambda g, i: (g, fwd(g, i), 0)),
               pl.BlockSpec((bps, chunk, GROUP), lambda g, i: (g, rev(g, i), 0))] if emit else []
    o_shapes = [jax.ShapeDtypeStruct((bsz, seq, GROUP), F32)] * 2 if emit else []
    full = lambda a: pl.BlockSpec(a.shape, lambda g, i: (0,) * a.ndim)
    outs = pl.pallas_call(
        functools.partial(_hgrn_kernel, chunk=chunk, nlev=nlev, emit=emit, zero_lb=zero_lb),
        grid=(bsz // bps, nc),
        in_specs=chunk_specs(fwd, 0) + chunk_specs(rev, 1) + [full(lbc), full(mstack), full(lmask), st_spec],
        out_specs=o_specs + [st_spec],
        out_shape=o_shapes + [st_shape],
        scratch_shapes=[pltpu.VMEM((bps, 2, 4, LANES, LANES), F32)],
        compiler_params=_cparams(("parallel", "arbitrary")),
    )(p3, p3, p3, p3, p3, p3, lbc, mstack, lmask, state)
    if emit:
        return outs[0].reshape(bsz * seq, GROUP), outs[1].reshape(bsz * seq, GROUP), outs[2]
    return None, None, outs[0]


def _outproj_kernel(x_ref, ya_ref, yb_ref, hf_ref, hr_ref, mo_ref, of_ref, or_ref, hg_ref,
                    w_ref, g_ref, o_ref):
    acc = _dot(ya_ref[...], w_ref[0]) + _dot(yb_ref[...], w_ref[1])
    yc = _sigmoid(mo_ref[...]) * (hf_ref[...] + hr_ref[...])
    acc = acc + _dot(yc.astype(BF16), w_ref[2])
    o = of_ref[...] + or_ref[...]
    gate = _sigmoid(hg_ref[...])
    parts = []
    for h in range(4):
        cs = slice(h * LANES, (h + 1) * LANES)
        oh = o[:, cs]
        parts.append(oh * lax.rsqrt(jnp.mean(oh * oh, axis=-1, keepdims=True) + EPS) * gate[:, cs])
    yd = jnp.concatenate(parts, axis=1)
    acc = acc + _dot(yd.astype(BF16), w_ref[3])
    o_ref[...] = x_ref[...] + g_ref[...] * acc


def _out_proj(x, ya, yb, hf, hr, of, orv, p, w_all, layer, gate, rows_per_mod):
    t, d = x.shape
    tm = _row_tile(rows_per_mod, 512)
    per = rows_per_mod // tm
    grp = lambda cb=0: pl.BlockSpec((tm, GROUP), lambda i: (i, cb))
    return pl.pallas_call(
        _outproj_kernel,
        grid=(t // tm,),
        in_specs=[pl.BlockSpec((tm, d), lambda i: (i, 0)), grp(), grp(), grp(), grp(),
                  grp(C_MLO // GROUP), grp(), grp(), grp(C_HGG // GROUP),
                  pl.BlockSpec((None, 4, GROUP, d), lambda i: (layer, 0, 0, 0)),
                  pl.BlockSpec((None, 1, d), lambda i: (i // per, 0, 0))],
        out_specs=pl.BlockSpec((tm, d), lambda i: (i, 0)),
        out_shape=jax.ShapeDtypeStruct((t, d), F32),
        compiler_params=_cparams(("parallel",)),
    )(x, ya, yb, hf, hr, p, of, orv, p, w_all, gate)


def _ffn_kernel(x_ref, xp_ref, xn_ref, g_ref, sh_ref, sc_ref, gate_ref, wa_ref, wu_ref, wc_ref, wd_ref,
                o_ref, xs_ref, acc_ref, *, per, inner):
    i = pl.program_id(0)
    j = pl.program_id(1)
    tm = x_ref.shape[0]

    @pl.when(j == 0)
    def _():
        nm = lambda x: _norm_mod(x, g_ref[...], sh_ref[...], sc_ref[...])
        keep_p = jnp.where(i % per == 0, 0.0, 1.0)
        keep_n = jnp.where(i % per == per - 1, 0.0, 1.0)
        _norm_mod_rows(x_ref, xs_ref, tm, g_ref[...], sh_ref[...], sc_ref[...])
        xs_ref[tm:tm + HALO, :] = (nm(xn_ref[...]) * keep_n).astype(BF16)
        xs_ref[tm + HALO:, :] = (nm(xp_ref[...]) * keep_p).astype(BF16)
        acc_ref[...] = jnp.zeros_like(acc_ref)

    a = _dot(xs_ref[...], wa_ref[...])
    rows = a.shape[0]
    prv = pltpu.roll(a, 1, axis=0)[0:tm]
    nxt = pltpu.roll(a, rows - 1, axis=0)[0:tm]
    if inner:
        pos = lax.broadcasted_iota(jnp.int32, (tm, 1), 0) % inner
        prv = jnp.where(pos == 0, 0.0, prv)
        nxt = jnp.where(pos == inner - 1, 0.0, nxt)
    conv = prv * wc_ref[0:1, :] + a[0:tm] * wc_ref[1:2, :] + nxt * wc_ref[2:3, :]
    u = _dot(xs_ref[0:tm, :], wu_ref[...])
    act = (conv * _sigmoid(conv) * u).astype(BF16)
    acc_ref[...] += _dot(act, wd_ref[...])

    @pl.when(j == pl.num_programs(1) - 1)
    def _():
        o_ref[...] = x_ref[...] + gate_ref[...] * acc_ref[...]


def _ffn(x, gain, shift, scale, gate, w_up, w_conv, w_down, layer, seq):
    t, d = x.shape
    f = w_down.shape[1]
    tm = 512 if t % 512 == 0 else t
    assert seq % tm == 0 or (tm % seq == 0 and shift.shape[0] == 1)
    tf = 512 if f % 512 == 0 else f
    per = max(seq // tm, 1)
    inner = seq if seq < tm else 0
    nf = f // tf
    assert w_up.shape[1:] == (2 * nf, d, tf)
    step = tm // HALO
    nblk8 = t // HALO
    shared = shift.shape[0] == 1
    mod = pl.BlockSpec((None, 1, d), lambda i, j: (0 if shared else i // per, 0, 0))
    return pl.pallas_call(
        functools.partial(_ffn_kernel, per=per, inner=inner),
        grid=(t // tm, nf),
        in_specs=[pl.BlockSpec((tm, d), lambda i, j: (i, 0)),
                  pl.BlockSpec((HALO, d), lambda i, j: (jnp.maximum(i * step - 1, 0), 0)),
                  pl.BlockSpec((HALO, d), lambda i, j: (jnp.minimum((i + 1) * step, nblk8 - 1), 0)),
                  pl.BlockSpec((1, d), lambda i, j: (0, 0)), mod, mod, mod,
                  pl.BlockSpec((None, None, d, tf), lambda i, j: (layer, j, 0, 0)),
                  pl.BlockSpec((None, None, d, tf), lambda i, j: (layer, nf + j, 0, 0)),
                  pl.BlockSpec((None, 3, tf), lambda i, j: (layer, 0, j)),
                  pl.BlockSpec((None, tf, d), lambda i, j: (layer, j, 0))],
        out_specs=pl.BlockSpec((tm, d), lambda i, j: (i, 0)),
        out_shape=jax.ShapeDtypeStruct((t, d), F32),
        scratch_shapes=[pltpu.VMEM((tm + 2 * HALO, d), BF16), pltpu.VMEM((tm, d), F32)],
        compiler_params=_cparams(("parallel", "arbitrary")),
    )(x, x, x, gain, shift, scale, gate, w_up, w_up, w_conv, w_down)


def _perm_w_in(w):
    n, d = w.shape[:2]
    swq = w[..., 1536:2048].reshape(n, d, 2, 4, ATT_HD).transpose(0, 1, 3, 2, 4).reshape(n, d, GROUP)
    pad = jnp.zeros((n, d, LANES - 8), w.dtype)
    cols = [w[..., 0:1536], swq, w[..., 2304:3840], w[..., 3856:6416], w[..., 2048:2304],
            w[..., 3840:3848], pad, w[..., 3848:3856], pad]
    out = jnp.concatenate(cols, axis=-1).astype(BF16)
    assert out.shape[-1] == NP
    return out


def _perm_w_out(w):
    n, _, d = w.shape
    w4 = w.reshape(n, 4, GROUP, d)
    swo = w4[:, 1].reshape(n, 2, 4, ATT_HD, d).transpose(0, 2, 1, 3, 4).reshape(n, GROUP, d)
    return jnp.stack([w4[:, 0], swo, w4[:, 2], w4[:, 3]], axis=1).astype(BF16)


def _rope_tables(seq):
    t = jnp.arange(seq)
    half = ATT_HD // 2
    inv = ROPE_BASE ** (-jnp.arange(0, half, 2, dtype=F32) / half)
    ang_r = (t // GRID_W).astype(F32)[:, None] * inv
    ang_c = (t % GRID_W).astype(F32)[:, None] * inv
    cr, sr, cc, sc = jnp.cos(ang_r), jnp.sin(ang_r), jnp.cos(ang_c), jnp.sin(ang_c)
    cos = jnp.concatenate([cr, cr, cc, cc] * 2, axis=-1)
    sin = jnp.concatenate([-sr, sr, -sc, sc] * 2, axis=-1)
    return cos, sin


def kernel(x, c, ctx, c_ctx, w_mod, b_mod, norm_mix, norm_ffn, w_in, w_out, na_qk_gain, na_rpb,
           sw_qk_gain, sw_sink, ml_conv, ml_gate_bias, hg_lb, ffn_up, ffn_conv, ffn_down):
    bsz, seq, d = x.shape
    nctx = ctx.shape[1]
    depth = w_mod.shape[0]
    ml_chunk_ctx, ml_chunk = math.gcd(ML_CHUNK, nctx), math.gcd(ML_CHUNK, seq)
    chunk = math.gcd(HG_CHUNK, math.gcd(seq, nctx))
    assert bsz + 1 <= 8

    c_all = jnp.concatenate([c, c_ctx[None], jnp.zeros((8 - bsz - 1, d), F32)], axis=0)
    mod = _modulation(c_all, w_mod, b_mod).reshape(depth, 8, 6, d)
    cos, sin = _rope_tables(seq)
    bd = jnp.asarray(np.kron(np.eye(LANES // ATT_HD), np.ones((ATT_HD, ATT_HD))), BF16)
    lb_cum = jnp.cumsum(jax.nn.softmax(hg_lb.astype(F32), axis=0), axis=0)
    lower = lb_cum - lb_cum[:1]
    consts = _hgrn_constants(chunk)
    w_in_p = _perm_w_in(w_in)
    w_out_p = _perm_w_out(w_out)
    f2 = ffn_up.shape[2]
    ftile = 512 if (f2 // 2) % 512 == 0 else f2 // 2
    w_up = ffn_up.astype(BF16).reshape(depth, d, f2 // ftile, ftile).transpose(0, 2, 1, 3)
    w_dn = ffn_down.astype(BF16)
    w_cv = ffn_conv.astype(F32)

    h = x.reshape(bsz * seq, d)
    hc = ctx.reshape(bsz * nctx, d)
    for l in range(depth):
        emit_ctx = l < depth - 1
        lat = lambda k: mod[l, :bsz, k].reshape(bsz, 1, d)
        cx = lambda k: mod[l, bsz, k].reshape(1, 1, d)
        gain = norm_mix[l].reshape(1, d)
        p_lat = _in_proj(h, gain, lat(0), lat(1), w_in_p, l, seq)
        p_ctx = _in_proj(hc, gain, cx(0), cx(1), w_in_p, l, bsz * nctx)

        gna = jnp.tile(na_qk_gain[l].astype(F32), (1, GROUP // ATT_HD))
        gsw = jnp.tile(sw_qk_gain[l].astype(F32), (1, GROUP // ATT_HD))
        conv_w = ml_conv[l].astype(F32)
        naq, nak, nav, swq, swk, swv, qk_lat = _prep(p_lat, gna, gsw, bd, cos, sin, conv_w, seq, True)
        cnaq, cnak, cnav, cswq, cswk, cswv, qk_ctx = _prep(p_ctx, gna, gsw, bd, cos, sin, conv_w, nctx, False)
        sink = sw_sink[l].astype(F32)
        ya = _na_attention(naq, nak, nav, cnak, cnav, _na_bias_table(na_rpb[l], seq // GRID_W), bsz, seq, nctx)
        yb = _sw_attention(swq, swk, swv, cswk, cswv, sink, bsz, seq, nctx)

        gb = ml_gate_bias[l].astype(F32).reshape(2, 8)
        brow = jnp.zeros((8, LANES), F32).at[0:2, 0:8].set(gb)
        ml0 = (jnp.zeros((bsz, 2, 2, LANES, 2 * LANES), F32), jnp.zeros((bsz, 2, 8, LANES), F32))
        hcf, hcr, ml_state = _mlstm_scan(qk_ctx, p_ctx, brow, ml0, bsz, nctx, ml_chunk_ctx, emit_ctx)
        hf, hr, _ = _mlstm_scan(qk_lat, p_lat, brow, ml_state, bsz, seq, ml_chunk, True)

        lb = lower[l]
        lbc = jnp.stack([jnp.maximum(jnp.log(lb), NEG), jnp.log1p(-lb), 1.0 - lb], axis=1)
        lbc = jnp.concatenate([lbc, jnp.zeros((2, 5, GROUP), F32)], axis=1)
        hg0 = jnp.zeros((bsz, 2, 4, LANES, LANES), F32)
        ocf, ocr, hg_state = _hgrn_scan(p_ctx, lbc, consts, hg0, bsz, nctx, chunk, emit_ctx, l == 0)
        of, orv, _ = _hgrn_scan(p_lat, lbc, consts, hg_state, bsz, seq, chunk, True, l == 0)

        gain2 = norm_ffn[l].reshape(1, d)
        h = _out_proj(h, ya, yb, hf, hr, of, orv, p_lat, w_out_p, l, lat(2), seq)
        h = _ffn(h, gain2, lat(3), lat(4), lat(5), w_up, w_cv, w_dn, l, seq)
        if emit_ctx:
            yac, ybc = _ctx_attention(cnaq, cnak, cnav, cswq, cswk, cswv, sink, bsz, nctx)
            hc = _out_proj(hc, yac, ybc, hcf, hcr, ocf, ocr, p_ctx, w_out_p, l, cx(2), bsz * nctx)
            hc = _ffn(hc, gain2, cx(3), cx(4), cx(5), w_up, w_cv, w_dn, l, nctx)
    return h.reshape(bsz, seq, d)
```

```python
import functools
import math

import numpy as np
import jax
import jax.numpy as jnp
from jax import lax
from jax.experimental import pallas as pl
from jax.experimental.pallas import tpu as pltpu

F32 = jnp.float32
BF16 = jnp.bfloat16

GRID_W = 64
ATT_HD = 64
NA_ROWS = 8
NA_COLS = 16
SW_BLOCK = 128
ROPE_BASE = 10000.0
EPS = 1e-6
NEG = -1e30
LOG2E = 1.4426950408889634
GROUP = 512
LANES = 128
HALO = 8
VMEM_LIMIT = 56 * 1024 * 1024

C_NAQ, C_NAK, C_NAV, C_SWQ = 0, 512, 1024, 1536
C_MLQK, C_MLV, C_MLO = 2048, 2560, 3072
C_HGQ, C_HGI, C_HGF, C_HGG = 3584, 4096, 4608, 5632
C_SWK, C_SWV = 6144, 6272
C_MLI, C_MLF = 6400, 6528
NP = 6656
ML_CHUNK = 256
ML_STEP_ROWS = 1024
HG_CHUNK = 128
SW_PERM = (0, 4, 1, 5, 2, 6, 3, 7)


def _cparams(sem):
    return pltpu.CompilerParams(dimension_semantics=sem, vmem_limit_bytes=VMEM_LIMIT)


def _sigmoid(x):
    return 1.0 / (1.0 + jnp.exp(-x))


def _log1p_unit(e):
    return jnp.log(1.0 + e)


def _log_sigmoid(x):
    return jnp.minimum(x, 0.0) - _log1p_unit(jnp.exp(-jnp.abs(x)))


def _split3(x):
    hi = x.astype(BF16)
    r = x - hi.astype(F32)
    mid = r.astype(BF16)
    lo = (r - mid.astype(F32)).astype(BF16)
    return hi, mid, lo


def _dot(a, b):
    return jnp.dot(a, b, preferred_element_type=F32)


def _dot_nt(a, b):
    return lax.dot_general(a, b, (((1,), (1,)), ((), ())), preferred_element_type=F32)


def _dot_tn(a, b):
    return lax.dot_general(a, b, (((0,), (0,)), ((), ())), preferred_element_type=F32)


def _exact_left(m01, x):
    hi, mid, lo = _split3(x)
    return _dot(m01, hi) + _dot(m01, mid) + _dot(m01, lo)


def _mod_kernel(c_ref, w_ref, b_ref, o_ref):
    c = c_ref[...]
    a = c * _sigmoid(c)
    a_hi = a.astype(BF16)
    a_lo = (a - a_hi.astype(F32)).astype(BF16)
    w = w_ref[...]
    w_hi = w.astype(BF16)
    w_lo = (w - w_hi.astype(F32)).astype(BF16)
    o_ref[...] = _dot(a_hi, w_hi) + _dot(a_hi, w_lo) + _dot(a_lo, w_hi) + b_ref[...]


def _modulation(c_all, w_mod, b_mod):
    depth, d, n = w_mod.shape
    tn = 1024 if n % 1024 == 0 else n
    return pl.pallas_call(
        _mod_kernel,
        grid=(depth, n // tn),
        in_specs=[pl.BlockSpec((8, d), lambda l, j: (0, 0)),
                  pl.BlockSpec((None, d, tn), lambda l, j: (l, 0, j)),
                  pl.BlockSpec((None, 1, tn), lambda l, j: (l, 0, j))],
        out_specs=pl.BlockSpec((None, 8, tn), lambda l, j: (l, 0, j)),
        out_shape=jax.ShapeDtypeStruct((depth, 8, n), F32),
        compiler_params=_cparams(("parallel", "parallel")),
    )(c_all, w_mod, b_mod.reshape(depth, 1, n))


def _norm_mod(x, gain, shift, scale):
    y = x * lax.rsqrt(jnp.mean(x * x, axis=-1, keepdims=True) + EPS) * gain
    return y * (1.0 + scale) + shift


NORM_ROWS = 16


def _norm_mod_rows(src_ref, dst_ref, nrows, gain, shift, scale):
    amp = gain * (1.0 + scale)

    def body(c, carry):
        rows = pl.ds(pl.multiple_of(c * NORM_ROWS, NORM_ROWS), NORM_ROWS)
        x = src_ref[rows, :]
        y = x * lax.rsqrt(jnp.mean(x * x, axis=-1, keepdims=True) + EPS) * amp + shift
        dst_ref[rows, :] = y.astype(BF16)
        return carry

    assert nrows % NORM_ROWS == 0
    lax.fori_loop(0, nrows // NORM_ROWS, body, 0, unroll=4)


def _inproj_kernel(x_ref, g_ref, sh_ref, sc_ref, w_ref, o_ref, xn_ref):
    @pl.when(pl.program_id(1) == 0)
    def _():
        _norm_mod_rows(x_ref, xn_ref, x_ref.shape[0], g_ref[...], sh_ref[...], sc_ref[...])

    o_ref[...] = _dot(xn_ref[...], w_ref[...])


def _row_tile(rows_per_mod, pref):
    tm = min(pref, rows_per_mod)
    assert rows_per_mod % tm == 0
    return tm


def _in_proj(x, gain, shift, scale, w_all, layer, rows_per_mod):
    t, d = x.shape
    n = w_all.shape[2]
    tm = _row_tile(rows_per_mod, 512)
    tn = next(c for c in (1664, 512, n) if n % c == 0)
    per = rows_per_mod // tm
    return pl.pallas_call(
        _inproj_kernel,
        grid=(t // tm, n // tn),
        in_specs=[pl.BlockSpec((tm, d), lambda i, j: (i, 0)),
                  pl.BlockSpec((1, d), lambda i, j: (0, 0)),
                  pl.BlockSpec((None, 1, d), lambda i, j: (i // per, 0, 0)),
                  pl.BlockSpec((None, 1, d), lambda i, j: (i // per, 0, 0)),
                  pl.BlockSpec((None, d, tn), lambda i, j: (layer, 0, j))],
        out_specs=pl.BlockSpec((tm, tn), lambda i, j: (i, j)),
        out_shape=jax.ShapeDtypeStruct((t, n), F32),
        scratch_shapes=[pltpu.VMEM((tm, d), BF16)],
        compiler_params=_cparams(("parallel", "arbitrary")),
    )(x, gain, shift, scale, w_all)


def _head_rms(x, gain_row, bd):
    parts = []
    for g in range(x.shape[1] // LANES):
        xg = x[:, g * LANES:(g + 1) * LANES]
        ss = _dot((xg * xg).astype(BF16), bd)
        parts.append(xg * lax.rsqrt(ss * (1.0 / ATT_HD) + EPS))
    return (parts[0] if len(parts) == 1 else jnp.concatenate(parts, axis=1)) * gain_row


def _rope(x, cos, sin, first):
    w = x.shape[-1]
    nxt = pltpu.roll(x, w - 16, axis=1)
    prv = pltpu.roll(x, 16, axis=1)
    return x * cos + jnp.where(first, nxt, prv) * sin


def _prep_kernel(p_ref, pkv_ref, mlp_ref, mln_ref, gna_ref, gsw_ref, bd_ref, cos_ref, sin_ref, wconv_ref,
                 naq_ref, nak_ref, nav_ref, swq_ref, swk_ref, swv_ref, qk_ref, *, rope, per):
    i = pl.program_id(0)
    x = p_ref[:, C_MLQK:C_MLQK + GROUP]
    tm = x.shape[0]
    row = lax.broadcasted_iota(jnp.int32, (tm, 1), 0)
    keep_p = jnp.where(i % per == 0, 0.0, 1.0)
    keep_n = jnp.where(i % per == per - 1, 0.0, 1.0)
    prv = jnp.where(row == 0, mlp_ref[HALO - 1:HALO, :] * keep_p, pltpu.roll(x, 1, axis=0))
    nxt = jnp.where(row == tm - 1, mln_ref[0:1, :] * keep_n, pltpu.roll(x, tm - 1, axis=0))
    y = prv * wconv_ref[0:1, :] + x * wconv_ref[1:2, :] + nxt * wconv_ref[2:3, :]
    y = y * _sigmoid(y)
    qlane = lax.broadcasted_iota(jnp.int32, (1, GROUP), 1) < GROUP // 2
    qk_ref[...] = (y * jnp.where(qlane, ATT_HD ** -0.5, 1.0)).astype(BF16)

    bd = bd_ref[...]
    scale = ATT_HD ** -0.5
    naq_ref[...] = (_head_rms(p_ref[:, C_NAQ:C_NAQ + GROUP], gna_ref[0:1, :], bd) * scale).astype(BF16)
    nak_ref[...] = _head_rms(p_ref[:, C_NAK:C_NAK + GROUP], gna_ref[1:2, :], bd).astype(BF16)
    nav_ref[...] = p_ref[:, C_NAV:C_NAV + GROUP].astype(BF16)
    q = _head_rms(p_ref[:, C_SWQ:C_SWQ + GROUP], gsw_ref[0:1, :], bd)
    k = _head_rms(pkv_ref[:, 0:LANES], gsw_ref[1:2, 0:LANES], bd)
    if rope:
        cos = cos_ref[...]
        sin = sin_ref[...]
        lane = lax.broadcasted_iota(jnp.int32, (1, GROUP), 1)
        first = (lane % 32) < 16
        q = _rope(q, jnp.concatenate([cos] * 4, axis=1), jnp.concatenate([sin] * 4, axis=1), first)
        k = _rope(k, cos, sin, first[:, 0:LANES])
    swq_ref[...] = (q * scale).astype(BF16)
    swk_ref[...] = k.astype(BF16)
    swv_ref[...] = pkv_ref[:, LANES:2 * LANES].astype(BF16)


def _prep(p, gna, gsw, bd, cos, sin, conv_w, seq, rope):
    t = p.shape[0]
    tm = _row_tile(seq, 512)
    per = seq // tm
    step = tm // HALO
    nblk8 = t // HALO
    mlcol = C_MLQK // GROUP
    out = lambda w: jax.ShapeDtypeStruct((t, w), BF16)
    ospec = lambda w: pl.BlockSpec((tm, w), lambda i: (i, 0))
    return pl.pallas_call(
        functools.partial(_prep_kernel, rope=rope, per=per),
        grid=(t // tm,),
        in_specs=[pl.BlockSpec((tm, C_MLV), lambda i: (i, 0)),
                  pl.BlockSpec((tm, 2 * LANES), lambda i: (i, C_SWK // (2 * LANES))),
                  pl.BlockSpec((HALO, GROUP), lambda i: (jnp.maximum(i * step - 1, 0), mlcol)),
                  pl.BlockSpec((HALO, GROUP), lambda i: (jnp.minimum((i + 1) * step, nblk8 - 1), mlcol)),
                  pl.BlockSpec((2, GROUP), lambda i: (0, 0)),
                  pl.BlockSpec((2, GROUP), lambda i: (0, 0)),
                  pl.BlockSpec((LANES, LANES), lambda i: (0, 0)),
                  pl.BlockSpec((tm, LANES), lambda i: (i % per, 0)),
                  pl.BlockSpec((tm, LANES), lambda i: (i % per, 0)),
                  pl.BlockSpec((3, GROUP), lambda i: (0, 0))],
        out_specs=[ospec(GROUP), ospec(GROUP), ospec(GROUP), ospec(GROUP), ospec(LANES), ospec(LANES),
                   ospec(GROUP)],
        out_shape=[out(GROUP), out(GROUP), out(GROUP), out(GROUP), out(LANES), out(LANES), out(GROUP)],
        compiler_params=_cparams(("parallel",)),
    )(p, p, p, p, gna, gsw, bd, cos, sin, conv_w)


def _pair_rows(q_pair):
    lane = lax.broadcasted_iota(jnp.int32, q_pair.shape, 1)
    zero = jnp.zeros_like(q_pair)
    return jnp.concatenate([jnp.where(lane < ATT_HD, q_pair, zero),
                            jnp.where(lane >= ATT_HD, q_pair, zero)], axis=0)


def _pair_merge(o2, inv_l):
    m = o2.shape[0] // 2
    lane = lax.broadcasted_iota(jnp.int32, (m, LANES), 1)
    return jnp.where(lane < ATT_HD, o2[0:m] * inv_l[0:m], o2[m:] * inv_l[m:])


NA_RPS = 4


def _softmax_pv(parts, sink=None):
    m = functools.reduce(jnp.maximum, [jnp.max(s, axis=-1, keepdims=True) for s, _ in parts])
    if sink is not None:
        m = jnp.maximum(m, sink)
    acc = None
    for s, v in parts:
        e = jnp.exp((s - m).astype(BF16))
        o = _dot(e, jnp.concatenate([v, jnp.ones_like(v)], axis=1))
        acc = o if acc is None else acc + o
    l = acc[:, LANES:LANES + 1]
    if sink is not None:
        l = l + jnp.exp(sink - m)
    return acc[:, 0:LANES], l


NA_WIN = NA_ROWS + NA_RPS


def _na_window(step, rows):
    return jnp.clip(step * NA_RPS - NA_ROWS // 2, 0, rows - NA_WIN)


def _na_kernel(q_ref, k_ref, v_ref, kc_ref, vc_ref, b_ref, o_ref, *, rows):
    start = pl.multiple_of(_na_window(pl.program_id(1), rows) * GRID_W, GRID_W)
    nkey = NA_WIN * GRID_W
    for p in range(GROUP // LANES):
        cs = slice(p * LANES, (p + 1) * LANES)
        qs = jnp.concatenate([_pair_rows(q_ref[rr * GRID_W:(rr + 1) * GRID_W, cs]) for rr in range(NA_RPS)],
                             axis=0)
        kw = k_ref[pl.ds(start, nkey), cs]
        vw = v_ref[pl.ds(start, nkey), cs]
        s_nb = _dot_nt(qs, kw) + b_ref[p]
        s_cx = _dot_nt(qs, kc_ref[:, cs])
        o2, l = _softmax_pv([(s_nb, vw), (s_cx, vc_ref[:, cs])])
        inv_l = 1.0 / l
        for rr in range(NA_RPS):
            sl = slice(rr * 2 * GRID_W, (rr + 1) * 2 * GRID_W)
            o_ref[rr * GRID_W:(rr + 1) * GRID_W, cs] = _pair_merge(o2[sl], inv_l[sl]).astype(BF16)


def _na_attention(q, k, v, kc, vc, bias, bsz, seq, ctx):
    rows = seq // GRID_W
    assert rows >= NA_WIN and rows % NA_RPS == 0
    steps = rows // NA_RPS
    k3, v3 = k.reshape(bsz, seq, GROUP), v.reshape(bsz, seq, GROUP)
    kc3, vc3 = kc.reshape(bsz, ctx, GROUP), vc.reshape(bsz, ctx, GROUP)
    qspec = pl.BlockSpec((NA_RPS * GRID_W, GROUP), lambda b, r: (b * steps + r, 0))
    variant = lambda b, r: ((r > 0).astype(jnp.int32) + (r == steps - 1).astype(jnp.int32), 0, 0, 0)
    return pl.pallas_call(
        functools.partial(_na_kernel, rows=rows),
        grid=(bsz, steps),
        in_specs=[qspec,
                  pl.BlockSpec((None, seq, GROUP), lambda b, r: (b, 0, 0)),
                  pl.BlockSpec((None, seq, GROUP), lambda b, r: (b, 0, 0)),
                  pl.BlockSpec((None, ctx, GROUP), lambda b, r: (b, 0, 0)),
                  pl.BlockSpec((None, ctx, GROUP), lambda b, r: (b, 0, 0)),
                  pl.BlockSpec((None,) + bias.shape[1:], variant)],
        out_specs=qspec,
        out_shape=jax.ShapeDtypeStruct((bsz * seq, GROUP), BF16),
        compiler_params=_cparams(("parallel", "arbitrary")),
    )(q, k3, v3, kc3, vc3, bias)


def _na_bias_table(rpb, rows):
    nh = rpb.shape[0]
    steps = rows // NA_RPS
    c = np.arange(GRID_W)
    kc = np.arange(GRID_W)
    cstart = np.clip(c - NA_COLS // 2, 0, GRID_W - NA_COLS)
    ok = (kc[None, :] >= cstart[:, None]) & (kc[None, :] < cstart[:, None] + NA_COLS)
    dc = np.clip(kc[None, :] - c[:, None] + NA_COLS - 1, 0, 2 * NA_COLS - 2)
    onehot = (dc.reshape(1, -1) == np.arange(2 * NA_COLS - 1)[:, None]).astype(np.float32)
    t = jnp.dot(rpb.astype(F32).reshape(-1, 2 * NA_COLS - 1), onehot, precision=lax.Precision.HIGHEST)
    t = jnp.where(ok[None, None], t.reshape(nh, -1, GRID_W, GRID_W), NEG)

    def step_layout(s):
        ws = int(np.clip(s * NA_RPS - NA_ROWS // 2, 0, rows - NA_WIN))
        r = s * NA_RPS + np.arange(NA_RPS)[:, None]
        rs = np.clip(r - NA_ROWS // 2, 0, rows - NA_ROWS)
        krow = ws + np.arange(NA_WIN)[None, :]
        return (krow - r + NA_ROWS - 1), (krow >= rs) & (krow < rs + NA_ROWS)

    layouts = [step_layout(s) for s in range(steps)]
    variants = [layouts[0], layouts[1], layouts[-1]]
    for s, (idx, valid) in enumerate(layouts):
        want = variants[(s > 0) + (s == steps - 1)]
        assert (valid == want[1]).all() and (idx[valid] == want[0][want[1]]).all()

    ndr = 2 * NA_ROWS - 1
    width = (ndr + 3) * GRID_W
    t2 = jnp.transpose(t, (0, 2, 1, 3)).reshape(nh, GRID_W, ndr * GRID_W)
    t2 = jnp.pad(t2, ((0, 0), (0, 0), (GRID_W, width - (ndr + 1) * GRID_W)), constant_values=NEG)
    t2s = jnp.pad(t2[:, :, GRID_W:], ((0, 0), (0, 0), (0, GRID_W)), constant_values=NEG)

    def build(ta_ref, tb_ref, o_ref):
        lane = lax.broadcasted_iota(jnp.int32, (GRID_W, LANES), 1)
        neg = jnp.full((GRID_W, LANES), NEG, F32)
        for v, (idx, valid) in enumerate(variants):
            for rr in range(NA_RPS):
                for e in range(2):
                    row0 = (rr * 2 + e) * GRID_W
                    for jj in range(NA_WIN // 2):
                        j = 2 * jj
                        ok0, ok1 = bool(valid[rr, j]), bool(valid[rr, j + 1])
                        if not (ok0 or ok1):
                            piece = neg
                        else:
                            blk = int(idx[rr, j] if ok0 else idx[rr, j + 1] - 1) + 1
                            src = ta_ref if blk % 2 == 0 else tb_ref
                            off = (blk - blk % 2) * GRID_W
                            piece = src[e, :, off:off + LANES]
                            if not ok0:
                                piece = jnp.where(lane >= GRID_W, piece, NEG)
                            if not ok1:
                                piece = jnp.where(lane < GRID_W, piece, NEG)
                        o_ref[v, row0:row0 + GRID_W, j * GRID_W:(j + 2) * GRID_W] = piece

    npair = nh // 2
    tspec = pl.BlockSpec((2, GRID_W, width), lambda p: (p, 0, 0))
    return pl.pallas_call(
        build,
        grid=(npair,),
        in_specs=[tspec, tspec],
        out_specs=pl.BlockSpec((3, None, NA_RPS * 2 * GRID_W, NA_WIN * GRID_W), lambda p: (0, p, 0, 0)),
        out_shape=jax.ShapeDtypeStruct((3, npair, NA_RPS * 2 * GRID_W, NA_WIN * GRID_W), F32),
        compiler_params=_cparams(("parallel",)),
    )(t2, t2s)


SW_BPS = 2


def _sw_kernel(sink_ref, q_ref, k_ref, v_ref, kc_ref, vc_ref, o_ref, *, seq):
    nq = SW_BPS * SW_BLOCK
    nwin = (SW_BPS + 2) * SW_BLOCK
    q0 = pl.program_id(1) * nq
    start = pl.multiple_of(jnp.clip(q0 - SW_BLOCK, 0, seq - nwin), SW_BLOCK)
    kw = k_ref[pl.ds(start, nwin), :]
    vw = v_ref[pl.ds(start, nwin), :]
    kc = kc_ref[...]
    vc = vc_ref[...]
    row = lax.broadcasted_iota(jnp.int32, (2 * nq, nwin), 0)
    col = lax.broadcasted_iota(jnp.int32, (2 * nq, nwin), 1)
    ok = jnp.abs((start + col) - (q0 + row % nq)) <= SW_BLOCK
    half = lax.broadcasted_iota(jnp.int32, (2 * nq, 1), 0) < nq
    for j in range(GROUP // LANES):
        cs = slice(j * LANES, (j + 1) * LANES)
        qs = _pair_rows(q_ref[:, cs])
        s_bd = jnp.where(ok, _dot_nt(qs, kw), NEG)
        s_cx = _dot_nt(qs, kc)
        sink = jnp.where(half, sink_ref[j], sink_ref[j + 4])
        o2, l = _softmax_pv([(s_bd, vw), (s_cx, vc)], sink)
        o_ref[:, cs] = _pair_merge(o2, 1.0 / l).astype(BF16)


def _sw_attention(q, k, v, kc, vc, sink, bsz, seq, ctx):
    assert seq % (SW_BPS * SW_BLOCK) == 0 and seq >= (SW_BPS + 2) * SW_BLOCK
    nb = seq // (SW_BPS * SW_BLOCK)
    k3, v3 = k.reshape(bsz, seq, LANES), v.reshape(bsz, seq, LANES)
    kc3, vc3 = kc.reshape(bsz, ctx, LANES), vc.reshape(bsz, ctx, LANES)
    return pl.pallas_call(
        functools.partial(_sw_kernel, seq=seq),
        grid=(bsz, nb),
        in_specs=[pl.BlockSpec(memory_space=pltpu.SMEM),
                  pl.BlockSpec((SW_BPS * SW_BLOCK, GROUP), lambda b, n: (b * nb + n, 0)),
                  pl.BlockSpec((None, seq, LANES), lambda b, n: (b, 0, 0)),
                  pl.BlockSpec((None, seq, LANES), lambda b, n: (b, 0, 0)),
                  pl.BlockSpec((None, ctx, LANES), lambda b, n: (b, 0, 0)),
                  pl.BlockSpec((None, ctx, LANES), lambda b, n: (b, 0, 0))],
        out_specs=pl.BlockSpec((SW_BPS * SW_BLOCK, GROUP), lambda b, n: (b * nb + n, 0)),
        out_shape=jax.ShapeDtypeStruct((bsz * seq, GROUP), BF16),
        compiler_params=_cparams(("parallel", "arbitrary")),
    )(sink, q, k3, v3, kc3, vc3)


def _ctx_attn_kernel(sink_ref, qa_ref, ka_ref, va_ref, qb_ref, kb_ref, vb_ref, oa_ref, ob_ref):
    for p in range(GROUP // LANES):
        cs = slice(p * LANES, (p + 1) * LANES)
        qs = _pair_rows(qa_ref[:, cs])
        o2, l = _softmax_pv([(_dot_nt(qs, ka_ref[:, cs]), va_ref[:, cs])])
        oa_ref[:, cs] = _pair_merge(o2, 1.0 / l).astype(BF16)
    ctx = qb_ref.shape[0]
    half = lax.broadcasted_iota(jnp.int32, (2 * ctx, 1), 0) < ctx
    for j in range(GROUP // LANES):
        cs = slice(j * LANES, (j + 1) * LANES)
        qs = _pair_rows(qb_ref[:, cs])
        sink = jnp.where(half, sink_ref[j], sink_ref[j + 4])
        o2, l = _softmax_pv([(_dot_nt(qs, kb_ref[...]), vb_ref[...])], sink)
        ob_ref[:, cs] = _pair_merge(o2, 1.0 / l).astype(BF16)


def _ctx_attention(qa, ka, va, qb, kb, vb, sink, bsz, ctx):
    big = pl.BlockSpec((ctx, GROUP), lambda b: (b, 0))
    small = pl.BlockSpec((ctx, LANES), lambda b: (b, 0))
    return pl.pallas_call(
        _ctx_attn_kernel,
        grid=(bsz,),
        in_specs=[pl.BlockSpec(memory_space=pltpu.SMEM), big, big, big, big, small, small],
        out_specs=[big, big],
        out_shape=[jax.ShapeDtypeStruct((bsz * ctx, GROUP), BF16)] * 2,
        compiler_params=_cparams(("parallel",)),
    )(sink, qa, ka, va, qb, kb, vb)


def _mlstm_kernel(*refs, chunk, emit):
    (qkf_ref, vf_ref, gif_ref, gff_ref, qkr_ref, vr_ref, gir_ref, gfr_ref,
     brow_ref, c0_ref, m0_ref) = refs[:11]
    if emit:
        hf_ref, hr_ref = refs[11:13]
        rest = refs[13:]
    else:
        hf_ref = hr_ref = None
        rest = refs[11:]
    c_out, m_out, c_s, m_s = rest
    i = pl.program_id(1)

    @pl.when(i == 0)
    def _():
        c_s[...] = c0_ref[...]
        m_s[...] = m0_ref[...]

    row = lax.broadcasted_iota(jnp.int32, (chunk, chunk), 0)
    col = lax.broadcasted_iota(jnp.int32, (chunk, chunk), 1)
    trow = lax.broadcasted_iota(jnp.int32, (chunk, 1), 0)
    lane = lax.broadcasted_iota(jnp.int32, (chunk, LANES), 1)
    srow = lax.broadcasted_iota(jnp.int32, (LANES, 1), 0)
    dirs = ((qkf_ref, vf_ref, gif_ref, gff_ref, hf_ref), (qkr_ref, vr_ref, gir_ref, gfr_ref, hr_ref))
    chains = [(b, d) for b in range(qkf_ref.shape[0]) for d in range(2)]
    old_c = {(b, d, pp): c_s[b, d, pp] for b, d in chains for pp in range(2)}
    old_m = {(b, d): m_s[b, d, 0:1, :] for b, d in chains}
    new_c, new_m = {}, {}
    for b, d in chains:
        qk_ref, v_ref, gi_ref, gf_ref, h_ref = [r if r is None else r.at[b] for r in dirs[d]]
        causal = (col <= row) if d == 0 else (col >= row)
        ipre = gi_ref[...] + brow_ref[0:1, :]
        fl = _log_sigmoid(gf_ref[...] + brow_ref[1:2, :])
        fcum = _exact_left(jnp.where(causal, 1.0, 0.0).astype(BF16), fl)
        ftot = jnp.sum(fl, axis=0, keepdims=True)
        m_prev = old_m[b, d]
        w_log = ftot - fcum + ipre
        m_new = jnp.maximum(ftot + m_prev, jnp.max(w_log, axis=0, keepdims=True))
        cd = jnp.exp(ftot + m_prev - m_new)
        w = jnp.exp(w_log - m_new)
        if emit:
            b_col = ipre - fcum
            run = b_col
            sh = 1
            while sh < chunk:
                if d == 0:
                    run = jnp.maximum(run, jnp.where(trow >= sh, pltpu.roll(run, sh, axis=0), NEG))
                else:
                    run = jnp.maximum(run, jnp.where(trow < chunk - sh, pltpu.roll(run, chunk - sh, axis=0), NEG))
                sh *= 2
            log_inter = fcum + m_prev
            m_row = jnp.maximum(log_inter, fcum + run)
            inter_w = jnp.exp(log_inter - m_row)
            fcm = fcum - m_row
            floor = jnp.exp(-m_row)
            b_rows = b_col.T[0:8, :]
        for pp in range(2):
            qp = qk_ref[:, pp * LANES:(pp + 1) * LANES]
            kp = qk_ref[:, GROUP // 2 + pp * LANES:GROUP // 2 + (pp + 1) * LANES]
            cn = old_c[b, d, pp]
            inc = None
            for e in range(2):
                h = 2 * pp + e
                c = 4 * d + h
                sel = (lane < ATT_HD) if e == 0 else (lane >= ATT_HD)
                vh = v_ref[:, h * LANES:(h + 1) * LANES]
                if emit:
                    qm = jnp.where(sel, qp, jnp.zeros_like(qp))
                    d_mat = jnp.exp(jnp.where(causal, fcm[:, c:c + 1] + b_rows[c:c + 1, :], NEG))
                    s_mat = (_dot_nt(qm, kp) * d_mat).astype(BF16)
                    vb = vh.astype(BF16)
                    acc = (inter_w[:, c:c + 1] * _dot(qm, cn.astype(BF16))
                           + _dot(s_mat, jnp.concatenate([vb, jnp.ones_like(vb)], axis=1)))
                    den = jnp.maximum(jnp.abs(acc[:, LANES:]), floor[:, c:c + 1])
                    h_ref[:, h * LANES:(h + 1) * LANES] = acc[:, 0:LANES] / den
                km = jnp.where(sel, kp, jnp.zeros_like(kp))
                wc = jnp.broadcast_to(w[:, c:c + 1], (chunk, LANES))
                upd = _dot_tn(km, jnp.concatenate([wc * vh, wc], axis=1).astype(BF16))
                inc = upd if inc is None else inc + upd
            c_lo, c_hi = 4 * d + 2 * pp, 4 * d + 2 * pp + 1
            new_c[b, d, pp] = (jnp.where(srow < ATT_HD, cd[:, c_lo:c_lo + 1], cd[:, c_hi:c_hi + 1]) * cn
                               + inc)
        new_m[b, d] = m_new
    for key, val in new_c.items():
        c_s[key] = val
    for (b, d), val in new_m.items():
        m_s[b, d, 0:1, :] = val

    @pl.when(i == pl.num_programs(1) - 1)
    def _():
        c_out[...] = c_s[...]
        m_out[...] = m_s[...]


def _mlstm_scan(qk, p, brow, state, bsz, seq, chunk, emit):
    nc = seq // chunk
    bps = bsz if bsz * chunk <= ML_STEP_ROWS else max(1, ML_STEP_ROWS // chunk)
    assert bsz % bps == 0
    fwd = lambda g, i: i
    rev = lambda g, i: nc - 1 - i
    qk3 = qk.reshape(bsz, seq, GROUP)
    p3 = p.reshape(bsz, seq, NP)

    def chunk_specs(idx):
        return [pl.BlockSpec((bps, chunk, GROUP), lambda g, i: (g, idx(g, i), 0)),
                pl.BlockSpec((bps, chunk, GROUP), lambda g, i: (g, idx(g, i), C_MLV // GROUP)),
                pl.BlockSpec((bps, chunk, LANES), lambda g, i: (g, idx(g, i), C_MLI // LANES)),
                pl.BlockSpec((bps, chunk, LANES), lambda g, i: (g, idx(g, i), C_MLF // LANES))]

    st_dims = [(2, 2, LANES, 2 * LANES), (2, 8, LANES)]
    st_specs = [pl.BlockSpec((bps,) + s, lambda g, i, n=len(s): (g,) + (0,) * n) for s in st_dims]
    st_shapes = [jax.ShapeDtypeStruct((bsz,) + s, F32) for s in st_dims]
    h_specs = [pl.BlockSpec((bps, chunk, GROUP), lambda g, i: (g, fwd(g, i), 0)),
               pl.BlockSpec((bps, chunk, GROUP), lambda g, i: (g, rev(g, i), 0))] if emit else []
    h_shapes = [jax.ShapeDtypeStruct((bsz, seq, GROUP), F32)] * 2 if emit else []
    outs = pl.pallas_call(
        functools.partial(_mlstm_kernel, chunk=chunk, emit=emit),
        grid=(bsz // bps, nc),
        in_specs=chunk_specs(fwd) + chunk_specs(rev)
        + [pl.BlockSpec((8, LANES), lambda g, i: (0, 0))] + st_specs,
        out_specs=h_specs + st_specs,
        out_shape=h_shapes + st_shapes,
        scratch_shapes=[pltpu.VMEM((bps,) + s, F32) for s in st_dims],
        compiler_params=_cparams(("parallel", "arbitrary")),
    )(qk3, p3, p3, p3, qk3, p3, p3, p3, brow, *state)
    if emit:
        return (outs[0].reshape(bsz * seq, GROUP), outs[1].reshape(bsz * seq, GROUP), tuple(outs[2:]))
    return None, None, tuple(outs)


def _hgrn_constants(chunk):
    nlev = int(np.log2(chunk))
    assert 2 ** nlev == chunk
    t = np.arange(chunk)
    mats = [(t[None, :] <= t[:, None])]
    masks = []
    for lev in range(nlev):
        n = chunk >> (lev + 1)
        b0 = t - t % (2 * n)
        upper = (t % (2 * n)) >= n
        m_up = (t[None, :] >= (b0 + n)[:, None]) & (t[None, :] <= t[:, None])
        m_lo = (t[None, :] > t[:, None]) & (t[None, :] <= (b0 + n - 1)[:, None])
        mats.append(np.where(upper[:, None], m_up, m_lo))
        same = (t[:, None] // (2 * n)) == (t[None, :] // (2 * n))
        masks.append(same & upper[:, None] & ~upper[None, :])
    masks.append(t[:, None] == t[None, :])
    mstack = np.concatenate(mats, axis=0).astype(np.float32)
    lmask = np.stack([np.kron(np.eye(2), m.astype(np.float32)) for m in masks])
    nr = mstack.shape[0] // chunk
    mstack_rev = mstack.reshape(nr, chunk, chunk)[:, ::-1, ::-1].reshape(-1, chunk)
    lmask_rev = lmask.reshape(nlev + 1, 2, chunk, 2, chunk)[:, :, ::-1, :, ::-1].reshape(lmask.shape)
    return (jnp.asarray(np.stack([mstack, mstack_rev]), BF16),
            jnp.asarray(np.stack([lmask, lmask_rev]), F32), nlev)


def _stack_pair(x, pp):
    return jnp.concatenate([x[:, (2 * pp) * LANES:(2 * pp + 1) * LANES],
                            x[:, (2 * pp + 1) * LANES:(2 * pp + 2) * LANES]], axis=0)


def _hgrn_kernel(*refs, chunk, nlev, emit, zero_lb):
    (qf_ref, vf_ref, ff_ref, qr_ref, vr_ref, fr_ref, lb_ref, ms_ref, lm_ref, s0_ref) = refs[:10]
    if emit:
        of_ref, or_ref = refs[10:12]
        rest = refs[12:]
    else:
        of_ref = or_ref = None
        rest = refs[10:]
    s_out, s_s = rest
    i = pl.program_id(1)

    @pl.when(i == 0)
    def _():
        s_s[...] = s0_ref[...]

    dirs = ((qf_ref, vf_ref, ff_ref, of_ref), (qr_ref, vr_ref, fr_ref, or_ref))
    chains = [(b, d) for b in range(vf_ref.shape[0]) for d in range(2)]
    old_s = {(b, d, h): s_s[b, d, h] for b, d in chains for h in range(4)}
    new_s = {}

    for b, d in chains:
        q_ref, v_ref, f_ref, o_ref = [r if r is None else r.at[b] for r in dirs[d]]
        fpre = f_ref[...]
        vb = v_ref[...].astype(BF16)
        e_f = jnp.exp(-jnp.abs(fpre))
        ls = jnp.minimum(fpre, 0.0) - _log1p_unit(e_f)
        sig_neg = jnp.where(fpre >= 0.0, e_f, 1.0) / (1.0 + e_f)
        if zero_lb:
            logf, k = ls, sig_neg
        else:
            a = lb_ref[d, 0:1, :]
            bb = lb_ref[d, 1:2, :] + ls
            logf = jnp.maximum(a, bb) + _log1p_unit(jnp.exp(-jnp.abs(a - bb)))
            k = lb_ref[d, 2:3, :] * sig_neg
        lf2 = logf * LOG2E
        hi = lf2.astype(BF16)
        lo = (lf2 - hi.astype(F32)).astype(BF16)
        gtot = jnp.sum(lf2, axis=0, keepdims=True)
        if emit:
            qpre = q_ref[...]
            q = qpre * _sigmoid(qpre)
            qb = q.astype(BF16)
            kb = k.astype(BF16)
        ms = ms_ref[d] if emit else ms_ref[d, 0:chunk]
        ex = _dot(ms, hi) + _dot(ms, lo)
        ktil = (k * jnp.exp2(gtot - ex[0:chunk])).astype(BF16)
        if emit:
            qe = (q * jnp.exp2(ex[0:chunk])).astype(BF16)
            lev_ops = []
            for lev in range(nlev):
                e_l = jnp.exp2(ex[(1 + lev) * chunk:(2 + lev) * chunk].astype(BF16))
                lev_ops.append((qb * e_l, kb * e_l))
        if emit:
            att = [lm_ref[d, nlev] * _dot_nt(_stack_pair(qb, pp), _stack_pair(kb, pp)) for pp in range(2)]
            for lev, (q_l, k_l) in enumerate(lev_ops):
                for pp in range(2):
                    att[pp] = att[pp] + lm_ref[d, lev] * _dot_nt(_stack_pair(q_l, pp), _stack_pair(k_l, pp))
            o_pair = [_dot(att[pp].astype(BF16), _stack_pair(vb, pp)) for pp in range(2)]
        for h in range(4):
            cs = slice(h * LANES, (h + 1) * LANES)
            s_t = old_s[b, d, h]
            if emit:
                pp, e = divmod(h, 2)
                o_ref[:, cs] = o_pair[pp][e * chunk:(e + 1) * chunk] + _dot_nt(qe[:, cs], s_t.astype(BF16))
            new_s[b, d, h] = s_t * jnp.exp2(gtot[:, cs]) + _dot_tn(vb[:, cs], ktil[:, cs])

    for key, val in new_s.items():
        s_s[key] = val

    @pl.when(i == pl.num_programs(1) - 1)
    def _():
        s_out[...] = s_s[...]


def _hgrn_scan(p, lbc, consts, state, bsz, seq, chunk, emit, zero_lb):
    mstack, lmask, nlev = consts
    nc = seq // chunk
    bps = bsz
    fwd = lambda g, i: i
    rev = lambda g, i: nc - 1 - i
    p3 = p.reshape(bsz, seq, NP)

    def chunk_specs(idx, d):
        return [pl.BlockSpec((bps, chunk, GROUP), lambda g, i: (g, idx(g, i), C_HGQ // GROUP)),
                pl.BlockSpec((bps, chunk, GROUP), lambda g, i: (g, idx(g, i), C_HGI // GROUP)),
                pl.BlockSpec((bps, chunk, GROUP), lambda g, i: (g, idx(g, i), C_HGF // GROUP + d))]

    st_spec = pl.BlockSpec((bps, 2, 4, LANES, LANES), lambda g, i: (g, 0, 0, 0, 0))
    st_shape = jax.ShapeDtypeStruct((bsz, 2, 4, LANES, LANES), F32)
    o_specs = [pl.BlockSpec((bps, chunk, GROUP), lambda g, i: (g, fwd(g, i), 0)),
               pl.BlockSpec((bps, chunk, GROUP), lambda g, i: (g, rev(g, i), 0))] if emit else []
    o_shapes = [jax.ShapeDtypeStruct((bsz, seq, GROUP), F32)] * 2 if emit else []
    full = lambda a: pl.BlockSpec(a.shape, lambda g, i: (0,) * a.ndim)
    outs = pl.pallas_call(
        functools.partial(_hgrn_kernel, chunk=chunk, nlev=nlev, emit=emit, zero_lb=zero_lb),
        grid=(bsz // bps, nc),
        in_specs=chunk_specs(fwd, 0) + chunk_specs(rev, 1) + [full(lbc), full(mstack), full(lmask), st_spec],
        out_specs=o_specs + [st_spec],
        out_shape=o_shapes + [st_shape],
        scratch_shapes=[pltpu.VMEM((bps, 2, 4, LANES, LANES), F32)],
        compiler_params=_cparams(("parallel", "arbitrary")),
    )(p3, p3, p3, p3, p3, p3, lbc, mstack, lmask, state)
    if emit:
        return outs[0].reshape(bsz * seq, GROUP), outs[1].reshape(bsz * seq, GROUP), outs[2]
    return None, None, outs[0]


def _outproj_kernel(x_ref, ya_ref, yb_ref, hf_ref, hr_ref, mo_ref, of_ref, or_ref, hg_ref,
                    w_ref, g_ref, o_ref):
    acc = _dot(ya_ref[...], w_ref[0]) + _dot(yb_ref[...], w_ref[1])
    yc = _sigmoid(mo_ref[...]) * (hf_ref[...] + hr_ref[...])
    acc = acc + _dot(yc.astype(BF16), w_ref[2])
    o = of_ref[...] + or_ref[...]
    gate = _sigmoid(hg_ref[...])
    parts = []
    for h in range(4):
        cs = slice(h * LANES, (h + 1) * LANES)
        oh = o[:, cs]
        parts.append(oh * lax.rsqrt(jnp.mean(oh * oh, axis=-1, keepdims=True) + EPS) * gate[:, cs])
    yd = jnp.concatenate(parts, axis=1)
    acc = acc + _dot(yd.astype(BF16), w_ref[3])
    o_ref[...] = x_ref[...] + g_ref[...] * acc


def _out_proj(x, ya, yb, hf, hr, of, orv, p, w_all, layer, gate, rows_per_mod):
    t, d = x.shape
    tm = _row_tile(rows_per_mod, 512)
    per = rows_per_mod // tm
    grp = lambda cb=0: pl.BlockSpec((tm, GROUP), lambda i: (i, cb))
    return pl.pallas_call(
        _outproj_kernel,
        grid=(t // tm,),
        in_specs=[pl.BlockSpec((tm, d), lambda i: (i, 0)), grp(), grp(), grp(), grp(),
                  grp(C_MLO // GROUP), grp(), grp(), grp(C_HGG // GROUP),
                  pl.BlockSpec((None, 4, GROUP, d), lambda i: (layer, 0, 0, 0)),
                  pl.BlockSpec((None, 1, d), lambda i: (i // per, 0, 0))],
        out_specs=pl.BlockSpec((tm, d), lambda i: (i, 0)),
        out_shape=jax.ShapeDtypeStruct((t, d), F32),
        compiler_params=_cparams(("parallel",)),
    )(x, ya, yb, hf, hr, p, of, orv, p, w_all, gate)


def _ffn_kernel(x_ref, xp_ref, xn_ref, g_ref, sh_ref, sc_ref, gate_ref, wa_ref, wu_ref, wc_ref, wd_ref,
                o_ref, xs_ref, acc_ref, *, per, inner):
    i = pl.program_id(0)
    j = pl.program_id(1)
    tm = x_ref.shape[0]

    @pl.when(j == 0)
    def _():
        nm = lambda x: _norm_mod(x, g_ref[...], sh_ref[...], sc_ref[...])
        keep_p = jnp.where(i % per == 0, 0.0, 1.0)
        keep_n = jnp.where(i % per == per - 1, 0.0, 1.0)
        _norm_mod_rows(x_ref, xs_ref, tm, g_ref[...], sh_ref[...], sc_ref[...])
        xs_ref[tm:tm + HALO, :] = (nm(xn_ref[...]) * keep_n).astype(BF16)
        xs_ref[tm + HALO:, :] = (nm(xp_ref[...]) * keep_p).astype(BF16)
        acc_ref[...] = jnp.zeros_like(acc_ref)

    a = _dot(xs_ref[...], wa_ref[...])
    rows = a.shape[0]
    prv = pltpu.roll(a, 1, axis=0)[0:tm]
    nxt = pltpu.roll(a, rows - 1, axis=0)[0:tm]
    if inner:
        pos = lax.broadcasted_iota(jnp.int32, (tm, 1), 0) % inner
        prv = jnp.where(pos == 0, 0.0, prv)
        nxt = jnp.where(pos == inner - 1, 0.0, nxt)
    conv = prv * wc_ref[0:1, :] + a[0:tm] * wc_ref[1:2, :] + nxt * wc_ref[2:3, :]
    u = _dot(xs_ref[0:tm, :], wu_ref[...])
    act = (conv * _sigmoid(conv) * u).astype(BF16)
    acc_ref[...] += _dot(act, wd_ref[...])

    @pl.when(j == pl.num_programs(1) - 1)
    def _():
        o_ref[...] = x_ref[...] + gate_ref[...] * acc_ref[...]


def _ffn(x, gain, shift, scale, gate, w_up, w_conv, w_down, layer, seq):
    t, d = x.shape
    f = w_down.shape[1]
    tm = 512 if t % 512 == 0 else t
    assert seq % tm == 0 or (tm % seq == 0 and shift.shape[0] == 1)
    tf = 512 if f % 512 == 0 else f
    per = max(seq // tm, 1)
    inner = seq if seq < tm else 0
    nf = f // tf
    step = tm // HALO
    nblk8 = t // HALO
    shared = shift.shape[0] == 1
    mod = pl.BlockSpec((None, 1, d), lambda i, j: (0 if shared else i // per, 0, 0))
    return pl.pallas_call(
        functools.partial(_ffn_kernel, per=per, inner=inner),
        grid=(t // tm, nf),
        in_specs=[pl.BlockSpec((tm, d), lambda i, j: (i, 0)),
                  pl.BlockSpec((HALO, d), lambda i, j: (jnp.maximum(i * step - 1, 0), 0)),
                  pl.BlockSpec((HALO, d), lambda i, j: (jnp.minimum((i + 1) * step, nblk8 - 1), 0)),
                  pl.BlockSpec((1, d), lambda i, j: (0, 0)), mod, mod, mod,
                  pl.BlockSpec((None, d, tf), lambda i, j: (layer, 0, j)),
                  pl.BlockSpec((None, d, tf), lambda i, j: (layer, 0, nf + j)),
                  pl.BlockSpec((None, 3, tf), lambda i, j: (layer, 0, j)),
                  pl.BlockSpec((None, tf, d), lambda i, j: (layer, j, 0))],
        out_specs=pl.BlockSpec((tm, d), lambda i, j: (i, 0)),
        out_shape=jax.ShapeDtypeStruct((t, d), F32),
        scratch_shapes=[pltpu.VMEM((tm + 2 * HALO, d), BF16), pltpu.VMEM((tm, d), F32)],
        compiler_params=_cparams(("parallel", "arbitrary")),
    )(x, x, x, gain, shift, scale, gate, w_up, w_up, w_conv, w_down)


def _perm_w_in(w):
    n, d = w.shape[:2]
    swq = w[..., 1536:2048].reshape(n, d, 2, 4, ATT_HD).transpose(0, 1, 3, 2, 4).reshape(n, d, GROUP)
    pad = jnp.zeros((n, d, LANES - 8), w.dtype)
    cols = [w[..., 0:1536], swq, w[..., 2304:3840], w[..., 3856:6416], w[..., 2048:2304],
            w[..., 3840:3848], pad, w[..., 3848:3856], pad]
    out = jnp.concatenate(cols, axis=-1).astype(BF16)
    assert out.shape[-1] == NP
    return out


def _perm_w_out(w):
    n, _, d = w.shape
    w4 = w.reshape(n, 4, GROUP, d)
    swo = w4[:, 1].reshape(n, 2, 4, ATT_HD, d).transpose(0, 2, 1, 3, 4).reshape(n, GROUP, d)
    return jnp.stack([w4[:, 0], swo, w4[:, 2], w4[:, 3]], axis=1).astype(BF16)


def _rope_tables(seq):
    t = jnp.arange(seq)
    half = ATT_HD // 2
    inv = ROPE_BASE ** (-jnp.arange(0, half, 2, dtype=F32) / half)
    ang_r = (t // GRID_W).astype(F32)[:, None] * inv
    ang_c = (t % GRID_W).astype(F32)[:, None] * inv
    cr, sr, cc, sc = jnp.cos(ang_r), jnp.sin(ang_r), jnp.cos(ang_c), jnp.sin(ang_c)
    cos = jnp.concatenate([cr, cr, cc, cc] * 2, axis=-1)
    sin = jnp.concatenate([-sr, sr, -sc, sc] * 2, axis=-1)
    return cos, sin


def kernel(x, c, ctx, c_ctx, w_mod, b_mod, norm_mix, norm_ffn, w_in, w_out, na_qk_gain, na_rpb,
           sw_qk_gain, sw_sink, ml_conv, ml_gate_bias, hg_lb, ffn_up, ffn_conv, ffn_down):
    bsz, seq, d = x.shape
    nctx = ctx.shape[1]
    depth = w_mod.shape[0]
    ml_chunk_ctx, ml_chunk = math.gcd(ML_CHUNK, nctx), math.gcd(ML_CHUNK, seq)
    chunk = math.gcd(HG_CHUNK, math.gcd(seq, nctx))
    assert bsz + 1 <= 8

    c_all = jnp.concatenate([c, c_ctx[None], jnp.zeros((8 - bsz - 1, d), F32)], axis=0)
    mod = _modulation(c_all, w_mod, b_mod).reshape(depth, 8, 6, d)
    cos, sin = _rope_tables(seq)
    bd = jnp.asarray(np.kron(np.eye(LANES // ATT_HD), np.ones((ATT_HD, ATT_HD))), BF16)
    lb_cum = jnp.cumsum(jax.nn.softmax(hg_lb.astype(F32), axis=0), axis=0)
    lower = lb_cum - lb_cum[:1]
    consts = _hgrn_constants(chunk)
    w_in_p = _perm_w_in(w_in)
    w_out_p = _perm_w_out(w_out)
    w_up = ffn_up.astype(BF16)
    w_dn = ffn_down.astype(BF16)
    w_cv = ffn_conv.astype(F32)

    h = x.reshape(bsz * seq, d)
    hc = ctx.reshape(bsz * nctx, d)
    for l in range(depth):
        emit_ctx = l < depth - 1
        lat = lambda k: mod[l, :bsz, k].reshape(bsz, 1, d)
        cx = lambda k: mod[l, bsz, k].reshape(1, 1, d)
        gain = norm_mix[l].reshape(1, d)
        p_lat = _in_proj(h, gain, lat(0), lat(1), w_in_p, l, seq)
        p_ctx = _in_proj(hc, gain, cx(0), cx(1), w_in_p, l, bsz * nctx)

        gna = jnp.tile(na_qk_gain[l].astype(F32), (1, GROUP // ATT_HD))
        gsw = jnp.tile(sw_qk_gain[l].astype(F32), (1, GROUP // ATT_HD))
        conv_w = ml_conv[l].astype(F32)
        naq, nak, nav, swq, swk, swv, qk_lat = _prep(p_lat, gna, gsw, bd, cos, sin, conv_w, seq, True)
        cnaq, cnak, cnav, cswq, cswk, cswv, qk_ctx = _prep(p_ctx, gna, gsw, bd, cos, sin, conv_w, nctx, False)
        sink = sw_sink[l].astype(F32)
        ya = _na_attention(naq, nak, nav, cnak, cnav, _na_bias_table(na_rpb[l], seq // GRID_W), bsz, seq, nctx)
        yb = _sw_attention(swq, swk, swv, cswk, cswv, sink, bsz, seq, nctx)

        gb = ml_gate_bias[l].astype(F32).reshape(2, 8)
        brow = jnp.zeros((8, LANES), F32).at[0:2, 0:8].set(gb)
        ml0 = (jnp.zeros((bsz, 2, 2, LANES, 2 * LANES), F32), jnp.zeros((bsz, 2, 8, LANES), F32))
        hcf, hcr, ml_state = _mlstm_scan(qk_ctx, p_ctx, brow, ml0, bsz, nctx, ml_chunk_ctx, emit_ctx)
        hf, hr, _ = _mlstm_scan(qk_lat, p_lat, brow, ml_state, bsz, seq, ml_chunk, True)

        lb = lower[l]
        lbc = jnp.stack([jnp.maximum(jnp.log(lb), NEG), jnp.log1p(-lb), 1.0 - lb], axis=1)
        lbc = jnp.concatenate([lbc, jnp.zeros((2, 5, GROUP), F32)], axis=1)
        hg0 = jnp.zeros((bsz, 2, 4, LANES, LANES), F32)
        ocf, ocr, hg_state = _hgrn_scan(p_ctx, lbc, consts, hg0, bsz, nctx, chunk, emit_ctx, l == 0)
        of, orv, _ = _hgrn_scan(p_lat, lbc, consts, hg_state, bsz, seq, chunk, True, l == 0)

        gain2 = norm_ffn[l].reshape(1, d)
        h = _out_proj(h, ya, yb, hf, hr, of, orv, p_lat, w_out_p, l, lat(2), seq)
        h = _ffn(h, gain2, lat(3), lat(4), lat(5), w_up, w_cv, w_dn, l, seq)
        if emit_ctx:
            yac, ybc = _ctx_attention(cnaq, cnak, cnav, cswq, cswk, cswv, sink, bsz, nctx)
            hc = _out_proj(hc, yac, ybc, hcf, hcr, ocf, ocr, p_ctx, w_out_p, l, cx(2), bsz * nctx)
            hc = _ffn(hc, gain2, cx(3), cx(4), cx(5), w_up, w_cv, w_dn, l, nctx)
    return h.reshape(bsz, seq, d)
```
